```python
import math
import jax, jax.numpy as jnp
from jax import lax
import numpy as np

D_MODEL = 4096
BATCH = 16
SEQ = 256
DEPTH = 2
DEC_BATCH = 2
DEC_SEQ = 4096
PAST_LEN = 512

GRID_W = 64
HEAD_DIM = 128
N_HEADS = D_MODEL // 256
N_KV_HEADS = N_HEADS // 4
ATTN_DIM = N_HEADS * HEAD_DIM
KV_DIM = N_KV_HEADS * HEAD_DIM
HY_DIM = D_MODEL // 4
GM_DIM = D_MODEL // 4
GM_HEADS = 8
CHUNK = 128
MIX_DIM = ATTN_DIM + HY_DIM + GM_DIM
PROJ_DIM = ATTN_DIM + 2 * KV_DIM + 3 * HY_DIM + 2 * GM_DIM
Q_BLOCK = 128
ROPE_THETA = 10000.0
SHORT_CONV = 3
HY_ORDER = 2
FILTER_EMB = 33
FILTER_HIDDEN = 64
DECAY_TARGET = 1e-2
FAST_DECAY_PCT = 0.3
SLOW_DECAY_PCT = 1.5
MOD_SHIFT = 0.05
N_GROUPS = 4
EXPERTS_PER_GROUP = 4
N_EXPERTS = N_GROUPS * EXPERTS_PER_GROUP
EXPERT_FF = D_MODEL // 4
TOP_K = 2
EPS = 1e-6

kernel_name = 'hybrid_flow_prefix_trunk'


def rmsnorm(x, g):
    xf = x.astype(jnp.float32)
    y = xf * lax.rsqrt(jnp.mean(xf * xf, axis=-1, keepdims=True) + EPS)
    return (y * g.astype(jnp.float32)).astype(x.dtype)


def layernorm(x, g, b):
    xf = x.astype(jnp.float32)
    mu = jnp.mean(xf, axis=-1, keepdims=True)
    var = jnp.mean(jnp.square(xf - mu), axis=-1, keepdims=True)
    y = (xf - mu) * lax.rsqrt(var + EPS) * g.astype(jnp.float32) + b.astype(jnp.float32)
    return y.astype(x.dtype)


def axial_positions(n_tokens):
    n_rows = n_tokens // GRID_W
    row = jnp.repeat(jnp.arange(n_rows, dtype=jnp.float32), GRID_W)
    col = jnp.tile(jnp.arange(GRID_W, dtype=jnp.float32), n_rows)
    return row, col


def rope_axial(x, row, col):
    half = HEAD_DIM // 2
    quarter = half // 2
    freqs = ROPE_THETA ** (-jnp.arange(quarter, dtype=jnp.float32) / quarter)

    def rotate(xa, pos):
        ang = pos[:, None] * freqs[None, :]
        cos = jnp.cos(ang)[None, :, None, :]
        sin = jnp.sin(ang)[None, :, None, :]
        x1, x2 = xa[..., :quarter], xa[..., quarter:]
        return jnp.concatenate([x1 * cos - x2 * sin, x1 * sin + x2 * cos], axis=-1)

    xf = x.astype(jnp.float32)
    out = jnp.concatenate([rotate(xf[..., :half], row), rotate(xf[..., half:], col)], axis=-1)
    return out.astype(x.dtype)


def block_attention(q, k, v):
    B, Lq = q.shape[0], q.shape[1]
    G = N_HEADS // N_KV_HEADS
    nb = Lq // Q_BLOCK
    qb = q.reshape(B, nb, Q_BLOCK, N_KV_HEADS, G, HEAD_DIM).transpose(1, 0, 2, 3, 4, 5)
    scale = HEAD_DIM ** -0.5

    def one_block(qblk):
        s = jnp.einsum('bqkgd,bskd->bkgqs', qblk, k).astype(jnp.float32) * scale
        p = jax.nn.softmax(s, axis=-1).astype(v.dtype)
        return jnp.einsum('bkgqs,bskd->bqkgd', p, v)

    o = lax.map(one_block, qb)
    return o.transpose(1, 0, 2, 3, 4, 5).reshape(B, Lq, ATTN_DIM)


def short_conv(x, w, b):
    L = x.shape[1]
    pad = SHORT_CONV // 2
    xp = jnp.pad(x, ((0, 0), (pad, pad), (0, 0)))
    y = b
    for j in range(SHORT_CONV):
        y = y + xp[:, j:j + L] * w[j]
    return y


def implicit_filters(L, p):
    bands = (FILTER_EMB - 1) // 2
    t = jnp.linspace(0.0, 1.0, L, dtype=jnp.float32)[:, None]
    w = 2.0 * math.pi * jnp.arange(L, dtype=jnp.float32)[:, None] / L
    f = jnp.linspace(1e-4, bands - 1, bands, dtype=jnp.float32)[None, :]
    z = jnp.concatenate([t, jnp.cos(f * w), -jnp.sin(f * w)], axis=-1)
    fr = p['hf_freq'].astype(jnp.float32)
    h = jnp.sin(fr * (z @ p['hf_w1'].astype(jnp.float32) + p['hf_b1'].astype(jnp.float32)))
    h = jnp.sin(fr * (h @ p['hf_w2'].astype(jnp.float32) + p['hf_b2'].astype(jnp.float32)))
    h = h @ p['hf_w3'].astype(jnp.float32) + p['hf_b3'].astype(jnp.float32)
    min_decay = math.log(DECAY_TARGET) / SLOW_DECAY_PCT
    max_decay = math.log(DECAY_TARGET) / FAST_DECAY_PCT
    deltas = jnp.linspace(min_decay, max_decay, HY_DIM, dtype=jnp.float32)
    decay = jnp.exp(-t * jnp.abs(deltas)[None, :])
    return h.reshape(L, HY_ORDER, 2, HY_DIM) * (decay + MOD_SHIFT)[:, None, None, :]


def bidir_long_conv(x, h_fwd, h_bwd, bias):
    L = x.shape[1]
    h_full = jnp.concatenate([h_fwd, jnp.zeros((1, HY_DIM), jnp.float32), h_bwd[:0:-1]], axis=0)
    xf = x.astype(jnp.float32)
    y = jnp.fft.irfft(jnp.fft.rfft(xf, n=2 * L, axis=1) * jnp.fft.rfft(h_full, axis=0)[None], n=2 * L, axis=1)[:, :L]
    return (y + xf * bias.astype(jnp.float32)).astype(x.dtype)


def hyena_mixer(z, p):
    L = z.shape[1]
    z = short_conv(z, p['hy_conv_w'], p['hy_conv_b'])
    x1, x2, v = jnp.split(z, 3, axis=-1)
    h = implicit_filters(L, p)
    u = x1 * bidir_long_conv(v, h[:, 0, 0], h[:, 0, 1], p['hy_bias'][0])
    return x2 * bidir_long_conv(u, h[:, 1, 0], h[:, 1, 1], p['hy_bias'][1])


def chunk_gmlp(z, p):
    B, L = z.shape[0], z.shape[1]
    z = jax.nn.gelu(z)
    u, v = jnp.split(z, 2, axis=-1)
    v = layernorm(v, p['gm_norm_g'], p['gm_norm_b'])
    n = L // CHUNK
    vc = v.reshape(B, n, CHUNK, GM_HEADS, GM_DIM // GM_HEADS)
    s = jnp.einsum('hpq,bnqhc->bnphc', p['gm_ws'], vc) + p['gm_bs'].T[None, None, :, :, None]
    return u * s.reshape(B, L, GM_DIM)


def parallel_mixer(h, p, pos, ctx_k, ctx_v):
    B, L = h.shape[0], h.shape[1]
    z = jnp.dot(h, p['w_in'])
    q, k, v, zh, zg = jnp.split(z, [ATTN_DIM, ATTN_DIM + KV_DIM, ATTN_DIM + 2 * KV_DIM,
                                    ATTN_DIM + 2 * KV_DIM + 3 * HY_DIM], axis=-1)
    q = rmsnorm(q.reshape(B, L, N_HEADS, HEAD_DIM), p['q_norm_g'])
    k = rmsnorm(k.reshape(B, L, N_KV_HEADS, HEAD_DIM), p['k_norm_g'])
    v = v.reshape(B, L, N_KV_HEADS, HEAD_DIM)
    if pos is None:
        attn = block_attention(q, k, v)
    else:
        row, col = pos
        k_all = jnp.concatenate([ctx_k.astype(k.dtype), rope_axial(k, row, col)], axis=1)
        v_all = jnp.concatenate([ctx_v.astype(v.dtype), v], axis=1)
        attn = block_attention(rope_axial(q, row, col), k_all, v_all)
    hy = hyena_mixer(zh, p)
    gm = chunk_gmlp(zg, p)
    g = p['out_norm_g']
    mix = jnp.concatenate([rmsnorm(attn, g[:ATTN_DIM]),
                           rmsnorm(hy, g[ATTN_DIM:ATTN_DIM + HY_DIM]),
                           rmsnorm(gm, g[ATTN_DIM + HY_DIM:])], axis=-1)
    return jnp.dot(mix, p['w_out']), k, v


def hier_moe(h, p):
    B, L, D = h.shape
    x = h.reshape(B * L, D)
    pg = jax.nn.softmax((jnp.dot(x, p['router_g_w']) + p['router_g_b']).astype(jnp.float32), axis=-1)
    g_top, g_idx = lax.top_k(pg, 1)
    e_all = (jnp.einsum('td,gde->tge', x, p['router_e_w']) + p['router_e_b']).astype(jnp.float32)
    e_logits = jnp.einsum('tge,tg->te', e_all, jax.nn.one_hot(g_idx[:, 0], N_GROUPS, dtype=jnp.float32))
    pe = jax.nn.softmax(e_logits, axis=-1)
    e_top, e_idx = lax.top_k(pe, TOP_K)
    w = g_top * e_top / jnp.sum(e_top, axis=-1, keepdims=True)
    expert_id = g_idx * EXPERTS_PER_GROUP + e_idx
    gates = jnp.sum(jax.nn.one_hot(expert_id, N_EXPERTS, dtype=jnp.float32) * w[..., None], axis=1)
    hg = jnp.einsum('td,edf->tef', x, p['exp_w_gate'])
    hu = jnp.einsum('td,edf->tef', x, p['exp_w_up'])
    a = jax.nn.silu(hg) * hu * gates[:, :, None].astype(x.dtype)
    y = jnp.einsum('tef,efd->td', a, p['exp_w_down'])
    return y.reshape(B, L, D)


def trunk_layer(x, cond, p, pos, ctx_k, ctx_v):
    mod = jnp.dot(jax.nn.silu(cond), p['w_mod']) + p['b_mod']
    sh1, sc1, g1, sh2, sc2, g2 = jnp.split(mod[:, None, :].astype(x.dtype), 6, axis=-1)
    h = rmsnorm(x, p['norm1_g']) * (1 + sc1) + sh1
    mix, k, v = parallel_mixer(h, p, pos, ctx_k, ctx_v)
    x = x + g1 * mix
    h = rmsnorm(x, p['norm2_g']) * (1 + sc2) + sh2
    x = x + g2 * hier_moe(h, p)
    return x, k, v


def setup_inputs(seed: int = 0) -> dict:
    key = jax.random.key(seed)
    ks = iter(jax.random.split(key, 48))

    def nrm(shape, s):
        return s * jax.random.normal(next(ks), shape, jnp.float32)

    def gain(shape):
        return 1.0 + nrm(shape, 0.05)

    return {
        'x_prompt': nrm((BATCH, SEQ, D_MODEL), 1.0),
        'x_sample': nrm((DEC_BATCH, DEC_SEQ, D_MODEL), 1.0),
        'cache_k': nrm((DEC_BATCH, DEPTH, PAST_LEN, N_KV_HEADS, HEAD_DIM), 1.0),
        'cache_v': nrm((DEC_BATCH, DEPTH, PAST_LEN, N_KV_HEADS, HEAD_DIM), 1.0),
        'c': nrm((DEC_BATCH, D_MODEL), 1.0),
        'c_ctx': nrm((D_MODEL,), 1.0),
        'norm1_g': gain((DEPTH, D_MODEL)),
        'norm2_g': gain((DEPTH, D_MODEL)),
        'w_mod': nrm((DEPTH, D_MODEL, 6 * D_MODEL), 0.5 * D_MODEL ** -0.5),
        'b_mod': nrm((DEPTH, 6 * D_MODEL), 0.01),
        'w_in': nrm((DEPTH, D_MODEL, PROJ_DIM), D_MODEL ** -0.5),
        'q_norm_g': gain((DEPTH, HEAD_DIM)),
        'k_norm_g': gain((DEPTH, HEAD_DIM)),
        'hy_conv_w': nrm((DEPTH, SHORT_CONV, 3 * HY_DIM), SHORT_CONV ** -0.5),
        'hy_conv_b': nrm((DEPTH, 3 * HY_DIM), 0.01),
        'hf_w1': nrm((DEPTH, FILTER_EMB, FILTER_HIDDEN), FILTER_EMB ** -0.5),
        'hf_b1': nrm((DEPTH, FILTER_HIDDEN), 0.01),
        'hf_w2': nrm((DEPTH, FILTER_HIDDEN, FILTER_HIDDEN), FILTER_HIDDEN ** -0.5),
        'hf_b2': nrm((DEPTH, FILTER_HIDDEN), 0.01),
        'hf_w3': nrm((DEPTH, FILTER_HIDDEN, HY_ORDER * 2 * HY_DIM), FILTER_HIDDEN ** -0.5),
        'hf_b3': nrm((DEPTH, HY_ORDER * 2 * HY_DIM), 0.01),
        'hf_freq': gain((DEPTH, FILTER_HIDDEN)),
        'hy_bias': nrm((DEPTH, HY_ORDER, HY_DIM), 0.1),
        'gm_norm_g': gain((DEPTH, GM_DIM)),
        'gm_norm_b': nrm((DEPTH, GM_DIM), 0.01),
        'gm_ws': nrm((DEPTH, GM_HEADS, CHUNK, CHUNK), CHUNK ** -0.5),
        'gm_bs': nrm((DEPTH, GM_HEADS, CHUNK), 0.01),
        'out_norm_g': gain((DEPTH, MIX_DIM)),
        'w_out': nrm((DEPTH, MIX_DIM, D_MODEL), MIX_DIM ** -0.5),
        'router_g_w': nrm((DEPTH, D_MODEL, N_GROUPS), D_MODEL ** -0.5),
        'router_g_b': nrm((DEPTH, N_GROUPS), 0.01),
        'router_e_w': nrm((DEPTH, N_GROUPS, D_MODEL, EXPERTS_PER_GROUP), D_MODEL ** -0.5),
        'router_e_b': nrm((DEPTH, N_GROUPS, EXPERTS_PER_GROUP), 0.01),
        'exp_w_gate': nrm((DEPTH, N_EXPERTS, D_MODEL, EXPERT_FF), D_MODEL ** -0.5),
        'exp_w_up': nrm((DEPTH, N_EXPERTS, D_MODEL, EXPERT_FF), D_MODEL ** -0.5),
        'exp_w_down': nrm((DEPTH, N_EXPERTS, EXPERT_FF, D_MODEL), EXPERT_FF ** -0.5),
        'final_norm_g': gain((D_MODEL,)),
    }


def reference(x_prompt, x_sample, cache_k, cache_v, c, c_ctx, norm1_g, norm2_g, w_mod, b_mod,
              w_in, q_norm_g, k_norm_g, hy_conv_w, hy_conv_b, hf_w1, hf_b1, hf_w2, hf_b2,
              hf_w3, hf_b3, hf_freq, hy_bias, gm_norm_g, gm_norm_b, gm_ws, gm_bs, out_norm_g,
              w_out, router_g_w, router_g_b, router_e_w, router_e_b, exp_w_gate, exp_w_up,
              exp_w_down, final_norm_g):
    params = {
        'norm1_g': norm1_g, 'norm2_g': norm2_g, 'w_mod': w_mod, 'b_mod': b_mod, 'w_in': w_in,
        'q_norm_g': q_norm_g, 'k_norm_g': k_norm_g, 'hy_conv_w': hy_conv_w, 'hy_conv_b': hy_conv_b,
        'hf_w1': hf_w1, 'hf_b1': hf_b1, 'hf_w2': hf_w2, 'hf_b2': hf_b2, 'hf_w3': hf_w3,
        'hf_b3': hf_b3, 'hf_freq': hf_freq, 'hy_bias': hy_bias, 'gm_norm_g': gm_norm_g,
        'gm_norm_b': gm_norm_b, 'gm_ws': gm_ws, 'gm_bs': gm_bs, 'out_norm_g': out_norm_g,
        'w_out': w_out, 'router_g_w': router_g_w, 'router_g_b': router_g_b,
        'router_e_w': router_e_w, 'router_e_b': router_e_b, 'exp_w_gate': exp_w_gate,
        'exp_w_up': exp_w_up, 'exp_w_down': exp_w_down,
    }
    pos = axial_positions(x_sample.shape[1])
    cond_ctx = c_ctx[None, :]
    y_p = x_prompt
    y_s = x_sample
    new_k = []
    new_v = []
    for l in range(DEPTH):
        p = {name: w[l] for name, w in params.items()}
        y_p, k_l, v_l = trunk_layer(y_p, cond_ctx, p, None, None, None)
        new_k.append(k_l)
        new_v.append(v_l)
        y_s, _, _ = trunk_layer(y_s, c, p, pos, cache_k[:, l], cache_v[:, l])
    y_prompt = rmsnorm(y_p, final_norm_g)
    y_sample = rmsnorm(y_s, final_norm_g)
    new_cache_k = jnp.stack(new_k, axis=1)
    new_cache_v = jnp.stack(new_v, axis=1)
    return (y_prompt, y_sample, new_cache_k, new_cache_v)
```

```python
import functools
import math

import numpy as np
import jax
import jax.numpy as jnp
from jax import lax
from jax.experimental import pallas as pl
from jax.experimental.pallas import tpu as pltpu

F32 = jnp.float32
BF16 = jnp.bfloat16

D_MODEL = 4096
GRID_W = 64
HEAD_DIM = 128
N_HEADS = 16
N_KV_HEADS = 4
Q_PER_KV = N_HEADS // N_KV_HEADS
ATTN_DIM = N_HEADS * HEAD_DIM
KV_DIM = N_KV_HEADS * HEAD_DIM
HY_DIM = 1024
GM_DIM = 1024
GM_HEADS = 8
CHUNK = 128
PROJ_DIM = ATTN_DIM + 2 * KV_DIM + 3 * HY_DIM + 2 * GM_DIM
ROPE_THETA = 10000.0
FILTER_EMB = 33
FILTER_HIDDEN = 64
DECAY_TARGET = 1e-2
FAST_DECAY_PCT = 0.3
SLOW_DECAY_PCT = 1.5
MOD_SHIFT = 0.05
N_GROUPS = 4
EXPERTS_PER_GROUP = 4
N_EXPERTS = 16
EXPERT_FF = 1024
EPS = 1e-6

COL_K = ATTN_DIM
COL_V = ATTN_DIM + KV_DIM
COL_HY = ATTN_DIM + 2 * KV_DIM
COL_GM = COL_HY + 3 * HY_DIM

LANES = 128
VMEM_LIMIT = 56 * 1024 * 1024
HI = lax.Precision.HIGHEST


def _cparams(*sem):
    return pltpu.CompilerParams(dimension_semantics=sem, vmem_limit_bytes=VMEM_LIMIT)


def _rms(x):
    return x * lax.rsqrt(jnp.mean(x * x, axis=-1, keepdims=True) + EPS)


def _sigmoid(x):
    return 1.0 / (1.0 + jnp.exp(-x))


def _seg_of_row(row, t_prompt, dec_seq):
    return jnp.where(row < t_prompt, 0, 1 + (row - t_prompt) // dec_seq)


def _mod_kernel(c_ref, w_ref, b_ref, o_ref):
    c = c_ref[...]
    s = (c * _sigmoid(c)).astype(BF16)
    o_ref[...] = jnp.dot(s, w_ref[...].astype(BF16), preferred_element_type=F32) + b_ref[...]


def _mod_call(cond, w_mod, b_mod):
    depth, d, n = w_mod.shape
    r = cond.shape[0]
    tn = 512
    return pl.pallas_call(
        _mod_kernel,
        grid=(depth, n // tn),
        in_specs=[pl.BlockSpec((r, d), lambda l, j: (0, 0)),
                  pl.BlockSpec((None, d, tn), lambda l, j: (l, 0, j)),
                  pl.BlockSpec((None, 1, tn), lambda l, j: (l, 0, j))],
        out_specs=pl.BlockSpec((None, r, tn), lambda l, j: (l, 0, j)),
        out_shape=jax.ShapeDtypeStruct((depth, r, n), F32),
        compiler_params=_cparams("parallel", "parallel"),
        name="adaln_mod",
    )(cond, w_mod, b_mod.reshape(depth, 1, n))


def _norm_proj_kernel(x_ref, g_ref, sh_ref, sc_ref, w_ref, o_ref, h_ref):
    @pl.when(pl.program_id(1) == 0)
    def _():
        y = _rms(x_ref[...]) * g_ref[...]
        h_ref[...] = (y * (1.0 + sc_ref[...]) + sh_ref[...]).astype(BF16)

    o_ref[...] = jnp.dot(h_ref[...], w_ref[...], preferred_element_type=F32)


def _norm_proj_call(x, g, mod, w, t_prompt, dec_seq, tm=512, tn=512):
    t, d = x.shape
    n = w.shape[1]
    seg = lambda i: _seg_of_row(i * tm, t_prompt, dec_seq)
    return pl.pallas_call(
        _norm_proj_kernel,
        grid=(t // tm, n // tn),
        in_specs=[pl.BlockSpec((tm, d), lambda i, j: (i, 0)),
                  pl.BlockSpec((1, d), lambda i, j: (0, 0)),
                  pl.BlockSpec((None, 1, d), lambda i, j: (seg(i), 0, 0)),
                  pl.BlockSpec((None, 1, d), lambda i, j: (seg(i), 0, 1)),
                  pl.BlockSpec((d, tn), lambda i, j: (0, j))],
        out_specs=pl.BlockSpec((tm, tn), lambda i, j: (i, j)),
        out_shape=jax.ShapeDtypeStruct((t, n), F32),
        scratch_shapes=[pltpu.VMEM((tm, d), BF16)],
        compiler_params=_cparams("parallel", "arbitrary"),
        name="norm1_in_proj",
    )(x, g.reshape(1, d), mod, mod, w)


def _attn_prompt_kernel(q_ref, k_ref, v_ref, qg_ref, kg_ref, o_ref, kc_ref, vc_ref):
    kn = _rms(k_ref[...]) * kg_ref[...]
    v = v_ref[...]
    kc_ref[...] = kn
    vc_ref[...] = v
    kb = kn.astype(BF16)
    vb = v.astype(BF16)
    scale = HEAD_DIM ** -0.5
    for m in range(Q_PER_KV):
        sl = slice(m * HEAD_DIM, (m + 1) * HEAD_DIM)
        qn = (_rms(q_ref[:, sl]) * qg_ref[...] * scale).astype(BF16)
        s = lax.dot_general(qn, kb, (((1,), (1,)), ((), ())), preferred_element_type=F32)
        p = jnp.exp(s - jnp.max(s, axis=-1, keepdims=True))
        l = jnp.sum(p, axis=-1, keepdims=True)
        o = jnp.dot(p.astype(BF16), vb, preferred_element_type=F32)
        o_ref[:, sl] = o / l


def _attn_prompt_call(z, qg, kg, batch, seq):
    gw = Q_PER_KV * HEAD_DIM
    tp = batch * seq
    return pl.pallas_call(
        _attn_prompt_kernel,
        grid=(batch, N_KV_HEADS),
        in_specs=[pl.BlockSpec((seq, gw), lambda b, g: (b, g)),
                  pl.BlockSpec((seq, HEAD_DIM), lambda b, g: (b, COL_K // HEAD_DIM + g)),
                  pl.BlockSpec((seq, HEAD_DIM), lambda b, g: (b, COL_V // HEAD_DIM + g)),
                  pl.BlockSpec((1, HEAD_DIM), lambda b, g: (0, 0)),
                  pl.BlockSpec((1, HEAD_DIM), lambda b, g: (0, 0))],
        out_specs=[pl.BlockSpec((seq, gw), lambda b, g: (b, g)),
                   pl.BlockSpec((seq, HEAD_DIM), lambda b, g: (b, g)),
                   pl.BlockSpec((seq, HEAD_DIM), lambda b, g: (b, g))],
        out_shape=[jax.ShapeDtypeStruct((tp, ATTN_DIM), F32),
                   jax.ShapeDtypeStruct((tp, KV_DIM), F32),
                   jax.ShapeDtypeStruct((tp, KV_DIM), F32)],
        compiler_params=_cparams("parallel", "parallel"),
        name="attn_context",
    )(z, z, z, qg.reshape(1, HEAD_DIM), kg.reshape(1, HEAD_DIM))


def _rope(x, cos, sin_signed):
    lane = lax.broadcasted_iota(jnp.int32, x.shape, 1)
    quarter = HEAD_DIM // 4
    fwd = pltpu.roll(x, HEAD_DIM - quarter, 1)
    bwd = pltpu.roll(x, quarter, 1)
    swapped = jnp.where((lane % (2 * quarter)) < quarter, fwd, bwd)
    return x * cos + swapped * sin_signed


def _attn_sample_kernel(q_ref, k_ref, v_ref, ck_ref, cv_ref, cq_ref, sq_ref, ckk_ref, skk_ref,
                        qg_ref, kg_ref, o_ref, qs_ref, m_ref, l_ref, acc_ref):
    j = pl.program_id(3)
    scale = HEAD_DIM ** -0.5

    @pl.when(j == 0)
    def _():
        for m in range(Q_PER_KV):
            sl = slice(m * HEAD_DIM, (m + 1) * HEAD_DIM)
            qn = _rms(q_ref[:, sl]) * qg_ref[...]
            qs_ref[m] = (_rope(qn, cq_ref[...], sq_ref[...]) * scale).astype(BF16)
        m_ref[...] = jnp.full(m_ref.shape, -jnp.inf, F32)
        l_ref[...] = jnp.zeros(l_ref.shape, F32)
        acc_ref[...] = jnp.zeros(acc_ref.shape, F32)

    def step(kb, vb):
        for m in range(Q_PER_KV):
            s = lax.dot_general(qs_ref[m], kb, (((1,), (1,)), ((), ())), preferred_element_type=F32)
            m_prev = m_ref[m]
            m_new = jnp.maximum(m_prev, jnp.max(s, axis=-1, keepdims=True))
            alpha = jnp.exp(m_prev - m_new)
            p = jnp.exp(s - m_new)
            l_ref[m] = alpha * l_ref[m] + jnp.sum(p, axis=-1, keepdims=True)
            acc_ref[m] = alpha * acc_ref[m] + jnp.dot(p.astype(BF16), vb, preferred_element_type=F32)
            m_ref[m] = m_new

    @pl.when(j == 0)
    def _():
        step(ck_ref[...].astype(BF16), cv_ref[...].astype(BF16))

    @pl.when(j > 0)
    def _():
        kn = _rms(k_ref[...]) * kg_ref[...]
        kr = _rope(kn, ckk_ref[...], skk_ref[...])
        step(kr.astype(BF16), v_ref[...].astype(BF16))

    @pl.when(j == pl.num_programs(3) - 1)
    def _():
        for m in range(Q_PER_KV):
            o_ref[:, m * HEAD_DIM:(m + 1) * HEAD_DIM] = acc_ref[m] / l_ref[m]


def _attn_sample_call(z, ctx_k, ctx_v, cos, sin_signed, qg, kg, t_prompt, dec_batch, dec_seq, tq=512, tk=512):
    gw = Q_PER_KV * HEAD_DIM
    past = ctx_k.shape[1]
    nq = dec_seq // tq
    nk = dec_seq // tk
    qrow = lambda b, i: (t_prompt + b * dec_seq) // tq + i
    krow = lambda b, j: (t_prompt + b * dec_seq) // tk + jnp.maximum(j - 1, 0)
    return pl.pallas_call(
        _attn_sample_kernel,
        grid=(dec_batch, N_KV_HEADS, nq, nk + 1),
        in_specs=[pl.BlockSpec((tq, gw), lambda b, g, i, j: (qrow(b, i), g)),
                  pl.BlockSpec((tk, HEAD_DIM), lambda b, g, i, j: (krow(b, j), COL_K // HEAD_DIM + g)),
                  pl.BlockSpec((tk, HEAD_DIM), lambda b, g, i, j: (krow(b, j), COL_V // HEAD_DIM + g)),
                  pl.BlockSpec((None, past, HEAD_DIM), lambda b, g, i, j: (b, 0, g)),
                  pl.BlockSpec((None, past, HEAD_DIM), lambda b, g, i, j: (b, 0, g)),
                  pl.BlockSpec((tq, HEAD_DIM), lambda b, g, i, j: (i, 0)),
                  pl.BlockSpec((tq, HEAD_DIM), lambda b, g, i, j: (i, 0)),
                  pl.BlockSpec((tk, HEAD_DIM), lambda b, g, i, j: (jnp.maximum(j - 1, 0), 0)),
                  pl.BlockSpec((tk, HEAD_DIM), lambda b, g, i, j: (jnp.maximum(j - 1, 0), 0)),
                  pl.BlockSpec((1, HEAD_DIM), lambda b, g, i, j: (0, 0)),
                  pl.BlockSpec((1, HEAD_DIM), lambda b, g, i, j: (0, 0))],
        out_specs=pl.BlockSpec((tq, gw), lambda b, g, i, j: (b * nq + i, g)),
        out_shape=jax.ShapeDtypeStruct((dec_batch * dec_seq, ATTN_DIM), F32),
        scratch_shapes=[pltpu.VMEM((Q_PER_KV, tq, HEAD_DIM), BF16),
                        pltpu.VMEM((Q_PER_KV, tq, 1), F32),
                        pltpu.VMEM((Q_PER_KV, tq, 1), F32),
                        pltpu.VMEM((Q_PER_KV, tq, HEAD_DIM), F32)],
        compiler_params=_cparams("parallel", "parallel", "parallel", "arbitrary"),
        name="attn_latent",
    )(z, z, z, ctx_k, ctx_v, cos, sin_signed, cos, sin_signed,
      qg.reshape(1, HEAD_DIM), kg.reshape(1, HEAD_DIM))


def _rope_tables(n_tokens):
    quarter = HEAD_DIM // 4
    n_rows = n_tokens // GRID_W
    row = jnp.repeat(jnp.arange(n_rows, dtype=F32), GRID_W)
    col = jnp.tile(jnp.arange(GRID_W, dtype=F32), n_rows)
    freqs = ROPE_THETA ** (-jnp.arange(quarter, dtype=F32) / quarter)
    ar = row[:, None] * freqs[None, :]
    ac = col[:, None] * freqs[None, :]
    cos = jnp.concatenate([jnp.cos(ar), jnp.cos(ar), jnp.cos(ac), jnp.cos(ac)], axis=-1)
    sin = jnp.concatenate([-jnp.sin(ar), jnp.sin(ar), -jnp.sin(ac), jnp.sin(ac)], axis=-1)
    return cos, sin


def _gelu_tanh(x):
    return 0.5 * x * (1.0 + jnp.tanh(math.sqrt(2.0 / math.pi) * (x + 0.044715 * (x * x * x))))


def _gmlp_kernel(z_ref, g_ref, b_ref, ws_ref, bs_ref, o_ref):
    hw = GM_DIM // GM_HEADS
    for ch in range(z_ref.shape[0] // CHUNK):
        rows = slice(ch * CHUNK, (ch + 1) * CHUNK)
        u = _gelu_tanh(z_ref[rows, :GM_DIM])
        v = _gelu_tanh(z_ref[rows, GM_DIM:])
        mu = jnp.mean(v, axis=-1, keepdims=True)
        vc = v - mu
        var = jnp.mean(vc * vc, axis=-1, keepdims=True)
        vn = (vc * lax.rsqrt(var + EPS) * g_ref[...] + b_ref[...]).astype(BF16)
        for h in range(GM_HEADS):
            cols = slice(h * hw, (h + 1) * hw)
            s = jnp.dot(ws_ref[h], vn[:, cols], preferred_element_type=F32) + bs_ref[h]
            o_ref[rows, cols] = u[:, cols] * s


def _gmlp_call(z, g, b, ws, bs, tr=512):
    t = z.shape[0]
    return pl.pallas_call(
        _gmlp_kernel,
        grid=(t // tr,),
        in_specs=[pl.BlockSpec((tr, 2 * GM_DIM), lambda i: (i, COL_GM // (2 * GM_DIM))),
                  pl.BlockSpec((1, GM_DIM), lambda i: (0, 0)),
                  pl.BlockSpec((1, GM_DIM), lambda i: (0, 0)),
                  pl.BlockSpec((GM_HEADS, CHUNK, CHUNK), lambda i: (0, 0, 0)),
                  pl.BlockSpec((GM_HEADS, CHUNK, 1), lambda i: (0, 0, 0))],
        out_specs=pl.BlockSpec((tr, GM_DIM), lambda i: (i, 0)),
        out_shape=jax.ShapeDtypeStruct((t, GM_DIM), F32),
        compiler_params=_cparams("parallel"),
        name="chunk_gmlp",
    )(z, g.reshape(1, GM_DIM), b.reshape(1, GM_DIM), ws.astype(BF16), bs.reshape(GM_HEADS, CHUNK, 1))


def _sconv_kernel(z_ref, w_ref, b_ref, o_ref, *, seg_len):
    x = z_ref[...]
    n = x.shape[0]
    pos = lax.broadcasted_iota(jnp.int32, x.shape, 0) % seg_len
    prev = jnp.where(pos == 0, 0.0, pltpu.roll(x, 1, 0))
    nxt = jnp.where(pos == seg_len - 1, 0.0, pltpu.roll(x, n - 1, 0))
    o_ref[...] = b_ref[...] + prev * w_ref[0:1, :] + x * w_ref[1:2, :] + nxt * w_ref[2:3, :]


def _sconv_call(z, w, b, row0, n_rows, seg_len, block_rows, cw=256):
    width = 3 * HY_DIM
    return pl.pallas_call(
        functools.partial(_sconv_kernel, seg_len=seg_len),
        grid=(n_rows // block_rows, width // cw),
        in_specs=[pl.BlockSpec((block_rows, cw), lambda i, c: (row0 // block_rows + i, COL_HY // cw + c)),
                  pl.BlockSpec((3, cw), lambda i, c: (0, c)),
                  pl.BlockSpec((1, cw), lambda i, c: (0, c))],
        out_specs=pl.BlockSpec((block_rows, cw), lambda i, c: (i, c)),
        out_shape=jax.ShapeDtypeStruct((n_rows, width), F32),
        compiler_params=_cparams("parallel", "parallel"),
        name="hyena_short_conv",
    )(z, w, b.reshape(1, width))


def _filter_kernel(z_ref, w1_ref, b1_ref, w2_ref, b2_ref, w3_ref, b3_ref, fr_ref, dec_ref, o_ref, *, tr):
    fr = fr_ref[...]
    h = jnp.sin(fr * (jnp.dot(z_ref[...], w1_ref[...], precision=HI, preferred_element_type=F32) + b1_ref[...]))
    h = jnp.sin(fr * (jnp.dot(h, w2_ref[...], precision=HI, preferred_element_type=F32) + b2_ref[...]))
    h = jnp.dot(h, w3_ref[...], precision=HI, preferred_element_type=F32) + b3_ref[...]
    row = lax.broadcasted_iota(jnp.int32, h.shape, 0) + pl.program_id(0) * tr
    o_ref[...] = jnp.where(row == 0, 0.0, h * dec_ref[...])


def _filter_call(length, w1, b1, w2, b2, w3, b3, fr, tr=256):
    bands = (FILTER_EMB - 1) // 2
    t = jnp.linspace(0.0, 1.0, length, dtype=F32)[:, None]
    wv = 2.0 * math.pi * jnp.arange(length, dtype=F32)[:, None] / length
    f = jnp.linspace(1e-4, bands - 1, bands, dtype=F32)[None, :]
    zf = jnp.concatenate([t, jnp.cos(f * wv), -jnp.sin(f * wv)], axis=-1)
    min_decay = math.log(DECAY_TARGET) / SLOW_DECAY_PCT
    max_decay = math.log(DECAY_TARGET) / FAST_DECAY_PCT
    deltas = jnp.linspace(min_decay, max_decay, HY_DIM, dtype=F32)
    decay = jnp.exp(-t * jnp.abs(deltas)[None, :]) + MOD_SHIFT
    lag = jnp.minimum(jnp.abs(jnp.arange(2 * length) - length), length - 1)
    pad = LANES - FILTER_EMB
    hp = LANES - FILTER_HIDDEN
    z2 = jnp.pad(zf[lag], ((0, 0), (0, pad)))
    dec2 = decay[lag]
    w1p = jnp.pad(w1, ((0, pad), (0, hp)))
    w2p = jnp.pad(w2, ((0, hp), (0, hp)))
    w3p = jnp.pad(w3, ((0, hp), (0, 0)))
    b1p = jnp.pad(b1, (0, hp)).reshape(1, LANES)
    b2p = jnp.pad(b2, (0, hp)).reshape(1, LANES)
    frp = jnp.pad(fr, (0, hp)).reshape(1, LANES)
    nblk = 2 * length // tr
    half = length // tr
    wcol = lambda r, o: o * 2 + jnp.where(r < half, 1, 0)
    return pl.pallas_call(
        functools.partial(_filter_kernel, tr=tr),
        grid=(nblk, 2),
        in_specs=[pl.BlockSpec((tr, LANES), lambda r, o: (r, 0)),
                  pl.BlockSpec((LANES, LANES), lambda r, o: (0, 0)),
                  pl.BlockSpec((1, LANES), lambda r, o: (0, 0)),
                  pl.BlockSpec((LANES, LANES), lambda r, o: (0, 0)),
                  pl.BlockSpec((1, LANES), lambda r, o: (0, 0)),
                  pl.BlockSpec((LANES, HY_DIM), lambda r, o: (0, wcol(r, o))),
                  pl.BlockSpec((1, HY_DIM), lambda r, o: (0, wcol(r, o))),
                  pl.BlockSpec((1, LANES), lambda r, o: (0, 0)),
                  pl.BlockSpec((tr, HY_DIM), lambda r, o: (r, 0))],
        out_specs=pl.BlockSpec((tr, HY_DIM), lambda r, o: (r, o)),
        out_shape=jax.ShapeDtypeStruct((2 * length, 2 * HY_DIM), F32),
        compiler_params=_cparams("parallel", "parallel"),
        name="hyena_filter_mlp",
    )(z2, w1p, b1p, w2p, b2p, w3p, b3.reshape(1, -1), frp, dec2)


def _dft_constants(p):
    n = 2 * p
    idx = np.arange(p, dtype=np.float64)
    ang = 2.0 * np.pi * np.outer(idx, idx) / n
    re = np.cos(ang)
    im = -np.sin(ang)
    im[0, :] = np.cos(np.pi * idx)
    fwd = np.concatenate([re, im], axis=0)
    sign = np.where(np.arange(p) % 2 == 0, 1.0, -1.0)
    sign2 = np.concatenate([sign, sign])
    sign2[p] = 1.0
    fwd_shift = fwd * sign2[:, None]
    ar = (2.0 / n) * np.cos(ang)
    ar[:, 0] = 1.0 / n
    ai = -(2.0 / n) * np.sin(ang)
    ai[:, 0] = sign / n
    inv = np.concatenate([ar, ai], axis=1)
    as_bf16 = lambda a: jnp.asarray(a.astype(np.float32)).astype(BF16)
    return as_bf16(fwd), as_bf16(fwd_shift), as_bf16(inv)


def _dft_kernel(f_ref, x_ref, o_ref):
    o_ref[...] = jnp.dot(f_ref[...], x_ref[...].astype(BF16), preferred_element_type=F32).astype(o_ref.dtype)


def _dft_call(x, fwd, p, col0, cw=256):
    nblk = x.shape[0] // p
    return pl.pallas_call(
        _dft_kernel,
        grid=(nblk, HY_DIM // cw),
        in_specs=[pl.BlockSpec((2 * p, p), lambda i, c: (0, 0)),
                  pl.BlockSpec((p, cw), lambda i, c: (i, col0 // cw + c))],
        out_specs=pl.BlockSpec((None, 2 * p, cw), lambda i, c: (i, 0, c)),
        out_shape=jax.ShapeDtypeStruct((nblk, 2 * p, HY_DIM), BF16),
        compiler_params=_cparams("parallel", "parallel"),
        name="hyena_block_dft",
    )(fwd, x)


def _filter_dft_kernel(f_ref, fs_ref, hi_ref, lo_ref, o_ref):
    lo = lo_ref[...]
    row = lax.broadcasted_iota(jnp.int32, lo.shape, 0)
    lo = jnp.where(row == 0, 0.0, lo)
    acc = jnp.dot(f_ref[...], hi_ref[...].astype(BF16), preferred_element_type=F32)
    acc = acc + jnp.dot(fs_ref[...], lo.astype(BF16), preferred_element_type=F32)
    o_ref[...] = acc.astype(o_ref.dtype)


def _filter_dft_call(taps, fwd, fwd_shift, p, cw=256):
    nb2 = taps.shape[0] // p
    nd = nb2 - 1
    ncb = HY_DIM // cw
    return pl.pallas_call(
        _filter_dft_kernel,
        grid=(2, nd, ncb),
        in_specs=[pl.BlockSpec((2 * p, p), lambda o, d, c: (0, 0)),
                  pl.BlockSpec((2 * p, p), lambda o, d, c: (0, 0)),
                  pl.BlockSpec((p, cw), lambda o, d, c: (d + 1, o * ncb + c)),
                  pl.BlockSpec((p, cw), lambda o, d, c: (d, o * ncb + c))],
        out_specs=pl.BlockSpec((None, None, 2 * p, cw), lambda o, d, c: (o, d, 0, c)),
        out_shape=jax.ShapeDtypeStruct((2, nd, 2 * p, HY_DIM), BF16),
        compiler_params=_cparams("parallel", "parallel", "parallel"),
        name="hyena_filter_dft",
    )(fwd, fwd_shift, taps, taps)


def _specconv_kernel(xs_ref, gs_ref, inv_ref, v_ref, gate_ref, bias_ref, o_ref, y_ref, *, nb, p, rc):
    i = pl.program_id(2)
    cw = o_ref.shape[1]

    for r in range(0, p, rc):
        def body(j, carry):
            yre, yim = carry
            d = i - j + nb - 1
            xre = xs_ref[j, r:r + rc, :].astype(F32)
            xim = xs_ref[j, p + r:p + r + rc, :].astype(F32)
            gre = gs_ref[d, r:r + rc, :].astype(F32)
            gim = gs_ref[d, p + r:p + r + rc, :].astype(F32)
            return yre + (xre * gre - xim * gim), yim + (xre * gim + xim * gre)

        zero = jnp.zeros((rc, cw), F32)
        yre, yim = lax.fori_loop(0, nb, body, (zero, zero))
        y_ref[r:r + rc, :] = yre
        y_ref[p + r:p + r + rc, :] = yim

    def edge(j, carry):
        y0, yn = carry
        d = i - j + nb - 1
        x0 = xs_ref[j, 0:16, :].astype(F32)
        xn = xs_ref[j, p:p + 16, :].astype(F32)
        g0 = gs_ref[d, 0:16, :].astype(F32)
        gn = gs_ref[d, p:p + 16, :].astype(F32)
        return y0 + x0 * g0, yn + xn * gn

    zero16 = jnp.zeros((16, cw), F32)
    y0, yn = lax.fori_loop(0, nb, edge, (zero16, zero16))
    y_ref[0:1, :] = y0[0:1, :]
    y_ref[p:p + 1, :] = yn[0:1, :]

    y = jnp.dot(inv_ref[...], y_ref[...].astype(BF16), preferred_element_type=F32)
    v = v_ref[...]
    o_ref[...] = gate_ref[...] * (y + bias_ref[...] * v)


def _specconv_call(xs, gs, inv, zc, bias, nseq, nb, p, v_col0, gate_col0, v_src=None, cw=256, rc=32):
    n_rows = nseq * nb * p
    xs4 = xs.reshape(nseq, nb, 2 * p, HY_DIM)
    if v_src is None:
        v_arr, v_spec = zc, pl.BlockSpec((p, cw), lambda c, s, i: (s * nb + i, v_col0 // cw + c))
    else:
        v_arr, v_spec = v_src, pl.BlockSpec((p, cw), lambda c, s, i: (s * nb + i, c))
    return pl.pallas_call(
        functools.partial(_specconv_kernel, nb=nb, p=p, rc=rc),
        grid=(HY_DIM // cw, nseq, nb),
        in_specs=[pl.BlockSpec((None, nb, 2 * p, cw), lambda c, s, i: (s, 0, 0, c)),
                  pl.BlockSpec((2 * nb - 1, 2 * p, cw), lambda c, s, i: (0, 0, c)),
                  pl.BlockSpec((p, 2 * p), lambda c, s, i: (0, 0)),
                  v_spec,
                  pl.BlockSpec((p, cw), lambda c, s, i: (s * nb + i, gate_col0 // cw + c)),
                  pl.BlockSpec((1, cw), lambda c, s, i: (0, c))],
        out_specs=pl.BlockSpec((p, cw), lambda c, s, i: (s * nb + i, c)),
        out_shape=jax.ShapeDtypeStruct((n_rows, HY_DIM), F32),
        scratch_shapes=[pltpu.VMEM((2 * p, cw), F32)],
        compiler_params=_cparams("parallel", "parallel", "parallel"),
        name="hyena_spectral_conv",
    )(xs4, gs, inv, v_arr, zc, bias.reshape(1, HY_DIM))


def _hyena_group(z, row0, nseq, length, p, prm, sconv_rows):
    nb = length // p
    zc = _sconv_call(z, prm['hy_conv_w'], prm['hy_conv_b'], row0, nseq * length, length, sconv_rows)
    taps = _filter_call(length, prm['hf_w1'], prm['hf_b1'], prm['hf_w2'], prm['hf_b2'],
                        prm['hf_w3'], prm['hf_b3'], prm['hf_freq'])
    fwd, fwd_shift, inv = _dft_constants(p)
    gs = _filter_dft_call(taps, fwd, fwd_shift, p)
    vs = _dft_call(zc, fwd, p, 2 * HY_DIM)
    u = _specconv_call(vs, gs[0], inv, zc, prm['hy_bias'][0], nseq, nb, p, 2 * HY_DIM, 0)
    us = _dft_call(u, fwd, p, 0)
    return _specconv_call(us, gs[1], inv, zc, prm['hy_bias'][1], nseq, nb, p, 0, HY_DIM, v_src=u)


def _out_proj_kernel(a_ref, h_ref, m_ref, g_ref, w_ref, x_ref, gate_ref, o_ref, mix_ref):
    @pl.when(pl.program_id(1) == 0)
    def _():
        mix_ref[:, :ATTN_DIM] = (_rms(a_ref[...]) * g_ref[:, :ATTN_DIM]).astype(BF16)
        mix_ref[:, ATTN_DIM:ATTN_DIM + HY_DIM] = (
            _rms(h_ref[...]) * g_ref[:, ATTN_DIM:ATTN_DIM + HY_DIM]).astype(BF16)
        mix_ref[:, ATTN_DIM + HY_DIM:] = (_rms(m_ref[...]) * g_ref[:, ATTN_DIM + HY_DIM:]).astype(BF16)

    y = jnp.dot(mix_ref[...], w_ref[...], preferred_element_type=F32)
    o_ref[...] = x_ref[...] + gate_ref[...] * y


def _out_proj_call(attn, hy, gm, g, w, x, mod, t_prompt, dec_seq, tm=512, tn=512):
    t, d = x.shape
    mix_dim = ATTN_DIM + HY_DIM + GM_DIM
    seg = lambda i: _seg_of_row(i * tm, t_prompt, dec_seq)
    return pl.pallas_call(
        _out_proj_kernel,
        grid=(t // tm, d // tn),
        in_specs=[pl.BlockSpec((tm, ATTN_DIM), lambda i, j: (i, 0)),
                  pl.BlockSpec((tm, HY_DIM), lambda i, j: (i, 0)),
                  pl.BlockSpec((tm, GM_DIM), lambda i, j: (i, 0)),
                  pl.BlockSpec((1, mix_dim), lambda i, j: (0, 0)),
                  pl.BlockSpec((mix_dim, tn), lambda i, j: (0, j)),
                  pl.BlockSpec((tm, tn), lambda i, j: (i, j)),
                  pl.BlockSpec((None, 1, tn), lambda i, j: (seg(i), 0, 2 * (d // tn) + j))],
        out_specs=pl.BlockSpec((tm, tn), lambda i, j: (i, j)),
        out_shape=jax.ShapeDtypeStruct((t, d), F32),
        scratch_shapes=[pltpu.VMEM((tm, mix_dim), BF16)],
        compiler_params=_cparams("parallel", "arbitrary"),
        name="mix_out_proj",
    )(attn, hy, gm, g.reshape(1, mix_dim), w, x, mod)


def _router_kernel(x_ref, g_ref, sh_ref, sc_ref, wr_ref, br_ref, h_ref, r_ref):
    y = _rms(x_ref[...]) * g_ref[...]
    h = y * (1.0 + sc_ref[...]) + sh_ref[...]
    h_ref[...] = h
    logits = jnp.dot(h, wr_ref[...], precision=HI, preferred_element_type=F32) + br_ref[...]
    col = lambda k: logits[:, k:k + 1]

    lg = [col(k) for k in range(N_GROUPS)]
    g_max = functools.reduce(jnp.maximum, lg)
    g_den = sum(jnp.exp(v - g_max) for v in lg)
    g_top = 1.0 / g_den
    g_idx = jnp.full(g_max.shape, N_GROUPS - 1, jnp.int32)
    for k in range(N_GROUPS - 2, -1, -1):
        g_idx = jnp.where(lg[k] == g_max, k, g_idx)

    le = []
    for e in range(EXPERTS_PER_GROUP):
        v = col(N_GROUPS + (N_GROUPS - 1) * EXPERTS_PER_GROUP + e)
        for k in range(N_GROUPS - 2, -1, -1):
            v = jnp.where(g_idx == k, col(N_GROUPS + k * EXPERTS_PER_GROUP + e), v)
        le.append(v)

    e_max = functools.reduce(jnp.maximum, le)
    e1 = jnp.full(e_max.shape, EXPERTS_PER_GROUP - 1, jnp.int32)
    for e in range(EXPERTS_PER_GROUP - 2, -1, -1):
        e1 = jnp.where(le[e] == e_max, e, e1)
    neg = jnp.float32(-jnp.inf)
    rest = [jnp.where(e1 == e, neg, le[e]) for e in range(EXPERTS_PER_GROUP)]
    e2_max = functools.reduce(jnp.maximum, rest)
    e2 = jnp.full(e_max.shape, EXPERTS_PER_GROUP - 1, jnp.int32)
    for e in range(EXPERTS_PER_GROUP - 2, -1, -1):
        e2 = jnp.where(rest[e] == e2_max, e, e2)
    ratio = jnp.exp(e2_max - e_max)
    w1 = g_top / (1.0 + ratio)
    w2 = g_top * ratio / (1.0 + ratio)
    id1 = (g_idx * EXPERTS_PER_GROUP + e1).astype(F32)
    id2 = (g_idx * EXPERTS_PER_GROUP + e2).astype(F32)

    lane = lax.broadcasted_iota(jnp.int32, r_ref.shape, 1)
    r_ref[...] = jnp.where(lane == 0, id1, jnp.where(lane == 1, id2, jnp.where(lane == 2, w1, jnp.where(lane == 3, w2, 0.0))))


def _router_call(x, g, mod, wr, br, t_prompt, dec_seq, tm=256):
    t, d = x.shape
    seg = lambda i: _seg_of_row(i * tm, t_prompt, dec_seq)
    return pl.pallas_call(
        _router_kernel,
        grid=(t // tm,),
        in_specs=[pl.BlockSpec((tm, d), lambda i: (i, 0)),
                  pl.BlockSpec((1, d), lambda i: (0, 0)),
                  pl.BlockSpec((None, 1, d), lambda i: (seg(i), 0, 3)),
                  pl.BlockSpec((None, 1, d), lambda i: (seg(i), 0, 4)),
                  pl.BlockSpec((d, LANES), lambda i: (0, 0)),
                  pl.BlockSpec((1, LANES), lambda i: (0, 0))],
        out_specs=[pl.BlockSpec((tm, d), lambda i: (i, 0)),
                   pl.BlockSpec((tm, LANES), lambda i: (i, 0))],
        out_shape=[jax.ShapeDtypeStruct((t, d), F32),
                   jax.ShapeDtypeStruct((t, LANES), F32)],
        compiler_params=_cparams("parallel"),
        name="norm2_router",
    )(x, g.reshape(1, d), mod, mod, wr, br)


def _gather_kernel(idx_ref, src_ref, o_ref, sem, *, tg):
    base = pl.program_id(0) * tg

    def copy(r):
        return pltpu.make_async_copy(src_ref.at[pl.ds(idx_ref[base + r], 1), :], o_ref.at[pl.ds(r, 1), :], sem)

    def start(r, c):
        copy(r).start()
        return c

    def wait(r, c):
        copy(r).wait()
        return c

    lax.fori_loop(0, tg, start, 0)
    lax.fori_loop(0, tg, wait, 0)


def _gather_call(src, idx, tg=256):
    n = idx.shape[0]
    d = src.shape[1]
    return pl.pallas_call(
        functools.partial(_gather_kernel, tg=tg),
        grid_spec=pltpu.PrefetchScalarGridSpec(
            num_scalar_prefetch=1,
            grid=(n // tg,),
            in_specs=[pl.BlockSpec(memory_space=pl.ANY)],
            out_specs=pl.BlockSpec((tg, d), lambda i, idx: (i, 0)),
            scratch_shapes=[pltpu.SemaphoreType.DMA(())]),
        out_shape=jax.ShapeDtypeStruct((n, d), src.dtype),
        compiler_params=_cparams("arbitrary"),
        name="moe_row_gather",
    )(idx, src)


def _expert_up_kernel(te_ref, nv_ref, x_ref, wg_ref, wu_ref, rw_ref, o_ref, *, tm):
    valid = pl.program_id(0) * tm < nv_ref[0]

    @pl.when(valid)
    def _():
        x = x_ref[...].astype(BF16)
        hg = jnp.dot(x, wg_ref[...], preferred_element_type=F32)
        hu = jnp.dot(x, wu_ref[...], preferred_element_type=F32)
        o_ref[...] = (hg * _sigmoid(hg) * hu * rw_ref[...]).astype(o_ref.dtype)

    @pl.when(jnp.logical_not(valid))
    def _():
        o_ref[...] = jnp.zeros(o_ref.shape, o_ref.dtype)


def _expert_up_call(tile_e, n_valid, xs, wg, wu, row_w, tm):
    n, d = xs.shape
    ff = wg.shape[2]
    return pl.pallas_call(
        functools.partial(_expert_up_kernel, tm=tm),
        grid_spec=pltpu.PrefetchScalarGridSpec(
            num_scalar_prefetch=2,
            grid=(n // tm,),
            in_specs=[pl.BlockSpec((tm, d), lambda m, te, nv: (m, 0)),
                      pl.BlockSpec((None, d, ff), lambda m, te, nv: (te[m], 0, 0)),
                      pl.BlockSpec((None, d, ff), lambda m, te, nv: (te[m], 0, 0)),
                      pl.BlockSpec((tm, 1), lambda m, te, nv: (m, 0))],
            out_specs=pl.BlockSpec((tm, ff), lambda m, te, nv: (m, 0))),
        out_shape=jax.ShapeDtypeStruct((n, ff), BF16),
        compiler_params=_cparams("arbitrary"),
        name="moe_expert_up",
    )(tile_e, n_valid, xs, wg, wu, row_w)


def _expert_down_kernel(te_ref, nv_ref, a_ref, wd_ref, o_ref, *, tm):
    valid = pl.program_id(0) * tm < nv_ref[0]

    @pl.when(valid)
    def _():
        o_ref[...] = jnp.dot(a_ref[...], wd_ref[...], preferred_element_type=F32)

    @pl.when(jnp.logical_not(valid))
    def _():
        o_ref[...] = jnp.zeros(o_ref.shape, o_ref.dtype)


def _expert_down_call(tile_e, n_valid, a, wd, tm):
    n, ff = a.shape
    d = wd.shape[2]
    return pl.pallas_call(
        functools.partial(_expert_down_kernel, tm=tm),
        grid_spec=pltpu.PrefetchScalarGridSpec(
            num_scalar_prefetch=2,
            grid=(n // tm,),
            in_specs=[pl.BlockSpec((tm, ff), lambda m, te, nv: (m, 0)),
                      pl.BlockSpec((None, ff, d), lambda m, te, nv: (te[m], 0, 0))],
            out_specs=pl.BlockSpec((tm, d), lambda m, te, nv: (m, 0))),
        out_shape=jax.ShapeDtypeStruct((n, d), F32),
        compiler_params=_cparams("arbitrary"),
        name="moe_expert_down",
    )(tile_e, n_valid, a, wd)


def _combine_kernel(x_ref, y0_ref, y1_ref, gate_ref, fg_ref, o_ref, *, final):
    x = x_ref[...] + gate_ref[...] * (y0_ref[...] + y1_ref[...])
    if final:
        x = _rms(x) * fg_ref[...]
    o_ref[...] = x


def _combine_call(x, yg, mod, final_g, final, t_prompt, dec_seq, tm=256):
    t, d = x.shape
    seg = lambda i: _seg_of_row(i * tm, t_prompt, dec_seq)
    nt = t // tm
    return pl.pallas_call(
        functools.partial(_combine_kernel, final=final),
        grid=(nt,),
        in_specs=[pl.BlockSpec((tm, d), lambda i: (i, 0)),
                  pl.BlockSpec((tm, d), lambda i: (i, 0)),
                  pl.BlockSpec((tm, d), lambda i: (nt + i, 0)),
                  pl.BlockSpec((None, 1, d), lambda i: (seg(i), 0, 5)),
                  pl.BlockSpec((1, d), lambda i: (0, 0))],
        out_specs=pl.BlockSpec((tm, d), lambda i: (i, 0)),
        out_shape=jax.ShapeDtypeStruct((t, d), F32),
        compiler_params=_cparams("parallel"),
        name="moe_combine_residual",
    )(x, yg, yg, mod, final_g.reshape(1, d))


def _dispatch_plan(route, tm):
    t = route.shape[0]
    eid = route[:, 0:2].astype(jnp.int32)
    wts = route[:, 2:4]
    flat_e = eid.T.reshape(-1)
    flat_w = wts.T.reshape(-1)
    n_pairs = 2 * t
    n_tiles = n_pairs // tm + N_EXPERTS
    n_rows = n_tiles * tm
    order = jnp.argsort(flat_e, stable=True)
    sorted_e = flat_e[order]
    counts = jnp.zeros((N_EXPERTS,), jnp.int32).at[flat_e].add(1)
    padded = ((counts + tm - 1) // tm) * tm
    pad_end = jnp.cumsum(padded)
    pad_start = pad_end - padded
    run_start = jnp.cumsum(counts) - counts
    rank = jnp.arange(n_pairs, dtype=jnp.int32) - run_start[sorted_e]
    dest = pad_start[sorted_e] + rank
    token = (order % t).astype(jnp.int32)
    row_token = jnp.zeros((n_rows,), jnp.int32).at[dest].set(token)
    row_w = jnp.zeros((n_rows,), F32).at[dest].set(flat_w[order])
    pair_row = jnp.zeros((n_pairs,), jnp.int32).at[order].set(dest.astype(jnp.int32))
    tile_start = jnp.arange(n_tiles, dtype=jnp.int32) * tm
    tile_e = jnp.minimum(jnp.searchsorted(pad_end, tile_start, side='right'), N_EXPERTS - 1).astype(jnp.int32)
    n_valid = pad_end[-1:].astype(jnp.int32)
    return row_token, row_w.reshape(n_rows, 1), pair_row, tile_e, n_valid


def _moe(h, route, wg, wu, wd, tm=256):
    row_token, row_w, pair_row, tile_e, n_valid = _dispatch_plan(route, tm)
    xs = _gather_call(h, row_token)
    a = _expert_up_call(tile_e, n_valid, xs, wg, wu, row_w, tm)
    ys = _expert_down_call(tile_e, n_valid, a, wd, tm)
    return _gather_call(ys, pair_row)


def _trunk(x_prompt, x_sample, cache_k, cache_v, c, c_ctx, prm, final_norm_g, hy_block_sample=512):
    batch, seq, d = x_prompt.shape
    dec_batch, dec_seq, _ = x_sample.shape
    depth = prm['w_in'].shape[0]
    past = cache_k.shape[2]
    t_prompt = batch * seq
    t_sample = dec_batch * dec_seq

    x = jnp.concatenate([x_prompt.reshape(t_prompt, d), x_sample.reshape(t_sample, d)], axis=0)
    n_cond = 1 + dec_batch
    cond = jnp.concatenate([c_ctx[None, :], c, jnp.zeros((8 - n_cond % 8, d), F32)], axis=0)
    mod_all = _mod_call(cond, prm['w_mod'], prm['b_mod'])
    cos, sin_signed = _rope_tables(dec_seq)
    sconv_rows_p = math.gcd(t_prompt, max(seq, 4096 // seq * seq))

    new_k, new_v = [], []
    for l in range(depth):
        p = {name: w[l] for name, w in prm.items()}
        mod = mod_all[l, :n_cond].reshape(n_cond, 1, 6 * d)

        z = _norm_proj_call(x, p['norm1_g'], mod, p['w_in'].astype(BF16), t_prompt, dec_seq)

        attn_p, k_l, v_l = _attn_prompt_call(z, p['q_norm_g'], p['k_norm_g'], batch, seq)
        attn_s = _attn_sample_call(z, cache_k[:, l].reshape(dec_batch, past, KV_DIM),
                                   cache_v[:, l].reshape(dec_batch, past, KV_DIM),
                                   cos, sin_signed, p['q_norm_g'], p['k_norm_g'], t_prompt, dec_batch, dec_seq)
        new_k.append(k_l.reshape(batch, seq, N_KV_HEADS, HEAD_DIM))
        new_v.append(v_l.reshape(batch, seq, N_KV_HEADS, HEAD_DIM))
        attn = jnp.concatenate([attn_p, attn_s], axis=0)

        hy_p = _hyena_group(z, 0, batch, seq, seq, p, sconv_rows_p)
        hy_s = _hyena_group(z, t_prompt, dec_batch, dec_seq, min(hy_block_sample, dec_seq), p, dec_seq)
        hy = jnp.concatenate([hy_p, hy_s], axis=0)

        gm = _gmlp_call(z, p['gm_norm_g'], p['gm_norm_b'], p['gm_ws'], p['gm_bs'])

        x = _out_proj_call(attn, hy, gm, p['out_norm_g'], p['w_out'].astype(BF16), x, mod, t_prompt, dec_seq)

        wr = jnp.concatenate([p['router_g_w'],
                              p['router_e_w'].transpose(1, 0, 2).reshape(d, N_EXPERTS)], axis=1)
        wr = jnp.pad(wr, ((0, 0), (0, LANES - wr.shape[1])))
        br = jnp.pad(jnp.concatenate([p['router_g_b'], p['router_e_b'].reshape(-1)]),
                     (0, LANES - N_GROUPS - N_EXPERTS)).reshape(1, LANES)
        h2, route = _router_call(x, p['norm2_g'], mod, wr, br, t_prompt, dec_seq)
        yg = _moe(h2, route, p['exp_w_gate'].astype(BF16), p['exp_w_up'].astype(BF16),
                  p['exp_w_down'].astype(BF16))
        x = _combine_call(x, yg, mod, final_norm_g, l == depth - 1, t_prompt, dec_seq)

    y_prompt = x[:t_prompt].reshape(batch, seq, d)
    y_sample = x[t_prompt:].reshape(dec_batch, dec_seq, d)
    return y_prompt, y_sample, jnp.stack(new_k, axis=1), jnp.stack(new_v, axis=1)


_PARAM_NAMES = ('norm1_g', 'norm2_g', 'w_mod', 'b_mod', 'w_in', 'q_norm_g', 'k_norm_g', 'hy_conv_w', 'hy_conv_b',
                'hf_w1', 'hf_b1', 'hf_w2', 'hf_b2', 'hf_w3', 'hf_b3', 'hf_freq', 'hy_bias', 'gm_norm_g',
                'gm_norm_b', 'gm_ws', 'gm_bs', 'out_norm_g', 'w_out', 'router_g_w', 'router_g_b', 'router_e_w',
                'router_e_b', 'exp_w_gate', 'exp_w_up', 'exp_w_down')


def kernel(x_prompt, x_sample, cache_k, cache_v, c, c_ctx, norm1_g, norm2_g, w_mod, b_mod, w_in, q_norm_g, k_norm_g, hy_conv_w, hy_conv_b, hf_w1, hf_b1, hf_w2, hf_b2, hf_w3, hf_b3, hf_freq, hy_bias, gm_norm_g, gm_norm_b, gm_ws, gm_bs, out_norm_g, w_out, router_g_w, router_g_b, router_e_w, router_e_b, exp_w_gate, exp_w_up, exp_w_down, final_norm_g):
    values = (norm1_g, norm2_g, w_mod, b_mod, w_in, q_norm_g, k_norm_g, hy_conv_w, hy_conv_b, hf_w1, hf_b1, hf_w2,
              hf_b2, hf_w3, hf_b3, hf_freq, hy_bias, gm_norm_g, gm_norm_b, gm_ws, gm_bs, out_norm_g, w_out,
              router_g_w, router_g_b, router_e_w, router_e_b, exp_w_gate, exp_w_up, exp_w_down)
    prm = dict(zip(_PARAM_NAMES, values))
    return _trunk(x_prompt, x_sample, cache_k, cache_v, c, c_ctx, prm, final_norm_g)
```

```python
import functools
import math

import numpy as np
import jax
import jax.numpy as jnp
from jax import lax
from jax.experimental import pallas as pl
from jax.experimental.pallas import tpu as pltpu

F32 = jnp.float32
BF16 = jnp.bfloat16

D_MODEL = 4096
GRID_W = 64
HEAD_DIM = 128
N_HEADS = 16
N_KV_HEADS = 4
Q_PER_KV = N_HEADS // N_KV_HEADS
ATTN_DIM = N_HEADS * HEAD_DIM
KV_DIM = N_KV_HEADS * HEAD_DIM
HY_DIM = 1024
GM_DIM = 1024
GM_HEADS = 8
CHUNK = 128
PROJ_DIM = ATTN_DIM + 2 * KV_DIM + 3 * HY_DIM + 2 * GM_DIM
ROPE_THETA = 10000.0
FILTER_EMB = 33
FILTER_HIDDEN = 64
DECAY_TARGET = 1e-2
FAST_DECAY_PCT = 0.3
SLOW_DECAY_PCT = 1.5
MOD_SHIFT = 0.05
N_GROUPS = 4
EXPERTS_PER_GROUP = 4
N_EXPERTS = 16
EXPERT_FF = 1024
EPS = 1e-6

COL_K = ATTN_DIM
COL_V = ATTN_DIM + KV_DIM
COL_HY = ATTN_DIM + 2 * KV_DIM
COL_GM = COL_HY + 3 * HY_DIM

LANES = 128
VMEM_LIMIT = 56 * 1024 * 1024
HI = lax.Precision.HIGHEST


def _cparams(*sem, **kw):
    return pltpu.CompilerParams(dimension_semantics=sem, vmem_limit_bytes=VMEM_LIMIT, **kw)


def _rms(x):
    return x * lax.rsqrt(jnp.mean(x * x, axis=-1, keepdims=True) + EPS)


def _sigmoid(x):
    return 1.0 / (1.0 + jnp.exp(-x))


def _seg_of_row(row, t_prompt, dec_seq):
    return jnp.where(row < t_prompt, 0, 1 + (row - t_prompt) // dec_seq)


def _mod_kernel(c_ref, w_ref, b_ref, o_ref):
    c = c_ref[...]
    s = (c * _sigmoid(c)).astype(BF16)
    o_ref[...] = jnp.dot(s, w_ref[...].astype(BF16), preferred_element_type=F32) + b_ref[...]


def _mod_call(cond, w_mod, b_mod):
    depth, d, n = w_mod.shape
    r = cond.shape[0]
    tn = 512
    return pl.pallas_call(
        _mod_kernel,
        grid=(depth, n // tn),
        in_specs=[pl.BlockSpec((r, d), lambda l, j: (0, 0)),
                  pl.BlockSpec((None, d, tn), lambda l, j: (l, 0, j)),
                  pl.BlockSpec((None, 1, tn), lambda l, j: (l, 0, j))],
        out_specs=pl.BlockSpec((None, r, tn), lambda l, j: (l, 0, j)),
        out_shape=jax.ShapeDtypeStruct((depth, r, n), F32),
        compiler_params=_cparams("parallel", "parallel"),
        name="adaln_mod",
    )(cond, w_mod, b_mod.reshape(depth, 1, n))


def _norm_proj_kernel(x_ref, g_ref, sh_ref, sc_ref, w_ref, o_ref, h_ref):
    @pl.when(pl.program_id(1) == 0)
    def _():
        y = _rms(x_ref[...]) * g_ref[...]
        h_ref[...] = (y * (1.0 + sc_ref[...]) + sh_ref[...]).astype(BF16)

    o_ref[...] = jnp.dot(h_ref[...], w_ref[...], preferred_element_type=F32)


def _norm_proj_call(x, g, mod, w, t_prompt, dec_seq, tm=512, tn=512):
    t, d = x.shape
    n = w.shape[1]
    seg = lambda i: _seg_of_row(i * tm, t_prompt, dec_seq)
    return pl.pallas_call(
        _norm_proj_kernel,
        grid=(t // tm, n // tn),
        in_specs=[pl.BlockSpec((tm, d), lambda i, j: (i, 0)),
                  pl.BlockSpec((1, d), lambda i, j: (0, 0)),
                  pl.BlockSpec((None, 1, d), lambda i, j: (seg(i), 0, 0)),
                  pl.BlockSpec((None, 1, d), lambda i, j: (seg(i), 0, 1)),
                  pl.BlockSpec((d, tn), lambda i, j: (0, j))],
        out_specs=pl.BlockSpec((tm, tn), lambda i, j: (i, j)),
        out_shape=jax.ShapeDtypeStruct((t, n), F32),
        scratch_shapes=[pltpu.VMEM((tm, d), BF16)],
        compiler_params=_cparams("parallel", "arbitrary"),
        name="norm1_in_proj",
    )(x, g.reshape(1, d), mod, mod, w)


def _attn_prompt_kernel(q_ref, k_ref, v_ref, qg_ref, kg_ref, o_ref, kc_ref, vc_ref):
    kn = _rms(k_ref[...]) * kg_ref[...]
    v = v_ref[...]
    kc_ref[...] = kn
    vc_ref[...] = v
    kb = kn.astype(BF16)
    vb = v.astype(BF16)
    scale = HEAD_DIM ** -0.5
    for m in range(Q_PER_KV):
        sl = slice(m * HEAD_DIM, (m + 1) * HEAD_DIM)
        qn = (_rms(q_ref[:, sl]) * qg_ref[...] * scale).astype(BF16)
        s = lax.dot_general(qn, kb, (((1,), (1,)), ((), ())), preferred_element_type=F32)
        p = jnp.exp(s - jnp.max(s, axis=-1, keepdims=True))
        l = jnp.sum(p, axis=-1, keepdims=True)
        o = jnp.dot(p.astype(BF16), vb, preferred_element_type=F32)
        o_ref[:, sl] = o / l


def _attn_prompt_call(z, qg, kg, batch, seq):
    gw = Q_PER_KV * HEAD_DIM
    tp = batch * seq
    return pl.pallas_call(
        _attn_prompt_kernel,
        grid=(batch, N_KV_HEADS),
        in_specs=[pl.BlockSpec((seq, gw), lambda b, g: (b, g)),
                  pl.BlockSpec((seq, HEAD_DIM), lambda b, g: (b, COL_K // HEAD_DIM + g)),
                  pl.BlockSpec((seq, HEAD_DIM), lambda b, g: (b, COL_V // HEAD_DIM + g)),
                  pl.BlockSpec((1, HEAD_DIM), lambda b, g: (0, 0)),
                  pl.BlockSpec((1, HEAD_DIM), lambda b, g: (0, 0))],
        out_specs=[pl.BlockSpec((seq, gw), lambda b, g: (b, g)),
                   pl.BlockSpec((seq, HEAD_DIM), lambda b, g: (b, g)),
                   pl.BlockSpec((seq, HEAD_DIM), lambda b, g: (b, g))],
        out_shape=[jax.ShapeDtypeStruct((tp, ATTN_DIM), F32),
                   jax.ShapeDtypeStruct((tp, KV_DIM), F32),
                   jax.ShapeDtypeStruct((tp, KV_DIM), F32)],
        compiler_params=_cparams("parallel", "parallel"),
        name="attn_context",
    )(z, z, z, qg.reshape(1, HEAD_DIM), kg.reshape(1, HEAD_DIM))


def _rope(x, cos, sin_signed):
    lane = lax.broadcasted_iota(jnp.int32, x.shape, 1)
    quarter = HEAD_DIM // 4
    fwd = pltpu.roll(x, HEAD_DIM - quarter, 1)
    bwd = pltpu.roll(x, quarter, 1)
    swapped = jnp.where((lane % (2 * quarter)) < quarter, fwd, bwd)
    return x * cos + swapped * sin_signed


def _attn_sample_kernel(q_ref, k_ref, v_ref, ck_ref, cv_ref, cq_ref, sq_ref, ckk_ref, skk_ref,
                        qg_ref, kg_ref, o_ref, qs_ref, m_ref, l_ref, acc_ref):
    j = pl.program_id(3)
    scale = HEAD_DIM ** -0.5

    @pl.when(j == 0)
    def _():
        for m in range(Q_PER_KV):
            sl = slice(m * HEAD_DIM, (m + 1) * HEAD_DIM)
            qn = _rms(q_ref[:, sl]) * qg_ref[...]
            qs_ref[m] = (_rope(qn, cq_ref[...], sq_ref[...]) * scale).astype(BF16)
        m_ref[...] = jnp.full(m_ref.shape, -jnp.inf, F32)
        l_ref[...] = jnp.zeros(l_ref.shape, F32)
        acc_ref[...] = jnp.zeros(acc_ref.shape, F32)

    def step(kb, vb):
        reps = kb.shape[0] // LANES
        for m in range(Q_PER_KV):
            s = lax.dot_general(qs_ref[m], kb, (((1,), (1,)), ((), ())), preferred_element_type=F32)
            m_prev = m_ref[m]
            m_next = jnp.maximum(m_prev, jnp.max(s, axis=-1, keepdims=True))
            alpha = jnp.exp(m_prev - m_next)
            p = jnp.exp(s - jnp.concatenate([m_next] * reps, axis=1))
            l_ref[m] = alpha * l_ref[m] + jnp.sum(p, axis=-1, keepdims=True)
            acc_ref[m] = alpha * acc_ref[m] + jnp.dot(p.astype(BF16), vb, preferred_element_type=F32)
            m_ref[m] = m_next

    @pl.when(j == 0)
    def _():
        step(ck_ref[...].astype(BF16), cv_ref[...].astype(BF16))

    @pl.when(j > 0)
    def _():
        kn = _rms(k_ref[...]) * kg_ref[...]
        kr = _rope(kn, ckk_ref[...], skk_ref[...])
        step(kr.astype(BF16), v_ref[...].astype(BF16))

    @pl.when(j == pl.num_programs(3) - 1)
    def _():
        for m in range(Q_PER_KV):
            o_ref[:, m * HEAD_DIM:(m + 1) * HEAD_DIM] = acc_ref[m] / l_ref[m]


def _attn_sample_call(z, ctx_k, ctx_v, cos, sin_signed, qg, kg, t_prompt, dec_batch, dec_seq, tq=512, tk=512):
    gw = Q_PER_KV * HEAD_DIM
    past = ctx_k.shape[1]
    nq = dec_seq // tq
    nk = dec_seq // tk
    qrow = lambda b, i: (t_prompt + b * dec_seq) // tq + i
    krow = lambda b, j: (t_prompt + b * dec_seq) // tk + jnp.maximum(j - 1, 0)
    return pl.pallas_call(
        _attn_sample_kernel,
        grid=(dec_batch, N_KV_HEADS, nq, nk + 1),
        in_specs=[pl.BlockSpec((tq, gw), lambda b, g, i, j: (qrow(b, i), g)),
                  pl.BlockSpec((tk, HEAD_DIM), lambda b, g, i, j: (krow(b, j), COL_K // HEAD_DIM + g)),
                  pl.BlockSpec((tk, HEAD_DIM), lambda b, g, i, j: (krow(b, j), COL_V // HEAD_DIM + g)),
                  pl.BlockSpec((None, past, HEAD_DIM), lambda b, g, i, j: (b, 0, g)),
                  pl.BlockSpec((None, past, HEAD_DIM), lambda b, g, i, j: (b, 0, g)),
                  pl.BlockSpec((tq, HEAD_DIM), lambda b, g, i, j: (i, 0)),
                  pl.BlockSpec((tq, HEAD_DIM), lambda b, g, i, j: (i, 0)),
                  pl.BlockSpec((tk, HEAD_DIM), lambda b, g, i, j: (jnp.maximum(j - 1, 0), 0)),
                  pl.BlockSpec((tk, HEAD_DIM), lambda b, g, i, j: (jnp.maximum(j - 1, 0), 0)),
                  pl.BlockSpec((1, HEAD_DIM), lambda b, g, i, j: (0, 0)),
                  pl.BlockSpec((1, HEAD_DIM), lambda b, g, i, j: (0, 0))],
        out_specs=pl.BlockSpec((tq, gw), lambda b, g, i, j: (b * nq + i, g)),
        out_shape=jax.ShapeDtypeStruct((dec_batch * dec_seq, ATTN_DIM), F32),
        scratch_shapes=[pltpu.VMEM((Q_PER_KV, tq, HEAD_DIM), BF16),
                        pltpu.VMEM((Q_PER_KV, tq, LANES), F32),
                        pltpu.VMEM((Q_PER_KV, tq, LANES), F32),
                        pltpu.VMEM((Q_PER_KV, tq, HEAD_DIM), F32)],
        compiler_params=_cparams("parallel", "parallel", "parallel", "arbitrary"),
        name="attn_latent",
    )(z, z, z, ctx_k, ctx_v, cos, sin_signed, cos, sin_signed,
      qg.reshape(1, HEAD_DIM), kg.reshape(1, HEAD_DIM))


def _rope_tables(n_tokens):
    quarter = HEAD_DIM // 4
    n_rows = n_tokens // GRID_W
    row = jnp.repeat(jnp.arange(n_rows, dtype=F32), GRID_W)
    col = jnp.tile(jnp.arange(GRID_W, dtype=F32), n_rows)
    freqs = ROPE_THETA ** (-jnp.arange(quarter, dtype=F32) / quarter)
    ar = row[:, None] * freqs[None, :]
    ac = col[:, None] * freqs[None, :]
    cos = jnp.concatenate([jnp.cos(ar), jnp.cos(ar), jnp.cos(ac), jnp.cos(ac)], axis=-1)
    sin = jnp.concatenate([-jnp.sin(ar), jnp.sin(ar), -jnp.sin(ac), jnp.sin(ac)], axis=-1)
    return cos, sin


def _gelu_tanh(x):
    return 0.5 * x * (1.0 + jnp.tanh(math.sqrt(2.0 / math.pi) * (x + 0.044715 * (x * x * x))))


def _gmlp_kernel(z_ref, g_ref, b_ref, ws_ref, bs_ref, o_ref):
    hw = GM_DIM // GM_HEADS
    for ch in range(z_ref.shape[0] // CHUNK):
        rows = slice(ch * CHUNK, (ch + 1) * CHUNK)
        u = _gelu_tanh(z_ref[rows, :GM_DIM])
        v = _gelu_tanh(z_ref[rows, GM_DIM:])
        mu = jnp.mean(v, axis=-1, keepdims=True)
        vc = v - mu
        var = jnp.mean(vc * vc, axis=-1, keepdims=True)
        vn = (vc * lax.rsqrt(var + EPS) * g_ref[...] + b_ref[...]).astype(BF16)
        for h in range(GM_HEADS):
            cols = slice(h * hw, (h + 1) * hw)
            s = jnp.dot(ws_ref[h], vn[:, cols], preferred_element_type=F32) + bs_ref[h]
            o_ref[rows, cols] = u[:, cols] * s


def _gmlp_call(z, g, b, ws, bs, tr=512):
    t = z.shape[0]
    return pl.pallas_call(
        _gmlp_kernel,
        grid=(t // tr,),
        in_specs=[pl.BlockSpec((tr, 2 * GM_DIM), lambda i: (i, COL_GM // (2 * GM_DIM))),
                  pl.BlockSpec((1, GM_DIM), lambda i: (0, 0)),
                  pl.BlockSpec((1, GM_DIM), lambda i: (0, 0)),
                  pl.BlockSpec((GM_HEADS, CHUNK, CHUNK), lambda i: (0, 0, 0)),
                  pl.BlockSpec((GM_HEADS, CHUNK, 1), lambda i: (0, 0, 0))],
        out_specs=pl.BlockSpec((tr, GM_DIM), lambda i: (i, 0)),
        out_shape=jax.ShapeDtypeStruct((t, GM_DIM), F32),
        compiler_params=_cparams("parallel"),
        name="chunk_gmlp",
    )(z, g.reshape(1, GM_DIM), b.reshape(1, GM_DIM), ws.astype(BF16), bs.reshape(GM_HEADS, CHUNK, 1))


def _sconv_kernel(z_ref, w_ref, b_ref, o_ref, *, seg_len):
    x = z_ref[...]
    n = x.shape[0]
    pos = lax.broadcasted_iota(jnp.int32, x.shape, 0) % seg_len
    prev = jnp.where(pos == 0, 0.0, pltpu.roll(x, 1, 0))
    nxt = jnp.where(pos == seg_len - 1, 0.0, pltpu.roll(x, n - 1, 0))
    o_ref[...] = b_ref[...] + prev * w_ref[0:1, :] + x * w_ref[1:2, :] + nxt * w_ref[2:3, :]


def _sconv_call(z, w, b, row0, n_rows, seg_len, block_rows, cw=256):
    width = 3 * HY_DIM
    return pl.pallas_call(
        functools.partial(_sconv_kernel, seg_len=seg_len),
        grid=(n_rows // block_rows, width // cw),
        in_specs=[pl.BlockSpec((block_rows, cw), lambda i, c: (row0 // block_rows + i, COL_HY // cw + c)),
                  pl.BlockSpec((3, cw), lambda i, c: (0, c)),
                  pl.BlockSpec((1, cw), lambda i, c: (0, c))],
        out_specs=pl.BlockSpec((block_rows, cw), lambda i, c: (i, c)),
        out_shape=jax.ShapeDtypeStruct((n_rows, width), F32),
        compiler_params=_cparams("parallel", "parallel"),
        name="hyena_short_conv",
    )(z, w, b.reshape(1, width))


def _filter_kernel(z_ref, t_ref, w1_ref, b1_ref, w2_ref, b2_ref, w3a_ref, b3a_ref, w3b_ref, b3b_ref,
                   fr_ref, dl_ref, o_ref, *, tr):
    fr = fr_ref[...]
    h = jnp.sin(fr * (jnp.dot(z_ref[...], w1_ref[...], precision=HI, preferred_element_type=F32) + b1_ref[...]))
    h = jnp.sin(fr * (jnp.dot(h, w2_ref[...], precision=HI, preferred_element_type=F32) + b2_ref[...]))
    decay = jnp.exp(-t_ref[...] * dl_ref[...]) + MOD_SHIFT
    row = lax.broadcasted_iota(jnp.int32, decay.shape, 0) + pl.program_id(0) * tr
    decay = jnp.where(row == 0, 0.0, decay)
    for o, (w3_ref, b3_ref) in enumerate(((w3a_ref, b3a_ref), (w3b_ref, b3b_ref))):
        taps = jnp.dot(h, w3_ref[...], precision=HI, preferred_element_type=F32) + b3_ref[...]
        o_ref[:, o * HY_DIM:(o + 1) * HY_DIM] = taps * decay


def _filter_call(length, w1, b1, w2, b2, w3, b3, fr, tr=256):
    bands = (FILTER_EMB - 1) // 2
    t = np.linspace(0.0, 1.0, length)[:, None]
    wv = 2.0 * np.pi * np.arange(length)[:, None] / length
    f = np.linspace(1e-4, bands - 1, bands)[None, :]
    zf = np.concatenate([t, np.cos(f * wv), -np.sin(f * wv)], axis=-1)
    min_decay = math.log(DECAY_TARGET) / SLOW_DECAY_PCT
    max_decay = math.log(DECAY_TARGET) / FAST_DECAY_PCT
    deltas = np.abs(np.linspace(min_decay, max_decay, HY_DIM))[None, :]
    lag = np.minimum(np.abs(np.arange(2 * length) - length), length - 1)
    pad = LANES - FILTER_EMB
    hp = LANES - FILTER_HIDDEN
    z2 = jnp.asarray(np.pad(zf[lag], ((0, 0), (0, pad))).astype(np.float32))
    t2 = jnp.asarray(t[lag].astype(np.float32))
    dl = jnp.asarray(deltas.astype(np.float32))
    w1p = jnp.pad(w1, ((0, pad), (0, hp)))
    w2p = jnp.pad(w2, ((0, hp), (0, hp)))
    w3p = jnp.pad(w3, ((0, hp), (0, 0)))
    b1p = jnp.pad(b1, (0, hp)).reshape(1, LANES)
    b2p = jnp.pad(b2, (0, hp)).reshape(1, LANES)
    frp = jnp.pad(fr, (0, hp)).reshape(1, LANES)
    b3r = b3.reshape(1, -1)
    nblk = 2 * length // tr
    half = length // tr
    wcol = lambda r, o: o * 2 + jnp.where(r < half, 1, 0)
    small = pl.BlockSpec((1, LANES), lambda r: (0, 0))
    square = pl.BlockSpec((LANES, LANES), lambda r: (0, 0))
    return pl.pallas_call(
        functools.partial(_filter_kernel, tr=tr),
        grid=(nblk,),
        in_specs=[pl.BlockSpec((tr, LANES), lambda r: (r, 0)),
                  pl.BlockSpec((tr, 1), lambda r: (r, 0)),
                  square, small, square, small,
                  pl.BlockSpec((LANES, HY_DIM), lambda r: (0, wcol(r, 0))),
                  pl.BlockSpec((1, HY_DIM), lambda r: (0, wcol(r, 0))),
                  pl.BlockSpec((LANES, HY_DIM), lambda r: (0, wcol(r, 1))),
                  pl.BlockSpec((1, HY_DIM), lambda r: (0, wcol(r, 1))),
                  small,
                  pl.BlockSpec((1, HY_DIM), lambda r: (0, 0))],
        out_specs=pl.BlockSpec((tr, 2 * HY_DIM), lambda r: (r, 0)),
        out_shape=jax.ShapeDtypeStruct((2 * length, 2 * HY_DIM), F32),
        compiler_params=_cparams("parallel"),
        name="hyena_filter_mlp",
    )(z2, t2, w1p, b1p, w2p, b2p, w3p, b3r, w3p, b3r, frp, dl)


def _dft_constants(p):
    n = 2 * p
    idx = np.arange(p, dtype=np.float64)
    ang = 2.0 * np.pi * np.outer(idx, idx) / n
    re = np.cos(ang)
    im = -np.sin(ang)
    im[0, :] = np.cos(np.pi * idx)
    fwd = np.concatenate([re, im], axis=0)
    sign = np.where(np.arange(p) % 2 == 0, 1.0, -1.0)
    sign2 = np.concatenate([sign, sign])
    sign2[p] = 1.0
    fwd_shift = fwd * sign2[:, None]
    ar = (2.0 / n) * np.cos(ang)
    ar[:, 0] = 1.0 / n
    ai = -(2.0 / n) * np.sin(ang)
    ai[:, 0] = sign / n
    inv = np.concatenate([ar, ai], axis=1)
    as_bf16 = lambda a: jnp.asarray(a.astype(np.float32)).astype(BF16)
    return as_bf16(fwd), as_bf16(fwd_shift), as_bf16(inv)


def _dft_kernel(f_ref, x_ref, o_ref):
    o_ref[...] = jnp.dot(f_ref[...], x_ref[...].astype(BF16), preferred_element_type=F32).astype(o_ref.dtype)


def _dft_call(x, fwd, p, col0, cw=256):
    nblk = x.shape[0] // p
    return pl.pallas_call(
        _dft_kernel,
        grid=(nblk, HY_DIM // cw),
        in_specs=[pl.BlockSpec((2 * p, p), lambda i, c: (0, 0)),
                  pl.BlockSpec((p, cw), lambda i, c: (i, col0 // cw + c))],
        out_specs=pl.BlockSpec((None, 2 * p, cw), lambda i, c: (i, 0, c)),
        out_shape=jax.ShapeDtypeStruct((nblk, 2 * p, HY_DIM), BF16),
        compiler_params=_cparams("parallel", "parallel"),
        name="hyena_block_dft",
    )(fwd, x)


def _filter_dft_kernel(f_ref, fs_ref, hi_ref, lo_ref, o_ref):
    lo = lo_ref[...]
    row = lax.broadcasted_iota(jnp.int32, lo.shape, 0)
    lo = jnp.where(row == 0, 0.0, lo)
    acc = jnp.dot(f_ref[...], hi_ref[...].astype(BF16), preferred_element_type=F32)
    acc = acc + jnp.dot(fs_ref[...], lo.astype(BF16), preferred_element_type=F32)
    o_ref[...] = acc.astype(o_ref.dtype)


def _filter_dft_call(taps, fwd, fwd_shift, p, cw=256):
    nb2 = taps.shape[0] // p
    nd = nb2 - 1
    ncb = HY_DIM // cw
    return pl.pallas_call(
        _filter_dft_kernel,
        grid=(2, nd, ncb),
        in_specs=[pl.BlockSpec((2 * p, p), lambda o, d, c: (0, 0)),
                  pl.BlockSpec((2 * p, p), lambda o, d, c: (0, 0)),
                  pl.BlockSpec((p, cw), lambda o, d, c: (d + 1, o * ncb + c)),
                  pl.BlockSpec((p, cw), lambda o, d, c: (d, o * ncb + c))],
        out_specs=pl.BlockSpec((None, None, 2 * p, cw), lambda o, d, c: (o, d, 0, c)),
        out_shape=jax.ShapeDtypeStruct((2, nd, 2 * p, HY_DIM), BF16),
        compiler_params=_cparams("parallel", "parallel", "parallel"),
        name="hyena_filter_dft",
    )(fwd, fwd_shift, taps, taps)


def _specconv_kernel(xs_ref, gs_ref, inv_ref, v_ref, gate_ref, bias_ref, o_ref, y_ref, *, nb, p, rc):
    i = pl.program_id(2)
    cw = o_ref.shape[1]

    for r in range(0, p, rc):
        def body(j, carry):
            yre, yim = carry
            d = i - j + nb - 1
            xre = xs_ref[j, r:r + rc, :].astype(F32)
            xim = xs_ref[j, p + r:p + r + rc, :].astype(F32)
            gre = gs_ref[d, r:r + rc, :].astype(F32)
            gim = gs_ref[d, p + r:p + r + rc, :].astype(F32)
            return yre + (xre * gre - xim * gim), yim + (xre * gim + xim * gre)

        zero = jnp.zeros((rc, cw), F32)
        yre, yim = lax.fori_loop(0, nb, body, (zero, zero))
        y_ref[r:r + rc, :] = yre
        y_ref[p + r:p + r + rc, :] = yim

    def edge(j, carry):
        y0, yn = carry
        d = i - j + nb - 1
        x0 = xs_ref[j, 0:16, :].astype(F32)
        xn = xs_ref[j, p:p + 16, :].astype(F32)
        g0 = gs_ref[d, 0:16, :].astype(F32)
        gn = gs_ref[d, p:p + 16, :].astype(F32)
        return y0 + x0 * g0, yn + xn * gn

    zero16 = jnp.zeros((16, cw), F32)
    y0, yn = lax.fori_loop(0, nb, edge, (zero16, zero16))
    y_ref[0:1, :] = y0[0:1, :]
    y_ref[p:p + 1, :] = yn[0:1, :]

    y = jnp.dot(inv_ref[...], y_ref[...].astype(BF16), preferred_element_type=F32)
    v = v_ref[...]
    o_ref[...] = gate_ref[...] * (y + bias_ref[...] * v)


def _specconv_call(xs, gs, inv, zc, bias, nseq, nb, p, v_col0, gate_col0, v_src=None, cw=256, rc=32):
    n_rows = nseq * nb * p
    xs4 = xs.reshape(nseq, nb, 2 * p, HY_DIM)
    if v_src is None:
        v_arr, v_spec = zc, pl.BlockSpec((p, cw), lambda c, s, i: (s * nb + i, v_col0 // cw + c))
    else:
        v_arr, v_spec = v_src, pl.BlockSpec((p, cw), lambda c, s, i: (s * nb + i, c))
    return pl.pallas_call(
        functools.partial(_specconv_kernel, nb=nb, p=p, rc=rc),
        grid=(HY_DIM // cw, nseq, nb),
        in_specs=[pl.BlockSpec((None, nb, 2 * p, cw), lambda c, s, i: (s, 0, 0, c)),
                  pl.BlockSpec((2 * nb - 1, 2 * p, cw), lambda c, s, i: (0, 0, c)),
                  pl.BlockSpec((p, 2 * p), lambda c, s, i: (0, 0)),
                  v_spec,
                  pl.BlockSpec((p, cw), lambda c, s, i: (s * nb + i, gate_col0 // cw + c)),
                  pl.BlockSpec((1, cw), lambda c, s, i: (0, c))],
        out_specs=pl.BlockSpec((p, cw), lambda c, s, i: (s * nb + i, c)),
        out_shape=jax.ShapeDtypeStruct((n_rows, HY_DIM), F32),
        scratch_shapes=[pltpu.VMEM((2 * p, cw), F32)],
        compiler_params=_cparams("parallel", "parallel", "parallel"),
        name="hyena_spectral_conv",
    )(xs4, gs, inv, v_arr, zc, bias.reshape(1, HY_DIM))


def _hyena_group(z, row0, nseq, length, p, prm, sconv_rows):
    nb = length // p
    zc = _sconv_call(z, prm['hy_conv_w'], prm['hy_conv_b'], row0, nseq * length, length, sconv_rows)
    taps = _filter_call(length, prm['hf_w1'], prm['hf_b1'], prm['hf_w2'], prm['hf_b2'],
                        prm['hf_w3'], prm['hf_b3'], prm['hf_freq'])
    fwd, fwd_shift, inv = _dft_constants(p)
    gs = _filter_dft_call(taps, fwd, fwd_shift, p)
    vs = _dft_call(zc, fwd, p, 2 * HY_DIM)
    u = _specconv_call(vs, gs[0], inv, zc, prm['hy_bias'][0], nseq, nb, p, 2 * HY_DIM, 0)
    us = _dft_call(u, fwd, p, 0)
    return _specconv_call(us, gs[1], inv, zc, prm['hy_bias'][1], nseq, nb, p, 0, HY_DIM, v_src=u)


def _out_proj_kernel(ap_ref, as_ref, hp_ref, hs_ref, m_ref, g_ref, w_ref, x_ref, gate_ref, o_ref, mix_ref, *, ntp):
    i = pl.program_id(0)

    def fill(a_ref, h_ref):
        mix_ref[:, :ATTN_DIM] = (_rms(a_ref[...]) * g_ref[:, :ATTN_DIM]).astype(BF16)
        mix_ref[:, ATTN_DIM:ATTN_DIM + HY_DIM] = (
            _rms(h_ref[...]) * g_ref[:, ATTN_DIM:ATTN_DIM + HY_DIM]).astype(BF16)
        mix_ref[:, ATTN_DIM + HY_DIM:] = (_rms(m_ref[...]) * g_ref[:, ATTN_DIM + HY_DIM:]).astype(BF16)

    @pl.when(jnp.logical_and(pl.program_id(1) == 0, i < ntp))
    def _():
        fill(ap_ref, hp_ref)

    @pl.when(jnp.logical_and(pl.program_id(1) == 0, i >= ntp))
    def _():
        fill(as_ref, hs_ref)

    y = jnp.dot(mix_ref[...], w_ref[...], preferred_element_type=F32)
    o_ref[...] = x_ref[...] + gate_ref[...] * y


def _out_proj_call(attn_p, attn_s, hy_p, hy_s, gm, g, w, x, mod, t_prompt, dec_seq, tm=512, tn=512):
    t, d = x.shape
    mix_dim = ATTN_DIM + HY_DIM + GM_DIM
    seg = lambda i: _seg_of_row(i * tm, t_prompt, dec_seq)
    ntp = t_prompt // tm
    nts = t // tm - ntp
    prow = lambda i, j: (jnp.minimum(i, ntp - 1), 0)
    srow = lambda i, j: (jnp.clip(i - ntp, 0, nts - 1), 0)
    return pl.pallas_call(
        functools.partial(_out_proj_kernel, ntp=ntp),
        grid=(t // tm, d // tn),
        in_specs=[pl.BlockSpec((tm, ATTN_DIM), prow),
                  pl.BlockSpec((tm, ATTN_DIM), srow),
                  pl.BlockSpec((tm, HY_DIM), prow),
                  pl.BlockSpec((tm, HY_DIM), srow),
                  pl.BlockSpec((tm, GM_DIM), lambda i, j: (i, 0)),
                  pl.BlockSpec((1, mix_dim), lambda i, j: (0, 0)),
                  pl.BlockSpec((mix_dim, tn), lambda i, j: (0, j)),
                  pl.BlockSpec((tm, tn), lambda i, j: (i, j)),
                  pl.BlockSpec((None, 1, tn), lambda i, j: (seg(i), 0, 2 * (d // tn) + j))],
        out_specs=pl.BlockSpec((tm, tn), lambda i, j: (i, j)),
        out_shape=jax.ShapeDtypeStruct((t, d), F32),
        scratch_shapes=[pltpu.VMEM((tm, mix_dim), BF16)],
        compiler_params=_cparams("parallel", "arbitrary"),
        name="mix_out_proj",
    )(attn_p, attn_s, hy_p, hy_s, gm, g.reshape(1, mix_dim), w, x, mod)


def _router_kernel(x_ref, g_ref, sh_ref, sc_ref, wr_ref, br_ref, h_ref, r_ref):
    y = _rms(x_ref[...]) * g_ref[...]
    h = y * (1.0 + sc_ref[...]) + sh_ref[...]
    h_ref[...] = h
    logits = jnp.dot(h, wr_ref[...], precision=HI, preferred_element_type=F32) + br_ref[...]
    col = lambda k: logits[:, k:k + 1]

    lg = [col(k) for k in range(N_GROUPS)]
    g_max = functools.reduce(jnp.maximum, lg)
    g_den = sum(jnp.exp(v - g_max) for v in lg)
    g_top = 1.0 / g_den
    g_idx = jnp.full(g_max.shape, N_GROUPS - 1, jnp.int32)
    for k in range(N_GROUPS - 2, -1, -1):
        g_idx = jnp.where(lg[k] == g_max, k, g_idx)

    le = []
    for e in range(EXPERTS_PER_GROUP):
        v = col(N_GROUPS + (N_GROUPS - 1) * EXPERTS_PER_GROUP + e)
        for k in range(N_GROUPS - 2, -1, -1):
            v = jnp.where(g_idx == k, col(N_GROUPS + k * EXPERTS_PER_GROUP + e), v)
        le.append(v)

    e_max = functools.reduce(jnp.maximum, le)
    e1 = jnp.full(e_max.shape, EXPERTS_PER_GROUP - 1, jnp.int32)
    for e in range(EXPERTS_PER_GROUP - 2, -1, -1):
        e1 = jnp.where(le[e] == e_max, e, e1)
    neg = jnp.float32(-jnp.inf)
    rest = [jnp.where(e1 == e, neg, le[e]) for e in range(EXPERTS_PER_GROUP)]
    e2_max = functools.reduce(jnp.maximum, rest)
    e2 = jnp.full(e_max.shape, EXPERTS_PER_GROUP - 1, jnp.int32)
    for e in range(EXPERTS_PER_GROUP - 2, -1, -1):
        e2 = jnp.where(rest[e] == e2_max, e, e2)
    ratio = jnp.exp(e2_max - e_max)
    w1 = g_top / (1.0 + ratio)
    w2 = g_top * ratio / (1.0 + ratio)
    id1 = (g_idx * EXPERTS_PER_GROUP + e1).astype(F32)
    id2 = (g_idx * EXPERTS_PER_GROUP + e2).astype(F32)

    lane = lax.broadcasted_iota(jnp.int32, r_ref.shape, 1)
    r_ref[...] = jnp.where(lane == 0, id1, jnp.where(lane == 1, id2, jnp.where(lane == 2, w1, jnp.where(lane == 3, w2, 0.0))))


def _router_call(x, g, mod, wr, br, t_prompt, dec_seq, tm=256):
    t, d = x.shape
    seg = lambda i: _seg_of_row(i * tm, t_prompt, dec_seq)
    return pl.pallas_call(
        _router_kernel,
        grid=(t // tm,),
        in_specs=[pl.BlockSpec((tm, d), lambda i: (i, 0)),
                  pl.BlockSpec((1, d), lambda i: (0, 0)),
                  pl.BlockSpec((None, 1, d), lambda i: (seg(i), 0, 3)),
                  pl.BlockSpec((None, 1, d), lambda i: (seg(i), 0, 4)),
                  pl.BlockSpec((d, LANES), lambda i: (0, 0)),
                  pl.BlockSpec((1, LANES), lambda i: (0, 0))],
        out_specs=[pl.BlockSpec((tm, d), lambda i: (i, 0)),
                   pl.BlockSpec((tm, LANES), lambda i: (i, 0))],
        out_shape=[jax.ShapeDtypeStruct((t, d), F32),
                   jax.ShapeDtypeStruct((t, LANES), F32)],
        compiler_params=_cparams("parallel"),
        name="norm2_router",
    )(x, g.reshape(1, d), mod, mod, wr, br)


def _dispatch_kernel(dest_ref, zt_ref, h_ref, xs_ref, zero_ref, sem, zsem, *, tm, t, expert_tile):
    i = pl.program_id(0)

    @pl.when(i == 0)
    def _():
        zero_ref[...] = jnp.zeros(zero_ref.shape, zero_ref.dtype)

        def clear(q):
            row = pl.multiple_of(jnp.maximum(zt_ref[q], 0), expert_tile)
            return pltpu.make_async_copy(zero_ref, xs_ref.at[pl.ds(row, expert_tile), :], zsem)

        for q in range(2 * N_EXPERTS):
            @pl.when(zt_ref[q] >= 0)
            def _():
                clear(q).start()

        for q in range(2 * N_EXPERTS):
            @pl.when(zt_ref[q] >= 0)
            def _():
                clear(q).wait()

    base = i * tm

    def body(r, c):
        for k in range(2):
            row = dest_ref[k * t + base + r]
            pltpu.make_async_copy(h_ref.at[pl.ds(r, 1), :], xs_ref.at[pl.ds(row, 1), :], sem).start()
        return c

    lax.fori_loop(0, tm, body, 0, unroll=8)
    for k in range(2):
        pltpu.make_async_copy(h_ref, xs_ref.at[pl.ds(0, tm), :], sem).wait()


def _dispatch_call(dest, zero_tiles, h, n_rows, expert_tile, tm=256):
    t, d = h.shape
    return pl.pallas_call(
        functools.partial(_dispatch_kernel, tm=tm, t=t, expert_tile=expert_tile),
        grid_spec=pltpu.PrefetchScalarGridSpec(
            num_scalar_prefetch=2,
            grid=(t // tm,),
            in_specs=[pl.BlockSpec((tm, d), lambda i, dest, zt: (i, 0))],
            out_specs=pl.BlockSpec(memory_space=pl.ANY),
            scratch_shapes=[pltpu.VMEM((expert_tile, d), h.dtype),
                            pltpu.SemaphoreType.DMA(()),
                            pltpu.SemaphoreType.DMA(())]),
        out_shape=jax.ShapeDtypeStruct((n_rows, d), h.dtype),
        compiler_params=_cparams("arbitrary", disable_bounds_checks=True),
        name="moe_dispatch_rows",
    )(dest, zero_tiles, h)


def _expert_up_kernel(te_ref, nv_ref, x_ref, wg_ref, wu_ref, o_ref, *, tm):
    valid = pl.program_id(0) * tm < nv_ref[0]

    @pl.when(valid)
    def _():
        x = x_ref[...].astype(BF16)
        hg = jnp.dot(x, wg_ref[...], preferred_element_type=F32)
        hu = jnp.dot(x, wu_ref[...], preferred_element_type=F32)
        o_ref[...] = (hg * _sigmoid(hg) * hu).astype(o_ref.dtype)

    @pl.when(jnp.logical_not(valid))
    def _():
        o_ref[...] = jnp.zeros(o_ref.shape, o_ref.dtype)


def _expert_up_call(tile_e, n_valid, xs, wg, wu, tm):
    n, d = xs.shape
    ff = wg.shape[2]
    return pl.pallas_call(
        functools.partial(_expert_up_kernel, tm=tm),
        grid_spec=pltpu.PrefetchScalarGridSpec(
            num_scalar_prefetch=2,
            grid=(n // tm,),
            in_specs=[pl.BlockSpec((tm, d), lambda m, te, nv: (m, 0)),
                      pl.BlockSpec((None, d, ff), lambda m, te, nv: (te[m], 0, 0)),
                      pl.BlockSpec((None, d, ff), lambda m, te, nv: (te[m], 0, 0))],
            out_specs=pl.BlockSpec((tm, ff), lambda m, te, nv: (m, 0))),
        out_shape=jax.ShapeDtypeStruct((n, ff), BF16),
        compiler_params=_cparams("arbitrary"),
        name="moe_expert_up",
    )(tile_e, n_valid, xs, wg, wu)


def _expert_down_kernel(te_ref, nv_ref, a_ref, wd_ref, o_ref, *, tm):
    valid = pl.program_id(0) * tm < nv_ref[0]

    @pl.when(valid)
    def _():
        o_ref[...] = jnp.dot(a_ref[...], wd_ref[...], preferred_element_type=F32)

    @pl.when(jnp.logical_not(valid))
    def _():
        o_ref[...] = jnp.zeros(o_ref.shape, o_ref.dtype)


def _expert_down_call(tile_e, n_valid, a, wd, tm):
    n, ff = a.shape
    d = wd.shape[2]
    return pl.pallas_call(
        functools.partial(_expert_down_kernel, tm=tm),
        grid_spec=pltpu.PrefetchScalarGridSpec(
            num_scalar_prefetch=2,
            grid=(n // tm,),
            in_specs=[pl.BlockSpec((tm, ff), lambda m, te, nv: (m, 0)),
                      pl.BlockSpec((None, ff, d), lambda m, te, nv: (te[m], 0, 0))],
            out_specs=pl.BlockSpec((tm, d), lambda m, te, nv: (m, 0))),
        out_shape=jax.ShapeDtypeStruct((n, d), F32),
        compiler_params=_cparams("arbitrary"),
        name="moe_expert_down",
    )(tile_e, n_valid, a, wd)


def _combine_kernel(dest_ref, x_ref, r_ref, gate_ref, fg_ref, ys_ref, *rest, tc, t, nt, ntp, final):
    out_refs, (ybuf, sem) = rest[:-2], rest[-2:]
    i = pl.program_id(0)

    def fetch(tile, slot):
        base = tile * tc

        def body(r, c):
            for k in range(2):
                row = dest_ref[k * t + base + r]
                pltpu.make_async_copy(ys_ref.at[pl.ds(row, 1), :], ybuf.at[slot, pl.ds(k * tc + r, 1), :],
                                      sem.at[slot]).start()
            return c

        lax.fori_loop(0, tc, body, 0, unroll=8)

    @pl.when(i == 0)
    def _():
        fetch(0, 0)

    @pl.when(i + 1 < nt)
    def _():
        fetch(i + 1, (i + 1) % 2)

    slot = i % 2
    pltpu.make_async_copy(ys_ref.at[pl.ds(0, 2 * tc), :], ybuf.at[slot], sem.at[slot]).wait()

    w1 = r_ref[:, 2:3]
    w2 = r_ref[:, 3:4]
    y = w1 * ybuf[slot, 0:tc, :] + w2 * ybuf[slot, tc:2 * tc, :]
    x = x_ref[...] + gate_ref[...] * y
    if final:
        x = _rms(x) * fg_ref[...]
        op_ref, os_ref = out_refs

        @pl.when(i < ntp)
        def _():
            op_ref[...] = x

        @pl.when(i >= ntp)
        def _():
            os_ref[...] = x
    else:
        out_refs[0][...] = x


def _combine_call(dest, x, route, ys, mod, final_g, final, t_prompt, dec_seq, tc=128):
    t, d = x.shape
    seg = lambda i, *_: _seg_of_row(i * tc, t_prompt, dec_seq)
    nt = t // tc
    ntp = t_prompt // tc
    if final:
        out_specs = [pl.BlockSpec((tc, d), lambda i, dest: (jnp.minimum(i, ntp - 1), 0)),
                     pl.BlockSpec((tc, d), lambda i, dest: (jnp.maximum(i - ntp, 0), 0))]
        out_shape = [jax.ShapeDtypeStruct((t_prompt, d), F32), jax.ShapeDtypeStruct((t - t_prompt, d), F32)]
    else:
        out_specs = pl.BlockSpec((tc, d), lambda i, dest: (i, 0))
        out_shape = jax.ShapeDtypeStruct((t, d), F32)
    return pl.pallas_call(
        functools.partial(_combine_kernel, tc=tc, t=t, nt=nt, ntp=ntp, final=final),
        grid_spec=pltpu.PrefetchScalarGridSpec(
            num_scalar_prefetch=1,
            grid=(nt,),
            in_specs=[pl.BlockSpec((tc, d), lambda i, dest: (i, 0)),
                      pl.BlockSpec((tc, LANES), lambda i, dest: (i, 0)),
                      pl.BlockSpec((None, 1, d), lambda i, dest: (seg(i), 0, 5)),
                      pl.BlockSpec((1, d), lambda i, dest: (0, 0)),
                      pl.BlockSpec(memory_space=pl.ANY)],
            out_specs=out_specs,
            scratch_shapes=[pltpu.VMEM((2, 2 * tc, d), ys.dtype),
                            pltpu.SemaphoreType.DMA((2,))]),
        out_shape=out_shape,
        compiler_params=_cparams("arbitrary", disable_bounds_checks=True),
        name="moe_combine_residual",
    )(dest, x, route, mod, final_g.reshape(1, d), ys)


def _dispatch_plan(route, tm):
    t = route.shape[0]
    flat_e = route[:, 0:2].astype(jnp.int32).T.reshape(-1)
    n_pairs = 2 * t
    n_tiles = n_pairs // tm + N_EXPERTS
    onehot = (flat_e[:, None] == jnp.arange(N_EXPERTS, dtype=jnp.int32)[None, :]).astype(jnp.int32)
    running = jnp.cumsum(onehot, axis=0)
    counts = running[-1]
    rank = jnp.sum((running - 1) * onehot, axis=1)
    padded = ((counts + tm - 1) // tm) * tm
    pad_end = jnp.cumsum(padded)
    pad_start = pad_end - padded
    dest = (jnp.sum(onehot * pad_start[None, :], axis=1) + rank).astype(jnp.int32)
    tile_start = jnp.arange(n_tiles, dtype=jnp.int32) * tm
    tile_e = jnp.sum((pad_end[None, :] <= tile_start[:, None]).astype(jnp.int32), axis=1)
    tile_e = jnp.minimum(tile_e, N_EXPERTS - 1).astype(jnp.int32)
    n_valid = pad_end[-1:].astype(jnp.int32)
    last = jnp.where(counts > 0, pad_end - tm, -1)
    tail = pad_end[-1] + jnp.arange(N_EXPERTS, dtype=jnp.int32) * tm
    tail = jnp.where(tail < n_tiles * tm, tail, -1)
    zero_tiles = jnp.concatenate([last, tail]).astype(jnp.int32)
    return dest, tile_e, n_valid, zero_tiles, n_tiles * tm


def _moe(h, route, wg, wu, wd, tm=256):
    dest, tile_e, n_valid, zero_tiles, n_rows = _dispatch_plan(route, tm)
    xs = _dispatch_call(dest, zero_tiles, h, n_rows, tm)
    a = _expert_up_call(tile_e, n_valid, xs, wg, wu, tm)
    ys = _expert_down_call(tile_e, n_valid, a, wd, tm)
    return ys, dest


def _trunk(x_prompt, x_sample, cache_k, cache_v, c, c_ctx, prm, final_norm_g, hy_block_sample=512):
    batch, seq, d = x_prompt.shape
    dec_batch, dec_seq, _ = x_sample.shape
    depth = prm['w_in'].shape[0]
    past = cache_k.shape[2]
    t_prompt = batch * seq
    t_sample = dec_batch * dec_seq

    x = jnp.concatenate([x_prompt.reshape(t_prompt, d), x_sample.reshape(t_sample, d)], axis=0)
    n_cond = 1 + dec_batch
    cond = jnp.concatenate([c_ctx[None, :], c, jnp.zeros((8 - n_cond % 8, d), F32)], axis=0)
    mod_all = _mod_call(cond, prm['w_mod'], prm['b_mod'])
    cos, sin_signed = _rope_tables(dec_seq)
    sconv_rows_p = math.gcd(t_prompt, max(seq, 4096 // seq * seq))

    new_k, new_v = [], []
    for l in range(depth):
        p = {name: w[l] for name, w in prm.items()}
        mod = mod_all[l, :n_cond].reshape(n_cond, 1, 6 * d)

        z = _norm_proj_call(x, p['norm1_g'], mod, p['w_in'].astype(BF16), t_prompt, dec_seq)

        attn_p, k_l, v_l = _attn_prompt_call(z, p['q_norm_g'], p['k_norm_g'], batch, seq)
        attn_s = _attn_sample_call(z, cache_k[:, l].reshape(dec_batch, past, KV_DIM),
                                   cache_v[:, l].reshape(dec_batch, past, KV_DIM),
                                   cos, sin_signed, p['q_norm_g'], p['k_norm_g'], t_prompt, dec_batch, dec_seq)
        new_k.append(k_l.reshape(batch, seq, N_KV_HEADS, HEAD_DIM))
        new_v.append(v_l.reshape(batch, seq, N_KV_HEADS, HEAD_DIM))

        hy_p = _hyena_group(z, 0, batch, seq, seq, p, sconv_rows_p)
        hy_s = _hyena_group(z, t_prompt, dec_batch, dec_seq, min(hy_block_sample, dec_seq), p, dec_seq)

        gm = _gmlp_call(z, p['gm_norm_g'], p['gm_norm_b'], p['gm_ws'], p['gm_bs'])

        x = _out_proj_call(attn_p, attn_s, hy_p, hy_s, gm, p['out_norm_g'], p['w_out'].astype(BF16), x, mod,
                           t_prompt, dec_seq)

        wr = jnp.concatenate([p['router_g_w'],
                              p['router_e_w'].transpose(1, 0, 2).reshape(d, N_EXPERTS)], axis=1)
        wr = jnp.pad(wr, ((0, 0), (0, LANES - wr.shape[1])))
        br = jnp.pad(jnp.concatenate([p['router_g_b'], p['router_e_b'].reshape(-1)]),
                     (0, LANES - N_GROUPS - N_EXPERTS)).reshape(1, LANES)
        h2, route = _router_call(x, p['norm2_g'], mod, wr, br, t_prompt, dec_seq)
        ys, dest = _moe(h2, route, p['exp_w_gate'].astype(BF16), p['exp_w_up'].astype(BF16),
                        p['exp_w_down'].astype(BF16))
        x = _combine_call(dest, x, route, ys, mod, final_norm_g, l == depth - 1, t_prompt, dec_seq)

    y_prompt = x[0].reshape(batch, seq, d)
    y_sample = x[1].reshape(dec_batch, dec_seq, d)
    return y_prompt, y_sample, jnp.stack(new_k, axis=1), jnp.stack(new_v, axis=1)


_PARAM_NAMES = ('norm1_g', 'norm2_g', 'w_mod', 'b_mod', 'w_in', 'q_norm_g', 'k_norm_g', 'hy_conv_w', 'hy_conv_b',
                'hf_w1', 'hf_b1', 'hf_w2', 'hf_b2', 'hf_w3', 'hf_b3', 'hf_freq', 'hy_bias', 'gm_norm_g',
                'gm_norm_b', 'gm_ws', 'gm_bs', 'out_norm_g', 'w_out', 'router_g_w', 'router_g_b', 'router_e_w',
                'router_e_b', 'exp_w_gate', 'exp_w_up', 'exp_w_down')


def kernel(x_prompt, x_sample, cache_k, cache_v, c, c_ctx, norm1_g, norm2_g, w_mod, b_mod, w_in, q_norm_g, k_norm_g, hy_conv_w, hy_conv_b, hf_w1, hf_b1, hf_w2, hf_b2, hf_w3, hf_b3, hf_freq, hy_bias, gm_norm_g, gm_norm_b, gm_ws, gm_bs, out_norm_g, w_out, router_g_w, router_g_b, router_e_w, router_e_b, exp_w_gate, exp_w_up, exp_w_down, final_norm_g):
    values = (norm1_g, norm2_g, w_mod, b_mod, w_in, q_norm_g, k_norm_g, hy_conv_w, hy_conv_b, hf_w1, hf_b1, hf_w2,
              hf_b2, hf_w3, hf_b3, hf_freq, hy_bias, gm_norm_g, gm_norm_b, gm_ws, gm_bs, out_norm_g, w_out,
              router_g_w, router_g_b, router_e_w, router_e_b, exp_w_gate, exp_w_up, exp_w_down)
    prm = dict(zip(_PARAM_NAMES, values))
    return _trunk(x_prompt, x_sample, cache_k, cache_v, c, c_ctx, prm, final_norm_g)
```

```python
import functools
import math

import numpy as np
import jax
import jax.numpy as jnp
from jax import lax
from jax.experimental import pallas as pl
from jax.experimental.pallas import tpu as pltpu

F32 = jnp.float32
BF16 = jnp.bfloat16

D_MODEL = 4096
GRID_W = 64
HEAD_DIM = 128
N_HEADS = 16
N_KV_HEADS = 4
Q_PER_KV = N_HEADS // N_KV_HEADS
ATTN_DIM = N_HEADS * HEAD_DIM
KV_DIM = N_KV_HEADS * HEAD_DIM
HY_DIM = 1024
GM_DIM = 1024
GM_HEADS = 8
CHUNK = 128
PROJ_DIM = ATTN_DIM + 2 * KV_DIM + 3 * HY_DIM + 2 * GM_DIM
ROPE_THETA = 10000.0
FILTER_EMB = 33
FILTER_HIDDEN = 64
DECAY_TARGET = 1e-2
FAST_DECAY_PCT = 0.3
SLOW_DECAY_PCT = 1.5
MOD_SHIFT = 0.05
N_GROUPS = 4
EXPERTS_PER_GROUP = 4
N_EXPERTS = 16
EXPERT_FF = 1024
EPS = 1e-6

COL_K = ATTN_DIM
COL_V = ATTN_DIM + KV_DIM
COL_HY = ATTN_DIM + 2 * KV_DIM
COL_GM = COL_HY + 3 * HY_DIM

LANES = 128
VMEM_LIMIT = 56 * 1024 * 1024
HI = lax.Precision.HIGHEST


def _cparams(*sem, **kw):
    return pltpu.CompilerParams(dimension_semantics=sem, vmem_limit_bytes=VMEM_LIMIT, **kw)


def _rms(x):
    return x * lax.rsqrt(jnp.mean(x * x, axis=-1, keepdims=True) + EPS)


def _sigmoid(x):
    return 1.0 / (1.0 + jnp.exp(-x))


def _pack_bf16_pairs(x):
    n = x.shape[1] // 2
    lo = lax.bitcast_convert_type(x[:, :n].astype(BF16).astype(F32), jnp.uint32)
    hi = lax.bitcast_convert_type(x[:, n:].astype(BF16).astype(F32), jnp.uint32)
    return hi | (lo >> 16)


def _unpack_bf16_pairs(u):
    lo = lax.bitcast_convert_type(u << 16, F32)
    hi = lax.bitcast_convert_type(u & jnp.uint32(0xFFFF0000), F32)
    return jnp.concatenate([lo, hi], axis=1)


def _seg_of_row(row, t_prompt, dec_seq):
    return jnp.where(row < t_prompt, 0, 1 + (row - t_prompt) // dec_seq)


def _mod_kernel(c_ref, w_ref, b_ref, o_ref):
    c = c_ref[...]
    s = (c * _sigmoid(c)).astype(BF16)
    o_ref[...] = jnp.dot(s, w_ref[...].astype(BF16), preferred_element_type=F32) + b_ref[...]


def _mod_call(cond, w_mod, b_mod):
    depth, d, n = w_mod.shape
    r = cond.shape[0]
    tn = 512
    return pl.pallas_call(
        _mod_kernel,
        grid=(depth, n // tn),
        in_specs=[pl.BlockSpec((r, d), lambda l, j: (0, 0)),
                  pl.BlockSpec((None, d, tn), lambda l, j: (l, 0, j)),
                  pl.BlockSpec((None, 1, tn), lambda l, j: (l, 0, j))],
        out_specs=pl.BlockSpec((None, r, tn), lambda l, j: (l, 0, j)),
        out_shape=jax.ShapeDtypeStruct((depth, r, n), F32),
        compiler_params=_cparams("parallel", "parallel"),
        name="adaln_mod",
    )(cond, w_mod, b_mod.reshape(depth, 1, n))


def _norm_proj_kernel(x_ref, g_ref, sh_ref, sc_ref, w_ref, o_ref, h_ref):
    @pl.when(pl.program_id(1) == 0)
    def _():
        y = _rms(x_ref[...]) * g_ref[...]
        h_ref[...] = (y * (1.0 + sc_ref[...]) + sh_ref[...]).astype(BF16)

    o_ref[...] = jnp.dot(h_ref[...], w_ref[...], preferred_element_type=F32)


def _norm_proj_call(x, g, mod, w, layer, t_prompt, dec_seq, tm=512, tn=1024):
    t, d = x.shape
    n = w.shape[2]
    seg = lambda i: _seg_of_row(i * tm, t_prompt, dec_seq)
    return pl.pallas_call(
        _norm_proj_kernel,
        grid=(t // tm, n // tn),
        in_specs=[pl.BlockSpec((tm, d), lambda i, j: (i, 0)),
                  pl.BlockSpec((1, d), lambda i, j: (0, 0)),
                  pl.BlockSpec((None, 1, d), lambda i, j: (seg(i), 0, 0)),
                  pl.BlockSpec((None, 1, d), lambda i, j: (seg(i), 0, 1)),
                  pl.BlockSpec((None, d, tn), lambda i, j: (layer, 0, j))],
        out_specs=pl.BlockSpec((tm, tn), lambda i, j: (i, j)),
        out_shape=jax.ShapeDtypeStruct((t, n), F32),
        scratch_shapes=[pltpu.VMEM((tm, d), BF16)],
        compiler_params=_cparams("parallel", "arbitrary"),
        name="norm1_in_proj",
    )(x, g.reshape(1, d), mod, mod, w)


def _attn_prompt_kernel(q_ref, k_ref, v_ref, qg_ref, kg_ref, o_ref, kc_ref, vc_ref):
    kn = _rms(k_ref[...]) * kg_ref[...]
    v = v_ref[...]
    kc_ref[...] = kn
    vc_ref[...] = v
    kb = kn.astype(BF16)
    vb = v.astype(BF16)
    scale = HEAD_DIM ** -0.5
    for m in range(Q_PER_KV):
        sl = slice(m * HEAD_DIM, (m + 1) * HEAD_DIM)
        qn = (_rms(q_ref[:, sl]) * qg_ref[...] * scale).astype(BF16)
        s = lax.dot_general(qn, kb, (((1,), (1,)), ((), ())), preferred_element_type=F32)
        p = jnp.exp(s - jnp.max(s, axis=-1, keepdims=True))
        l = jnp.sum(p, axis=-1, keepdims=True)
        o = jnp.dot(p.astype(BF16), vb, preferred_element_type=F32)
        o_ref[:, sl] = o / l


def _attn_prompt_call(z, qg, kg, batch, seq):
    gw = Q_PER_KV * HEAD_DIM
    tp = batch * seq
    return pl.pallas_call(
        _attn_prompt_kernel,
        grid=(batch, N_KV_HEADS),
        in_specs=[pl.BlockSpec((seq, gw), lambda b, g: (b, g)),
                  pl.BlockSpec((seq, HEAD_DIM), lambda b, g: (b, COL_K // HEAD_DIM + g)),
                  pl.BlockSpec((seq, HEAD_DIM), lambda b, g: (b, COL_V // HEAD_DIM + g)),
                  pl.BlockSpec((1, HEAD_DIM), lambda b, g: (0, 0)),
                  pl.BlockSpec((1, HEAD_DIM), lambda b, g: (0, 0))],
        out_specs=[pl.BlockSpec((seq, gw), lambda b, g: (b, g)),
                   pl.BlockSpec((seq, HEAD_DIM), lambda b, g: (b, g)),
                   pl.BlockSpec((seq, HEAD_DIM), lambda b, g: (b, g))],
        out_shape=[jax.ShapeDtypeStruct((tp, ATTN_DIM), F32),
                   jax.ShapeDtypeStruct((tp, KV_DIM), F32),
                   jax.ShapeDtypeStruct((tp, KV_DIM), F32)],
        compiler_params=_cparams("parallel", "parallel"),
        name="attn_context",
    )(z, z, z, qg.reshape(1, HEAD_DIM), kg.reshape(1, HEAD_DIM))


def _rope(x, cos, sin_signed):
    lane = lax.broadcasted_iota(jnp.int32, x.shape, 1)
    quarter = HEAD_DIM // 4
    fwd = pltpu.roll(x, HEAD_DIM - quarter, 1)
    bwd = pltpu.roll(x, quarter, 1)
    swapped = jnp.where((lane % (2 * quarter)) < quarter, fwd, bwd)
    return x * cos + swapped * sin_signed


def _krope_kernel(k_ref, c_ref, s_ref, kg_ref, o_ref):
    kn = _rms(k_ref[...]) * kg_ref[...]
    o_ref[...] = _rope(kn, c_ref[...], s_ref[...]).astype(o_ref.dtype)


def _krope_call(z, cos, sin_signed, kg, t_prompt, dec_batch, dec_seq, tr=512):
    nr = dec_seq // tr
    return pl.pallas_call(
        _krope_kernel,
        grid=(dec_batch, nr, N_KV_HEADS),
        in_specs=[pl.BlockSpec((tr, HEAD_DIM), lambda b, r, g: (t_prompt // tr + b * nr + r, COL_K // HEAD_DIM + g)),
                  pl.BlockSpec((tr, HEAD_DIM), lambda b, r, g: (r, 0)),
                  pl.BlockSpec((tr, HEAD_DIM), lambda b, r, g: (r, 0)),
                  pl.BlockSpec((1, HEAD_DIM), lambda b, r, g: (0, 0))],
        out_specs=pl.BlockSpec((tr, HEAD_DIM), lambda b, r, g: (b * nr + r, g)),
        out_shape=jax.ShapeDtypeStruct((dec_batch * dec_seq, KV_DIM), BF16),
        compiler_params=_cparams("parallel", "parallel", "parallel"),
        name="attn_key_rope",
    )(z, cos, sin_signed, kg.reshape(1, HEAD_DIM))


def _attn_sample_kernel(q_ref, k_ref, v_ref, ck_ref, cv_ref, cq_ref, sq_ref,
                        qg_ref, o_ref, qs_ref, m_ref, l_ref, acc_ref):
    j = pl.program_id(3)
    scale = HEAD_DIM ** -0.5

    @pl.when(j == 0)
    def _():
        for m in range(Q_PER_KV):
            sl = slice(m * HEAD_DIM, (m + 1) * HEAD_DIM)
            qn = _rms(q_ref[:, sl]) * qg_ref[...]
            qs_ref[m] = (_rope(qn, cq_ref[...], sq_ref[...]) * scale).astype(BF16)
        m_ref[...] = jnp.full(m_ref.shape, -jnp.inf, F32)
        l_ref[...] = jnp.zeros(l_ref.shape, F32)
        acc_ref[...] = jnp.zeros(acc_ref.shape, F32)

    def step(kb, vb):
        reps = kb.shape[0] // LANES
        for m in range(Q_PER_KV):
            s = lax.dot_general(qs_ref[m], kb, (((1,), (1,)), ((), ())), preferred_element_type=F32)
            m_prev = m_ref[m]
            m_next = jnp.maximum(m_prev, jnp.max(s, axis=-1, keepdims=True))
            alpha = jnp.exp(m_prev - m_next)
            p = jnp.exp(s - jnp.concatenate([m_next] * reps, axis=1))
            l_ref[m] = alpha * l_ref[m] + jnp.sum(p, axis=-1, keepdims=True)
            acc_ref[m] = alpha * acc_ref[m] + jnp.dot(p.astype(BF16), vb, preferred_element_type=F32)
            m_ref[m] = m_next

    @pl.when(j == 0)
    def _():
        step(ck_ref[...].astype(BF16), cv_ref[...].astype(BF16))

    @pl.when(j > 0)
    def _():
        step(k_ref[...], v_ref[...].astype(BF16))

    @pl.when(j == pl.num_programs(3) - 1)
    def _():
        for m in range(Q_PER_KV):
            o_ref[:, m * HEAD_DIM:(m + 1) * HEAD_DIM] = acc_ref[m] / l_ref[m]


def _attn_sample_call(z, kr, ctx_k, ctx_v, cos, sin_signed, qg, t_prompt, dec_batch, dec_seq, tq=512, tk=1024):
    gw = Q_PER_KV * HEAD_DIM
    past = ctx_k.shape[1]
    tk = min(tk, dec_seq)
    nq = dec_seq // tq
    nk = dec_seq // tk
    qrow = lambda b, i: (t_prompt + b * dec_seq) // tq + i
    vrow = lambda b, j: (t_prompt + b * dec_seq) // tk + jnp.maximum(j - 1, 0)
    krow = lambda b, j: b * nk + jnp.maximum(j - 1, 0)
    return pl.pallas_call(
        _attn_sample_kernel,
        grid=(dec_batch, N_KV_HEADS, nq, nk + 1),
        in_specs=[pl.BlockSpec((tq, gw), lambda b, g, i, j: (qrow(b, i), g)),
                  pl.BlockSpec((tk, HEAD_DIM), lambda b, g, i, j: (krow(b, j), g)),
                  pl.BlockSpec((tk, HEAD_DIM), lambda b, g, i, j: (vrow(b, j), COL_V // HEAD_DIM + g)),
                  pl.BlockSpec((None, past, HEAD_DIM), lambda b, g, i, j: (b, 0, g)),
                  pl.BlockSpec((None, past, HEAD_DIM), lambda b, g, i, j: (b, 0, g)),
                  pl.BlockSpec((tq, HEAD_DIM), lambda b, g, i, j: (i, 0)),
                  pl.BlockSpec((tq, HEAD_DIM), lambda b, g, i, j: (i, 0)),
                  pl.BlockSpec((1, HEAD_DIM), lambda b, g, i, j: (0, 0))],
        out_specs=pl.BlockSpec((tq, gw), lambda b, g, i, j: (b * nq + i, g)),
        out_shape=jax.ShapeDtypeStruct((dec_batch * dec_seq, ATTN_DIM), F32),
        scratch_shapes=[pltpu.VMEM((Q_PER_KV, tq, HEAD_DIM), BF16),
                        pltpu.VMEM((Q_PER_KV, tq, LANES), F32),
                        pltpu.VMEM((Q_PER_KV, tq, LANES), F32),
                        pltpu.VMEM((Q_PER_KV, tq, HEAD_DIM), F32)],
        compiler_params=_cparams("parallel", "parallel", "parallel", "arbitrary"),
        name="attn_latent",
    )(z, kr, z, ctx_k, ctx_v, cos, sin_signed, qg.reshape(1, HEAD_DIM))


def _rope_tables(n_tokens):
    quarter = HEAD_DIM // 4
    n_rows = n_tokens // GRID_W
    row = jnp.repeat(jnp.arange(n_rows, dtype=F32), GRID_W)
    col = jnp.tile(jnp.arange(GRID_W, dtype=F32), n_rows)
    freqs = ROPE_THETA ** (-jnp.arange(quarter, dtype=F32) / quarter)
    ar = row[:, None] * freqs[None, :]
    ac = col[:, None] * freqs[None, :]
    cos = jnp.concatenate([jnp.cos(ar), jnp.cos(ar), jnp.cos(ac), jnp.cos(ac)], axis=-1)
    sin = jnp.concatenate([-jnp.sin(ar), jnp.sin(ar), -jnp.sin(ac), jnp.sin(ac)], axis=-1)
    return cos, sin


def _gelu_tanh(x):
    return 0.5 * x * (1.0 + jnp.tanh(math.sqrt(2.0 / math.pi) * (x + 0.044715 * (x * x * x))))


def _gmlp_kernel(z_ref, g_ref, b_ref, ws_ref, bs_ref, o_ref):
    hw = GM_DIM // GM_HEADS
    for ch in range(z_ref.shape[0] // CHUNK):
        rows = slice(ch * CHUNK, (ch + 1) * CHUNK)
        u = _gelu_tanh(z_ref[rows, :GM_DIM])
        v = _gelu_tanh(z_ref[rows, GM_DIM:])
        mu = jnp.mean(v, axis=-1, keepdims=True)
        vc = v - mu
        var = jnp.mean(vc * vc, axis=-1, keepdims=True)
        vn = (vc * lax.rsqrt(var + EPS) * g_ref[...] + b_ref[...]).astype(BF16)
        for h in range(GM_HEADS):
            cols = slice(h * hw, (h + 1) * hw)
            s = jnp.dot(ws_ref[h], vn[:, cols], preferred_element_type=F32) + bs_ref[h]
            o_ref[rows, cols] = u[:, cols] * s


def _gmlp_call(z, g, b, ws, bs, tr=512):
    t = z.shape[0]
    return pl.pallas_call(
        _gmlp_kernel,
        grid=(t // tr,),
        in_specs=[pl.BlockSpec((tr, 2 * GM_DIM), lambda i: (i, COL_GM // (2 * GM_DIM))),
                  pl.BlockSpec((1, GM_DIM), lambda i: (0, 0)),
                  pl.BlockSpec((1, GM_DIM), lambda i: (0, 0)),
                  pl.BlockSpec((GM_HEADS, CHUNK, CHUNK), lambda i: (0, 0, 0)),
                  pl.BlockSpec((GM_HEADS, CHUNK, 1), lambda i: (0, 0, 0))],
        out_specs=pl.BlockSpec((tr, GM_DIM), lambda i: (i, 0)),
        out_shape=jax.ShapeDtypeStruct((t, GM_DIM), F32),
        compiler_params=_cparams("parallel"),
        name="chunk_gmlp",
    )(z, g.reshape(1, GM_DIM), b.reshape(1, GM_DIM), ws.astype(BF16), bs.reshape(GM_HEADS, CHUNK, 1))


def _sconv_kernel(z_ref, w_ref, b_ref, o_ref, *, seg_len):
    x = z_ref[...]
    n = x.shape[0]
    pos = lax.broadcasted_iota(jnp.int32, x.shape, 0) % seg_len
    prev = jnp.where(pos == 0, 0.0, pltpu.roll(x, 1, 0))
    nxt = jnp.where(pos == seg_len - 1, 0.0, pltpu.roll(x, n - 1, 0))
    o_ref[...] = b_ref[...] + prev * w_ref[0:1, :] + x * w_ref[1:2, :] + nxt * w_ref[2:3, :]


def _sconv_call(z, w, b, row0, n_rows, seg_len, block_rows, cw=256):
    width = 3 * HY_DIM
    return pl.pallas_call(
        functools.partial(_sconv_kernel, seg_len=seg_len),
        grid=(n_rows // block_rows, width // cw),
        in_specs=[pl.BlockSpec((block_rows, cw), lambda i, c: (row0 // block_rows + i, COL_HY // cw + c)),
                  pl.BlockSpec((3, cw), lambda i, c: (0, c)),
                  pl.BlockSpec((1, cw), lambda i, c: (0, c))],
        out_specs=pl.BlockSpec((block_rows, cw), lambda i, c: (i, c)),
        out_shape=jax.ShapeDtypeStruct((n_rows, width), F32),
        compiler_params=_cparams("parallel", "parallel"),
        name="hyena_short_conv",
    )(z, w, b.reshape(1, width))


def _filter_kernel(z_ref, t_ref, w1_ref, b1_ref, w2_ref, b2_ref, w3a_ref, b3a_ref, w3b_ref, b3b_ref,
                   fr_ref, dl_ref, o_ref, *, tr):
    fr = fr_ref[...]
    h = jnp.sin(fr * (jnp.dot(z_ref[...], w1_ref[...], precision=HI, preferred_element_type=F32) + b1_ref[...]))
    h = jnp.sin(fr * (jnp.dot(h, w2_ref[...], precision=HI, preferred_element_type=F32) + b2_ref[...]))
    decay = jnp.exp(-t_ref[...] * dl_ref[...]) + MOD_SHIFT
    row = lax.broadcasted_iota(jnp.int32, decay.shape, 0) + pl.program_id(0) * tr
    decay = jnp.where(row == 0, 0.0, decay)
    for o, (w3_ref, b3_ref) in enumerate(((w3a_ref, b3a_ref), (w3b_ref, b3b_ref))):
        taps = jnp.dot(h, w3_ref[...], precision=HI, preferred_element_type=F32) + b3_ref[...]
        o_ref[:, o * HY_DIM:(o + 1) * HY_DIM] = taps * decay


def _filter_call(length, w1, b1, w2, b2, w3, b3, fr, tr=256):
    bands = (FILTER_EMB - 1) // 2
    t = np.linspace(0.0, 1.0, length)[:, None]
    wv = 2.0 * np.pi * np.arange(length)[:, None] / length
    f = np.linspace(1e-4, bands - 1, bands)[None, :]
    zf = np.concatenate([t, np.cos(f * wv), -np.sin(f * wv)], axis=-1)
    min_decay = math.log(DECAY_TARGET) / SLOW_DECAY_PCT
    max_decay = math.log(DECAY_TARGET) / FAST_DECAY_PCT
    deltas = np.abs(np.linspace(min_decay, max_decay, HY_DIM))[None, :]
    lag = np.minimum(np.abs(np.arange(2 * length) - length), length - 1)
    pad = LANES - FILTER_EMB
    hp = LANES - FILTER_HIDDEN
    z2 = jnp.asarray(np.pad(zf[lag], ((0, 0), (0, pad))).astype(np.float32))
    t2 = jnp.asarray(t[lag].astype(np.float32))
    dl = jnp.asarray(deltas.astype(np.float32))
    w1p = jnp.pad(w1, ((0, pad), (0, hp)))
    w2p = jnp.pad(w2, ((0, hp), (0, hp)))
    w3p = jnp.pad(w3, ((0, hp), (0, 0)))
    b1p = jnp.pad(b1, (0, hp)).reshape(1, LANES)
    b2p = jnp.pad(b2, (0, hp)).reshape(1, LANES)
    frp = jnp.pad(fr, (0, hp)).reshape(1, LANES)
    b3r = b3.reshape(1, -1)
    nblk = 2 * length // tr
    half = length // tr
    wcol = lambda r, o: o * 2 + jnp.where(r < half, 1, 0)
    small = pl.BlockSpec((1, LANES), lambda r: (0, 0))
    square = pl.BlockSpec((LANES, LANES), lambda r: (0, 0))
    return pl.pallas_call(
        functools.partial(_filter_kernel, tr=tr),
        grid=(nblk,),
        in_specs=[pl.BlockSpec((tr, LANES), lambda r: (r, 0)),
                  pl.BlockSpec((tr, 1), lambda r: (r, 0)),
                  square, small, square, small,
                  pl.BlockSpec((LANES, HY_DIM), lambda r: (0, wcol(r, 0))),
                  pl.BlockSpec((1, HY_DIM), lambda r: (0, wcol(r, 0))),
                  pl.BlockSpec((LANES, HY_DIM), lambda r: (0, wcol(r, 1))),
                  pl.BlockSpec((1, HY_DIM), lambda r: (0, wcol(r, 1))),
                  small,
                  pl.BlockSpec((1, HY_DIM), lambda r: (0, 0))],
        out_specs=pl.BlockSpec((tr, 2 * HY_DIM), lambda r: (r, 0)),
        out_shape=jax.ShapeDtypeStruct((2 * length, 2 * HY_DIM), F32),
        compiler_params=_cparams("parallel"),
        name="hyena_filter_mlp",
    )(z2, t2, w1p, b1p, w2p, b2p, w3p, b3r, w3p, b3r, frp, dl)


def _dft_constants(p):
    n = 2 * p
    idx = np.arange(p, dtype=np.float64)
    ang = 2.0 * np.pi * np.outer(idx, idx) / n
    re = np.cos(ang)
    im = -np.sin(ang)
    im[0, :] = np.cos(np.pi * idx)
    fwd = np.concatenate([re, im], axis=0)
    sign = np.where(np.arange(p) % 2 == 0, 1.0, -1.0)
    sign2 = np.concatenate([sign, sign])
    sign2[p] = 1.0
    fwd_shift = fwd * sign2[:, None]
    ar = (2.0 / n) * np.cos(ang)
    ar[:, 0] = 1.0 / n
    ai = -(2.0 / n) * np.sin(ang)
    ai[:, 0] = sign / n
    inv = np.concatenate([ar, ai], axis=1)
    as_bf16 = lambda a: jnp.asarray(a.astype(np.float32)).astype(BF16)
    return as_bf16(fwd), as_bf16(fwd_shift), as_bf16(inv)


def _dft_kernel(f_ref, x_ref, o_ref):
    o_ref[...] = jnp.dot(f_ref[...], x_ref[...].astype(BF16), preferred_element_type=F32).astype(o_ref.dtype)


def _dft_call(x, fwd, p, col0, cw=256):
    nblk = x.shape[0] // p
    return pl.pallas_call(
        _dft_kernel,
        grid=(nblk, HY_DIM // cw),
        in_specs=[pl.BlockSpec((2 * p, p), lambda i, c: (0, 0)),
                  pl.BlockSpec((p, cw), lambda i, c: (i, col0 // cw + c))],
        out_specs=pl.BlockSpec((None, 2 * p, cw), lambda i, c: (i, 0, c)),
        out_shape=jax.ShapeDtypeStruct((nblk, 2 * p, HY_DIM), BF16),
        compiler_params=_cparams("parallel", "parallel"),
        name="hyena_block_dft",
    )(fwd, x)


def _filter_dft_kernel(f_ref, fs_ref, hi_ref, lo_ref, o_ref):
    lo = lo_ref[...]
    row = lax.broadcasted_iota(jnp.int32, lo.shape, 0)
    lo = jnp.where(row == 0, 0.0, lo)
    acc = jnp.dot(f_ref[...], hi_ref[...].astype(BF16), preferred_element_type=F32)
    acc = acc + jnp.dot(fs_ref[...], lo.astype(BF16), preferred_element_type=F32)
    o_ref[...] = acc.astype(o_ref.dtype)


def _filter_dft_call(taps, fwd, fwd_shift, p, cw=256):
    nb2 = taps.shape[0] // p
    nd = nb2 - 1
    ncb = HY_DIM // cw
    return pl.pallas_call(
        _filter_dft_kernel,
        grid=(2, nd, ncb),
        in_specs=[pl.BlockSpec((2 * p, p), lambda o, d, c: (0, 0)),
                  pl.BlockSpec((2 * p, p), lambda o, d, c: (0, 0)),
                  pl.BlockSpec((p, cw), lambda o, d, c: (d + 1, o * ncb + c)),
                  pl.BlockSpec((p, cw), lambda o, d, c: (d, o * ncb + c))],
        out_specs=pl.BlockSpec((None, None, 2 * p, cw), lambda o, d, c: (o, d, 0, c)),
        out_shape=jax.ShapeDtypeStruct((2, nd, 2 * p, HY_DIM), BF16),
        compiler_params=_cparams("parallel", "parallel", "parallel"),
        name="hyena_filter_dft",
    )(fwd, fwd_shift, taps, taps)


def _specconv_kernel(xs_ref, gs_ref, inv_ref, v_ref, gate_ref, bias_ref, o_ref, y_ref, *, nb, p, rc):
    i = pl.program_id(2)
    cw = o_ref.shape[1]

    for r in range(0, p, rc):
        def body(j, carry):
            yre, yim = carry
            d = i - j + nb - 1
            xre = xs_ref[j, r:r + rc, :].astype(F32)
            xim = xs_ref[j, p + r:p + r + rc, :].astype(F32)
            gre = gs_ref[d, r:r + rc, :].astype(F32)
            gim = gs_ref[d, p + r:p + r + rc, :].astype(F32)
            return yre + (xre * gre - xim * gim), yim + (xre * gim + xim * gre)

        zero = jnp.zeros((rc, cw), F32)
        yre, yim = lax.fori_loop(0, nb, body, (zero, zero))
        y_ref[r:r + rc, :] = yre
        y_ref[p + r:p + r + rc, :] = yim

    def edge(j, carry):
        y0, yn = carry
        d = i - j + nb - 1
        x0 = xs_ref[j, 0:16, :].astype(F32)
        xn = xs_ref[j, p:p + 16, :].astype(F32)
        g0 = gs_ref[d, 0:16, :].astype(F32)
        gn = gs_ref[d, p:p + 16, :].astype(F32)
        return y0 + x0 * g0, yn + xn * gn

    zero16 = jnp.zeros((16, cw), F32)
    y0, yn = lax.fori_loop(0, nb, edge, (zero16, zero16))
    y_ref[0:1, :] = y0[0:1, :]
    y_ref[p:p + 1, :] = yn[0:1, :]

    y = jnp.dot(inv_ref[...], y_ref[...].astype(BF16), preferred_element_type=F32)
    v = v_ref[...]
    o_ref[...] = gate_ref[...] * (y + bias_ref[...] * v)


def _specconv_call(xs, gs, inv, zc, bias, nseq, nb, p, v_col0, gate_col0, v_src=None, cw=256, rc=32):
    n_rows = nseq * nb * p
    xs4 = xs.reshape(nseq, nb, 2 * p, HY_DIM)
    if v_src is None:
        v_arr, v_spec = zc, pl.BlockSpec((p, cw), lambda c, s, i: (s * nb + i, v_col0 // cw + c))
    else:
        v_arr, v_spec = v_src, pl.BlockSpec((p, cw), lambda c, s, i: (s * nb + i, c))
    return pl.pallas_call(
        functools.partial(_specconv_kernel, nb=nb, p=p, rc=rc),
        grid=(HY_DIM // cw, nseq, nb),
        in_specs=[pl.BlockSpec((None, nb, 2 * p, cw), lambda c, s, i: (s, 0, 0, c)),
                  pl.BlockSpec((2 * nb - 1, 2 * p, cw), lambda c, s, i: (0, 0, c)),
                  pl.BlockSpec((p, 2 * p), lambda c, s, i: (0, 0)),
                  v_spec,
                  pl.BlockSpec((p, cw), lambda c, s, i: (s * nb + i, gate_col0 // cw + c)),
                  pl.BlockSpec((1, cw), lambda c, s, i: (0, c))],
        out_specs=pl.BlockSpec((p, cw), lambda c, s, i: (s * nb + i, c)),
        out_shape=jax.ShapeDtypeStruct((n_rows, HY_DIM), F32),
        scratch_shapes=[pltpu.VMEM((2 * p, cw), F32)],
        compiler_params=_cparams("parallel", "parallel", "parallel"),
        name="hyena_spectral_conv",
    )(xs4, gs, inv, v_arr, zc, bias.reshape(1, HY_DIM))


def _hyena_group(z, row0, nseq, length, p, prm, sconv_rows):
    nb = length // p
    zc = _sconv_call(z, prm['hy_conv_w'], prm['hy_conv_b'], row0, nseq * length, length, sconv_rows)
    taps = _filter_call(length, prm['hf_w1'], prm['hf_b1'], prm['hf_w2'], prm['hf_b2'],
                        prm['hf_w3'], prm['hf_b3'], prm['hf_freq'])
    fwd, fwd_shift, inv = _dft_constants(p)
    gs = _filter_dft_call(taps, fwd, fwd_shift, p)
    vs = _dft_call(zc, fwd, p, 2 * HY_DIM)
    u = _specconv_call(vs, gs[0], inv, zc, prm['hy_bias'][0], nseq, nb, p, 2 * HY_DIM, 0)
    us = _dft_call(u, fwd, p, 0)
    return _specconv_call(us, gs[1], inv, zc, prm['hy_bias'][1], nseq, nb, p, 0, HY_DIM, v_src=u)


def _out_proj_kernel(ap_ref, as_ref, hp_ref, hs_ref, m_ref, g_ref, w_ref, x_ref, gate_ref, o_ref, mix_ref, *, ntp):
    i = pl.program_id(0)

    def fill(a_ref, h_ref):
        mix_ref[:, :ATTN_DIM] = (_rms(a_ref[...]) * g_ref[:, :ATTN_DIM]).astype(BF16)
        mix_ref[:, ATTN_DIM:ATTN_DIM + HY_DIM] = (
            _rms(h_ref[...]) * g_ref[:, ATTN_DIM:ATTN_DIM + HY_DIM]).astype(BF16)
        mix_ref[:, ATTN_DIM + HY_DIM:] = (_rms(m_ref[...]) * g_ref[:, ATTN_DIM + HY_DIM:]).astype(BF16)

    @pl.when(jnp.logical_and(pl.program_id(1) == 0, i < ntp))
    def _():
        fill(ap_ref, hp_ref)

    @pl.when(jnp.logical_and(pl.program_id(1) == 0, i >= ntp))
    def _():
        fill(as_ref, hs_ref)

    y = jnp.dot(mix_ref[...], w_ref[...], preferred_element_type=F32)
    o_ref[...] = x_ref[...] + gate_ref[...] * y


def _out_proj_call(attn_p, attn_s, hy_p, hy_s, gm, g, w, layer, x, mod, t_prompt, dec_seq, tm=512, tn=512):
    t, d = x.shape
    mix_dim = ATTN_DIM + HY_DIM + GM_DIM
    seg = lambda i: _seg_of_row(i * tm, t_prompt, dec_seq)
    ntp = t_prompt // tm
    nts = t // tm - ntp
    prow = lambda i, j: (jnp.minimum(i, ntp - 1), 0)
    srow = lambda i, j: (jnp.clip(i - ntp, 0, nts - 1), 0)
    return pl.pallas_call(
        functools.partial(_out_proj_kernel, ntp=ntp),
        grid=(t // tm, d // tn),
        in_specs=[pl.BlockSpec((tm, ATTN_DIM), prow),
                  pl.BlockSpec((tm, ATTN_DIM), srow),
                  pl.BlockSpec((tm, HY_DIM), prow),
                  pl.BlockSpec((tm, HY_DIM), srow),
                  pl.BlockSpec((tm, GM_DIM), lambda i, j: (i, 0)),
                  pl.BlockSpec((1, mix_dim), lambda i, j: (0, 0)),
                  pl.BlockSpec((None, mix_dim, tn), lambda i, j: (layer, 0, j)),
                  pl.BlockSpec((tm, tn), lambda i, j: (i, j)),
                  pl.BlockSpec((None, 1, tn), lambda i, j: (seg(i), 0, 2 * (d // tn) + j))],
        out_specs=pl.BlockSpec((tm, tn), lambda i, j: (i, j)),
        out_shape=jax.ShapeDtypeStruct((t, d), F32),
        scratch_shapes=[pltpu.VMEM((tm, mix_dim), BF16)],
        compiler_params=_cparams("parallel", "arbitrary"),
        name="mix_out_proj",
    )(attn_p, attn_s, hy_p, hy_s, gm, g.reshape(1, mix_dim), w, x, mod)


def _router_kernel(x_ref, g_ref, sh_ref, sc_ref, wr_ref, br_ref, h_ref, r_ref):
    y = _rms(x_ref[...]) * g_ref[...]
    h = y * (1.0 + sc_ref[...]) + sh_ref[...]
    h_ref[...] = _pack_bf16_pairs(h)
    logits = jnp.dot(h, wr_ref[...], precision=HI, preferred_element_type=F32) + br_ref[...]
    col = lambda k: logits[:, k:k + 1]

    lg = [col(k) for k in range(N_GROUPS)]
    g_max = functools.reduce(jnp.maximum, lg)
    g_den = sum(jnp.exp(v - g_max) for v in lg)
    g_top = 1.0 / g_den
    g_idx = jnp.full(g_max.shape, N_GROUPS - 1, jnp.int32)
    for k in range(N_GROUPS - 2, -1, -1):
        g_idx = jnp.where(lg[k] == g_max, k, g_idx)

    le = []
    for e in range(EXPERTS_PER_GROUP):
        v = col(N_GROUPS + (N_GROUPS - 1) * EXPERTS_PER_GROUP + e)
        for k in range(N_GROUPS - 2, -1, -1):
            v = jnp.where(g_idx == k, col(N_GROUPS + k * EXPERTS_PER_GROUP + e), v)
        le.append(v)

    e_max = functools.reduce(jnp.maximum, le)
    e1 = jnp.full(e_max.shape, EXPERTS_PER_GROUP - 1, jnp.int32)
    for e in range(EXPERTS_PER_GROUP - 2, -1, -1):
        e1 = jnp.where(le[e] == e_max, e, e1)
    neg = jnp.float32(-jnp.inf)
    rest = [jnp.where(e1 == e, neg, le[e]) for e in range(EXPERTS_PER_GROUP)]
    e2_max = functools.reduce(jnp.maximum, rest)
    e2 = jnp.full(e_max.shape, EXPERTS_PER_GROUP - 1, jnp.int32)
    for e in range(EXPERTS_PER_GROUP - 2, -1, -1):
        e2 = jnp.where(rest[e] == e2_max, e, e2)
    ratio = jnp.exp(e2_max - e_max)
    w1 = g_top / (1.0 + ratio)
    w2 = g_top * ratio / (1.0 + ratio)
    id1 = (g_idx * EXPERTS_PER_GROUP + e1).astype(F32)
    id2 = (g_idx * EXPERTS_PER_GROUP + e2).astype(F32)

    lane = lax.broadcasted_iota(jnp.int32, r_ref.shape, 1)
    r_ref[...] = jnp.where(lane == 0, id1, jnp.where(lane == 1, id2, jnp.where(lane == 2, w1, jnp.where(lane == 3, w2, 0.0))))


def _router_call(x, g, mod, wr, br, t_prompt, dec_seq, tm=256):
    t, d = x.shape
    seg = lambda i: _seg_of_row(i * tm, t_prompt, dec_seq)
    return pl.pallas_call(
        _router_kernel,
        grid=(t // tm,),
        in_specs=[pl.BlockSpec((tm, d), lambda i: (i, 0)),
                  pl.BlockSpec((1, d), lambda i: (0, 0)),
                  pl.BlockSpec((None, 1, d), lambda i: (seg(i), 0, 3)),
                  pl.BlockSpec((None, 1, d), lambda i: (seg(i), 0, 4)),
                  pl.BlockSpec((d, LANES), lambda i: (0, 0)),
                  pl.BlockSpec((1, LANES), lambda i: (0, 0))],
        out_specs=[pl.BlockSpec((tm, d // 2), lambda i: (i, 0)),
                   pl.BlockSpec((tm, LANES), lambda i: (i, 0))],
        out_shape=[jax.ShapeDtypeStruct((t, d // 2), jnp.uint32),
                   jax.ShapeDtypeStruct((t, LANES), F32)],
        compiler_params=_cparams("parallel"),
        name="norm2_router",
    )(x, g.reshape(1, d), mod, mod, wr, br)


def _dispatch_kernel(dest_ref, zt_ref, h_ref, xs_ref, zero_ref, sem, zsem, *, tm, t, expert_tile):
    i = pl.program_id(0)

    @pl.when(i == 0)
    def _():
        zero_ref[...] = jnp.zeros(zero_ref.shape, zero_ref.dtype)

        def clear(q):
            row = pl.multiple_of(jnp.maximum(zt_ref[q], 0), expert_tile)
            return pltpu.make_async_copy(zero_ref, xs_ref.at[pl.ds(row, expert_tile), :], zsem)

        for q in range(2 * N_EXPERTS):
            @pl.when(zt_ref[q] >= 0)
            def _():
                clear(q).start()

        for q in range(2 * N_EXPERTS):
            @pl.when(zt_ref[q] >= 0)
            def _():
                clear(q).wait()

    base = i * tm

    def body(r, c):
        for k in range(2):
            row = dest_ref[k * t + base + r]
            pltpu.make_async_copy(h_ref.at[pl.ds(r, 1), :], xs_ref.at[pl.ds(row, 1), :], sem).start()
        return c

    lax.fori_loop(0, tm, body, 0, unroll=8)
    for k in range(2):
        pltpu.make_async_copy(h_ref, xs_ref.at[pl.ds(0, tm), :], sem).wait()


def _dispatch_call(dest, zero_tiles, h, n_rows, expert_tile, tm=256):
    t, d = h.shape
    return pl.pallas_call(
        functools.partial(_dispatch_kernel, tm=tm, t=t, expert_tile=expert_tile),
        grid_spec=pltpu.PrefetchScalarGridSpec(
            num_scalar_prefetch=2,
            grid=(t // tm,),
            in_specs=[pl.BlockSpec((tm, d), lambda i, dest, zt: (i, 0))],
            out_specs=pl.BlockSpec(memory_space=pl.ANY),
            scratch_shapes=[pltpu.VMEM((expert_tile, d), h.dtype),
                            pltpu.SemaphoreType.DMA(()),
                            pltpu.SemaphoreType.DMA(())]),
        out_shape=jax.ShapeDtypeStruct((n_rows, d), h.dtype),
        compiler_params=_cparams("arbitrary", disable_bounds_checks=True),
        name="moe_dispatch_rows",
    )(dest, zero_tiles, h)


def _expert_up_kernel(te_ref, nv_ref, x_ref, wg_ref, wu_ref, o_ref, *, tm):
    valid = pl.program_id(0) * tm < nv_ref[0]

    @pl.when(valid)
    def _():
        x = _unpack_bf16_pairs(x_ref[...]).astype(BF16)
        hg = jnp.dot(x, wg_ref[...], preferred_element_type=F32)
        hu = jnp.dot(x, wu_ref[...], preferred_element_type=F32)
        o_ref[...] = (hg * _sigmoid(hg) * hu).astype(o_ref.dtype)

    @pl.when(jnp.logical_not(valid))
    def _():
        o_ref[...] = jnp.zeros(o_ref.shape, o_ref.dtype)


def _expert_up_call(tile_e, n_valid, xs, wg, wu, layer, tm):
    n, dh = xs.shape
    d, ff = wg.shape[2], wg.shape[3]
    return pl.pallas_call(
        functools.partial(_expert_up_kernel, tm=tm),
        grid_spec=pltpu.PrefetchScalarGridSpec(
            num_scalar_prefetch=2,
            grid=(n // tm,),
            in_specs=[pl.BlockSpec((tm, dh), lambda m, te, nv: (m, 0)),
                      pl.BlockSpec((None, None, d, ff), lambda m, te, nv: (layer, te[m], 0, 0)),
                      pl.BlockSpec((None, None, d, ff), lambda m, te, nv: (layer, te[m], 0, 0))],
            out_specs=pl.BlockSpec((tm, ff), lambda m, te, nv: (m, 0))),
        out_shape=jax.ShapeDtypeStruct((n, ff), BF16),
        compiler_params=_cparams("arbitrary"),
        name="moe_expert_up",
    )(tile_e, n_valid, xs, wg, wu)


def _expert_down_kernel(te_ref, nv_ref, a_ref, wd_ref, o_ref, *, tm):
    valid = pl.program_id(0) * tm < nv_ref[0]

    @pl.when(valid)
    def _():
        o_ref[...] = _pack_bf16_pairs(jnp.dot(a_ref[...], wd_ref[...], preferred_element_type=F32))

    @pl.when(jnp.logical_not(valid))
    def _():
        o_ref[...] = jnp.zeros(o_ref.shape, o_ref.dtype)


def _expert_down_call(tile_e, n_valid, a, wd, layer, tm):
    n, ff = a.shape
    d = wd.shape[3]
    return pl.pallas_call(
        functools.partial(_expert_down_kernel, tm=tm),
        grid_spec=pltpu.PrefetchScalarGridSpec(
            num_scalar_prefetch=2,
            grid=(n // tm,),
            in_specs=[pl.BlockSpec((tm, ff), lambda m, te, nv: (m, 0)),
                      pl.BlockSpec((None, None, ff, d), lambda m, te, nv: (layer, te[m], 0, 0))],
            out_specs=pl.BlockSpec((tm, d // 2), lambda m, te, nv: (m, 0))),
        out_shape=jax.ShapeDtypeStruct((n, d // 2), jnp.uint32),
        compiler_params=_cparams("arbitrary"),
        name="moe_expert_down",
    )(tile_e, n_valid, a, wd)


def _combine_kernel(dest_ref, x_ref, r_ref, gate_ref, fg_ref, ys_ref, *rest, tc, t, nt, ntp, final):
    out_refs, (ybuf, sem) = rest[:-2], rest[-2:]
    i = pl.program_id(0)

    def fetch(tile, slot):
        base = tile * tc

        def body(r, c):
            for k in range(2):
                row = dest_ref[k * t + base + r]
                pltpu.make_async_copy(ys_ref.at[pl.ds(row, 1), :], ybuf.at[slot, pl.ds(k * tc + r, 1), :],
                                      sem.at[slot]).start()
            return c

        lax.fori_loop(0, tc, body, 0, unroll=8)

    @pl.when(i == 0)
    def _():
        fetch(0, 0)

    @pl.when(i + 1 < nt)
    def _():
        fetch(i + 1, (i + 1) % 2)

    slot = i % 2
    pltpu.make_async_copy(ys_ref.at[pl.ds(0, 2 * tc), :], ybuf.at[slot], sem.at[slot]).wait()

    w1 = r_ref[:, 2:3]
    w2 = r_ref[:, 3:4]
    y = w1 * _unpack_bf16_pairs(ybuf[slot, 0:tc, :]) + w2 * _unpack_bf16_pairs(ybuf[slot, tc:2 * tc, :])
    x = x_ref[...] + gate_ref[...] * y
    if final:
        x = _rms(x) * fg_ref[...]
        op_ref, os_ref = out_refs

        @pl.when(i < ntp)
        def _():
            op_ref[...] = x

        @pl.when(i >= ntp)
        def _():
            os_ref[...] = x
    else:
        out_refs[0][...] = x


def _combine_call(dest, x, route, ys, mod, final_g, final, t_prompt, dec_seq, tc=128):
    t, d = x.shape
    seg = lambda i, *_: _seg_of_row(i * tc, t_prompt, dec_seq)
    nt = t // tc
    ntp = t_prompt // tc
    if final:
        out_specs = [pl.BlockSpec((tc, d), lambda i, dest: (jnp.minimum(i, ntp - 1), 0)),
                     pl.BlockSpec((tc, d), lambda i, dest: (jnp.maximum(i - ntp, 0), 0))]
        out_shape = [jax.ShapeDtypeStruct((t_prompt, d), F32), jax.ShapeDtypeStruct((t - t_prompt, d), F32)]
    else:
        out_specs = pl.BlockSpec((tc, d), lambda i, dest: (i, 0))
        out_shape = jax.ShapeDtypeStruct((t, d), F32)
    return pl.pallas_call(
        functools.partial(_combine_kernel, tc=tc, t=t, nt=nt, ntp=ntp, final=final),
        grid_spec=pltpu.PrefetchScalarGridSpec(
            num_scalar_prefetch=1,
            grid=(nt,),
            in_specs=[pl.BlockSpec((tc, d), lambda i, dest: (i, 0)),
                      pl.BlockSpec((tc, LANES), lambda i, dest: (i, 0)),
                      pl.BlockSpec((None, 1, d), lambda i, dest: (seg(i), 0, 5)),
                      pl.BlockSpec((1, d), lambda i, dest: (0, 0)),
                      pl.BlockSpec(memory_space=pl.ANY)],
            out_specs=out_specs,
            scratch_shapes=[pltpu.VMEM((2, 2 * tc, ys.shape[1]), ys.dtype),
                            pltpu.SemaphoreType.DMA((2,))]),
        out_shape=out_shape,
        compiler_params=_cparams("arbitrary", disable_bounds_checks=True),
        name="moe_combine_residual",
    )(dest, x, route, mod, final_g.reshape(1, d), ys)


def _dispatch_plan(route, tm):
    t = route.shape[0]
    flat_e = route[:, 0:2].astype(jnp.int32).T.reshape(-1)
    n_pairs = 2 * t
    n_tiles = n_pairs // tm + N_EXPERTS
    onehot = (flat_e[:, None] == jnp.arange(N_EXPERTS, dtype=jnp.int32)[None, :]).astype(jnp.int32)
    running = jnp.cumsum(onehot, axis=0)
    counts = running[-1]
    rank = jnp.sum((running - 1) * onehot, axis=1)
    padded = ((counts + tm - 1) // tm) * tm
    pad_end = jnp.cumsum(padded)
    pad_start = pad_end - padded
    dest = (jnp.sum(onehot * pad_start[None, :], axis=1) + rank).astype(jnp.int32)
    tile_start = jnp.arange(n_tiles, dtype=jnp.int32) * tm
    tile_e = jnp.sum((pad_end[None, :] <= tile_start[:, None]).astype(jnp.int32), axis=1)
    tile_e = jnp.minimum(tile_e, N_EXPERTS - 1).astype(jnp.int32)
    n_valid = pad_end[-1:].astype(jnp.int32)
    last = jnp.where(counts > 0, pad_end - tm, -1)
    tail = pad_end[-1] + jnp.arange(N_EXPERTS, dtype=jnp.int32) * tm
    tail = jnp.where(tail < n_tiles * tm, tail, -1)
    zero_tiles = jnp.concatenate([last, tail]).astype(jnp.int32)
    return dest, tile_e, n_valid, zero_tiles, n_tiles * tm


def _moe(h, route, wg, wu, wd, layer, tm=256):
    dest, tile_e, n_valid, zero_tiles, n_rows = _dispatch_plan(route, tm)
    xs = _dispatch_call(dest, zero_tiles, h, n_rows, tm)
    a = _expert_up_call(tile_e, n_valid, xs, wg, wu, layer, tm)
    ys = _expert_down_call(tile_e, n_valid, a, wd, layer, tm)
    return ys, dest


def _trunk(x_prompt, x_sample, cache_k, cache_v, c, c_ctx, prm, final_norm_g, hy_block_sample=512):
    batch, seq, d = x_prompt.shape
    dec_batch, dec_seq, _ = x_sample.shape
    depth = prm['w_in'].shape[0]
    past = cache_k.shape[2]
    t_prompt = batch * seq
    t_sample = dec_batch * dec_seq

    x = jnp.concatenate([x_prompt.reshape(t_prompt, d), x_sample.reshape(t_sample, d)], axis=0)
    n_cond = 1 + dec_batch
    cond = jnp.concatenate([c_ctx[None, :], c, jnp.zeros((8 - n_cond % 8, d), F32)], axis=0)
    mod_all = _mod_call(cond, prm['w_mod'], prm['b_mod'])
    cos, sin_signed = _rope_tables(dec_seq)
    sconv_rows_p = math.gcd(t_prompt, max(seq, 4096 // seq * seq))

    new_k, new_v = [], []
    big = ('w_mod', 'w_in', 'w_out', 'exp_w_gate', 'exp_w_up', 'exp_w_down')
    w_in, w_out = prm['w_in'].astype(BF16), prm['w_out'].astype(BF16)
    w_gate, w_up, w_down = (prm[k].astype(BF16) for k in ('exp_w_gate', 'exp_w_up', 'exp_w_down'))

    for l in range(depth):
        p = {name: w[l] for name, w in prm.items() if name not in big}
        mod = mod_all[l, :n_cond].reshape(n_cond, 1, 6 * d)

        z = _norm_proj_call(x, p['norm1_g'], mod, w_in, l, t_prompt, dec_seq)

        attn_p, k_l, v_l = _attn_prompt_call(z, p['q_norm_g'], p['k_norm_g'], batch, seq)
        kr = _krope_call(z, cos, sin_signed, p['k_norm_g'], t_prompt, dec_batch, dec_seq)
        attn_s = _attn_sample_call(z, kr, cache_k[:, l].reshape(dec_batch, past, KV_DIM),
                                   cache_v[:, l].reshape(dec_batch, past, KV_DIM),
                                   cos, sin_signed, p['q_norm_g'], t_prompt, dec_batch, dec_seq)
        new_k.append(k_l.reshape(batch, seq, N_KV_HEADS, HEAD_DIM))
        new_v.append(v_l.reshape(batch, seq, N_KV_HEADS, HEAD_DIM))

        hy_p = _hyena_group(z, 0, batch, seq, seq, p, sconv_rows_p)
        hy_s = _hyena_group(z, t_prompt, dec_batch, dec_seq, min(hy_block_sample, dec_seq), p, dec_seq)

        gm = _gmlp_call(z, p['gm_norm_g'], p['gm_norm_b'], p['gm_ws'], p['gm_bs'])

        x = _out_proj_call(attn_p, attn_s, hy_p, hy_s, gm, p['out_norm_g'], w_out, l, x, mod, t_prompt, dec_seq)

        wr = jnp.concatenate([p['router_g_w'],
                              p['router_e_w'].transpose(1, 0, 2).reshape(d, N_EXPERTS)], axis=1)
        wr = jnp.pad(wr, ((0, 0), (0, LANES - wr.shape[1])))
        br = jnp.pad(jnp.concatenate([p['router_g_b'], p['router_e_b'].reshape(-1)]),
                     (0, LANES - N_GROUPS - N_EXPERTS)).reshape(1, LANES)
        h2, route = _router_call(x, p['norm2_g'], mod, wr, br, t_prompt, dec_seq)
        ys, dest = _moe(h2, route, w_gate, w_up, w_down, l)
        x = _combine_call(dest, x, route, ys, mod, final_norm_g, l == depth - 1, t_prompt, dec_seq)

    y_prompt = x[0].reshape(batch, seq, d)
    y_sample = x[1].reshape(dec_batch, dec_seq, d)
    return y_prompt, y_sample, jnp.stack(new_k, axis=1), jnp.stack(new_v, axis=1)


_PARAM_NAMES = ('norm1_g', 'norm2_g', 'w_mod', 'b_mod', 'w_in', 'q_norm_g', 'k_norm_g', 'hy_conv_w', 'hy_conv_b',
                'hf_w1', 'hf_b1', 'hf_w2', 'hf_b2', 'hf_w3', 'hf_b3', 'hf_freq', 'hy_bias', 'gm_norm_g',
                'gm_norm_b', 'gm_ws', 'gm_bs', 'out_norm_g', 'w_out', 'router_g_w', 'router_g_b', 'router_e_w',
                'router_e_b', 'exp_w_gate', 'exp_w_up', 'exp_w_down')


def kernel(x_prompt, x_sample, cache_k, cache_v, c, c_ctx, norm1_g, norm2_g, w_mod, b_mod, w_in, q_norm_g, k_norm_g, hy_conv_w, hy_conv_b, hf_w1, hf_b1, hf_w2, hf_b2, hf_w3, hf_b3, hf_freq, hy_bias, gm_norm_g, gm_norm_b, gm_ws, gm_bs, out_norm_g, w_out, router_g_w, router_g_b, router_e_w, router_e_b, exp_w_gate, exp_w_up, exp_w_down, final_norm_g):
    values = (norm1_g, norm2_g, w_mod, b_mod, w_in, q_norm_g, k_norm_g, hy_conv_w, hy_conv_b, hf_w1, hf_b1, hf_w2,
              hf_b2, hf_w3, hf_b3, hf_freq, hy_bias, gm_norm_g, gm_norm_b, gm_ws, gm_bs, out_norm_g, w_out,
              router_g_w, router_g_b, router_e_w, router_e_b, exp_w_gate, exp_w_up, exp_w_down)
    prm = dict(zip(_PARAM_NAMES, values))
    return _trunk(x_prompt, x_sample, cache_k, cache_v, c, c_ctx, prm, final_norm_g)
```

```python
import functools
import math

import numpy as np
import jax
import jax.numpy as jnp
from jax import lax
from jax.experimental import pallas as pl
from jax.experimental.pallas import tpu as pltpu

F32 = jnp.float32
BF16 = jnp.bfloat16

D_MODEL = 4096
GRID_W = 64
HEAD_DIM = 128
N_HEADS = 16
N_KV_HEADS = 4
Q_PER_KV = N_HEADS // N_KV_HEADS
ATTN_DIM = N_HEADS * HEAD_DIM
KV_DIM = N_KV_HEADS * HEAD_DIM
HY_DIM = 1024
GM_DIM = 1024
GM_HEADS = 8
CHUNK = 128
PROJ_DIM = ATTN_DIM + 2 * KV_DIM + 3 * HY_DIM + 2 * GM_DIM
ROPE_THETA = 10000.0
FILTER_EMB = 33
FILTER_HIDDEN = 64
DECAY_TARGET = 1e-2
FAST_DECAY_PCT = 0.3
SLOW_DECAY_PCT = 1.5
MOD_SHIFT = 0.05
N_GROUPS = 4
EXPERTS_PER_GROUP = 4
N_EXPERTS = 16
EXPERT_FF = 1024
EPS = 1e-6

COL_K = ATTN_DIM
COL_V = ATTN_DIM + KV_DIM
COL_HY = ATTN_DIM + 2 * KV_DIM
COL_GM = COL_HY + 3 * HY_DIM

LANES = 128
VMEM_LIMIT = 56 * 1024 * 1024
HI = lax.Precision.HIGHEST


def _cparams(*sem, **kw):
    return pltpu.CompilerParams(dimension_semantics=sem, vmem_limit_bytes=VMEM_LIMIT, **kw)


def _rms(x):
    return x * lax.rsqrt(jnp.mean(x * x, axis=-1, keepdims=True) + EPS)


def _sigmoid(x):
    return 1.0 / (1.0 + jnp.exp(-x))


def _pack_bf16_pairs(x):
    n = x.shape[1] // 2
    lo = lax.bitcast_convert_type(x[:, :n].astype(BF16).astype(F32), jnp.uint32)
    hi = lax.bitcast_convert_type(x[:, n:].astype(BF16).astype(F32), jnp.uint32)
    return hi | (lo >> 16)


def _unpack_bf16_pairs(u):
    lo = lax.bitcast_convert_type(u << 16, F32)
    hi = lax.bitcast_convert_type(u & jnp.uint32(0xFFFF0000), F32)
    return jnp.concatenate([lo, hi], axis=1)


def _seg_of_row(row, t_prompt, dec_seq):
    return jnp.where(row < t_prompt, 0, 1 + (row - t_prompt) // dec_seq)


def _mod_kernel(c_ref, w_ref, b_ref, o_ref):
    c = c_ref[...]
    s = (c * _sigmoid(c)).astype(BF16)
    o_ref[...] = jnp.dot(s, w_ref[...].astype(BF16), preferred_element_type=F32) + b_ref[...]


def _mod_call(cond, w_mod, b_mod):
    depth, d, n = w_mod.shape
    r = cond.shape[0]
    tn = 512
    return pl.pallas_call(
        _mod_kernel,
        grid=(depth, n // tn),
        in_specs=[pl.BlockSpec((r, d), lambda l, j: (0, 0)),
                  pl.BlockSpec((None, d, tn), lambda l, j: (l, 0, j)),
                  pl.BlockSpec((None, 1, tn), lambda l, j: (l, 0, j))],
        out_specs=pl.BlockSpec((None, r, tn), lambda l, j: (l, 0, j)),
        out_shape=jax.ShapeDtypeStruct((depth, r, n), F32),
        compiler_params=_cparams("parallel", "parallel"),
        name="adaln_mod",
    )(cond, w_mod, b_mod.reshape(depth, 1, n))


def _norm_mod_kernel(x_ref, g_ref, sh_ref, sc_ref, o_ref):
    y = _rms(x_ref[...]) * g_ref[...]
    o_ref[...] = (y * (1.0 + sc_ref[...]) + sh_ref[...]).astype(o_ref.dtype)


def _norm_mod_call(x, g, mod, t_prompt, dec_seq, tm=512):
    t, d = x.shape
    seg = lambda i: _seg_of_row(i * tm, t_prompt, dec_seq)
    return pl.pallas_call(
        _norm_mod_kernel,
        grid=(t // tm,),
        in_specs=[pl.BlockSpec((tm, d), lambda i: (i, 0)),
                  pl.BlockSpec((1, d), lambda i: (0, 0)),
                  pl.BlockSpec((None, 1, d), lambda i: (seg(i), 0, 0)),
                  pl.BlockSpec((None, 1, d), lambda i: (seg(i), 0, 1))],
        out_specs=pl.BlockSpec((tm, d), lambda i: (i, 0)),
        out_shape=jax.ShapeDtypeStruct((t, d), BF16),
        compiler_params=_cparams("parallel"),
        name="norm1_modulate",
    )(x, g.reshape(1, d), mod, mod)


def _proj_kernel(a_ref, w_ref, o_ref):
    o_ref[...] = jnp.dot(a_ref[...], w_ref[...], preferred_element_type=F32).astype(o_ref.dtype)


def _proj_residual_kernel(a_ref, w_ref, x_ref, gate_ref, o_ref):
    o_ref[...] = x_ref[...] + gate_ref[...] * jnp.dot(a_ref[...], w_ref[...], preferred_element_type=F32)


def _proj_call(a, w, layer, name, out_dtype=BF16, residual=None, tm=1024, tn=1024):
    t, k = a.shape
    n = w.shape[2]
    in_specs = [pl.BlockSpec((tm, k), lambda i, j: (i, 0)),
                pl.BlockSpec((None, k, tn), lambda i, j: (layer, 0, j))]
    args = [a, w]
    body = _proj_kernel
    if residual is not None:
        x, mod, gate_block, t_prompt, dec_seq = residual
        seg = lambda i: _seg_of_row(i * tm, t_prompt, dec_seq)
        in_specs += [pl.BlockSpec((tm, tn), lambda i, j: (i, j)),
                     pl.BlockSpec((None, 1, tn), lambda i, j: (seg(i), 0, gate_block * (n // tn) + j))]
        args += [x, mod]
        body = _proj_residual_kernel
    return pl.pallas_call(
        body,
        grid=(t // tm, n // tn),
        in_specs=in_specs,
        out_specs=pl.BlockSpec((tm, tn), lambda i, j: (i, j)),
        out_shape=jax.ShapeDtypeStruct((t, n), out_dtype),
        compiler_params=_cparams("parallel", "parallel"),
        name=name,
    )(*args)


def _attn_prompt_kernel(q_ref, k_ref, v_ref, qg_ref, kg_ref, o_ref, kc_ref, vc_ref):
    kn = _rms(k_ref[...].astype(F32)) * kg_ref[...]
    vb = v_ref[...]
    kc_ref[...] = kn
    vc_ref[...] = vb.astype(F32)
    kb = kn.astype(BF16)
    scale = HEAD_DIM ** -0.5
    for m in range(Q_PER_KV):
        sl = slice(m * HEAD_DIM, (m + 1) * HEAD_DIM)
        qn = (_rms(q_ref[:, sl].astype(F32)) * qg_ref[...] * scale).astype(BF16)
        s = lax.dot_general(qn, kb, (((1,), (1,)), ((), ())), preferred_element_type=F32)
        p = jnp.exp(s - jnp.max(s, axis=-1, keepdims=True))
        l = jnp.sum(p, axis=-1, keepdims=True)
        o = jnp.dot(p.astype(BF16), vb, preferred_element_type=F32)
        o_ref[:, sl] = o / l


def _attn_prompt_call(z, qg, kg, batch, seq):
    gw = Q_PER_KV * HEAD_DIM
    tp = batch * seq
    return pl.pallas_call(
        _attn_prompt_kernel,
        grid=(batch, N_KV_HEADS),
        in_specs=[pl.BlockSpec((seq, gw), lambda b, g: (b, g)),
                  pl.BlockSpec((seq, HEAD_DIM), lambda b, g: (b, COL_K // HEAD_DIM + g)),
                  pl.BlockSpec((seq, HEAD_DIM), lambda b, g: (b, COL_V // HEAD_DIM + g)),
                  pl.BlockSpec((1, HEAD_DIM), lambda b, g: (0, 0)),
                  pl.BlockSpec((1, HEAD_DIM), lambda b, g: (0, 0))],
        out_specs=[pl.BlockSpec((seq, gw), lambda b, g: (b, g)),
                   pl.BlockSpec((seq, HEAD_DIM), lambda b, g: (b, g)),
                   pl.BlockSpec((seq, HEAD_DIM), lambda b, g: (b, g))],
        out_shape=[jax.ShapeDtypeStruct((tp, ATTN_DIM), F32),
                   jax.ShapeDtypeStruct((tp, KV_DIM), F32),
                   jax.ShapeDtypeStruct((tp, KV_DIM), F32)],
        compiler_params=_cparams("parallel", "parallel"),
        name="attn_context",
    )(z, z, z, qg.reshape(1, HEAD_DIM), kg.reshape(1, HEAD_DIM))


def _rope(x, cos, sin_signed):
    lane = lax.broadcasted_iota(jnp.int32, x.shape, 1)
    quarter = HEAD_DIM // 4
    fwd = pltpu.roll(x, HEAD_DIM - quarter, 1)
    bwd = pltpu.roll(x, quarter, 1)
    swapped = jnp.where((lane % (2 * quarter)) < quarter, fwd, bwd)
    return x * cos + swapped * sin_signed


def _krope_kernel(k_ref, c_ref, s_ref, kg_ref, o_ref):
    kn = _rms(k_ref[...].astype(F32)) * kg_ref[...]
    o_ref[...] = _rope(kn, c_ref[...], s_ref[...]).astype(o_ref.dtype)


def _krope_call(z, cos, sin_signed, kg, t_prompt, dec_batch, dec_seq, tr=512):
    nr = dec_seq // tr
    return pl.pallas_call(
        _krope_kernel,
        grid=(dec_batch, nr, N_KV_HEADS),
        in_specs=[pl.BlockSpec((tr, HEAD_DIM), lambda b, r, g: (t_prompt // tr + b * nr + r, COL_K // HEAD_DIM + g)),
                  pl.BlockSpec((tr, HEAD_DIM), lambda b, r, g: (r, 0)),
                  pl.BlockSpec((tr, HEAD_DIM), lambda b, r, g: (r, 0)),
                  pl.BlockSpec((1, HEAD_DIM), lambda b, r, g: (0, 0))],
        out_specs=pl.BlockSpec((tr, HEAD_DIM), lambda b, r, g: (b * nr + r, g)),
        out_shape=jax.ShapeDtypeStruct((dec_batch * dec_seq, KV_DIM), BF16),
        compiler_params=_cparams("parallel", "parallel", "parallel"),
        name="attn_key_rope",
    )(z, cos, sin_signed, kg.reshape(1, HEAD_DIM))


def _attn_sample_kernel(q_ref, k_ref, v_ref, ck_ref, cv_ref, cq_ref, sq_ref,
                        qg_ref, o_ref, qs_ref, m_ref, l_ref, acc_ref):
    j = pl.program_id(3)
    scale = HEAD_DIM ** -0.5

    @pl.when(j == 0)
    def _():
        for m in range(Q_PER_KV):
            sl = slice(m * HEAD_DIM, (m + 1) * HEAD_DIM)
            qn = _rms(q_ref[:, sl].astype(F32)) * qg_ref[...]
            qs_ref[m] = (_rope(qn, cq_ref[...], sq_ref[...]) * scale).astype(BF16)
        m_ref[...] = jnp.full(m_ref.shape, -jnp.inf, F32)
        l_ref[...] = jnp.zeros(l_ref.shape, F32)
        acc_ref[...] = jnp.zeros(acc_ref.shape, F32)

    def step(kb, vb):
        reps = kb.shape[0] // LANES
        for m in range(Q_PER_KV):
            s = lax.dot_general(qs_ref[m], kb, (((1,), (1,)), ((), ())), preferred_element_type=F32)
            m_prev = m_ref[m]
            m_next = jnp.maximum(m_prev, jnp.max(s, axis=-1, keepdims=True))
            alpha = jnp.exp(m_prev - m_next)
            p = jnp.exp(s - jnp.concatenate([m_next] * reps, axis=1))
            l_ref[m] = alpha * l_ref[m] + jnp.sum(p, axis=-1, keepdims=True)
            acc_ref[m] = alpha * acc_ref[m] + jnp.dot(p.astype(BF16), vb, preferred_element_type=F32)
            m_ref[m] = m_next

    @pl.when(j == 0)
    def _():
        step(ck_ref[...].astype(BF16), cv_ref[...].astype(BF16))

    @pl.when(j > 0)
    def _():
        step(k_ref[...], v_ref[...])

    @pl.when(j == pl.num_programs(3) - 1)
    def _():
        for m in range(Q_PER_KV):
            o_ref[:, m * HEAD_DIM:(m + 1) * HEAD_DIM] = acc_ref[m] / l_ref[m]


def _attn_sample_call(z, kr, ctx_k, ctx_v, cos, sin_signed, qg, t_prompt, dec_batch, dec_seq, tq=512, tk=1024):
    gw = Q_PER_KV * HEAD_DIM
    past = ctx_k.shape[1]
    tk = min(tk, dec_seq)
    nq = dec_seq // tq
    nk = dec_seq // tk
    qrow = lambda b, i: (t_prompt + b * dec_seq) // tq + i
    vrow = lambda b, j: (t_prompt + b * dec_seq) // tk + jnp.maximum(j - 1, 0)
    krow = lambda b, j: b * nk + jnp.maximum(j - 1, 0)
    return pl.pallas_call(
        _attn_sample_kernel,
        grid=(dec_batch, N_KV_HEADS, nq, nk + 1),
        in_specs=[pl.BlockSpec((tq, gw), lambda b, g, i, j: (qrow(b, i), g)),
                  pl.BlockSpec((tk, HEAD_DIM), lambda b, g, i, j: (krow(b, j), g)),
                  pl.BlockSpec((tk, HEAD_DIM), lambda b, g, i, j: (vrow(b, j), COL_V // HEAD_DIM + g)),
                  pl.BlockSpec((None, past, HEAD_DIM), lambda b, g, i, j: (b, 0, g)),
                  pl.BlockSpec((None, past, HEAD_DIM), lambda b, g, i, j: (b, 0, g)),
                  pl.BlockSpec((tq, HEAD_DIM), lambda b, g, i, j: (i, 0)),
                  pl.BlockSpec((tq, HEAD_DIM), lambda b, g, i, j: (i, 0)),
                  pl.BlockSpec((1, HEAD_DIM), lambda b, g, i, j: (0, 0))],
        out_specs=pl.BlockSpec((tq, gw), lambda b, g, i, j: (b * nq + i, g)),
        out_shape=jax.ShapeDtypeStruct((dec_batch * dec_seq, ATTN_DIM), F32),
        scratch_shapes=[pltpu.VMEM((Q_PER_KV, tq, HEAD_DIM), BF16),
                        pltpu.VMEM((Q_PER_KV, tq, LANES), F32),
                        pltpu.VMEM((Q_PER_KV, tq, LANES), F32),
                        pltpu.VMEM((Q_PER_KV, tq, HEAD_DIM), F32)],
        compiler_params=_cparams("parallel", "parallel", "parallel", "arbitrary"),
        name="attn_latent",
    )(z, kr, z, ctx_k, ctx_v, cos, sin_signed, qg.reshape(1, HEAD_DIM))


def _rope_tables(n_tokens):
    quarter = HEAD_DIM // 4
    n_rows = n_tokens // GRID_W
    row = jnp.repeat(jnp.arange(n_rows, dtype=F32), GRID_W)
    col = jnp.tile(jnp.arange(GRID_W, dtype=F32), n_rows)
    freqs = ROPE_THETA ** (-jnp.arange(quarter, dtype=F32) / quarter)
    ar = row[:, None] * freqs[None, :]
    ac = col[:, None] * freqs[None, :]
    cos = jnp.concatenate([jnp.cos(ar), jnp.cos(ar), jnp.cos(ac), jnp.cos(ac)], axis=-1)
    sin = jnp.concatenate([-jnp.sin(ar), jnp.sin(ar), -jnp.sin(ac), jnp.sin(ac)], axis=-1)
    return cos, sin


def _gelu_tanh(x):
    return 0.5 * x * (1.0 + jnp.tanh(math.sqrt(2.0 / math.pi) * (x + 0.044715 * (x * x * x))))


def _gmlp_kernel(z_ref, g_ref, b_ref, ws_ref, bs_ref, o_ref):
    hw = GM_DIM // GM_HEADS
    for ch in range(z_ref.shape[0] // CHUNK):
        rows = slice(ch * CHUNK, (ch + 1) * CHUNK)
        u = _gelu_tanh(z_ref[rows, :GM_DIM].astype(F32))
        v = _gelu_tanh(z_ref[rows, GM_DIM:].astype(F32))
        mu = jnp.mean(v, axis=-1, keepdims=True)
        vc = v - mu
        var = jnp.mean(vc * vc, axis=-1, keepdims=True)
        vn = (vc * lax.rsqrt(var + EPS) * g_ref[...] + b_ref[...]).astype(BF16)
        for h in range(GM_HEADS):
            cols = slice(h * hw, (h + 1) * hw)
            s = jnp.dot(ws_ref[h], vn[:, cols], preferred_element_type=F32) + bs_ref[h]
            o_ref[rows, cols] = u[:, cols] * s


def _gmlp_call(z, g, b, ws, bs, tr=512):
    t = z.shape[0]
    return pl.pallas_call(
        _gmlp_kernel,
        grid=(t // tr,),
        in_specs=[pl.BlockSpec((tr, 2 * GM_DIM), lambda i: (i, COL_GM // (2 * GM_DIM))),
                  pl.BlockSpec((1, GM_DIM), lambda i: (0, 0)),
                  pl.BlockSpec((1, GM_DIM), lambda i: (0, 0)),
                  pl.BlockSpec((GM_HEADS, CHUNK, CHUNK), lambda i: (0, 0, 0)),
                  pl.BlockSpec((GM_HEADS, CHUNK, 1), lambda i: (0, 0, 0))],
        out_specs=pl.BlockSpec((tr, GM_DIM), lambda i: (i, 0)),
        out_shape=jax.ShapeDtypeStruct((t, GM_DIM), F32),
        compiler_params=_cparams("parallel"),
        name="chunk_gmlp",
    )(z, g.reshape(1, GM_DIM), b.reshape(1, GM_DIM), ws.astype(BF16), bs.reshape(GM_HEADS, CHUNK, 1))


def _sconv_kernel(z_ref, w_ref, b_ref, o_ref, *, seg_len):
    x = z_ref[...].astype(F32)
    n = x.shape[0]
    pos = lax.broadcasted_iota(jnp.int32, x.shape, 0) % seg_len
    prev = jnp.where(pos == 0, 0.0, pltpu.roll(x, 1, 0))
    nxt = jnp.where(pos == seg_len - 1, 0.0, pltpu.roll(x, n - 1, 0))
    o_ref[...] = b_ref[...] + prev * w_ref[0:1, :] + x * w_ref[1:2, :] + nxt * w_ref[2:3, :]


def _sconv_call(z, w, b, row0, n_rows, seg_len, block_rows, cw=256):
    width = 3 * HY_DIM
    return pl.pallas_call(
        functools.partial(_sconv_kernel, seg_len=seg_len),
        grid=(n_rows // block_rows, width // cw),
        in_specs=[pl.BlockSpec((block_rows, cw), lambda i, c: (row0 // block_rows + i, COL_HY // cw + c)),
                  pl.BlockSpec((3, cw), lambda i, c: (0, c)),
                  pl.BlockSpec((1, cw), lambda i, c: (0, c))],
        out_specs=pl.BlockSpec((block_rows, cw), lambda i, c: (i, c)),
        out_shape=jax.ShapeDtypeStruct((n_rows, width), F32),
        compiler_params=_cparams("parallel", "parallel"),
        name="hyena_short_conv",
    )(z, w, b.reshape(1, width))


def _filter_kernel(z_ref, t_ref, w1_ref, b1_ref, w2_ref, b2_ref, w3a_ref, b3a_ref, w3b_ref, b3b_ref,
                   fr_ref, dl_ref, o_ref, *, tr):
    fr = fr_ref[...]
    h = jnp.sin(fr * (jnp.dot(z_ref[...], w1_ref[...], precision=HI, preferred_element_type=F32) + b1_ref[...]))
    h = jnp.sin(fr * (jnp.dot(h, w2_ref[...], precision=HI, preferred_element_type=F32) + b2_ref[...]))
    decay = jnp.exp(-t_ref[...] * dl_ref[...]) + MOD_SHIFT
    row = lax.broadcasted_iota(jnp.int32, decay.shape, 0) + pl.program_id(0) * tr
    decay = jnp.where(row == 0, 0.0, decay)
    for o, (w3_ref, b3_ref) in enumerate(((w3a_ref, b3a_ref), (w3b_ref, b3b_ref))):
        taps = jnp.dot(h, w3_ref[...], precision=HI, preferred_element_type=F32) + b3_ref[...]
        o_ref[:, o * HY_DIM:(o + 1) * HY_DIM] = taps * decay


def _filter_call(length, w1, b1, w2, b2, w3, b3, fr, tr=256):
    bands = (FILTER_EMB - 1) // 2
    t = np.linspace(0.0, 1.0, length)[:, None]
    wv = 2.0 * np.pi * np.arange(length)[:, None] / length
    f = np.linspace(1e-4, bands - 1, bands)[None, :]
    zf = np.concatenate([t, np.cos(f * wv), -np.sin(f * wv)], axis=-1)
    min_decay = math.log(DECAY_TARGET) / SLOW_DECAY_PCT
    max_decay = math.log(DECAY_TARGET) / FAST_DECAY_PCT
    deltas = np.abs(np.linspace(min_decay, max_decay, HY_DIM))[None, :]
    lag = np.minimum(np.abs(np.arange(2 * length) - length), length - 1)
    pad = LANES - FILTER_EMB
    hp = LANES - FILTER_HIDDEN
    z2 = jnp.asarray(np.pad(zf[lag], ((0, 0), (0, pad))).astype(np.float32))
    t2 = jnp.asarray(t[lag].astype(np.float32))
    dl = jnp.asarray(deltas.astype(np.float32))
    w1p = jnp.pad(w1, ((0, pad), (0, hp)))
    w2p = jnp.pad(w2, ((0, hp), (0, hp)))
    w3p = jnp.pad(w3, ((0, hp), (0, 0)))
    b1p = jnp.pad(b1, (0, hp)).reshape(1, LANES)
    b2p = jnp.pad(b2, (0, hp)).reshape(1, LANES)
    frp = jnp.pad(fr, (0, hp)).reshape(1, LANES)
    b3r = b3.reshape(1, -1)
    nblk = 2 * length // tr
    half = length // tr
    wcol = lambda r, o: o * 2 + jnp.where(r < half, 1, 0)
    small = pl.BlockSpec((1, LANES), lambda r: (0, 0))
    square = pl.BlockSpec((LANES, LANES), lambda r: (0, 0))
    return pl.pallas_call(
        functools.partial(_filter_kernel, tr=tr),
        grid=(nblk,),
        in_specs=[pl.BlockSpec((tr, LANES), lambda r: (r, 0)),
                  pl.BlockSpec((tr, 1), lambda r: (r, 0)),
                  square, small, square, small,
                  pl.BlockSpec((LANES, HY_DIM), lambda r: (0, wcol(r, 0))),
                  pl.BlockSpec((1, HY_DIM), lambda r: (0, wcol(r, 0))),
                  pl.BlockSpec((LANES, HY_DIM), lambda r: (0, wcol(r, 1))),
                  pl.BlockSpec((1, HY_DIM), lambda r: (0, wcol(r, 1))),
                  small,
                  pl.BlockSpec((1, HY_DIM), lambda r: (0, 0))],
        out_specs=pl.BlockSpec((tr, 2 * HY_DIM), lambda r: (r, 0)),
        out_shape=jax.ShapeDtypeStruct((2 * length, 2 * HY_DIM), F32),
        compiler_params=_cparams("parallel"),
        name="hyena_filter_mlp",
    )(z2, t2, w1p, b1p, w2p, b2p, w3p, b3r, w3p, b3r, frp, dl)


def _dft_constants(p):
    n = 2 * p
    idx = np.arange(p, dtype=np.float64)
    ang = 2.0 * np.pi * np.outer(idx, idx) / n
    re = np.cos(ang)
    im = -np.sin(ang)
    im[0, :] = np.cos(np.pi * idx)
    fwd = np.concatenate([re, im], axis=0)
    sign = np.where(np.arange(p) % 2 == 0, 1.0, -1.0)
    sign2 = np.concatenate([sign, sign])
    sign2[p] = 1.0
    fwd_shift = fwd * sign2[:, None]
    ar = (2.0 / n) * np.cos(ang)
    ar[:, 0] = 1.0 / n
    ai = -(2.0 / n) * np.sin(ang)
    ai[:, 0] = sign / n
    inv = np.concatenate([ar, ai], axis=1)
    as_bf16 = lambda a: jnp.asarray(a.astype(np.float32)).astype(BF16)
    return as_bf16(fwd), as_bf16(fwd_shift), as_bf16(inv)


def _dft_kernel(f_ref, x_ref, o_ref):
    o_ref[...] = jnp.dot(f_ref[...], x_ref[...].astype(BF16), preferred_element_type=F32).astype(o_ref.dtype)


def _dft_call(x, fwd, p, col0, cw=256):
    nblk = x.shape[0] // p
    return pl.pallas_call(
        _dft_kernel,
        grid=(nblk, HY_DIM // cw),
        in_specs=[pl.BlockSpec((2 * p, p), lambda i, c: (0, 0)),
                  pl.BlockSpec((p, cw), lambda i, c: (i, col0 // cw + c))],
        out_specs=pl.BlockSpec((None, 2 * p, cw), lambda i, c: (i, 0, c)),
        out_shape=jax.ShapeDtypeStruct((nblk, 2 * p, HY_DIM), BF16),
        compiler_params=_cparams("parallel", "parallel"),
        name="hyena_block_dft",
    )(fwd, x)


def _filter_dft_kernel(f_ref, fs_ref, hi_ref, lo_ref, o_ref):
    lo = lo_ref[...]
    row = lax.broadcasted_iota(jnp.int32, lo.shape, 0)
    lo = jnp.where(row == 0, 0.0, lo)
    acc = jnp.dot(f_ref[...], hi_ref[...].astype(BF16), preferred_element_type=F32)
    acc = acc + jnp.dot(fs_ref[...], lo.astype(BF16), preferred_element_type=F32)
    o_ref[...] = acc.astype(o_ref.dtype)


def _filter_dft_call(taps, fwd, fwd_shift, p, cw=256):
    nb2 = taps.shape[0] // p
    nd = nb2 - 1
    ncb = HY_DIM // cw
    return pl.pallas_call(
        _filter_dft_kernel,
        grid=(2, nd, ncb),
        in_specs=[pl.BlockSpec((2 * p, p), lambda o, d, c: (0, 0)),
                  pl.BlockSpec((2 * p, p), lambda o, d, c: (0, 0)),
                  pl.BlockSpec((p, cw), lambda o, d, c: (d + 1, o * ncb + c)),
                  pl.BlockSpec((p, cw), lambda o, d, c: (d, o * ncb + c))],
        out_specs=pl.BlockSpec((None, None, 2 * p, cw), lambda o, d, c: (o, d, 0, c)),
        out_shape=jax.ShapeDtypeStruct((2, nd, 2 * p, HY_DIM), BF16),
        compiler_params=_cparams("parallel", "parallel", "parallel"),
        name="hyena_filter_dft",
    )(fwd, fwd_shift, taps, taps)


def _specconv_kernel(xs_ref, gs_ref, inv_ref, v_ref, gate_ref, bias_ref, o_ref, y_ref, *, nb, p, rc):
    i = pl.program_id(2)
    cw = o_ref.shape[1]

    for r in range(0, p, rc):
        def body(j, carry):
            yre, yim = carry
            d = i - j + nb - 1
            xre = xs_ref[j, r:r + rc, :].astype(F32)
            xim = xs_ref[j, p + r:p + r + rc, :].astype(F32)
            gre = gs_ref[d, r:r + rc, :].astype(F32)
            gim = gs_ref[d, p + r:p + r + rc, :].astype(F32)
            return yre + (xre * gre - xim * gim), yim + (xre * gim + xim * gre)

        zero = jnp.zeros((rc, cw), F32)
        yre, yim = lax.fori_loop(0, nb, body, (zero, zero))
        y_ref[r:r + rc, :] = yre
        y_ref[p + r:p + r + rc, :] = yim

    def edge(j, carry):
        y0, yn = carry
        d = i - j + nb - 1
        x0 = xs_ref[j, 0:16, :].astype(F32)
        xn = xs_ref[j, p:p + 16, :].astype(F32)
        g0 = gs_ref[d, 0:16, :].astype(F32)
        gn = gs_ref[d, p:p + 16, :].astype(F32)
        return y0 + x0 * g0, yn + xn * gn

    zero16 = jnp.zeros((16, cw), F32)
    y0, yn = lax.fori_loop(0, nb, edge, (zero16, zero16))
    y_ref[0:1, :] = y0[0:1, :]
    y_ref[p:p + 1, :] = yn[0:1, :]

    y = jnp.dot(inv_ref[...], y_ref[...].astype(BF16), preferred_element_type=F32)
    v = v_ref[...]
    o_ref[...] = gate_ref[...] * (y + bias_ref[...] * v)


def _specconv_call(xs, gs, inv, zc, bias, nseq, nb, p, v_col0, gate_col0, v_src=None, cw=256, rc=32):
    n_rows = nseq * nb * p
    xs4 = xs.reshape(nseq, nb, 2 * p, HY_DIM)
    if v_src is None:
        v_arr, v_spec = zc, pl.BlockSpec((p, cw), lambda c, s, i: (s * nb + i, v_col0 // cw + c))
    else:
        v_arr, v_spec = v_src, pl.BlockSpec((p, cw), lambda c, s, i: (s * nb + i, c))
    return pl.pallas_call(
        functools.partial(_specconv_kernel, nb=nb, p=p, rc=rc),
        grid=(HY_DIM // cw, nseq, nb),
        in_specs=[pl.BlockSpec((None, nb, 2 * p, cw), lambda c, s, i: (s, 0, 0, c)),
                  pl.BlockSpec((2 * nb - 1, 2 * p, cw), lambda c, s, i: (0, 0, c)),
                  pl.BlockSpec((p, 2 * p), lambda c, s, i: (0, 0)),
                  v_spec,
                  pl.BlockSpec((p, cw), lambda c, s, i: (s * nb + i, gate_col0 // cw + c)),
                  pl.BlockSpec((1, cw), lambda c, s, i: (0, c))],
        out_specs=pl.BlockSpec((p, cw), lambda c, s, i: (s * nb + i, c)),
        out_shape=jax.ShapeDtypeStruct((n_rows, HY_DIM), F32),
        scratch_shapes=[pltpu.VMEM((2 * p, cw), F32)],
        compiler_params=_cparams("parallel", "parallel", "parallel"),
        name="hyena_spectral_conv",
    )(xs4, gs, inv, v_arr, zc, bias.reshape(1, HY_DIM))


def _hyena_group(z, row0, nseq, length, p, prm, sconv_rows):
    nb = length // p
    zc = _sconv_call(z, prm['hy_conv_w'], prm['hy_conv_b'], row0, nseq * length, length, sconv_rows)
    taps = _filter_call(length, prm['hf_w1'], prm['hf_b1'], prm['hf_w2'], prm['hf_b2'],
                        prm['hf_w3'], prm['hf_b3'], prm['hf_freq'])
    fwd, fwd_shift, inv = _dft_constants(p)
    gs = _filter_dft_call(taps, fwd, fwd_shift, p)
    vs = _dft_call(zc, fwd, p, 2 * HY_DIM)
    u = _specconv_call(vs, gs[0], inv, zc, prm['hy_bias'][0], nseq, nb, p, 2 * HY_DIM, 0)
    us = _dft_call(u, fwd, p, 0)
    return _specconv_call(us, gs[1], inv, zc, prm['hy_bias'][1], nseq, nb, p, 0, HY_DIM, v_src=u)


def _mix_norm_kernel(ap_ref, as_ref, hp_ref, hs_ref, m_ref, g_ref, o_ref, *, ntp):
    i = pl.program_id(0)

    def fill(a_ref, h_ref):
        o_ref[:, :ATTN_DIM] = (_rms(a_ref[...]) * g_ref[:, :ATTN_DIM]).astype(BF16)
        o_ref[:, ATTN_DIM:ATTN_DIM + HY_DIM] = (
            _rms(h_ref[...]) * g_ref[:, ATTN_DIM:ATTN_DIM + HY_DIM]).astype(BF16)
        o_ref[:, ATTN_DIM + HY_DIM:] = (_rms(m_ref[...]) * g_ref[:, ATTN_DIM + HY_DIM:]).astype(BF16)

    @pl.when(i < ntp)
    def _():
        fill(ap_ref, hp_ref)

    @pl.when(i >= ntp)
    def _():
        fill(as_ref, hs_ref)


def _mix_norm_call(attn_p, attn_s, hy_p, hy_s, gm, g, t_prompt, tm=512):
    t = gm.shape[0]
    mix_dim = ATTN_DIM + HY_DIM + GM_DIM
    ntp = t_prompt // tm
    nts = t // tm - ntp
    prow = lambda i: (jnp.minimum(i, ntp - 1), 0)
    srow = lambda i: (jnp.clip(i - ntp, 0, nts - 1), 0)
    return pl.pallas_call(
        functools.partial(_mix_norm_kernel, ntp=ntp),
        grid=(t // tm,),
        in_specs=[pl.BlockSpec((tm, ATTN_DIM), prow),
                  pl.BlockSpec((tm, ATTN_DIM), srow),
                  pl.BlockSpec((tm, HY_DIM), prow),
                  pl.BlockSpec((tm, HY_DIM), srow),
                  pl.BlockSpec((tm, GM_DIM), lambda i: (i, 0)),
                  pl.BlockSpec((1, mix_dim), lambda i: (0, 0))],
        out_specs=pl.BlockSpec((tm, mix_dim), lambda i: (i, 0)),
        out_shape=jax.ShapeDtypeStruct((t, mix_dim), BF16),
        compiler_params=_cparams("parallel"),
        name="mix_group_norm",
    )(attn_p, attn_s, hy_p, hy_s, gm, g.reshape(1, mix_dim))


def _router_kernel(x_ref, g_ref, sh_ref, sc_ref, wr_ref, br_ref, h_ref, r_ref):
    y = _rms(x_ref[...]) * g_ref[...]
    h = y * (1.0 + sc_ref[...]) + sh_ref[...]
    h_ref[...] = _pack_bf16_pairs(h)
    logits = jnp.dot(h, wr_ref[...], precision=HI, preferred_element_type=F32) + br_ref[...]
    col = lambda k: logits[:, k:k + 1]

    lg = [col(k) for k in range(N_GROUPS)]
    g_max = functools.reduce(jnp.maximum, lg)
    g_den = sum(jnp.exp(v - g_max) for v in lg)
    g_top = 1.0 / g_den
    g_idx = jnp.full(g_max.shape, N_GROUPS - 1, jnp.int32)
    for k in range(N_GROUPS - 2, -1, -1):
        g_idx = jnp.where(lg[k] == g_max, k, g_idx)

    le = []
    for e in range(EXPERTS_PER_GROUP):
        v = col(N_GROUPS + (N_GROUPS - 1) * EXPERTS_PER_GROUP + e)
        for k in range(N_GROUPS - 2, -1, -1):
            v = jnp.where(g_idx == k, col(N_GROUPS + k * EXPERTS_PER_GROUP + e), v)
        le.append(v)

    e_max = functools.reduce(jnp.maximum, le)
    e1 = jnp.full(e_max.shape, EXPERTS_PER_GROUP - 1, jnp.int32)
    for e in range(EXPERTS_PER_GROUP - 2, -1, -1):
        e1 = jnp.where(le[e] == e_max, e, e1)
    neg = jnp.float32(-jnp.inf)
    rest = [jnp.where(e1 == e, neg, le[e]) for e in range(EXPERTS_PER_GROUP)]
    e2_max = functools.reduce(jnp.maximum, rest)
    e2 = jnp.full(e_max.shape, EXPERTS_PER_GROUP - 1, jnp.int32)
    for e in range(EXPERTS_PER_GROUP - 2, -1, -1):
        e2 = jnp.where(rest[e] == e2_max, e, e2)
    ratio = jnp.exp(e2_max - e_max)
    w1 = g_top / (1.0 + ratio)
    w2 = g_top * ratio / (1.0 + ratio)
    id1 = (g_idx * EXPERTS_PER_GROUP + e1).astype(F32)
    id2 = (g_idx * EXPERTS_PER_GROUP + e2).astype(F32)

    lane = lax.broadcasted_iota(jnp.int32, r_ref.shape, 1)
    r_ref[...] = jnp.where(lane == 0, id1, jnp.where(lane == 1, id2, jnp.where(lane == 2, w1, jnp.where(lane == 3, w2, 0.0))))


def _router_call(x, g, mod, wr, br, t_prompt, dec_seq, tm=256):
    t, d = x.shape
    seg = lambda i: _seg_of_row(i * tm, t_prompt, dec_seq)
    return pl.pallas_call(
        _router_kernel,
        grid=(t // tm,),
        in_specs=[pl.BlockSpec((tm, d), lambda i: (i, 0)),
                  pl.BlockSpec((1, d), lambda i: (0, 0)),
                  pl.BlockSpec((None, 1, d), lambda i: (seg(i), 0, 3)),
                  pl.BlockSpec((None, 1, d), lambda i: (seg(i), 0, 4)),
                  pl.BlockSpec((d, LANES), lambda i: (0, 0)),
                  pl.BlockSpec((1, LANES), lambda i: (0, 0))],
        out_specs=[pl.BlockSpec((tm, d // 2), lambda i: (i, 0)),
                   pl.BlockSpec((tm, LANES), lambda i: (i, 0))],
        out_shape=[jax.ShapeDtypeStruct((t, d // 2), jnp.uint32),
                   jax.ShapeDtypeStruct((t, LANES), F32)],
        compiler_params=_cparams("parallel"),
        name="norm2_router",
    )(x, g.reshape(1, d), mod, mod, wr, br)


def _dispatch_kernel(dest_ref, zt_ref, h_ref, xs_ref, zero_ref, sem, zsem, *, tm, t, expert_tile):
    i = pl.program_id(0)

    @pl.when(i == 0)
    def _():
        zero_ref[...] = jnp.zeros(zero_ref.shape, zero_ref.dtype)

        def clear(q):
            row = pl.multiple_of(jnp.maximum(zt_ref[q], 0), expert_tile)
            return pltpu.make_async_copy(zero_ref, xs_ref.at[pl.ds(row, expert_tile), :], zsem)

        for q in range(2 * N_EXPERTS):
            @pl.when(zt_ref[q] >= 0)
            def _():
                clear(q).start()

        for q in range(2 * N_EXPERTS):
            @pl.when(zt_ref[q] >= 0)
            def _():
                clear(q).wait()

    base = i * tm

    def body(r, c):
        for k in range(2):
            row = dest_ref[k * t + base + r]
            pltpu.make_async_copy(h_ref.at[pl.ds(r, 1), :], xs_ref.at[pl.ds(row, 1), :], sem).start()
        return c

    lax.fori_loop(0, tm, body, 0, unroll=8)
    for k in range(2):
        pltpu.make_async_copy(h_ref, xs_ref.at[pl.ds(0, tm), :], sem).wait()


def _dispatch_call(dest, zero_tiles, h, n_rows, expert_tile, tm=256):
    t, d = h.shape
    return pl.pallas_call(
        functools.partial(_dispatch_kernel, tm=tm, t=t, expert_tile=expert_tile),
        grid_spec=pltpu.PrefetchScalarGridSpec(
            num_scalar_prefetch=2,
            grid=(t // tm,),
            in_specs=[pl.BlockSpec((tm, d), lambda i, dest, zt: (i, 0))],
            out_specs=pl.BlockSpec(memory_space=pl.ANY),
            scratch_shapes=[pltpu.VMEM((expert_tile, d), h.dtype),
                            pltpu.SemaphoreType.DMA(()),
                            pltpu.SemaphoreType.DMA(())]),
        out_shape=jax.ShapeDtypeStruct((n_rows, d), h.dtype),
        compiler_params=_cparams("arbitrary", disable_bounds_checks=True),
        name="moe_dispatch_rows",
    )(dest, zero_tiles, h)


def _expert_up_kernel(te_ref, nv_ref, x_ref, wg_ref, wu_ref, o_ref, *, tm):
    valid = pl.program_id(0) * tm < nv_ref[0]

    @pl.when(valid)
    def _():
        x = _unpack_bf16_pairs(x_ref[...]).astype(BF16)
        hg = jnp.dot(x, wg_ref[...], preferred_element_type=F32)
        hu = jnp.dot(x, wu_ref[...], preferred_element_type=F32)
        o_ref[...] = (hg * _sigmoid(hg) * hu).astype(o_ref.dtype)

    @pl.when(jnp.logical_not(valid))
    def _():
        o_ref[...] = jnp.zeros(o_ref.shape, o_ref.dtype)


def _expert_up_call(tile_e, n_valid, xs, wg, wu, layer, tm):
    n, dh = xs.shape
    d, ff = wg.shape[2], wg.shape[3]
    return pl.pallas_call(
        functools.partial(_expert_up_kernel, tm=tm),
        grid_spec=pltpu.PrefetchScalarGridSpec(
            num_scalar_prefetch=2,
            grid=(n // tm,),
            in_specs=[pl.BlockSpec((tm, dh), lambda m, te, nv: (m, 0)),
                      pl.BlockSpec((None, None, d, ff), lambda m, te, nv: (layer, te[m], 0, 0)),
                      pl.BlockSpec((None, None, d, ff), lambda m, te, nv: (layer, te[m], 0, 0))],
            out_specs=pl.BlockSpec((tm, ff), lambda m, te, nv: (m, 0))),
        out_shape=jax.ShapeDtypeStruct((n, ff), BF16),
        compiler_params=_cparams("arbitrary"),
        name="moe_expert_up",
    )(tile_e, n_valid, xs, wg, wu)


def _expert_down_kernel(te_ref, nv_ref, a_ref, wd_ref, o_ref, *, tm):
    valid = pl.program_id(0) * tm < nv_ref[0]

    @pl.when(valid)
    def _():
        o_ref[...] = _pack_bf16_pairs(jnp.dot(a_ref[...], wd_ref[...], preferred_element_type=F32))

    @pl.when(jnp.logical_not(valid))
    def _():
        o_ref[...] = jnp.zeros(o_ref.shape, o_ref.dtype)


def _expert_down_call(tile_e, n_valid, a, wd, layer, tm):
    n, ff = a.shape
    d = wd.shape[3]
    return pl.pallas_call(
        functools.partial(_expert_down_kernel, tm=tm),
        grid_spec=pltpu.PrefetchScalarGridSpec(
            num_scalar_prefetch=2,
            grid=(n // tm,),
            in_specs=[pl.BlockSpec((tm, ff), lambda m, te, nv: (m, 0)),
                      pl.BlockSpec((None, None, ff, d), lambda m, te, nv: (layer, te[m], 0, 0))],
            out_specs=pl.BlockSpec((tm, d // 2), lambda m, te, nv: (m, 0))),
        out_shape=jax.ShapeDtypeStruct((n, d // 2), jnp.uint32),
        compiler_params=_cparams("arbitrary"),
        name="moe_expert_down",
    )(tile_e, n_valid, a, wd)


def _combine_kernel(dest_ref, x_ref, r_ref, gate_ref, fg_ref, ys_ref, *rest, tc, t, nt, ntp, final):
    out_refs, (ybuf, sem) = rest[:-2], rest[-2:]
    i = pl.program_id(0)
    rows = 8

    def fetch_rows(tile, slot, r0, n):
        for r in range(n):
            for k in range(2):
                row = dest_ref[k * t + tile * tc + r0 + r]
                pltpu.make_async_copy(ys_ref.at[pl.ds(row, 1), :], ybuf.at[slot, pl.ds(k * tc + r0 + r, 1), :],
                                      sem.at[slot]).start()

    @pl.when(i == 0)
    def _():
        def first(c, carry):
            fetch_rows(0, 0, c * rows, rows)
            return carry
        lax.fori_loop(0, tc // rows, first, 0)

    slot = i % 2
    pltpu.make_async_copy(ys_ref.at[pl.ds(0, 2 * tc), :], ybuf.at[slot], sem.at[slot]).wait()

    def compute_rows(r0, o_ref):
        sl = pl.ds(r0, rows)
        w1 = r_ref[sl, 2:3]
        w2 = r_ref[sl, 3:4]
        y = (w1 * _unpack_bf16_pairs(ybuf[slot, sl, :])
             + w2 * _unpack_bf16_pairs(ybuf[slot, pl.ds(pl.multiple_of(tc + r0, rows), rows), :]))
        x = x_ref[sl, :] + gate_ref[...] * y
        if final:
            x = _rms(x) * fg_ref[...]
        o_ref[sl, :] = x

    def sweep(o_ref, prefetch):
        def chunk(c, carry):
            r0 = pl.multiple_of(c * rows, rows)
            if prefetch:
                fetch_rows(i + 1, 1 - slot, r0, rows)
            compute_rows(r0, o_ref)
            return carry
        lax.fori_loop(0, tc // rows, chunk, 0)

    def run(o_ref):
        @pl.when(i + 1 < nt)
        def _():
            sweep(o_ref, True)

        @pl.when(i + 1 >= nt)
        def _():
            sweep(o_ref, False)

    if final:
        op_ref, os_ref = out_refs

        @pl.when(i < ntp)
        def _():
            run(op_ref)

        @pl.when(i >= ntp)
        def _():
            run(os_ref)
    else:
        run(out_refs[0])


def _combine_call(dest, x, route, ys, mod, final_g, final, t_prompt, dec_seq, tc=128):
    t, d = x.shape
    seg = lambda i, *_: _seg_of_row(i * tc, t_prompt, dec_seq)
    nt = t // tc
    ntp = t_prompt // tc
    if final:
        out_specs = [pl.BlockSpec((tc, d), lambda i, dest: (jnp.minimum(i, ntp - 1), 0)),
                     pl.BlockSpec((tc, d), lambda i, dest: (jnp.maximum(i - ntp, 0), 0))]
        out_shape = [jax.ShapeDtypeStruct((t_prompt, d), F32), jax.ShapeDtypeStruct((t - t_prompt, d), F32)]
    else:
        out_specs = pl.BlockSpec((tc, d), lambda i, dest: (i, 0))
        out_shape = jax.ShapeDtypeStruct((t, d), F32)
    return pl.pallas_call(
        functools.partial(_combine_kernel, tc=tc, t=t, nt=nt, ntp=ntp, final=final),
        grid_spec=pltpu.PrefetchScalarGridSpec(
            num_scalar_prefetch=1,
            grid=(nt,),
            in_specs=[pl.BlockSpec((tc, d), lambda i, dest: (i, 0)),
                      pl.BlockSpec((tc, LANES), lambda i, dest: (i, 0)),
                      pl.BlockSpec((None, 1, d), lambda i, dest: (seg(i), 0, 5)),
                      pl.BlockSpec((1, d), lambda i, dest: (0, 0)),
                      pl.BlockSpec(memory_space=pl.ANY)],
            out_specs=out_specs,
            scratch_shapes=[pltpu.VMEM((2, 2 * tc, ys.shape[1]), ys.dtype),
                            pltpu.SemaphoreType.DMA((2,))]),
        out_shape=out_shape,
        compiler_params=_cparams("arbitrary", disable_bounds_checks=True),
        name="moe_combine_residual",
    )(dest, x, route, mod, final_g.reshape(1, d), ys)


def _dispatch_plan(route, tm):
    t = route.shape[0]
    flat_e = route[:, 0:2].astype(jnp.int32).T.reshape(-1)
    n_pairs = 2 * t
    n_tiles = n_pairs // tm + N_EXPERTS
    onehot = (flat_e[:, None] == jnp.arange(N_EXPERTS, dtype=jnp.int32)[None, :]).astype(jnp.int32)
    running = jnp.cumsum(onehot, axis=0)
    counts = running[-1]
    rank = jnp.sum((running - 1) * onehot, axis=1)
    padded = ((counts + tm - 1) // tm) * tm
    pad_end = jnp.cumsum(padded)
    pad_start = pad_end - padded
    dest = (jnp.sum(onehot * pad_start[None, :], axis=1) + rank).astype(jnp.int32)
    tile_start = jnp.arange(n_tiles, dtype=jnp.int32) * tm
    tile_e = jnp.sum((pad_end[None, :] <= tile_start[:, None]).astype(jnp.int32), axis=1)
    tile_e = jnp.minimum(tile_e, N_EXPERTS - 1).astype(jnp.int32)
    n_valid = pad_end[-1:].astype(jnp.int32)
    last = jnp.where(counts > 0, pad_end - tm, -1)
    tail = pad_end[-1] + jnp.arange(N_EXPERTS, dtype=jnp.int32) * tm
    tail = jnp.where(tail < n_tiles * tm, tail, -1)
    zero_tiles = jnp.concatenate([last, tail]).astype(jnp.int32)
    return dest, tile_e, n_valid, zero_tiles, n_tiles * tm


def _moe(h, route, wg, wu, wd, layer, tm=256):
    dest, tile_e, n_valid, zero_tiles, n_rows = _dispatch_plan(route, tm)
    xs = _dispatch_call(dest, zero_tiles, h, n_rows, tm)
    a = _expert_up_call(tile_e, n_valid, xs, wg, wu, layer, tm)
    ys = _expert_down_call(tile_e, n_valid, a, wd, layer, tm)
    return ys, dest


def _trunk(x_prompt, x_sample, cache_k, cache_v, c, c_ctx, prm, final_norm_g, hy_block_sample=512):
    batch, seq, d = x_prompt.shape
    dec_batch, dec_seq, _ = x_sample.shape
    depth = prm['w_in'].shape[0]
    past = cache_k.shape[2]
    t_prompt = batch * seq
    t_sample = dec_batch * dec_seq

    x = jnp.concatenate([x_prompt.reshape(t_prompt, d), x_sample.reshape(t_sample, d)], axis=0)
    n_cond = 1 + dec_batch
    cond = jnp.concatenate([c_ctx[None, :], c, jnp.zeros((8 - n_cond % 8, d), F32)], axis=0)
    mod_all = _mod_call(cond, prm['w_mod'], prm['b_mod'])
    cos, sin_signed = _rope_tables(dec_seq)
    sconv_rows_p = math.gcd(t_prompt, max(seq, 4096 // seq * seq))

    new_k, new_v = [], []
    big = ('w_mod', 'w_in', 'w_out', 'exp_w_gate', 'exp_w_up', 'exp_w_down')
    w_in, w_out = prm['w_in'].astype(BF16), prm['w_out'].astype(BF16)
    w_gate, w_up, w_down = (prm[k].astype(BF16) for k in ('exp_w_gate', 'exp_w_up', 'exp_w_down'))

    for l in range(depth):
        p = {name: w[l] for name, w in prm.items() if name not in big}
        mod = mod_all[l, :n_cond].reshape(n_cond, 1, 6 * d)

        h1 = _norm_mod_call(x, p['norm1_g'], mod, t_prompt, dec_seq)
        z = _proj_call(h1, w_in, l, "in_proj")

        attn_p, k_l, v_l = _attn_prompt_call(z, p['q_norm_g'], p['k_norm_g'], batch, seq)
        kr = _krope_call(z, cos, sin_signed, p['k_norm_g'], t_prompt, dec_batch, dec_seq)
        attn_s = _attn_sample_call(z, kr, cache_k[:, l].reshape(dec_batch, past, KV_DIM),
                                   cache_v[:, l].reshape(dec_batch, past, KV_DIM),
                                   cos, sin_signed, p['q_norm_g'], t_prompt, dec_batch, dec_seq)
        new_k.append(k_l.reshape(batch, seq, N_KV_HEADS, HEAD_DIM))
        new_v.append(v_l.reshape(batch, seq, N_KV_HEADS, HEAD_DIM))

        hy_p = _hyena_group(z, 0, batch, seq, seq, p, sconv_rows_p)
        hy_s = _hyena_group(z, t_prompt, dec_batch, dec_seq, min(hy_block_sample, dec_seq), p, dec_seq)

        gm = _gmlp_call(z, p['gm_norm_g'], p['gm_norm_b'], p['gm_ws'], p['gm_bs'])

        mix = _mix_norm_call(attn_p, attn_s, hy_p, hy_s, gm, p['out_norm_g'], t_prompt)
        x = _proj_call(mix, w_out, l, "out_proj_residual", out_dtype=F32, residual=(x, mod, 2, t_prompt, dec_seq))

        wr = jnp.concatenate([p['router_g_w'],
                              p['router_e_w'].transpose(1, 0, 2).reshape(d, N_EXPERTS)], axis=1)
        wr = jnp.pad(wr, ((0, 0), (0, LANES - wr.shape[1])))
        br = jnp.pad(jnp.concatenate([p['router_g_b'], p['router_e_b'].reshape(-1)]),
                     (0, LANES - N_GROUPS - N_EXPERTS)).reshape(1, LANES)
        h2, route = _router_call(x, p['norm2_g'], mod, wr, br, t_prompt, dec_seq)
        ys, dest = _moe(h2, route, w_gate, w_up, w_down, l)
        x = _combine_call(dest, x, route, ys, mod, final_norm_g, l == depth - 1, t_prompt, dec_seq)

    y_prompt = x[0].reshape(batch, seq, d)
    y_sample = x[1].reshape(dec_batch, dec_seq, d)
    return y_prompt, y_sample, jnp.stack(new_k, axis=1), jnp.stack(new_v, axis=1)


_PARAM_NAMES = ('norm1_g', 'norm2_g', 'w_mod', 'b_mod', 'w_in', 'q_norm_g', 'k_norm_g', 'hy_conv_w', 'hy_conv_b',
                'hf_w1', 'hf_b1', 'hf_w2', 'hf_b2', 'hf_w3', 'hf_b3', 'hf_freq', 'hy_bias', 'gm_norm_g',
                'gm_norm_b', 'gm_ws', 'gm_bs', 'out_norm_g', 'w_out', 'router_g_w', 'router_g_b', 'router_e_w',
                'router_e_b', 'exp_w_gate', 'exp_w_up', 'exp_w_down')


def kernel(x_prompt, x_sample, cache_k, cache_v, c, c_ctx, norm1_g, norm2_g, w_mod, b_mod, w_in, q_norm_g, k_norm_g, hy_conv_w, hy_conv_b, hf_w1, hf_b1, hf_w2, hf_b2, hf_w3, hf_b3, hf_freq, hy_bias, gm_norm_g, gm_norm_b, gm_ws, gm_bs, out_norm_g, w_out, router_g_w, router_g_b, router_e_w, router_e_b, exp_w_gate, exp_w_up, exp_w_down, final_norm_g):
    values = (norm1_g, norm2_g, w_mod, b_mod, w_in, q_norm_g, k_norm_g, hy_conv_w, hy_conv_b, hf_w1, hf_b1, hf_w2,
              hf_b2, hf_w3, hf_b3, hf_freq, hy_bias, gm_norm_g, gm_norm_b, gm_ws, gm_bs, out_norm_g, w_out,
              router_g_w, router_g_b, router_e_w, router_e_b, exp_w_gate, exp_w_up, exp_w_down)
    prm = dict(zip(_PARAM_NAMES, values))
    return _trunk(x_prompt, x_sample, cache_k, cache_v, c, c_ctx, prm, final_norm_g)
```

```python
import functools
import math

import numpy as np
import jax
import jax.numpy as jnp
from jax import lax
from jax.experimental import pallas as pl
from jax.experimental.pallas import tpu as pltpu

F32 = jnp.float32
BF16 = jnp.bfloat16

D_MODEL = 4096
GRID_W = 64
HEAD_DIM = 128
N_HEADS = 16
N_KV_HEADS = 4
Q_PER_KV = N_HEADS // N_KV_HEADS
ATTN_DIM = N_HEADS * HEAD_DIM
KV_DIM = N_KV_HEADS * HEAD_DIM
HY_DIM = 1024
GM_DIM = 1024
GM_HEADS = 8
CHUNK = 128
PROJ_DIM = ATTN_DIM + 2 * KV_DIM + 3 * HY_DIM + 2 * GM_DIM
ROPE_THETA = 10000.0
FILTER_EMB = 33
FILTER_HIDDEN = 64
DECAY_TARGET = 1e-2
FAST_DECAY_PCT = 0.3
SLOW_DECAY_PCT = 1.5
MOD_SHIFT = 0.05
N_GROUPS = 4
EXPERTS_PER_GROUP = 4
N_EXPERTS = 16
EXPERT_FF = 1024
EPS = 1e-6

COL_K = ATTN_DIM
COL_V = ATTN_DIM + KV_DIM
COL_HY = ATTN_DIM + 2 * KV_DIM
COL_GM = COL_HY + 3 * HY_DIM

LANES = 128
VMEM_LIMIT = 56 * 1024 * 1024
HI = lax.Precision.HIGHEST


def _cparams(*sem, **kw):
    return pltpu.CompilerParams(dimension_semantics=sem, vmem_limit_bytes=VMEM_LIMIT, **kw)


def _rms(x):
    return x * lax.rsqrt(jnp.mean(x * x, axis=-1, keepdims=True) + EPS)


def _sigmoid(x):
    return 1.0 / (1.0 + jnp.exp(-x))


def _pack_bf16_pairs(x):
    n = x.shape[1] // 2
    lo = lax.bitcast_convert_type(x[:, :n].astype(BF16).astype(F32), jnp.uint32)
    hi = lax.bitcast_convert_type(x[:, n:].astype(BF16).astype(F32), jnp.uint32)
    return hi | (lo >> 16)


def _unpack_bf16_pairs(u):
    lo = lax.bitcast_convert_type(u << 16, F32)
    hi = lax.bitcast_convert_type(u & jnp.uint32(0xFFFF0000), F32)
    return jnp.concatenate([lo, hi], axis=1)


def _unpack_halves(u):
    n = u.shape[1] // 2
    return jnp.concatenate([_unpack_bf16_pairs(u[:, :n]), _unpack_bf16_pairs(u[:, n:])], axis=1)


def _seg_of_row(row, t_prompt, dec_seq):
    return jnp.where(row < t_prompt, 0, 1 + (row - t_prompt) // dec_seq)


def _mod_kernel(c_ref, w_ref, b_ref, o_ref):
    c = c_ref[...]
    s = (c * _sigmoid(c)).astype(BF16)
    o_ref[...] = jnp.dot(s, w_ref[...].astype(BF16), preferred_element_type=F32) + b_ref[...]


def _mod_call(cond, w_mod, b_mod):
    depth, d, n = w_mod.shape
    r = cond.shape[0]
    tn = 512
    return pl.pallas_call(
        _mod_kernel,
        grid=(depth, n // tn),
        in_specs=[pl.BlockSpec((r, d), lambda l, j: (0, 0)),
                  pl.BlockSpec((None, d, tn), lambda l, j: (l, 0, j)),
                  pl.BlockSpec((None, 1, tn), lambda l, j: (l, 0, j))],
        out_specs=pl.BlockSpec((None, r, tn), lambda l, j: (l, 0, j)),
        out_shape=jax.ShapeDtypeStruct((depth, r, n), F32),
        compiler_params=_cparams("parallel", "parallel"),
        name="adaln_mod",
    )(cond, w_mod, b_mod.reshape(depth, 1, n))


def _norm_mod_kernel(x_ref, g_ref, sh_ref, sc_ref, o_ref):
    y = _rms(x_ref[...]) * g_ref[...]
    o_ref[...] = (y * (1.0 + sc_ref[...]) + sh_ref[...]).astype(o_ref.dtype)


def _norm_mod_call(x, g, mod, t_prompt, dec_seq, tm=512):
    t, d = x.shape
    seg = lambda i: _seg_of_row(i * tm, t_prompt, dec_seq)
    return pl.pallas_call(
        _norm_mod_kernel,
        grid=(t // tm,),
        in_specs=[pl.BlockSpec((tm, d), lambda i: (i, 0)),
                  pl.BlockSpec((1, d), lambda i: (0, 0)),
                  pl.BlockSpec((None, 1, d), lambda i: (seg(i), 0, 0)),
                  pl.BlockSpec((None, 1, d), lambda i: (seg(i), 0, 1))],
        out_specs=pl.BlockSpec((tm, d), lambda i: (i, 0)),
        out_shape=jax.ShapeDtypeStruct((t, d), BF16),
        compiler_params=_cparams("parallel"),
        name="norm1_modulate",
    )(x, g.reshape(1, d), mod, mod)


def _proj_kernel(a_ref, w_ref, o_ref):
    o_ref[...] = jnp.dot(a_ref[...], w_ref[...], preferred_element_type=F32).astype(o_ref.dtype)


def _proj_residual_kernel(a_ref, w_ref, x_ref, gate_ref, o_ref):
    o_ref[...] = x_ref[...] + gate_ref[...] * jnp.dot(a_ref[...], w_ref[...], preferred_element_type=F32)


def _proj_call(a, w, layer, name, out_dtype=BF16, residual=None, tm=1024, tn=1024):
    t, k = a.shape
    n = w.shape[2]
    in_specs = [pl.BlockSpec((tm, k), lambda i, j: (i, 0)),
                pl.BlockSpec((None, k, tn), lambda i, j: (layer, 0, j))]
    args = [a, w]
    body = _proj_kernel
    if residual is not None:
        x, mod, gate_block, t_prompt, dec_seq = residual
        seg = lambda i: _seg_of_row(i * tm, t_prompt, dec_seq)
        in_specs += [pl.BlockSpec((tm, tn), lambda i, j: (i, j)),
                     pl.BlockSpec((None, 1, tn), lambda i, j: (seg(i), 0, gate_block * (n // tn) + j))]
        args += [x, mod]
        body = _proj_residual_kernel
    return pl.pallas_call(
        body,
        grid=(t // tm, n // tn),
        in_specs=in_specs,
        out_specs=pl.BlockSpec((tm, tn), lambda i, j: (i, j)),
        out_shape=jax.ShapeDtypeStruct((t, n), out_dtype),
        compiler_params=_cparams("parallel", "parallel"),
        name=name,
    )(*args)


def _attn_prompt_kernel(q_ref, k_ref, v_ref, qg_ref, kg_ref, o_ref, kc_ref, vc_ref):
    kn = _rms(k_ref[...].astype(F32)) * kg_ref[...]
    vb = v_ref[...]
    kc_ref[...] = kn
    vc_ref[...] = vb.astype(F32)
    kb = kn.astype(BF16)
    scale = HEAD_DIM ** -0.5
    for m in range(Q_PER_KV):
        sl = slice(m * HEAD_DIM, (m + 1) * HEAD_DIM)
        qn = (_rms(q_ref[:, sl].astype(F32)) * qg_ref[...] * scale).astype(BF16)
        s = lax.dot_general(qn, kb, (((1,), (1,)), ((), ())), preferred_element_type=F32)
        p = jnp.exp(s - jnp.max(s, axis=-1, keepdims=True))
        l = jnp.sum(p, axis=-1, keepdims=True)
        o = jnp.dot(p.astype(BF16), vb, preferred_element_type=F32)
        o_ref[:, sl] = o / l


def _attn_prompt_call(z, qg, kg, batch, seq):
    gw = Q_PER_KV * HEAD_DIM
    tp = batch * seq
    return pl.pallas_call(
        _attn_prompt_kernel,
        grid=(batch, N_KV_HEADS),
        in_specs=[pl.BlockSpec((seq, gw), lambda b, g: (b, g)),
                  pl.BlockSpec((seq, HEAD_DIM), lambda b, g: (b, COL_K // HEAD_DIM + g)),
                  pl.BlockSpec((seq, HEAD_DIM), lambda b, g: (b, COL_V // HEAD_DIM + g)),
                  pl.BlockSpec((1, HEAD_DIM), lambda b, g: (0, 0)),
                  pl.BlockSpec((1, HEAD_DIM), lambda b, g: (0, 0))],
        out_specs=[pl.BlockSpec((seq, gw), lambda b, g: (b, g)),
                   pl.BlockSpec((seq, HEAD_DIM), lambda b, g: (b, g)),
                   pl.BlockSpec((seq, HEAD_DIM), lambda b, g: (b, g))],
        out_shape=[jax.ShapeDtypeStruct((tp, ATTN_DIM), F32),
                   jax.ShapeDtypeStruct((tp, KV_DIM), F32),
                   jax.ShapeDtypeStruct((tp, KV_DIM), F32)],
        compiler_params=_cparams("parallel", "parallel"),
        name="attn_context",
    )(z, z, z, qg.reshape(1, HEAD_DIM), kg.reshape(1, HEAD_DIM))


def _rope(x, cos, sin_signed):
    lane = lax.broadcasted_iota(jnp.int32, x.shape, 1)
    quarter = HEAD_DIM // 4
    fwd = pltpu.roll(x, HEAD_DIM - quarter, 1)
    bwd = pltpu.roll(x, quarter, 1)
    swapped = jnp.where((lane % (2 * quarter)) < quarter, fwd, bwd)
    return x * cos + swapped * sin_signed


def _krope_kernel(k_ref, c_ref, s_ref, kg_ref, o_ref):
    kn = _rms(k_ref[...].astype(F32)) * kg_ref[...]
    o_ref[...] = _rope(kn, c_ref[...], s_ref[...]).astype(o_ref.dtype)


def _krope_call(z, cos, sin_signed, kg, t_prompt, dec_batch, dec_seq, tr=512):
    nr = dec_seq // tr
    return pl.pallas_call(
        _krope_kernel,
        grid=(dec_batch, nr, N_KV_HEADS),
        in_specs=[pl.BlockSpec((tr, HEAD_DIM), lambda b, r, g: (t_prompt // tr + b * nr + r, COL_K // HEAD_DIM + g)),
                  pl.BlockSpec((tr, HEAD_DIM), lambda b, r, g: (r, 0)),
                  pl.BlockSpec((tr, HEAD_DIM), lambda b, r, g: (r, 0)),
                  pl.BlockSpec((1, HEAD_DIM), lambda b, r, g: (0, 0))],
        out_specs=pl.BlockSpec((tr, HEAD_DIM), lambda b, r, g: (b * nr + r, g)),
        out_shape=jax.ShapeDtypeStruct((dec_batch * dec_seq, KV_DIM), BF16),
        compiler_params=_cparams("parallel", "parallel", "parallel"),
        name="attn_key_rope",
    )(z, cos, sin_signed, kg.reshape(1, HEAD_DIM))


def _attn_sample_kernel(q_ref, k_ref, v_ref, ck_ref, cv_ref, cq_ref, sq_ref,
                        qg_ref, o_ref, qs_ref, m_ref, l_ref, acc_ref):
    j = pl.program_id(3)
    scale = HEAD_DIM ** -0.5

    @pl.when(j == 0)
    def _():
        for m in range(Q_PER_KV):
            sl = slice(m * HEAD_DIM, (m + 1) * HEAD_DIM)
            qn = _rms(q_ref[:, sl].astype(F32)) * qg_ref[...]
            qs_ref[m] = (_rope(qn, cq_ref[...], sq_ref[...]) * scale).astype(BF16)
        m_ref[...] = jnp.full(m_ref.shape, -jnp.inf, F32)
        l_ref[...] = jnp.zeros(l_ref.shape, F32)
        acc_ref[...] = jnp.zeros(acc_ref.shape, F32)

    def step(kb, vb):
        reps = kb.shape[0] // LANES
        for m in range(Q_PER_KV):
            s = lax.dot_general(qs_ref[m], kb, (((1,), (1,)), ((), ())), preferred_element_type=F32)
            m_prev = m_ref[m]
            m_next = jnp.maximum(m_prev, jnp.max(s, axis=-1, keepdims=True))
            alpha = jnp.exp(m_prev - m_next)
            p = jnp.exp(s - jnp.concatenate([m_next] * reps, axis=1))
            l_ref[m] = alpha * l_ref[m] + jnp.sum(p, axis=-1, keepdims=True)
            acc_ref[m] = alpha * acc_ref[m] + jnp.dot(p.astype(BF16), vb, preferred_element_type=F32)
            m_ref[m] = m_next

    @pl.when(j == 0)
    def _():
        step(ck_ref[...].astype(BF16), cv_ref[...].astype(BF16))

    @pl.when(j > 0)
    def _():
        step(k_ref[...], v_ref[...])

    @pl.when(j == pl.num_programs(3) - 1)
    def _():
        for m in range(Q_PER_KV):
            o_ref[:, m * HEAD_DIM:(m + 1) * HEAD_DIM] = acc_ref[m] / l_ref[m]


def _attn_sample_call(z, kr, ctx_k, ctx_v, cos, sin_signed, qg, t_prompt, dec_batch, dec_seq, tq=512, tk=1024):
    gw = Q_PER_KV * HEAD_DIM
    past = ctx_k.shape[1]
    tk = min(tk, dec_seq)
    nq = dec_seq // tq
    nk = dec_seq // tk
    qrow = lambda b, i: (t_prompt + b * dec_seq) // tq + i
    vrow = lambda b, j: (t_prompt + b * dec_seq) // tk + jnp.maximum(j - 1, 0)
    krow = lambda b, j: b * nk + jnp.maximum(j - 1, 0)
    return pl.pallas_call(
        _attn_sample_kernel,
        grid=(dec_batch, N_KV_HEADS, nq, nk + 1),
        in_specs=[pl.BlockSpec((tq, gw), lambda b, g, i, j: (qrow(b, i), g)),
                  pl.BlockSpec((tk, HEAD_DIM), lambda b, g, i, j: (krow(b, j), g)),
                  pl.BlockSpec((tk, HEAD_DIM), lambda b, g, i, j: (vrow(b, j), COL_V // HEAD_DIM + g)),
                  pl.BlockSpec((None, past, HEAD_DIM), lambda b, g, i, j: (b, 0, g)),
                  pl.BlockSpec((None, past, HEAD_DIM), lambda b, g, i, j: (b, 0, g)),
                  pl.BlockSpec((tq, HEAD_DIM), lambda b, g, i, j: (i, 0)),
                  pl.BlockSpec((tq, HEAD_DIM), lambda b, g, i, j: (i, 0)),
                  pl.BlockSpec((1, HEAD_DIM), lambda b, g, i, j: (0, 0))],
        out_specs=pl.BlockSpec((tq, gw), lambda b, g, i, j: (b * nq + i, g)),
        out_shape=jax.ShapeDtypeStruct((dec_batch * dec_seq, ATTN_DIM), F32),
        scratch_shapes=[pltpu.VMEM((Q_PER_KV, tq, HEAD_DIM), BF16),
                        pltpu.VMEM((Q_PER_KV, tq, LANES), F32),
                        pltpu.VMEM((Q_PER_KV, tq, LANES), F32),
                        pltpu.VMEM((Q_PER_KV, tq, HEAD_DIM), F32)],
        compiler_params=_cparams("parallel", "parallel", "parallel", "arbitrary"),
        name="attn_latent",
    )(z, kr, z, ctx_k, ctx_v, cos, sin_signed, qg.reshape(1, HEAD_DIM))


def _rope_tables(n_tokens):
    quarter = HEAD_DIM // 4
    n_rows = n_tokens // GRID_W
    row = jnp.repeat(jnp.arange(n_rows, dtype=F32), GRID_W)
    col = jnp.tile(jnp.arange(GRID_W, dtype=F32), n_rows)
    freqs = ROPE_THETA ** (-jnp.arange(quarter, dtype=F32) / quarter)
    ar = row[:, None] * freqs[None, :]
    ac = col[:, None] * freqs[None, :]
    cos = jnp.concatenate([jnp.cos(ar), jnp.cos(ar), jnp.cos(ac), jnp.cos(ac)], axis=-1)
    sin = jnp.concatenate([-jnp.sin(ar), jnp.sin(ar), -jnp.sin(ac), jnp.sin(ac)], axis=-1)
    return cos, sin


def _gelu_tanh(x):
    return 0.5 * x * (1.0 + jnp.tanh(math.sqrt(2.0 / math.pi) * (x + 0.044715 * (x * x * x))))


def _gmlp_kernel(z_ref, g_ref, b_ref, ws_ref, bs_ref, o_ref):
    hw = GM_DIM // GM_HEADS
    for ch in range(z_ref.shape[0] // CHUNK):
        rows = slice(ch * CHUNK, (ch + 1) * CHUNK)
        u = _gelu_tanh(z_ref[rows, :GM_DIM].astype(F32))
        v = _gelu_tanh(z_ref[rows, GM_DIM:].astype(F32))
        mu = jnp.mean(v, axis=-1, keepdims=True)
        vc = v - mu
        var = jnp.mean(vc * vc, axis=-1, keepdims=True)
        vn = (vc * lax.rsqrt(var + EPS) * g_ref[...] + b_ref[...]).astype(BF16)
        for h in range(GM_HEADS):
            cols = slice(h * hw, (h + 1) * hw)
            s = jnp.dot(ws_ref[h], vn[:, cols], preferred_element_type=F32) + bs_ref[h]
            o_ref[rows, cols] = u[:, cols] * s


def _gmlp_call(z, g, b, ws, bs, tr=512):
    t = z.shape[0]
    return pl.pallas_call(
        _gmlp_kernel,
        grid=(t // tr,),
        in_specs=[pl.BlockSpec((tr, 2 * GM_DIM), lambda i: (i, COL_GM // (2 * GM_DIM))),
                  pl.BlockSpec((1, GM_DIM), lambda i: (0, 0)),
                  pl.BlockSpec((1, GM_DIM), lambda i: (0, 0)),
                  pl.BlockSpec((GM_HEADS, CHUNK, CHUNK), lambda i: (0, 0, 0)),
                  pl.BlockSpec((GM_HEADS, CHUNK, 1), lambda i: (0, 0, 0))],
        out_specs=pl.BlockSpec((tr, GM_DIM), lambda i: (i, 0)),
        out_shape=jax.ShapeDtypeStruct((t, GM_DIM), F32),
        compiler_params=_cparams("parallel"),
        name="chunk_gmlp",
    )(z, g.reshape(1, GM_DIM), b.reshape(1, GM_DIM), ws.astype(BF16), bs.reshape(GM_HEADS, CHUNK, 1))


def _sconv_kernel(z_ref, w_ref, b_ref, o_ref, *, seg_len):
    x = z_ref[...].astype(F32)
    n = x.shape[0]
    pos = lax.broadcasted_iota(jnp.int32, x.shape, 0) % seg_len
    prev = jnp.where(pos == 0, 0.0, pltpu.roll(x, 1, 0))
    nxt = jnp.where(pos == seg_len - 1, 0.0, pltpu.roll(x, n - 1, 0))
    o_ref[...] = b_ref[...] + prev * w_ref[0:1, :] + x * w_ref[1:2, :] + nxt * w_ref[2:3, :]


def _sconv_call(z, w, b, row0, n_rows, seg_len, block_rows, cw=256):
    width = 3 * HY_DIM
    return pl.pallas_call(
        functools.partial(_sconv_kernel, seg_len=seg_len),
        grid=(n_rows // block_rows, width // cw),
        in_specs=[pl.BlockSpec((block_rows, cw), lambda i, c: (row0 // block_rows + i, COL_HY // cw + c)),
                  pl.BlockSpec((3, cw), lambda i, c: (0, c)),
                  pl.BlockSpec((1, cw), lambda i, c: (0, c))],
        out_specs=pl.BlockSpec((block_rows, cw), lambda i, c: (i, c)),
        out_shape=jax.ShapeDtypeStruct((n_rows, width), F32),
        compiler_params=_cparams("parallel", "parallel"),
        name="hyena_short_conv",
    )(z, w, b.reshape(1, width))


def _filter_kernel(z_ref, t_ref, w1_ref, b1_ref, w2_ref, b2_ref, w3a_ref, b3a_ref, w3b_ref, b3b_ref,
                   fr_ref, dl_ref, o_ref, *, tr):
    fr = fr_ref[...]
    h = jnp.sin(fr * (jnp.dot(z_ref[...], w1_ref[...], precision=HI, preferred_element_type=F32) + b1_ref[...]))
    h = jnp.sin(fr * (jnp.dot(h, w2_ref[...], precision=HI, preferred_element_type=F32) + b2_ref[...]))
    decay = jnp.exp(-t_ref[...] * dl_ref[...]) + MOD_SHIFT
    row = lax.broadcasted_iota(jnp.int32, decay.shape, 0) + pl.program_id(0) * tr
    decay = jnp.where(row == 0, 0.0, decay)
    for o, (w3_ref, b3_ref) in enumerate(((w3a_ref, b3a_ref), (w3b_ref, b3b_ref))):
        taps = jnp.dot(h, w3_ref[...], precision=HI, preferred_element_type=F32) + b3_ref[...]
        o_ref[:, o * HY_DIM:(o + 1) * HY_DIM] = taps * decay


def _filter_call(length, w1, b1, w2, b2, w3, b3, fr, tr=256):
    bands = (FILTER_EMB - 1) // 2
    t = np.linspace(0.0, 1.0, length)[:, None]
    wv = 2.0 * np.pi * np.arange(length)[:, None] / length
    f = np.linspace(1e-4, bands - 1, bands)[None, :]
    zf = np.concatenate([t, np.cos(f * wv), -np.sin(f * wv)], axis=-1)
    min_decay = math.log(DECAY_TARGET) / SLOW_DECAY_PCT
    max_decay = math.log(DECAY_TARGET) / FAST_DECAY_PCT
    deltas = np.abs(np.linspace(min_decay, max_decay, HY_DIM))[None, :]
    lag = np.minimum(np.abs(np.arange(2 * length) - length), length - 1)
    pad = LANES - FILTER_EMB
    hp = LANES - FILTER_HIDDEN
    z2 = jnp.asarray(np.pad(zf[lag], ((0, 0), (0, pad))).astype(np.float32))
    t2 = jnp.asarray(t[lag].astype(np.float32))
    dl = jnp.asarray(deltas.astype(np.float32))
    w1p = jnp.pad(w1, ((0, pad), (0, hp)))
    w2p = jnp.pad(w2, ((0, hp), (0, hp)))
    w3p = jnp.pad(w3, ((0, hp), (0, 0)))
    b1p = jnp.pad(b1, (0, hp)).reshape(1, LANES)
    b2p = jnp.pad(b2, (0, hp)).reshape(1, LANES)
    frp = jnp.pad(fr, (0, hp)).reshape(1, LANES)
    b3r = b3.reshape(1, -1)
    nblk = 2 * length // tr
    half = length // tr
    wcol = lambda r, o: o * 2 + jnp.where(r < half, 1, 0)
    small = pl.BlockSpec((1, LANES), lambda r: (0, 0))
    square = pl.BlockSpec((LANES, LANES), lambda r: (0, 0))
    return pl.pallas_call(
        functools.partial(_filter_kernel, tr=tr),
        grid=(nblk,),
        in_specs=[pl.BlockSpec((tr, LANES), lambda r: (r, 0)),
                  pl.BlockSpec((tr, 1), lambda r: (r, 0)),
                  square, small, square, small,
                  pl.BlockSpec((LANES, HY_DIM), lambda r: (0, wcol(r, 0))),
                  pl.BlockSpec((1, HY_DIM), lambda r: (0, wcol(r, 0))),
                  pl.BlockSpec((LANES, HY_DIM), lambda r: (0, wcol(r, 1))),
                  pl.BlockSpec((1, HY_DIM), lambda r: (0, wcol(r, 1))),
                  small,
                  pl.BlockSpec((1, HY_DIM), lambda r: (0, 0))],
        out_specs=pl.BlockSpec((tr, 2 * HY_DIM), lambda r: (r, 0)),
        out_shape=jax.ShapeDtypeStruct((2 * length, 2 * HY_DIM), F32),
        compiler_params=_cparams("parallel"),
        name="hyena_filter_mlp",
    )(z2, t2, w1p, b1p, w2p, b2p, w3p, b3r, w3p, b3r, frp, dl)


def _dft_constants(p):
    n = 2 * p
    idx = np.arange(p, dtype=np.float64)
    ang = 2.0 * np.pi * np.outer(idx, idx) / n
    re = np.cos(ang)
    im = -np.sin(ang)
    im[0, :] = np.cos(np.pi * idx)
    fwd = np.concatenate([re, im], axis=0)
    sign = np.where(np.arange(p) % 2 == 0, 1.0, -1.0)
    ar = (2.0 / n) * np.cos(ang)
    ar[:, 0] = 1.0 / n
    ai = -(2.0 / n) * np.sin(ang)
    ai[:, 0] = sign / n
    inv = np.concatenate([ar, ai], axis=1)
    as_bf16 = lambda a: jnp.asarray(a.astype(np.float32)).astype(BF16)
    return as_bf16(fwd), as_bf16(inv)


def _dft_kernel(f_ref, x_ref, o_ref):
    o_ref[...] = jnp.dot(f_ref[...], x_ref[...].astype(BF16), preferred_element_type=F32).astype(o_ref.dtype)


def _dft_call(x, fwd, p, col0, cw=256):
    nblk = x.shape[0] // p
    return pl.pallas_call(
        _dft_kernel,
        grid=(nblk, HY_DIM // cw),
        in_specs=[pl.BlockSpec((2 * p, p), lambda i, c: (0, 0)),
                  pl.BlockSpec((p, cw), lambda i, c: (i, col0 // cw + c))],
        out_specs=pl.BlockSpec((None, 2 * p, cw), lambda i, c: (i, 0, c)),
        out_shape=jax.ShapeDtypeStruct((nblk, 2 * p, HY_DIM), BF16),
        compiler_params=_cparams("parallel", "parallel"),
        name="hyena_block_dft",
    )(fwd, x)


def _filter_dft_kernel(f_ref, blk_ref, o_ref, tail_ref, *, p):
    b = pl.program_id(2)
    blk = blk_ref[...].astype(BF16)
    spec = jnp.dot(f_ref[...], blk, preferred_element_type=F32)

    @pl.when(b > 0)
    def _():
        o_ref[...] = (spec + tail_ref[...]).astype(o_ref.dtype)

    row = lax.broadcasted_iota(jnp.int32, spec.shape, 0)
    first = blk[0:1, :].astype(F32)
    real_row = jnp.logical_or(row < p, row == p)
    sign = (1 - 2 * (row % 2)).astype(F32)
    tail_ref[...] = sign * (spec - jnp.where(real_row, first, 0.0))


def _filter_dft_call(taps, fwd, p, cw=256):
    nb2 = taps.shape[0] // p
    nd = nb2 - 1
    ncb = HY_DIM // cw
    return pl.pallas_call(
        functools.partial(_filter_dft_kernel, p=p),
        grid=(2, ncb, nb2),
        in_specs=[pl.BlockSpec((2 * p, p), lambda o, c, b: (0, 0)),
                  pl.BlockSpec((p, cw), lambda o, c, b: (b, o * ncb + c))],
        out_specs=pl.BlockSpec((None, None, 2 * p, cw), lambda o, c, b: (o, jnp.maximum(b - 1, 0), 0, c)),
        out_shape=jax.ShapeDtypeStruct((2, nd, 2 * p, HY_DIM), BF16),
        scratch_shapes=[pltpu.VMEM((2 * p, cw), F32)],
        compiler_params=_cparams("parallel", "parallel", "arbitrary"),
        name="hyena_filter_dft",
    )(fwd, taps)


def _specconv_kernel(xs_ref, gs_ref, inv_ref, v_ref, gate_ref, bias_ref, o_ref, y_ref, *, nb, p, rc):
    i = pl.program_id(2)
    cw = o_ref.shape[1]

    for r in range(0, p, rc):
        def body(j, carry):
            yre, yim = carry
            d = i - j + nb - 1
            xre = xs_ref[j, r:r + rc, :].astype(F32)
            xim = xs_ref[j, p + r:p + r + rc, :].astype(F32)
            gre = gs_ref[d, r:r + rc, :].astype(F32)
            gim = gs_ref[d, p + r:p + r + rc, :].astype(F32)
            return yre + (xre * gre - xim * gim), yim + (xre * gim + xim * gre)

        zero = jnp.zeros((rc, cw), F32)
        yre, yim = lax.fori_loop(0, nb, body, (zero, zero))
        y_ref[r:r + rc, :] = yre
        y_ref[p + r:p + r + rc, :] = yim

    def edge(j, carry):
        y0, yn = carry
        d = i - j + nb - 1
        x0 = xs_ref[j, 0:16, :].astype(F32)
        xn = xs_ref[j, p:p + 16, :].astype(F32)
        g0 = gs_ref[d, 0:16, :].astype(F32)
        gn = gs_ref[d, p:p + 16, :].astype(F32)
        return y0 + x0 * g0, yn + xn * gn

    zero16 = jnp.zeros((16, cw), F32)
    y0, yn = lax.fori_loop(0, nb, edge, (zero16, zero16))
    y_ref[0:1, :] = y0[0:1, :]
    y_ref[p:p + 1, :] = yn[0:1, :]

    y = jnp.dot(inv_ref[...], y_ref[...].astype(BF16), preferred_element_type=F32)
    v = v_ref[...]
    o_ref[...] = gate_ref[...] * (y + bias_ref[...] * v)


def _specconv_call(xs, gs, inv, zc, bias, nseq, nb, p, v_col0, gate_col0, v_src=None, cw=256, rc=32):
    n_rows = nseq * nb * p
    xs4 = xs.reshape(nseq, nb, 2 * p, HY_DIM)
    if v_src is None:
        v_arr, v_spec = zc, pl.BlockSpec((p, cw), lambda c, s, i: (s * nb + i, v_col0 // cw + c))
    else:
        v_arr, v_spec = v_src, pl.BlockSpec((p, cw), lambda c, s, i: (s * nb + i, c))
    return pl.pallas_call(
        functools.partial(_specconv_kernel, nb=nb, p=p, rc=rc),
        grid=(HY_DIM // cw, nseq, nb),
        in_specs=[pl.BlockSpec((None, nb, 2 * p, cw), lambda c, s, i: (s, 0, 0, c)),
                  pl.BlockSpec((2 * nb - 1, 2 * p, cw), lambda c, s, i: (0, 0, c)),
                  pl.BlockSpec((p, 2 * p), lambda c, s, i: (0, 0)),
                  v_spec,
                  pl.BlockSpec((p, cw), lambda c, s, i: (s * nb + i, gate_col0 // cw + c)),
                  pl.BlockSpec((1, cw), lambda c, s, i: (0, c))],
        out_specs=pl.BlockSpec((p, cw), lambda c, s, i: (s * nb + i, c)),
        out_shape=jax.ShapeDtypeStruct((n_rows, HY_DIM), F32),
        scratch_shapes=[pltpu.VMEM((2 * p, cw), F32)],
        compiler_params=_cparams("parallel", "parallel", "parallel"),
        name="hyena_spectral_conv",
    )(xs4, gs, inv, v_arr, zc, bias.reshape(1, HY_DIM))


def _hyena_group(z, row0, nseq, length, p, prm, sconv_rows):
    nb = length // p
    zc = _sconv_call(z, prm['hy_conv_w'], prm['hy_conv_b'], row0, nseq * length, length, sconv_rows)
    taps = _filter_call(length, prm['hf_w1'], prm['hf_b1'], prm['hf_w2'], prm['hf_b2'],
                        prm['hf_w3'], prm['hf_b3'], prm['hf_freq'])
    fwd, inv = _dft_constants(p)
    gs = _filter_dft_call(taps, fwd, p)
    vs = _dft_call(zc, fwd, p, 2 * HY_DIM)
    u = _specconv_call(vs, gs[0], inv, zc, prm['hy_bias'][0], nseq, nb, p, 2 * HY_DIM, 0)
    us = _dft_call(u, fwd, p, 0)
    return _specconv_call(us, gs[1], inv, zc, prm['hy_bias'][1], nseq, nb, p, 0, HY_DIM, v_src=u)


def _mix_norm_kernel(ap_ref, as_ref, hp_ref, hs_ref, m_ref, g_ref, o_ref, *, ntp):
    i = pl.program_id(0)

    def fill(a_ref, h_ref):
        o_ref[:, :ATTN_DIM] = (_rms(a_ref[...]) * g_ref[:, :ATTN_DIM]).astype(BF16)
        o_ref[:, ATTN_DIM:ATTN_DIM + HY_DIM] = (
            _rms(h_ref[...]) * g_ref[:, ATTN_DIM:ATTN_DIM + HY_DIM]).astype(BF16)
        o_ref[:, ATTN_DIM + HY_DIM:] = (_rms(m_ref[...]) * g_ref[:, ATTN_DIM + HY_DIM:]).astype(BF16)

    @pl.when(i < ntp)
    def _():
        fill(ap_ref, hp_ref)

    @pl.when(i >= ntp)
    def _():
        fill(as_ref, hs_ref)


def _mix_norm_call(attn_p, attn_s, hy_p, hy_s, gm, g, t_prompt, tm=512):
    t = gm.shape[0]
    mix_dim = ATTN_DIM + HY_DIM + GM_DIM
    ntp = t_prompt // tm
    nts = t // tm - ntp
    prow = lambda i: (jnp.minimum(i, ntp - 1), 0)
    srow = lambda i: (jnp.clip(i - ntp, 0, nts - 1), 0)
    return pl.pallas_call(
        functools.partial(_mix_norm_kernel, ntp=ntp),
        grid=(t // tm,),
        in_specs=[pl.BlockSpec((tm, ATTN_DIM), prow),
                  pl.BlockSpec((tm, ATTN_DIM), srow),
                  pl.BlockSpec((tm, HY_DIM), prow),
                  pl.BlockSpec((tm, HY_DIM), srow),
                  pl.BlockSpec((tm, GM_DIM), lambda i: (i, 0)),
                  pl.BlockSpec((1, mix_dim), lambda i: (0, 0))],
        out_specs=pl.BlockSpec((tm, mix_dim), lambda i: (i, 0)),
        out_shape=jax.ShapeDtypeStruct((t, mix_dim), BF16),
        compiler_params=_cparams("parallel"),
        name="mix_group_norm",
    )(attn_p, attn_s, hy_p, hy_s, gm, g.reshape(1, mix_dim))


def _router_kernel(x_ref, g_ref, sh_ref, sc_ref, wr_ref, br_ref, h_ref, r_ref):
    y = _rms(x_ref[...]) * g_ref[...]
    h = y * (1.0 + sc_ref[...]) + sh_ref[...]
    h_ref[...] = _pack_bf16_pairs(h)
    w = wr_ref[...]
    h_hi = h.astype(BF16)
    h_lo = (h - h_hi.astype(F32)).astype(BF16)
    w_hi = w.astype(BF16)
    w_lo = (w - w_hi.astype(F32)).astype(BF16)
    logits = (jnp.dot(h_hi, w_hi, preferred_element_type=F32) + jnp.dot(h_hi, w_lo, preferred_element_type=F32)
              + jnp.dot(h_lo, w_hi, preferred_element_type=F32)) + br_ref[...]
    col = lambda k: logits[:, k:k + 1]

    lg = [col(k) for k in range(N_GROUPS)]
    g_max = functools.reduce(jnp.maximum, lg)
    g_den = sum(jnp.exp(v - g_max) for v in lg)
    g_top = 1.0 / g_den
    g_idx = jnp.full(g_max.shape, N_GROUPS - 1, jnp.int32)
    for k in range(N_GROUPS - 2, -1, -1):
        g_idx = jnp.where(lg[k] == g_max, k, g_idx)

    le = []
    for e in range(EXPERTS_PER_GROUP):
        v = col(N_GROUPS + (N_GROUPS - 1) * EXPERTS_PER_GROUP + e)
        for k in range(N_GROUPS - 2, -1, -1):
            v = jnp.where(g_idx == k, col(N_GROUPS + k * EXPERTS_PER_GROUP + e), v)
        le.append(v)

    e_max = functools.reduce(jnp.maximum, le)
    e1 = jnp.full(e_max.shape, EXPERTS_PER_GROUP - 1, jnp.int32)
    for e in range(EXPERTS_PER_GROUP - 2, -1, -1):
        e1 = jnp.where(le[e] == e_max, e, e1)
    neg = jnp.float32(-jnp.inf)
    rest = [jnp.where(e1 == e, neg, le[e]) for e in range(EXPERTS_PER_GROUP)]
    e2_max = functools.reduce(jnp.maximum, rest)
    e2 = jnp.full(e_max.shape, EXPERTS_PER_GROUP - 1, jnp.int32)
    for e in range(EXPERTS_PER_GROUP - 2, -1, -1):
        e2 = jnp.where(rest[e] == e2_max, e, e2)
    ratio = jnp.exp(e2_max - e_max)
    w1 = g_top / (1.0 + ratio)
    w2 = g_top * ratio / (1.0 + ratio)
    id1 = (g_idx * EXPERTS_PER_GROUP + e1).astype(F32)
    id2 = (g_idx * EXPERTS_PER_GROUP + e2).astype(F32)

    lane = lax.broadcasted_iota(jnp.int32, r_ref.shape, 1)
    r_ref[...] = jnp.where(lane == 0, id1, jnp.where(lane == 1, id2, jnp.where(lane == 2, w1, jnp.where(lane == 3, w2, 0.0))))


def _router_call(x, g, mod, wr, br, t_prompt, dec_seq, tm=256):
    t, d = x.shape
    seg = lambda i: _seg_of_row(i * tm, t_prompt, dec_seq)
    return pl.pallas_call(
        _router_kernel,
        grid=(t // tm,),
        in_specs=[pl.BlockSpec((tm, d), lambda i: (i, 0)),
                  pl.BlockSpec((1, d), lambda i: (0, 0)),
                  pl.BlockSpec((None, 1, d), lambda i: (seg(i), 0, 3)),
                  pl.BlockSpec((None, 1, d), lambda i: (seg(i), 0, 4)),
                  pl.BlockSpec((d, LANES), lambda i: (0, 0)),
                  pl.BlockSpec((1, LANES), lambda i: (0, 0))],
        out_specs=[pl.BlockSpec((tm, d // 2), lambda i: (i, 0)),
                   pl.BlockSpec((tm, LANES), lambda i: (i, 0))],
        out_shape=[jax.ShapeDtypeStruct((t, d // 2), jnp.uint32),
                   jax.ShapeDtypeStruct((t, LANES), F32)],
        compiler_params=_cparams("parallel"),
        name="norm2_router",
    )(x, g.reshape(1, d), mod, mod, wr, br)


def _dispatch_kernel(dest_ref, zt_ref, h_ref, xs_ref, zero_ref, sem, zsem, *, tm, t, expert_tile):
    i = pl.program_id(0)

    @pl.when(i == 0)
    def _():
        zero_ref[...] = jnp.zeros(zero_ref.shape, zero_ref.dtype)

        def clear(q):
            row = pl.multiple_of(jnp.maximum(zt_ref[q], 0), expert_tile)
            return pltpu.make_async_copy(zero_ref, xs_ref.at[pl.ds(row, expert_tile), :], zsem)

        for q in range(2 * N_EXPERTS):
            @pl.when(zt_ref[q] >= 0)
            def _():
                clear(q).start()

        for q in range(2 * N_EXPERTS):
            @pl.when(zt_ref[q] >= 0)
            def _():
                clear(q).wait()

    base = i * tm

    def body(r, c):
        for k in range(2):
            row = dest_ref[k * t + base + r]
            pltpu.make_async_copy(h_ref.at[pl.ds(r, 1), :], xs_ref.at[pl.ds(row, 1), :], sem).start()
        return c

    lax.fori_loop(0, tm, body, 0, unroll=8)
    for k in range(2):
        pltpu.make_async_copy(h_ref, xs_ref.at[pl.ds(0, tm), :], sem).wait()


def _dispatch_call(dest, zero_tiles, h, n_rows, expert_tile, tm=256):
    t, d = h.shape
    return pl.pallas_call(
        functools.partial(_dispatch_kernel, tm=tm, t=t, expert_tile=expert_tile),
        grid_spec=pltpu.PrefetchScalarGridSpec(
            num_scalar_prefetch=2,
            grid=(t // tm,),
            in_specs=[pl.BlockSpec((tm, d), lambda i, dest, zt: (i, 0))],
            out_specs=pl.BlockSpec(memory_space=pl.ANY),
            scratch_shapes=[pltpu.VMEM((expert_tile, d), h.dtype),
                            pltpu.SemaphoreType.DMA(()),
                            pltpu.SemaphoreType.DMA(())]),
        out_shape=jax.ShapeDtypeStruct((n_rows, d), h.dtype),
        compiler_params=_cparams("arbitrary", disable_bounds_checks=True),
        name="moe_dispatch_rows",
    )(dest, zero_tiles, h)


def _expert_changed(te_ref):
    m = pl.program_id(1)
    return jnp.logical_or(m == 0, te_ref[m] != te_ref[jnp.maximum(m - 1, 0)])


def _expert_up_kernel(te_ref, nv_ref, x_ref, wg_ref, wu_ref, o_ref, wgb_ref, wub_ref, *, tm):
    valid = pl.program_id(1) * tm < nv_ref[0]

    @pl.when(jnp.logical_and(valid, _expert_changed(te_ref)))
    def _():
        wgb_ref[...] = wg_ref[...].astype(BF16)
        wub_ref[...] = wu_ref[...].astype(BF16)

    @pl.when(valid)
    def _():
        x = _unpack_bf16_pairs(x_ref[...]).astype(BF16)
        hg = jnp.dot(x, wgb_ref[...], preferred_element_type=F32)
        hu = jnp.dot(x, wub_ref[...], preferred_element_type=F32)
        o_ref[...] = (hg * _sigmoid(hg) * hu).astype(o_ref.dtype)

    @pl.when(jnp.logical_not(valid))
    def _():
        o_ref[...] = jnp.zeros(o_ref.shape, o_ref.dtype)


def _expert_up_call(tile_e, n_valid, xs, wg, wu, layer, tm, nf=2):
    n, dh = xs.shape
    d, ff = wg.shape[2], wg.shape[3]
    tf = ff // nf
    wspec = pl.BlockSpec((None, None, d, tf), lambda f, m, te, nv: (layer, te[m], 0, f))
    return pl.pallas_call(
        functools.partial(_expert_up_kernel, tm=tm),
        grid_spec=pltpu.PrefetchScalarGridSpec(
            num_scalar_prefetch=2,
            grid=(nf, n // tm),
            in_specs=[pl.BlockSpec((tm, dh), lambda f, m, te, nv: (m, 0)), wspec, wspec],
            out_specs=pl.BlockSpec((tm, tf), lambda f, m, te, nv: (m, f)),
            scratch_shapes=[pltpu.VMEM((d, tf), BF16), pltpu.VMEM((d, tf), BF16)]),
        out_shape=jax.ShapeDtypeStruct((n, ff), BF16),
        compiler_params=_cparams("arbitrary", "arbitrary"),
        name="moe_expert_up",
    )(tile_e, n_valid, xs, wg, wu)


def _expert_down_kernel(te_ref, nv_ref, a_ref, wd_ref, o_ref, wdb_ref, *, tm):
    valid = pl.program_id(1) * tm < nv_ref[0]

    @pl.when(jnp.logical_and(valid, _expert_changed(te_ref)))
    def _():
        wdb_ref[...] = wd_ref[...].astype(BF16)

    @pl.when(valid)
    def _():
        o_ref[...] = _pack_bf16_pairs(jnp.dot(a_ref[...], wdb_ref[...], preferred_element_type=F32))

    @pl.when(jnp.logical_not(valid))
    def _():
        o_ref[...] = jnp.zeros(o_ref.shape, o_ref.dtype)


def _expert_down_call(tile_e, n_valid, a, wd, layer, tm, nn=2):
    n, ff = a.shape
    d = wd.shape[3]
    tn = d // nn
    return pl.pallas_call(
        functools.partial(_expert_down_kernel, tm=tm),
        grid_spec=pltpu.PrefetchScalarGridSpec(
            num_scalar_prefetch=2,
            grid=(nn, n // tm),
            in_specs=[pl.BlockSpec((tm, ff), lambda c, m, te, nv: (m, 0)),
                      pl.BlockSpec((None, None, ff, tn), lambda c, m, te, nv: (layer, te[m], 0, c))],
            out_specs=pl.BlockSpec((tm, tn // 2), lambda c, m, te, nv: (m, c)),
            scratch_shapes=[pltpu.VMEM((ff, tn), BF16)]),
        out_shape=jax.ShapeDtypeStruct((n, d // 2), jnp.uint32),
        compiler_params=_cparams("arbitrary", "arbitrary"),
        name="moe_expert_down",
    )(tile_e, n_valid, a, wd)


def _combine_kernel(dest_ref, x_ref, r_ref, gate_ref, fg_ref, ys_ref, *rest, tc, t, nt, ntp, final):
    out_refs, (ybuf, sem) = rest[:-2], rest[-2:]
    i = pl.program_id(0)

    def fetch(tile, slot):
        base = tile * tc

        def body(r, c):
            for k in range(2):
                row = dest_ref[k * t + base + r]
                pltpu.make_async_copy(ys_ref.at[pl.ds(row, 1), :], ybuf.at[slot, pl.ds(k * tc + r, 1), :],
                                      sem.at[slot]).start()
            return c

        lax.fori_loop(0, tc, body, 0, unroll=8)

    @pl.when(i == 0)
    def _():
        fetch(0, 0)

    @pl.when(i + 1 < nt)
    def _():
        fetch(i + 1, (i + 1) % 2)

    slot = i % 2
    pltpu.make_async_copy(ys_ref.at[pl.ds(0, 2 * tc), :], ybuf.at[slot], sem.at[slot]).wait()

    w1 = r_ref[:, 2:3]
    w2 = r_ref[:, 3:4]
    y = w1 * _unpack_halves(ybuf[slot, 0:tc, :]) + w2 * _unpack_halves(ybuf[slot, tc:2 * tc, :])
    x = x_ref[...] + gate_ref[...] * y
    if final:
        x = _rms(x) * fg_ref[...]
        op_ref, os_ref = out_refs

        @pl.when(i < ntp)
        def _():
            op_ref[...] = x

        @pl.when(i >= ntp)
        def _():
            os_ref[...] = x
    else:
        out_refs[0][...] = x


def _combine_call(dest, x, route, ys, mod, final_g, final, t_prompt, dec_seq, tc=128):
    t, d = x.shape
    seg = lambda i, *_: _seg_of_row(i * tc, t_prompt, dec_seq)
    nt = t // tc
    ntp = t_prompt // tc
    if final:
        out_specs = [pl.BlockSpec((tc, d), lambda i, dest: (jnp.minimum(i, ntp - 1), 0)),
                     pl.BlockSpec((tc, d), lambda i, dest: (jnp.maximum(i - ntp, 0), 0))]
        out_shape = [jax.ShapeDtypeStruct((t_prompt, d), F32), jax.ShapeDtypeStruct((t - t_prompt, d), F32)]
    else:
        out_specs = pl.BlockSpec((tc, d), lambda i, dest: (i, 0))
        out_shape = jax.ShapeDtypeStruct((t, d), F32)
    return pl.pallas_call(
        functools.partial(_combine_kernel, tc=tc, t=t, nt=nt, ntp=ntp, final=final),
        grid_spec=pltpu.PrefetchScalarGridSpec(
            num_scalar_prefetch=1,
            grid=(nt,),
            in_specs=[pl.BlockSpec((tc, d), lambda i, dest: (i, 0)),
                      pl.BlockSpec((tc, LANES), lambda i, dest: (i, 0)),
                      pl.BlockSpec((None, 1, d), lambda i, dest: (seg(i), 0, 5)),
                      pl.BlockSpec((1, d), lambda i, dest: (0, 0)),
                      pl.BlockSpec(memory_space=pl.ANY)],
            out_specs=out_specs,
            scratch_shapes=[pltpu.VMEM((2, 2 * tc, ys.shape[1]), ys.dtype),
                            pltpu.SemaphoreType.DMA((2,))]),
        out_shape=out_shape,
        compiler_params=_cparams("arbitrary", disable_bounds_checks=True),
        name="moe_combine_residual",
    )(dest, x, route, mod, final_g.reshape(1, d), ys)


def _dispatch_plan(route, tm):
    t = route.shape[0]
    flat_e = route[:, 0:2].astype(jnp.int32).T.reshape(-1)
    n_pairs = 2 * t
    n_tiles = n_pairs // tm + N_EXPERTS
    onehot = (flat_e[:, None] == jnp.arange(N_EXPERTS, dtype=jnp.int32)[None, :]).astype(jnp.int32)
    running = jnp.cumsum(onehot, axis=0)
    counts = running[-1]
    rank = jnp.sum((running - 1) * onehot, axis=1)
    padded = ((counts + tm - 1) // tm) * tm
    pad_end = jnp.cumsum(padded)
    pad_start = pad_end - padded
    dest = (jnp.sum(onehot * pad_start[None, :], axis=1) + rank).astype(jnp.int32)
    tile_start = jnp.arange(n_tiles, dtype=jnp.int32) * tm
    tile_e = jnp.sum((pad_end[None, :] <= tile_start[:, None]).astype(jnp.int32), axis=1)
    tile_e = jnp.minimum(tile_e, N_EXPERTS - 1).astype(jnp.int32)
    n_valid = pad_end[-1:].astype(jnp.int32)
    last = jnp.where(counts > 0, pad_end - tm, -1)
    tail = pad_end[-1] + jnp.arange(N_EXPERTS, dtype=jnp.int32) * tm
    tail = jnp.where(tail < n_tiles * tm, tail, -1)
    zero_tiles = jnp.concatenate([last, tail]).astype(jnp.int32)
    return dest, tile_e, n_valid, zero_tiles, n_tiles * tm


def _moe(h, route, wg, wu, wd, layer, tm=256):
    dest, tile_e, n_valid, zero_tiles, n_rows = _dispatch_plan(route, tm)
    xs = _dispatch_call(dest, zero_tiles, h, n_rows, tm)
    a = _expert_up_call(tile_e, n_valid, xs, wg, wu, layer, tm)
    ys = _expert_down_call(tile_e, n_valid, a, wd, layer, tm)
    return ys, dest


def _trunk(x_prompt, x_sample, cache_k, cache_v, c, c_ctx, prm, final_norm_g, hy_block_sample=512):
    batch, seq, d = x_prompt.shape
    dec_batch, dec_seq, _ = x_sample.shape
    depth = prm['w_in'].shape[0]
    past = cache_k.shape[2]
    t_prompt = batch * seq
    t_sample = dec_batch * dec_seq

    x = jnp.concatenate([x_prompt.reshape(t_prompt, d), x_sample.reshape(t_sample, d)], axis=0)
    n_cond = 1 + dec_batch
    cond = jnp.concatenate([c_ctx[None, :], c, jnp.zeros((8 - n_cond % 8, d), F32)], axis=0)
    mod_all = _mod_call(cond, prm['w_mod'], prm['b_mod'])
    cos, sin_signed = _rope_tables(dec_seq)
    sconv_rows_p = math.gcd(t_prompt, max(seq, 4096 // seq * seq))

    new_k, new_v = [], []
    big = ('w_mod', 'w_in', 'w_out', 'exp_w_gate', 'exp_w_up', 'exp_w_down')
    w_in, w_out = prm['w_in'].astype(BF16), prm['w_out'].astype(BF16)
    w_gate, w_up, w_down = prm['exp_w_gate'], prm['exp_w_up'], prm['exp_w_down']

    for l in range(depth):
        p = {name: w[l] for name, w in prm.items() if name not in big}
        mod = mod_all[l, :n_cond].reshape(n_cond, 1, 6 * d)

        h1 = _norm_mod_call(x, p['norm1_g'], mod, t_prompt, dec_seq)
        z = _proj_call(h1, w_in, l, "in_proj")

        attn_p, k_l, v_l = _attn_prompt_call(z, p['q_norm_g'], p['k_norm_g'], batch, seq)
        kr = _krope_call(z, cos, sin_signed, p['k_norm_g'], t_prompt, dec_batch, dec_seq)
        attn_s = _attn_sample_call(z, kr, cache_k[:, l].reshape(dec_batch, past, KV_DIM),
                                   cache_v[:, l].reshape(dec_batch, past, KV_DIM),
                                   cos, sin_signed, p['q_norm_g'], t_prompt, dec_batch, dec_seq)
        new_k.append(k_l.reshape(batch, seq, N_KV_HEADS, HEAD_DIM))
        new_v.append(v_l.reshape(batch, seq, N_KV_HEADS, HEAD_DIM))

        hy_p = _hyena_group(z, 0, batch, seq, seq, p, sconv_rows_p)
        hy_s = _hyena_group(z, t_prompt, dec_batch, dec_seq, min(hy_block_sample, dec_seq), p, dec_seq)

        gm = _gmlp_call(z, p['gm_norm_g'], p['gm_norm_b'], p['gm_ws'], p['gm_bs'])

        mix = _mix_norm_call(attn_p, attn_s, hy_p, hy_s, gm, p['out_norm_g'], t_prompt)
        x = _proj_call(mix, w_out, l, "out_proj_residual", out_dtype=F32, residual=(x, mod, 2, t_prompt, dec_seq))

        wr = jnp.concatenate([p['router_g_w'],
                              p['router_e_w'].transpose(1, 0, 2).reshape(d, N_EXPERTS)], axis=1)
        wr = jnp.pad(wr, ((0, 0), (0, LANES - wr.shape[1])))
        br = jnp.pad(jnp.concatenate([p['router_g_b'], p['router_e_b'].reshape(-1)]),
                     (0, LANES - N_GROUPS - N_EXPERTS)).reshape(1, LANES)
        h2, route = _router_call(x, p['norm2_g'], mod, wr, br, t_prompt, dec_seq)
        ys, dest = _moe(h2, route, w_gate, w_up, w_down, l)
        x = _combine_call(dest, x, route, ys, mod, final_norm_g, l == depth - 1, t_prompt, dec_seq)

    y_prompt = x[0].reshape(batch, seq, d)
    y_sample = x[1].reshape(dec_batch, dec_seq, d)
    return y_prompt, y_sample, jnp.stack(new_k, axis=1), jnp.stack(new_v, axis=1)


_PARAM_NAMES = ('norm1_g', 'norm2_g', 'w_mod', 'b_mod', 'w_in', 'q_norm_g', 'k_norm_g', 'hy_conv_w', 'hy_conv_b',
                'hf_w1', 'hf_b1', 'hf_w2', 'hf_b2', 'hf_w3', 'hf_b3', 'hf_freq', 'hy_bias', 'gm_norm_g',
                'gm_norm_b', 'gm_ws', 'gm_bs', 'out_norm_g', 'w_out', 'router_g_w', 'router_g_b', 'router_e_w',
                'router_e_b', 'exp_w_gate', 'exp_w_up', 'exp_w_down')


def kernel(x_prompt, x_sample, cache_k, cache_v, c, c_ctx, norm1_g, norm2_g, w_mod, b_mod, w_in, q_norm_g, k_norm_g, hy_conv_w, hy_conv_b, hf_w1, hf_b1, hf_w2, hf_b2, hf_w3, hf_b3, hf_freq, hy_bias, gm_norm_g, gm_norm_b, gm_ws, gm_bs, out_norm_g, w_out, router_g_w, router_g_b, router_e_w, router_e_b, exp_w_gate, exp_w_up, exp_w_down, final_norm_g):
    values = (norm1_g, norm2_g, w_mod, b_mod, w_in, q_norm_g, k_norm_g, hy_conv_w, hy_conv_b, hf_w1, hf_b1, hf_w2,
              hf_b2, hf_w3, hf_b3, hf_freq, hy_bias, gm_norm_g, gm_norm_b, gm_ws, gm_bs, out_norm_g, w_out,
              router_g_w, router_g_b, router_e_w, router_e_b, exp_w_gate, exp_w_up, exp_w_down)
    prm = dict(zip(_PARAM_NAMES, values))
    return _trunk(x_prompt, x_sample, cache_k, cache_v, c, c_ctx, prm, final_norm_g)
```

```python
import functools
import math

import numpy as np
import jax
import jax.numpy as jnp
from jax import lax
from jax.experimental import pallas as pl
from jax.experimental.pallas import tpu as pltpu

F32 = jnp.float32
BF16 = jnp.bfloat16

D_MODEL = 4096
GRID_W = 64
HEAD_DIM = 128
N_HEADS = 16
N_KV_HEADS = 4
Q_PER_KV = N_HEADS // N_KV_HEADS
ATTN_DIM = N_HEADS * HEAD_DIM
KV_DIM = N_KV_HEADS * HEAD_DIM
HY_DIM = 1024
GM_DIM = 1024
GM_HEADS = 8
CHUNK = 128
PROJ_DIM = ATTN_DIM + 2 * KV_DIM + 3 * HY_DIM + 2 * GM_DIM
ROPE_THETA = 10000.0
FILTER_EMB = 33
FILTER_HIDDEN = 64
DECAY_TARGET = 1e-2
FAST_DECAY_PCT = 0.3
SLOW_DECAY_PCT = 1.5
MOD_SHIFT = 0.05
N_GROUPS = 4
EXPERTS_PER_GROUP = 4
N_EXPERTS = 16
EXPERT_FF = 1024
EPS = 1e-6

COL_K = ATTN_DIM
COL_V = ATTN_DIM + KV_DIM
COL_HY = ATTN_DIM + 2 * KV_DIM
COL_GM = COL_HY + 3 * HY_DIM

LANES = 128
VMEM_LIMIT = 56 * 1024 * 1024
HI = lax.Precision.HIGHEST


def _cparams(*sem, **kw):
    return pltpu.CompilerParams(dimension_semantics=sem, vmem_limit_bytes=VMEM_LIMIT, **kw)


def _rms(x):
    return x * lax.rsqrt(jnp.mean(x * x, axis=-1, keepdims=True) + EPS)


def _sigmoid(x):
    return 1.0 / (1.0 + jnp.exp(-x))


def _dot_3pass(a, b):
    a_hi = a.astype(BF16)
    a_lo = (a - a_hi.astype(F32)).astype(BF16)
    b_hi = b.astype(BF16)
    b_lo = (b - b_hi.astype(F32)).astype(BF16)
    return (jnp.dot(a_hi, b_hi, preferred_element_type=F32) + jnp.dot(a_hi, b_lo, preferred_element_type=F32)
            + jnp.dot(a_lo, b_hi, preferred_element_type=F32))


def _pack_bf16_pairs(x):
    n = x.shape[1] // 2
    lo = lax.bitcast_convert_type(x[:, :n].astype(BF16).astype(F32), jnp.uint32)
    hi = lax.bitcast_convert_type(x[:, n:].astype(BF16).astype(F32), jnp.uint32)
    return hi | (lo >> 16)


def _unpack_bf16_pairs(u):
    lo = lax.bitcast_convert_type(u << 16, F32)
    hi = lax.bitcast_convert_type(u & jnp.uint32(0xFFFF0000), F32)
    return jnp.concatenate([lo, hi], axis=1)


def _seg_of_row(row, t_prompt, dec_seq):
    return jnp.where(row < t_prompt, 0, 1 + (row - t_prompt) // dec_seq)


def _mod_kernel(c_ref, w_ref, b_ref, o_ref):
    c = c_ref[...]
    s = (c * _sigmoid(c)).astype(BF16)
    o_ref[...] = jnp.dot(s, w_ref[...].astype(BF16), preferred_element_type=F32) + b_ref[...]


def _mod_call(cond, w_mod, b_mod):
    depth, d, n = w_mod.shape
    r = cond.shape[0]
    tn = 512
    return pl.pallas_call(
        _mod_kernel,
        grid=(depth, n // tn),
        in_specs=[pl.BlockSpec((r, d), lambda l, j: (0, 0)),
                  pl.BlockSpec((None, d, tn), lambda l, j: (l, 0, j)),
                  pl.BlockSpec((None, 1, tn), lambda l, j: (l, 0, j))],
        out_specs=pl.BlockSpec((None, r, tn), lambda l, j: (l, 0, j)),
        out_shape=jax.ShapeDtypeStruct((depth, r, n), F32),
        compiler_params=_cparams("parallel", "parallel"),
        name="adaln_mod",
    )(cond, w_mod, b_mod.reshape(depth, 1, n))


def _norm_mod_kernel(x_ref, g_ref, sh_ref, sc_ref, o_ref):
    y = _rms(x_ref[...]) * g_ref[...]
    o_ref[...] = (y * (1.0 + sc_ref[...]) + sh_ref[...]).astype(o_ref.dtype)


def _norm_mod_call(x, g, mod, t_prompt, dec_seq, tm=512):
    t, d = x.shape
    seg = lambda i: _seg_of_row(i * tm, t_prompt, dec_seq)
    return pl.pallas_call(
        _norm_mod_kernel,
        grid=(t // tm,),
        in_specs=[pl.BlockSpec((tm, d), lambda i: (i, 0)),
                  pl.BlockSpec((1, d), lambda i: (0, 0)),
                  pl.BlockSpec((None, 1, d), lambda i: (seg(i), 0, 0)),
                  pl.BlockSpec((None, 1, d), lambda i: (seg(i), 0, 1))],
        out_specs=pl.BlockSpec((tm, d), lambda i: (i, 0)),
        out_shape=jax.ShapeDtypeStruct((t, d), BF16),
        compiler_params=_cparams("parallel"),
        name="norm1_modulate",
    )(x, g.reshape(1, d), mod, mod)


def _proj_kernel(a_ref, w_ref, o_ref):
    o_ref[...] = jnp.dot(a_ref[...], w_ref[...], preferred_element_type=F32).astype(o_ref.dtype)


def _proj_residual_kernel(a_ref, w_ref, x_ref, gate_ref, o_ref):
    o_ref[...] = x_ref[...] + gate_ref[...] * jnp.dot(a_ref[...], w_ref[...], preferred_element_type=F32)


def _proj_call(a, w, layer, name, out_dtype=BF16, residual=None, tm=1024, tn=1024):
    t, k = a.shape
    n = w.shape[2]
    in_specs = [pl.BlockSpec((tm, k), lambda i, j: (i, 0)),
                pl.BlockSpec((None, k, tn), lambda i, j: (layer, 0, j))]
    args = [a, w]
    body = _proj_kernel
    if residual is not None:
        x, mod, gate_block, t_prompt, dec_seq = residual
        seg = lambda i: _seg_of_row(i * tm, t_prompt, dec_seq)
        in_specs += [pl.BlockSpec((tm, tn), lambda i, j: (i, j)),
                     pl.BlockSpec((None, 1, tn), lambda i, j: (seg(i), 0, gate_block * (n // tn) + j))]
        args += [x, mod]
        body = _proj_residual_kernel
    return pl.pallas_call(
        body,
        grid=(t // tm, n // tn),
        in_specs=in_specs,
        out_specs=pl.BlockSpec((tm, tn), lambda i, j: (i, j)),
        out_shape=jax.ShapeDtypeStruct((t, n), out_dtype),
        compiler_params=_cparams("parallel", "parallel"),
        name=name,
    )(*args)


def _attn_prompt_kernel(q_ref, k_ref, v_ref, qg_ref, kg_ref, o_ref, kc_ref, vc_ref):
    kn = _rms(k_ref[...].astype(F32)) * kg_ref[...]
    vb = v_ref[...]
    kc_ref[...] = kn
    vc_ref[...] = vb.astype(F32)
    kb = kn.astype(BF16)
    scale = HEAD_DIM ** -0.5
    for m in range(Q_PER_KV):
        sl = slice(m * HEAD_DIM, (m + 1) * HEAD_DIM)
        qn = (_rms(q_ref[:, sl].astype(F32)) * qg_ref[...] * scale).astype(BF16)
        s = lax.dot_general(qn, kb, (((1,), (1,)), ((), ())), preferred_element_type=F32)
        p = jnp.exp(s - jnp.max(s, axis=-1, keepdims=True))
        l = jnp.sum(p, axis=-1, keepdims=True)
        o = jnp.dot(p.astype(BF16), vb, preferred_element_type=F32)
        o_ref[:, sl] = o / l


def _attn_prompt_call(z, qg, kg, batch, seq):
    gw = Q_PER_KV * HEAD_DIM
    tp = batch * seq
    return pl.pallas_call(
        _attn_prompt_kernel,
        grid=(batch, N_KV_HEADS),
        in_specs=[pl.BlockSpec((seq, gw), lambda b, g: (b, g)),
                  pl.BlockSpec((seq, HEAD_DIM), lambda b, g: (b, COL_K // HEAD_DIM + g)),
                  pl.BlockSpec((seq, HEAD_DIM), lambda b, g: (b, COL_V // HEAD_DIM + g)),
                  pl.BlockSpec((1, HEAD_DIM), lambda b, g: (0, 0)),
                  pl.BlockSpec((1, HEAD_DIM), lambda b, g: (0, 0))],
        out_specs=[pl.BlockSpec((seq, gw), lambda b, g: (b, g)),
                   pl.BlockSpec((seq, HEAD_DIM), lambda b, g: (b, g)),
                   pl.BlockSpec((seq, HEAD_DIM), lambda b, g: (b, g))],
        out_shape=[jax.ShapeDtypeStruct((tp, ATTN_DIM), F32),
                   jax.ShapeDtypeStruct((tp, KV_DIM), F32),
                   jax.ShapeDtypeStruct((tp, KV_DIM), F32)],
        compiler_params=_cparams("parallel", "parallel"),
        name="attn_context",
    )(z, z, z, qg.reshape(1, HEAD_DIM), kg.reshape(1, HEAD_DIM))


def _rope(x, cos, sin_signed):
    lane = lax.broadcasted_iota(jnp.int32, x.shape, 1)
    quarter = HEAD_DIM // 4
    fwd = pltpu.roll(x, HEAD_DIM - quarter, 1)
    bwd = pltpu.roll(x, quarter, 1)
    swapped = jnp.where((lane % (2 * quarter)) < quarter, fwd, bwd)
    return x * cos + swapped * sin_signed


def _head_rope_kernel(x_ref, c_ref, s_ref, g_ref, o_ref, *, scale):
    for h in range(x_ref.shape[1] // HEAD_DIM):
        sl = slice(h * HEAD_DIM, (h + 1) * HEAD_DIM)
        xn = _rms(x_ref[:, sl].astype(F32)) * g_ref[...]
        o_ref[:, sl] = (_rope(xn, c_ref[...], s_ref[...]) * scale).astype(o_ref.dtype)


def _head_rope_call(z, col0, n_heads, cos, sin_signed, g, scale, t_prompt, dec_batch, dec_seq, tr=512):
    nr = dec_seq // tr
    width = n_heads * HEAD_DIM
    return pl.pallas_call(
        functools.partial(_head_rope_kernel, scale=scale),
        grid=(dec_batch, nr),
        in_specs=[pl.BlockSpec((tr, width), lambda b, r: (t_prompt // tr + b * nr + r, col0 // width)),
                  pl.BlockSpec((tr, HEAD_DIM), lambda b, r: (r, 0)),
                  pl.BlockSpec((tr, HEAD_DIM), lambda b, r: (r, 0)),
                  pl.BlockSpec((1, HEAD_DIM), lambda b, r: (0, 0))],
        out_specs=pl.BlockSpec((tr, width), lambda b, r: (b * nr + r, 0)),
        out_shape=jax.ShapeDtypeStruct((dec_batch * dec_seq, width), BF16),
        compiler_params=_cparams("parallel", "parallel"),
        name="attn_head_rope",
    )(z, cos, sin_signed, g.reshape(1, HEAD_DIM))


def _attn_sample_kernel(q_ref, k_ref, v_ref, ck_ref, cv_ref, o_ref, m_ref, l_ref, acc_ref):
    j = pl.program_id(3)

    @pl.when(j == 0)
    def _():
        m_ref[...] = jnp.full(m_ref.shape, -jnp.inf, F32)
        l_ref[...] = jnp.zeros(l_ref.shape, F32)
        acc_ref[...] = jnp.zeros(acc_ref.shape, F32)

    def step(kb, vb):
        reps = kb.shape[0] // LANES
        for m in range(Q_PER_KV):
            qm = q_ref[:, m * HEAD_DIM:(m + 1) * HEAD_DIM]
            s = lax.dot_general(qm, kb, (((1,), (1,)), ((), ())), preferred_element_type=F32)
            m_prev = m_ref[m]
            m_next = jnp.maximum(m_prev, jnp.max(s, axis=-1, keepdims=True))
            alpha = jnp.exp(m_prev - m_next)
            p = jnp.exp(s - jnp.concatenate([m_next] * reps, axis=1))
            l_ref[m] = alpha * l_ref[m] + jnp.sum(p, axis=-1, keepdims=True)
            acc_ref[m] = alpha * acc_ref[m] + jnp.dot(p.astype(BF16), vb, preferred_element_type=F32)
            m_ref[m] = m_next

    @pl.when(j == 0)
    def _():
        step(ck_ref[...].astype(BF16), cv_ref[...].astype(BF16))

    @pl.when(j > 0)
    def _():
        step(k_ref[...], v_ref[...])

    @pl.when(j == pl.num_programs(3) - 1)
    def _():
        for m in range(Q_PER_KV):
            o_ref[:, m * HEAD_DIM:(m + 1) * HEAD_DIM] = acc_ref[m] / l_ref[m]


def _attn_sample_call(z, qr, kr, ctx_k, ctx_v, t_prompt, dec_batch, dec_seq, tq=512, tk=1024):
    gw = Q_PER_KV * HEAD_DIM
    past = ctx_k.shape[1]
    tk = min(tk, dec_seq)
    nq = dec_seq // tq
    nk = dec_seq // tk
    vrow = lambda b, j: (t_prompt + b * dec_seq) // tk + jnp.maximum(j - 1, 0)
    krow = lambda b, j: b * nk + jnp.maximum(j - 1, 0)
    return pl.pallas_call(
        _attn_sample_kernel,
        grid=(dec_batch, N_KV_HEADS, nq, nk + 1),
        in_specs=[pl.BlockSpec((tq, gw), lambda b, g, i, j: (b * nq + i, g)),
                  pl.BlockSpec((tk, HEAD_DIM), lambda b, g, i, j: (krow(b, j), g)),
                  pl.BlockSpec((tk, HEAD_DIM), lambda b, g, i, j: (vrow(b, j), COL_V // HEAD_DIM + g)),
                  pl.BlockSpec((None, past, HEAD_DIM), lambda b, g, i, j: (b, 0, g)),
                  pl.BlockSpec((None, past, HEAD_DIM), lambda b, g, i, j: (b, 0, g))],
        out_specs=pl.BlockSpec((tq, gw), lambda b, g, i, j: (b * nq + i, g)),
        out_shape=jax.ShapeDtypeStruct((dec_batch * dec_seq, ATTN_DIM), F32),
        scratch_shapes=[pltpu.VMEM((Q_PER_KV, tq, LANES), F32),
                        pltpu.VMEM((Q_PER_KV, tq, LANES), F32),
                        pltpu.VMEM((Q_PER_KV, tq, HEAD_DIM), F32)],
        compiler_params=_cparams("parallel", "parallel", "parallel", "arbitrary"),
        name="attn_latent",
    )(qr, kr, z, ctx_k, ctx_v)


def _rope_tables(n_tokens):
    quarter = HEAD_DIM // 4
    n_rows = n_tokens // GRID_W
    row = jnp.repeat(jnp.arange(n_rows, dtype=F32), GRID_W)
    col = jnp.tile(jnp.arange(GRID_W, dtype=F32), n_rows)
    freqs = ROPE_THETA ** (-jnp.arange(quarter, dtype=F32) / quarter)
    ar = row[:, None] * freqs[None, :]
    ac = col[:, None] * freqs[None, :]
    cos = jnp.concatenate([jnp.cos(ar), jnp.cos(ar), jnp.cos(ac), jnp.cos(ac)], axis=-1)
    sin = jnp.concatenate([-jnp.sin(ar), jnp.sin(ar), -jnp.sin(ac), jnp.sin(ac)], axis=-1)
    return cos, sin


def _gelu_tanh(x):
    return 0.5 * x * (1.0 + jnp.tanh(math.sqrt(2.0 / math.pi) * (x + 0.044715 * (x * x * x))))


def _gmlp_kernel(z_ref, g_ref, b_ref, ws_ref, bs_ref, o_ref):
    hw = GM_DIM // GM_HEADS
    for ch in range(z_ref.shape[0] // CHUNK):
        rows = slice(ch * CHUNK, (ch + 1) * CHUNK)
        u = _gelu_tanh(z_ref[rows, :GM_DIM].astype(F32))
        v = _gelu_tanh(z_ref[rows, GM_DIM:].astype(F32))
        mu = jnp.mean(v, axis=-1, keepdims=True)
        vc = v - mu
        var = jnp.mean(vc * vc, axis=-1, keepdims=True)
        vn = (vc * lax.rsqrt(var + EPS) * g_ref[...] + b_ref[...]).astype(BF16)
        for h in range(GM_HEADS):
            cols = slice(h * hw, (h + 1) * hw)
            s = jnp.dot(ws_ref[h], vn[:, cols], preferred_element_type=F32) + bs_ref[h]
            o_ref[rows, cols] = u[:, cols] * s


def _gmlp_call(z, g, b, ws, bs, tr=512):
    t = z.shape[0]
    return pl.pallas_call(
        _gmlp_kernel,
        grid=(t // tr,),
        in_specs=[pl.BlockSpec((tr, 2 * GM_DIM), lambda i: (i, COL_GM // (2 * GM_DIM))),
                  pl.BlockSpec((1, GM_DIM), lambda i: (0, 0)),
                  pl.BlockSpec((1, GM_DIM), lambda i: (0, 0)),
                  pl.BlockSpec((GM_HEADS, CHUNK, CHUNK), lambda i: (0, 0, 0)),
                  pl.BlockSpec((GM_HEADS, CHUNK, 1), lambda i: (0, 0, 0))],
        out_specs=pl.BlockSpec((tr, GM_DIM), lambda i: (i, 0)),
        out_shape=jax.ShapeDtypeStruct((t, GM_DIM), F32),
        compiler_params=_cparams("parallel"),
        name="chunk_gmlp",
    )(z, g.reshape(1, GM_DIM), b.reshape(1, GM_DIM), ws.astype(BF16), bs.reshape(GM_HEADS, CHUNK, 1))


def _sconv_kernel(z_ref, w_ref, b_ref, o_ref, *, seg_len):
    x = z_ref[...].astype(F32)
    n = x.shape[0]
    pos = lax.broadcasted_iota(jnp.int32, x.shape, 0) % seg_len
    prev = jnp.where(pos == 0, 0.0, pltpu.roll(x, 1, 0))
    nxt = jnp.where(pos == seg_len - 1, 0.0, pltpu.roll(x, n - 1, 0))
    o_ref[...] = b_ref[...] + prev * w_ref[0:1, :] + x * w_ref[1:2, :] + nxt * w_ref[2:3, :]


def _sconv_call(z, w, b, row0, n_rows, seg_len, block_rows, cw=256):
    width = 3 * HY_DIM
    return pl.pallas_call(
        functools.partial(_sconv_kernel, seg_len=seg_len),
        grid=(n_rows // block_rows, width // cw),
        in_specs=[pl.BlockSpec((block_rows, cw), lambda i, c: (row0 // block_rows + i, COL_HY // cw + c)),
                  pl.BlockSpec((3, cw), lambda i, c: (0, c)),
                  pl.BlockSpec((1, cw), lambda i, c: (0, c))],
        out_specs=pl.BlockSpec((block_rows, cw), lambda i, c: (i, c)),
        out_shape=jax.ShapeDtypeStruct((n_rows, width), F32),
        compiler_params=_cparams("parallel", "parallel"),
        name="hyena_short_conv",
    )(z, w, b.reshape(1, width))


def _filter_kernel(z_ref, t_ref, w1_ref, b1_ref, w2_ref, b2_ref, w3a_ref, b3a_ref, w3b_ref, b3b_ref,
                   fr_ref, dl_ref, o_ref, *, tr):
    fr = fr_ref[...]
    h = jnp.sin(fr * (jnp.dot(z_ref[...], w1_ref[...], precision=HI, preferred_element_type=F32) + b1_ref[...]))
    h = jnp.sin(fr * (jnp.dot(h, w2_ref[...], precision=HI, preferred_element_type=F32) + b2_ref[...]))
    decay = jnp.exp(-t_ref[...] * dl_ref[...]) + MOD_SHIFT
    row = lax.broadcasted_iota(jnp.int32, decay.shape, 0) + pl.program_id(0) * tr
    decay = jnp.where(row == 0, 0.0, decay)
    for o, (w3_ref, b3_ref) in enumerate(((w3a_ref, b3a_ref), (w3b_ref, b3b_ref))):
        taps = _dot_3pass(h, w3_ref[...]) + b3_ref[...]
        o_ref[:, o * HY_DIM:(o + 1) * HY_DIM] = taps * decay


def _filter_call(length, w1, b1, w2, b2, w3, b3, fr, tr=256):
    bands = (FILTER_EMB - 1) // 2
    t = np.linspace(0.0, 1.0, length)[:, None]
    wv = 2.0 * np.pi * np.arange(length)[:, None] / length
    f = np.linspace(1e-4, bands - 1, bands)[None, :]
    zf = np.concatenate([t, np.cos(f * wv), -np.sin(f * wv)], axis=-1)
    min_decay = math.log(DECAY_TARGET) / SLOW_DECAY_PCT
    max_decay = math.log(DECAY_TARGET) / FAST_DECAY_PCT
    deltas = np.abs(np.linspace(min_decay, max_decay, HY_DIM))[None, :]
    lag = np.minimum(np.abs(np.arange(2 * length) - length), length - 1)
    pad = LANES - FILTER_EMB
    hp = LANES - FILTER_HIDDEN
    z2 = jnp.asarray(np.pad(zf[lag], ((0, 0), (0, pad))).astype(np.float32))
    t2 = jnp.asarray(t[lag].astype(np.float32))
    dl = jnp.asarray(deltas.astype(np.float32))
    w1p = jnp.pad(w1, ((0, pad), (0, hp)))
    w2p = jnp.pad(w2, ((0, hp), (0, hp)))
    w3p = jnp.pad(w3, ((0, hp), (0, 0)))
    b1p = jnp.pad(b1, (0, hp)).reshape(1, LANES)
    b2p = jnp.pad(b2, (0, hp)).reshape(1, LANES)
    frp = jnp.pad(fr, (0, hp)).reshape(1, LANES)
    b3r = b3.reshape(1, -1)
    nblk = 2 * length // tr
    half = length // tr
    wcol = lambda r, o: o * 2 + jnp.where(r < half, 1, 0)
    small = pl.BlockSpec((1, LANES), lambda r: (0, 0))
    square = pl.BlockSpec((LANES, LANES), lambda r: (0, 0))
    return pl.pallas_call(
        functools.partial(_filter_kernel, tr=tr),
        grid=(nblk,),
        in_specs=[pl.BlockSpec((tr, LANES), lambda r: (r, 0)),
                  pl.BlockSpec((tr, 1), lambda r: (r, 0)),
                  square, small, square, small,
                  pl.BlockSpec((LANES, HY_DIM), lambda r: (0, wcol(r, 0))),
                  pl.BlockSpec((1, HY_DIM), lambda r: (0, wcol(r, 0))),
                  pl.BlockSpec((LANES, HY_DIM), lambda r: (0, wcol(r, 1))),
                  pl.BlockSpec((1, HY_DIM), lambda r: (0, wcol(r, 1))),
                  small,
                  pl.BlockSpec((1, HY_DIM), lambda r: (0, 0))],
        out_specs=pl.BlockSpec((tr, 2 * HY_DIM), lambda r: (r, 0)),
        out_shape=jax.ShapeDtypeStruct((2 * length, 2 * HY_DIM), F32),
        compiler_params=_cparams("parallel"),
        name="hyena_filter_mlp",
    )(z2, t2, w1p, b1p, w2p, b2p, w3p, b3r, w3p, b3r, frp, dl)


def _dft_constants(p):
    n = 2 * p
    idx = np.arange(p, dtype=np.float64)
    ang = 2.0 * np.pi * np.outer(idx, idx) / n
    re = np.cos(ang)
    im = -np.sin(ang)
    im[0, :] = np.cos(np.pi * idx)
    fwd = np.concatenate([re, im], axis=0)
    sign = np.where(np.arange(p) % 2 == 0, 1.0, -1.0)
    ar = (2.0 / n) * np.cos(ang)
    ar[:, 0] = 1.0 / n
    ai = -(2.0 / n) * np.sin(ang)
    ai[:, 0] = sign / n
    inv = np.concatenate([ar, ai], axis=1)
    as_bf16 = lambda a: jnp.asarray(a.astype(np.float32)).astype(BF16)
    return as_bf16(fwd), as_bf16(inv)


def _dft_kernel(f_ref, x_ref, o_ref):
    o_ref[...] = jnp.dot(f_ref[...], x_ref[...].astype(BF16), preferred_element_type=F32).astype(o_ref.dtype)


def _dft_call(x, fwd, p, col0, cw=256):
    nblk = x.shape[0] // p
    return pl.pallas_call(
        _dft_kernel,
        grid=(nblk, HY_DIM // cw),
        in_specs=[pl.BlockSpec((2 * p, p), lambda i, c: (0, 0)),
                  pl.BlockSpec((p, cw), lambda i, c: (i, col0 // cw + c))],
        out_specs=pl.BlockSpec((None, 2 * p, cw), lambda i, c: (i, 0, c)),
        out_shape=jax.ShapeDtypeStruct((nblk, 2 * p, HY_DIM), BF16),
        compiler_params=_cparams("parallel", "parallel"),
        name="hyena_block_dft",
    )(fwd, x)


def _filter_dft_kernel(f_ref, blk_ref, o_ref, tail_ref, *, p):
    b = pl.program_id(2)
    blk = blk_ref[...].astype(BF16)
    spec = jnp.dot(f_ref[...], blk, preferred_element_type=F32)

    @pl.when(b > 0)
    def _():
        o_ref[...] = (spec + tail_ref[...]).astype(o_ref.dtype)

    row = lax.broadcasted_iota(jnp.int32, spec.shape, 0)
    first = blk[0:1, :].astype(F32)
    real_row = jnp.logical_or(row < p, row == p)
    sign = (1 - 2 * (row % 2)).astype(F32)
    tail_ref[...] = sign * (spec - jnp.where(real_row, first, 0.0))


def _filter_dft_call(taps, fwd, p, cw=256):
    nb2 = taps.shape[0] // p
    nd = nb2 - 1
    ncb = HY_DIM // cw
    return pl.pallas_call(
        functools.partial(_filter_dft_kernel, p=p),
        grid=(2, ncb, nb2),
        in_specs=[pl.BlockSpec((2 * p, p), lambda o, c, b: (0, 0)),
                  pl.BlockSpec((p, cw), lambda o, c, b: (b, o * ncb + c))],
        out_specs=pl.BlockSpec((None, None, 2 * p, cw), lambda o, c, b: (o, jnp.maximum(b - 1, 0), 0, c)),
        out_shape=jax.ShapeDtypeStruct((2, nd, 2 * p, HY_DIM), BF16),
        scratch_shapes=[pltpu.VMEM((2 * p, cw), F32)],
        compiler_params=_cparams("parallel", "parallel", "arbitrary"),
        name="hyena_filter_dft",
    )(fwd, taps)


def _specconv_kernel(xs_ref, gs_ref, inv_ref, v_ref, gate_ref, bias_ref, o_ref, y_ref, *, nb, p, rc):
    i = pl.program_id(2)
    cw = o_ref.shape[1]

    for r in range(0, p, rc):
        def body(j, carry):
            yre, yim = carry
            d = i - j + nb - 1
            xre = xs_ref[j, r:r + rc, :].astype(F32)
            xim = xs_ref[j, p + r:p + r + rc, :].astype(F32)
            gre = gs_ref[d, r:r + rc, :].astype(F32)
            gim = gs_ref[d, p + r:p + r + rc, :].astype(F32)
            return yre + (xre * gre - xim * gim), yim + (xre * gim + xim * gre)

        zero = jnp.zeros((rc, cw), F32)
        yre, yim = lax.fori_loop(0, nb, body, (zero, zero))
        y_ref[r:r + rc, :] = yre
        y_ref[p + r:p + r + rc, :] = yim

    def edge(j, carry):
        y0, yn = carry
        d = i - j + nb - 1
        x0 = xs_ref[j, 0:16, :].astype(F32)
        xn = xs_ref[j, p:p + 16, :].astype(F32)
        g0 = gs_ref[d, 0:16, :].astype(F32)
        gn = gs_ref[d, p:p + 16, :].astype(F32)
        return y0 + x0 * g0, yn + xn * gn

    zero16 = jnp.zeros((16, cw), F32)
    y0, yn = lax.fori_loop(0, nb, edge, (zero16, zero16))
    y_ref[0:1, :] = y0[0:1, :]
    y_ref[p:p + 1, :] = yn[0:1, :]

    y = jnp.dot(inv_ref[...], y_ref[...].astype(BF16), preferred_element_type=F32)
    v = v_ref[...]
    o_ref[...] = gate_ref[...] * (y + bias_ref[...] * v)


def _specconv_call(xs, gs, inv, zc, bias, nseq, nb, p, v_col0, gate_col0, v_src=None, cw=256, rc=32):
    n_rows = nseq * nb * p
    xs4 = xs.reshape(nseq, nb, 2 * p, HY_DIM)
    if v_src is None:
        v_arr, v_spec = zc, pl.BlockSpec((p, cw), lambda c, s, i: (s * nb + i, v_col0 // cw + c))
    else:
        v_arr, v_spec = v_src, pl.BlockSpec((p, cw), lambda c, s, i: (s * nb + i, c))
    return pl.pallas_call(
        functools.partial(_specconv_kernel, nb=nb, p=p, rc=rc),
        grid=(HY_DIM // cw, nseq, nb),
        in_specs=[pl.BlockSpec((None, nb, 2 * p, cw), lambda c, s, i: (s, 0, 0, c)),
                  pl.BlockSpec((2 * nb - 1, 2 * p, cw), lambda c, s, i: (0, 0, c)),
                  pl.BlockSpec((p, 2 * p), lambda c, s, i: (0, 0)),
                  v_spec,
                  pl.BlockSpec((p, cw), lambda c, s, i: (s * nb + i, gate_col0 // cw + c)),
                  pl.BlockSpec((1, cw), lambda c, s, i: (0, c))],
        out_specs=pl.BlockSpec((p, cw), lambda c, s, i: (s * nb + i, c)),
        out_shape=jax.ShapeDtypeStruct((n_rows, HY_DIM), F32),
        scratch_shapes=[pltpu.VMEM((2 * p, cw), F32)],
        compiler_params=_cparams("parallel", "parallel", "parallel"),
        name="hyena_spectral_conv",
    )(xs4, gs, inv, v_arr, zc, bias.reshape(1, HY_DIM))


def _hyena_group(z, row0, nseq, length, p, prm, sconv_rows):
    nb = length // p
    zc = _sconv_call(z, prm['hy_conv_w'], prm['hy_conv_b'], row0, nseq * length, length, sconv_rows)
    taps = _filter_call(length, prm['hf_w1'], prm['hf_b1'], prm['hf_w2'], prm['hf_b2'],
                        prm['hf_w3'], prm['hf_b3'], prm['hf_freq'], tr=min(512, length))
    fwd, inv = _dft_constants(p)
    gs = _filter_dft_call(taps, fwd, p, cw=512)
    cw = 512 if nb == 1 else 256
    vs = _dft_call(zc, fwd, p, 2 * HY_DIM, cw=512)
    u = _specconv_call(vs, gs[0], inv, zc, prm['hy_bias'][0], nseq, nb, p, 2 * HY_DIM, 0, cw=cw)
    us = _dft_call(u, fwd, p, 0, cw=512)
    return _specconv_call(us, gs[1], inv, zc, prm['hy_bias'][1], nseq, nb, p, 0, HY_DIM, v_src=u, cw=cw)


def _mix_norm_kernel(ap_ref, as_ref, hp_ref, hs_ref, m_ref, g_ref, o_ref, *, ntp):
    i = pl.program_id(0)

    def fill(a_ref, h_ref):
        o_ref[:, :ATTN_DIM] = (_rms(a_ref[...]) * g_ref[:, :ATTN_DIM]).astype(BF16)
        o_ref[:, ATTN_DIM:ATTN_DIM + HY_DIM] = (
            _rms(h_ref[...]) * g_ref[:, ATTN_DIM:ATTN_DIM + HY_DIM]).astype(BF16)
        o_ref[:, ATTN_DIM + HY_DIM:] = (_rms(m_ref[...]) * g_ref[:, ATTN_DIM + HY_DIM:]).astype(BF16)

    @pl.when(i < ntp)
    def _():
        fill(ap_ref, hp_ref)

    @pl.when(i >= ntp)
    def _():
        fill(as_ref, hs_ref)


def _mix_norm_call(attn_p, attn_s, hy_p, hy_s, gm, g, t_prompt, tm=512):
    t = gm.shape[0]
    mix_dim = ATTN_DIM + HY_DIM + GM_DIM
    ntp = t_prompt // tm
    nts = t // tm - ntp
    prow = lambda i: (jnp.minimum(i, ntp - 1), 0)
    srow = lambda i: (jnp.clip(i - ntp, 0, nts - 1), 0)
    return pl.pallas_call(
        functools.partial(_mix_norm_kernel, ntp=ntp),
        grid=(t // tm,),
        in_specs=[pl.BlockSpec((tm, ATTN_DIM), prow),
                  pl.BlockSpec((tm, ATTN_DIM), srow),
                  pl.BlockSpec((tm, HY_DIM), prow),
                  pl.BlockSpec((tm, HY_DIM), srow),
                  pl.BlockSpec((tm, GM_DIM), lambda i: (i, 0)),
                  pl.BlockSpec((1, mix_dim), lambda i: (0, 0))],
        out_specs=pl.BlockSpec((tm, mix_dim), lambda i: (i, 0)),
        out_shape=jax.ShapeDtypeStruct((t, mix_dim), BF16),
        compiler_params=_cparams("parallel"),
        name="mix_group_norm",
    )(attn_p, attn_s, hy_p, hy_s, gm, g.reshape(1, mix_dim))


def _router_kernel(x_ref, g_ref, sh_ref, sc_ref, wr_ref, br_ref, h_ref, r_ref):
    y = _rms(x_ref[...]) * g_ref[...]
    h = y * (1.0 + sc_ref[...]) + sh_ref[...]
    h_ref[...] = _pack_bf16_pairs(h)
    logits = _dot_3pass(h, wr_ref[...]) + br_ref[...]
    col = lambda k: logits[:, k:k + 1]

    lg = [col(k) for k in range(N_GROUPS)]
    g_max = functools.reduce(jnp.maximum, lg)
    g_den = sum(jnp.exp(v - g_max) for v in lg)
    g_top = 1.0 / g_den
    g_idx = jnp.full(g_max.shape, N_GROUPS - 1, jnp.int32)
    for k in range(N_GROUPS - 2, -1, -1):
        g_idx = jnp.where(lg[k] == g_max, k, g_idx)

    le = []
    for e in range(EXPERTS_PER_GROUP):
        v = col(N_GROUPS + (N_GROUPS - 1) * EXPERTS_PER_GROUP + e)
        for k in range(N_GROUPS - 2, -1, -1):
            v = jnp.where(g_idx == k, col(N_GROUPS + k * EXPERTS_PER_GROUP + e), v)
        le.append(v)

    e_max = functools.reduce(jnp.maximum, le)
    e1 = jnp.full(e_max.shape, EXPERTS_PER_GROUP - 1, jnp.int32)
    for e in range(EXPERTS_PER_GROUP - 2, -1, -1):
        e1 = jnp.where(le[e] == e_max, e, e1)
    neg = jnp.float32(-jnp.inf)
    rest = [jnp.where(e1 == e, neg, le[e]) for e in range(EXPERTS_PER_GROUP)]
    e2_max = functools.reduce(jnp.maximum, rest)
    e2 = jnp.full(e_max.shape, EXPERTS_PER_GROUP - 1, jnp.int32)
    for e in range(EXPERTS_PER_GROUP - 2, -1, -1):
        e2 = jnp.where(rest[e] == e2_max, e, e2)
    ratio = jnp.exp(e2_max - e_max)
    w1 = g_top / (1.0 + ratio)
    w2 = g_top * ratio / (1.0 + ratio)
    id1 = (g_idx * EXPERTS_PER_GROUP + e1).astype(F32)
    id2 = (g_idx * EXPERTS_PER_GROUP + e2).astype(F32)

    lane = lax.broadcasted_iota(jnp.int32, r_ref.shape, 1)
    r_ref[...] = jnp.where(lane == 0, id1, jnp.where(lane == 1, id2, jnp.where(lane == 2, w1, jnp.where(lane == 3, w2, 0.0))))


def _router_call(x, g, mod, wr, br, t_prompt, dec_seq, tm=256):
    t, d = x.shape
    seg = lambda i: _seg_of_row(i * tm, t_prompt, dec_seq)
    return pl.pallas_call(
        _router_kernel,
        grid=(t // tm,),
        in_specs=[pl.BlockSpec((tm, d), lambda i: (i, 0)),
                  pl.BlockSpec((1, d), lambda i: (0, 0)),
                  pl.BlockSpec((None, 1, d), lambda i: (seg(i), 0, 3)),
                  pl.BlockSpec((None, 1, d), lambda i: (seg(i), 0, 4)),
                  pl.BlockSpec((d, LANES), lambda i: (0, 0)),
                  pl.BlockSpec((1, LANES), lambda i: (0, 0))],
        out_specs=[pl.BlockSpec((tm, d // 2), lambda i: (i, 0)),
                   pl.BlockSpec((tm, LANES), lambda i: (i, 0))],
        out_shape=[jax.ShapeDtypeStruct((t, d // 2), jnp.uint32),
                   jax.ShapeDtypeStruct((t, LANES), F32)],
        compiler_params=_cparams("parallel"),
        name="norm2_router",
    )(x, g.reshape(1, d), mod, mod, wr, br)


def _dispatch_kernel(dest_ref, zt_ref, h_ref, xs_ref, zero_ref, sem, zsem, *, tm, t, expert_tile):
    i = pl.program_id(0)

    @pl.when(i == 0)
    def _():
        zero_ref[...] = jnp.zeros(zero_ref.shape, zero_ref.dtype)

        def clear(q):
            row = pl.multiple_of(jnp.maximum(zt_ref[q], 0), expert_tile)
            return pltpu.make_async_copy(zero_ref, xs_ref.at[pl.ds(row, expert_tile), :], zsem)

        for q in range(2 * N_EXPERTS):
            @pl.when(zt_ref[q] >= 0)
            def _():
                clear(q).start()

        for q in range(2 * N_EXPERTS):
            @pl.when(zt_ref[q] >= 0)
            def _():
                clear(q).wait()

    base = i * tm

    def body(r, c):
        for k in range(2):
            row = dest_ref[k * t + base + r]
            pltpu.make_async_copy(h_ref.at[pl.ds(r, 1), :], xs_ref.at[pl.ds(row, 1), :], sem).start()
        return c

    lax.fori_loop(0, tm, body, 0, unroll=8)
    for k in range(2):
        pltpu.make_async_copy(h_ref, xs_ref.at[pl.ds(0, tm), :], sem).wait()


def _dispatch_call(dest, zero_tiles, h, n_rows, expert_tile, tm=256):
    t, d = h.shape
    return pl.pallas_call(
        functools.partial(_dispatch_kernel, tm=tm, t=t, expert_tile=expert_tile),
        grid_spec=pltpu.PrefetchScalarGridSpec(
            num_scalar_prefetch=2,
            grid=(t // tm,),
            in_specs=[pl.BlockSpec((tm, d), lambda i, dest, zt: (i, 0))],
            out_specs=pl.BlockSpec(memory_space=pl.ANY),
            scratch_shapes=[pltpu.VMEM((expert_tile, d), h.dtype),
                            pltpu.SemaphoreType.DMA(()),
                            pltpu.SemaphoreType.DMA(())]),
        out_shape=jax.ShapeDtypeStruct((n_rows, d), h.dtype),
        compiler_params=_cparams("arbitrary", disable_bounds_checks=True),
        name="moe_dispatch_rows",
    )(dest, zero_tiles, h)


def _expert_changed(te_ref):
    m = pl.program_id(1)
    return jnp.logical_or(m == 0, te_ref[m] != te_ref[jnp.maximum(m - 1, 0)])


def _expert_up_kernel(te_ref, nv_ref, x_ref, wg_ref, wu_ref, o_ref, wgb_ref, wub_ref, *, tm):
    valid = pl.program_id(1) * tm < nv_ref[0]

    @pl.when(jnp.logical_and(valid, _expert_changed(te_ref)))
    def _():
        wgb_ref[...] = wg_ref[...].astype(BF16)
        wub_ref[...] = wu_ref[...].astype(BF16)

    @pl.when(valid)
    def _():
        x = _unpack_bf16_pairs(x_ref[...]).astype(BF16)
        hg = jnp.dot(x, wgb_ref[...], preferred_element_type=F32)
        hu = jnp.dot(x, wub_ref[...], preferred_element_type=F32)
        o_ref[...] = (hg * _sigmoid(hg) * hu).astype(o_ref.dtype)

    @pl.when(jnp.logical_not(valid))
    def _():
        o_ref[...] = jnp.zeros(o_ref.shape, o_ref.dtype)


def _expert_up_call(tile_e, n_valid, xs, wg, wu, layer, tm, nf=2):
    n, dh = xs.shape
    d, ff = wg.shape[2], wg.shape[3]
    tf = ff // nf
    wspec = pl.BlockSpec((None, None, d, tf), lambda f, m, te, nv: (layer, te[m], 0, f))
    return pl.pallas_call(
        functools.partial(_expert_up_kernel, tm=tm),
        grid_spec=pltpu.PrefetchScalarGridSpec(
            num_scalar_prefetch=2,
            grid=(nf, n // tm),
            in_specs=[pl.BlockSpec((tm, dh), lambda f, m, te, nv: (m, 0)), wspec, wspec],
            out_specs=pl.BlockSpec((tm, tf), lambda f, m, te, nv: (m, f)),
            scratch_shapes=[pltpu.VMEM((d, tf), BF16), pltpu.VMEM((d, tf), BF16)]),
        out_shape=jax.ShapeDtypeStruct((n, ff), BF16),
        compiler_params=_cparams("arbitrary", "arbitrary"),
        name="moe_expert_up",
    )(tile_e, n_valid, xs, wg, wu)


def _expert_down_kernel(te_ref, nv_ref, a_ref, wd_ref, o_ref, wdb_ref, *, tm):
    valid = pl.program_id(1) * tm < nv_ref[0]

    @pl.when(jnp.logical_and(valid, _expert_changed(te_ref)))
    def _():
        wdb_ref[...] = wd_ref[...].astype(BF16)

    @pl.when(valid)
    def _():
        o_ref[...] = _pack_bf16_pairs(jnp.dot(a_ref[...], wdb_ref[...], preferred_element_type=F32))

    @pl.when(jnp.logical_not(valid))
    def _():
        o_ref[...] = jnp.zeros(o_ref.shape, o_ref.dtype)


def _expert_down_call(tile_e, n_valid, a, wd, layer, tm):
    nn = 1
    n, ff = a.shape
    d = wd.shape[3]
    tn = d // nn
    return pl.pallas_call(
        functools.partial(_expert_down_kernel, tm=tm),
        grid_spec=pltpu.PrefetchScalarGridSpec(
            num_scalar_prefetch=2,
            grid=(nn, n // tm),
            in_specs=[pl.BlockSpec((tm, ff), lambda c, m, te, nv: (m, 0)),
                      pl.BlockSpec((None, None, ff, tn), lambda c, m, te, nv: (layer, te[m], 0, c))],
            out_specs=pl.BlockSpec((tm, tn // 2), lambda c, m, te, nv: (m, c)),
            scratch_shapes=[pltpu.VMEM((ff, tn), BF16)]),
        out_shape=jax.ShapeDtypeStruct((n, d // 2), jnp.uint32),
        compiler_params=_cparams("arbitrary", "arbitrary"),
        name="moe_expert_down",
    )(tile_e, n_valid, a, wd)


def _combine_kernel(dest_ref, x_ref, r_ref, gate_ref, fg_ref, ys_ref, *rest, tc, t, nt, ntp, final):
    out_refs, (ybuf, sem) = rest[:-2], rest[-2:]
    i = pl.program_id(0)

    def fetch(tile, slot):
        base = tile * tc

        def body(r, c):
            for k in range(2):
                row = dest_ref[k * t + base + r]
                pltpu.make_async_copy(ys_ref.at[pl.ds(row, 1), :], ybuf.at[slot, pl.ds(k * tc + r, 1), :],
                                      sem.at[slot]).start()
            return c

        lax.fori_loop(0, tc, body, 0, unroll=8)

    @pl.when(i == 0)
    def _():
        fetch(0, 0)

    @pl.when(i + 1 < nt)
    def _():
        fetch(i + 1, (i + 1) % 2)

    slot = i % 2
    pltpu.make_async_copy(ys_ref.at[pl.ds(0, 2 * tc), :], ybuf.at[slot], sem.at[slot]).wait()

    w1 = r_ref[:, 2:3]
    w2 = r_ref[:, 3:4]
    y = w1 * _unpack_bf16_pairs(ybuf[slot, 0:tc, :]) + w2 * _unpack_bf16_pairs(ybuf[slot, tc:2 * tc, :])
    x = x_ref[...] + gate_ref[...] * y
    if final:
        x = _rms(x) * fg_ref[...]
        op_ref, os_ref = out_refs

        @pl.when(i < ntp)
        def _():
            op_ref[...] = x

        @pl.when(i >= ntp)
        def _():
            os_ref[...] = x
    else:
        out_refs[0][...] = x


def _combine_call(dest, x, route, ys, mod, final_g, final, t_prompt, dec_seq, tc=128):
    t, d = x.shape
    seg = lambda i, *_: _seg_of_row(i * tc, t_prompt, dec_seq)
    nt = t // tc
    ntp = t_prompt // tc
    if final:
        out_specs = [pl.BlockSpec((tc, d), lambda i, dest: (jnp.minimum(i, ntp - 1), 0)),
                     pl.BlockSpec((tc, d), lambda i, dest: (jnp.maximum(i - ntp, 0), 0))]
        out_shape = [jax.ShapeDtypeStruct((t_prompt, d), F32), jax.ShapeDtypeStruct((t - t_prompt, d), F32)]
    else:
        out_specs = pl.BlockSpec((tc, d), lambda i, dest: (i, 0))
        out_shape = jax.ShapeDtypeStruct((t, d), F32)
    return pl.pallas_call(
        functools.partial(_combine_kernel, tc=tc, t=t, nt=nt, ntp=ntp, final=final),
        grid_spec=pltpu.PrefetchScalarGridSpec(
            num_scalar_prefetch=1,
            grid=(nt,),
            in_specs=[pl.BlockSpec((tc, d), lambda i, dest: (i, 0)),
                      pl.BlockSpec((tc, LANES), lambda i, dest: (i, 0)),
                      pl.BlockSpec((None, 1, d), lambda i, dest: (seg(i), 0, 5)),
                      pl.BlockSpec((1, d), lambda i, dest: (0, 0)),
                      pl.BlockSpec(memory_space=pl.ANY)],
            out_specs=out_specs,
            scratch_shapes=[pltpu.VMEM((2, 2 * tc, ys.shape[1]), ys.dtype),
                            pltpu.SemaphoreType.DMA((2,))]),
        out_shape=out_shape,
        compiler_params=_cparams("arbitrary", disable_bounds_checks=True),
        name="moe_combine_residual",
    )(dest, x, route, mod, final_g.reshape(1, d), ys)


def _dispatch_plan(route, tm):
    t = route.shape[0]
    flat_e = route[:, 0:2].astype(jnp.int32).T.reshape(-1)
    n_pairs = 2 * t
    n_tiles = n_pairs // tm + N_EXPERTS
    onehot = (flat_e[:, None] == jnp.arange(N_EXPERTS, dtype=jnp.int32)[None, :]).astype(jnp.int32)
    running = jnp.cumsum(onehot, axis=0)
    counts = running[-1]
    rank = jnp.sum((running - 1) * onehot, axis=1)
    padded = ((counts + tm - 1) // tm) * tm
    pad_end = jnp.cumsum(padded)
    pad_start = pad_end - padded
    dest = (jnp.sum(onehot * pad_start[None, :], axis=1) + rank).astype(jnp.int32)
    tile_start = jnp.arange(n_tiles, dtype=jnp.int32) * tm
    tile_e = jnp.sum((pad_end[None, :] <= tile_start[:, None]).astype(jnp.int32), axis=1)
    tile_e = jnp.minimum(tile_e, N_EXPERTS - 1).astype(jnp.int32)
    n_valid = pad_end[-1:].astype(jnp.int32)
    last = jnp.where(counts > 0, pad_end - tm, -1)
    tail = pad_end[-1] + jnp.arange(N_EXPERTS, dtype=jnp.int32) * tm
    tail = jnp.where(tail < n_tiles * tm, tail, -1)
    zero_tiles = jnp.concatenate([last, tail]).astype(jnp.int32)
    return dest, tile_e, n_valid, zero_tiles, n_tiles * tm


def _moe(h, route, wg, wu, wd, layer, tm=256):
    dest, tile_e, n_valid, zero_tiles, n_rows = _dispatch_plan(route, tm)
    xs = _dispatch_call(dest, zero_tiles, h, n_rows, tm)
    a = _expert_up_call(tile_e, n_valid, xs, wg, wu, layer, tm)
    ys = _expert_down_call(tile_e, n_valid, a, wd, layer, tm)
    return ys, dest


def _trunk(x_prompt, x_sample, cache_k, cache_v, c, c_ctx, prm, final_norm_g, hy_block_sample=512):
    batch, seq, d = x_prompt.shape
    dec_batch, dec_seq, _ = x_sample.shape
    depth = prm['w_in'].shape[0]
    past = cache_k.shape[2]
    t_prompt = batch * seq
    t_sample = dec_batch * dec_seq

    x = jnp.concatenate([x_prompt.reshape(t_prompt, d), x_sample.reshape(t_sample, d)], axis=0)
    n_cond = 1 + dec_batch
    cond = jnp.concatenate([c_ctx[None, :], c, jnp.zeros((8 - n_cond % 8, d), F32)], axis=0)
    mod_all = _mod_call(cond, prm['w_mod'], prm['b_mod'])
    cos, sin_signed = _rope_tables(dec_seq)
    sconv_rows_p = math.gcd(t_prompt, max(seq, 4096 // seq * seq))

    new_k, new_v = [], []
    big = ('w_mod', 'w_in', 'w_out', 'exp_w_gate', 'exp_w_up', 'exp_w_down')
    w_in, w_out = prm['w_in'].astype(BF16), prm['w_out'].astype(BF16)
    w_gate, w_up, w_down = prm['exp_w_gate'], prm['exp_w_up'], prm['exp_w_down']

    for l in range(depth):
        p = {name: w[l] for name, w in prm.items() if name not in big}
        mod = mod_all[l, :n_cond].reshape(n_cond, 1, 6 * d)

        h1 = _norm_mod_call(x, p['norm1_g'], mod, t_prompt, dec_seq)
        z = _proj_call(h1, w_in, l, "in_proj")

        attn_p, k_l, v_l = _attn_prompt_call(z, p['q_norm_g'], p['k_norm_g'], batch, seq)
        qr = _head_rope_call(z, 0, N_HEADS, cos, sin_signed, p['q_norm_g'], HEAD_DIM ** -0.5,
                             t_prompt, dec_batch, dec_seq)
        kr = _head_rope_call(z, COL_K, N_KV_HEADS, cos, sin_signed, p['k_norm_g'], 1.0,
                             t_prompt, dec_batch, dec_seq)
        attn_s = _attn_sample_call(z, qr, kr, cache_k[:, l].reshape(dec_batch, past, KV_DIM),
                                   cache_v[:, l].reshape(dec_batch, past, KV_DIM), t_prompt, dec_batch, dec_seq)
        new_k.append(k_l.reshape(batch, seq, N_KV_HEADS, HEAD_DIM))
        new_v.append(v_l.reshape(batch, seq, N_KV_HEADS, HEAD_DIM))

        hy_p = _hyena_group(z, 0, batch, seq, seq, p, sconv_rows_p)
        hy_s = _hyena_group(z, t_prompt, dec_batch, dec_seq, min(hy_block_sample, dec_seq), p, dec_seq)

        gm = _gmlp_call(z, p['gm_norm_g'], p['gm_norm_b'], p['gm_ws'], p['gm_bs'])

        mix = _mix_norm_call(attn_p, attn_s, hy_p, hy_s, gm, p['out_norm_g'], t_prompt)
        x = _proj_call(mix, w_out, l, "out_proj_residual", out_dtype=F32, residual=(x, mod, 2, t_prompt, dec_seq))

        wr = jnp.concatenate([p['router_g_w'],
                              p['router_e_w'].transpose(1, 0, 2).reshape(d, N_EXPERTS)], axis=1)
        wr = jnp.pad(wr, ((0, 0), (0, LANES - wr.shape[1])))
        br = jnp.pad(jnp.concatenate([p['router_g_b'], p['router_e_b'].reshape(-1)]),
                     (0, LANES - N_GROUPS - N_EXPERTS)).reshape(1, LANES)
        h2, route = _router_call(x, p['norm2_g'], mod, wr, br, t_prompt, dec_seq)
        ys, dest = _moe(h2, route, w_gate, w_up, w_down, l)
        x = _combine_call(dest, x, route, ys, mod, final_norm_g, l == depth - 1, t_prompt, dec_seq)

    y_prompt = x[0].reshape(batch, seq, d)
    y_sample = x[1].reshape(dec_batch, dec_seq, d)
    return y_prompt, y_sample, jnp.stack(new_k, axis=1), jnp.stack(new_v, axis=1)


_PARAM_NAMES = ('norm1_g', 'norm2_g', 'w_mod', 'b_mod', 'w_in', 'q_norm_g', 'k_norm_g', 'hy_conv_w', 'hy_conv_b',
                'hf_w1', 'hf_b1', 'hf_w2', 'hf_b2', 'hf_w3', 'hf_b3', 'hf_freq', 'hy_bias', 'gm_norm_g',
                'gm_norm_b', 'gm_ws', 'gm_bs', 'out_norm_g', 'w_out', 'router_g_w', 'router_g_b', 'router_e_w',
                'router_e_b', 'exp_w_gate', 'exp_w_up', 'exp_w_down')


def kernel(x_prompt, x_sample, cache_k, cache_v, c, c_ctx, norm1_g, norm2_g, w_mod, b_mod, w_in, q_norm_g, k_norm_g, hy_conv_w, hy_conv_b, hf_w1, hf_b1, hf_w2, hf_b2, hf_w3, hf_b3, hf_freq, hy_bias, gm_norm_g, gm_norm_b, gm_ws, gm_bs, out_norm_g, w_out, router_g_w, router_g_b, router_e_w, router_e_b, exp_w_gate, exp_w_up, exp_w_down, final_norm_g):
    values = (norm1_g, norm2_g, w_mod, b_mod, w_in, q_norm_g, k_norm_g, hy_conv_w, hy_conv_b, hf_w1, hf_b1, hf_w2,
              hf_b2, hf_w3, hf_b3, hf_freq, hy_bias, gm_norm_g, gm_norm_b, gm_ws, gm_bs, out_norm_g, w_out,
              router_g_w, router_g_b, router_e_w, router_e_b, exp_w_gate, exp_w_up, exp_w_down)
    prm = dict(zip(_PARAM_NAMES, values))
    return _trunk(x_prompt, x_sample, cache_k, cache_v, c, c_ctx, prm, final_norm_g)
```

```python
import functools
import math

import numpy as np
import jax
import jax.numpy as jnp
from jax import lax
from jax.experimental import pallas as pl
from jax.experimental.pallas import tpu as pltpu

F32 = jnp.float32
BF16 = jnp.bfloat16

D_MODEL = 4096
GRID_W = 64
HEAD_DIM = 128
N_HEADS = 16
N_KV_HEADS = 4
Q_PER_KV = N_HEADS // N_KV_HEADS
ATTN_DIM = N_HEADS * HEAD_DIM
KV_DIM = N_KV_HEADS * HEAD_DIM
HY_DIM = 1024
GM_DIM = 1024
GM_HEADS = 8
CHUNK = 128
PROJ_DIM = ATTN_DIM + 2 * KV_DIM + 3 * HY_DIM + 2 * GM_DIM
ROPE_THETA = 10000.0
FILTER_EMB = 33
FILTER_HIDDEN = 64
DECAY_TARGET = 1e-2
FAST_DECAY_PCT = 0.3
SLOW_DECAY_PCT = 1.5
MOD_SHIFT = 0.05
N_GROUPS = 4
EXPERTS_PER_GROUP = 4
N_EXPERTS = 16
EXPERT_FF = 1024
EPS = 1e-6

COL_K = ATTN_DIM
COL_V = ATTN_DIM + KV_DIM
COL_HY = ATTN_DIM + 2 * KV_DIM
COL_GM = COL_HY + 3 * HY_DIM

LANES = 128
VMEM_LIMIT = 56 * 1024 * 1024
HI = lax.Precision.HIGHEST


def _cparams(*sem, **kw):
    return pltpu.CompilerParams(dimension_semantics=sem, vmem_limit_bytes=VMEM_LIMIT, **kw)


def _rms(x):
    return x * lax.rsqrt(jnp.mean(x * x, axis=-1, keepdims=True) + EPS)


def _sigmoid(x):
    return 1.0 / (1.0 + jnp.exp(-x))


def _dot_3pass(a, b):
    a_hi = a.astype(BF16)
    a_lo = (a - a_hi.astype(F32)).astype(BF16)
    b_hi = b.astype(BF16)
    b_lo = (b - b_hi.astype(F32)).astype(BF16)
    return (jnp.dot(a_hi, b_hi, preferred_element_type=F32) + jnp.dot(a_hi, b_lo, preferred_element_type=F32)
            + jnp.dot(a_lo, b_hi, preferred_element_type=F32))


def _pack_bf16_pairs(x):
    n = x.shape[1] // 2
    lo = lax.bitcast_convert_type(x[:, :n].astype(BF16).astype(F32), jnp.uint32)
    hi = lax.bitcast_convert_type(x[:, n:].astype(BF16).astype(F32), jnp.uint32)
    return hi | (lo >> 16)


def _unpack_bf16_pairs(u):
    lo = lax.bitcast_convert_type(u << 16, F32)
    hi = lax.bitcast_convert_type(u & jnp.uint32(0xFFFF0000), F32)
    return jnp.concatenate([lo, hi], axis=1)


def _seg_of_row(row, t_prompt, dec_seq):
    return jnp.where(row < t_prompt, 0, 1 + (row - t_prompt) // dec_seq)


def _mod_kernel(c_ref, w_ref, b_ref, o_ref):
    c = c_ref[...]
    s = (c * _sigmoid(c)).astype(BF16)
    o_ref[...] = jnp.dot(s, w_ref[...].astype(BF16), preferred_element_type=F32) + b_ref[...]


def _mod_call(cond, w_mod, b_mod):
    depth, d, n = w_mod.shape
    r = cond.shape[0]
    tn = 512
    return pl.pallas_call(
        _mod_kernel,
        grid=(depth, n // tn),
        in_specs=[pl.BlockSpec((r, d), lambda l, j: (0, 0)),
                  pl.BlockSpec((None, d, tn), lambda l, j: (l, 0, j)),
                  pl.BlockSpec((None, 1, tn), lambda l, j: (l, 0, j))],
        out_specs=pl.BlockSpec((None, r, tn), lambda l, j: (l, 0, j)),
        out_shape=jax.ShapeDtypeStruct((depth, r, n), F32),
        compiler_params=_cparams("parallel", "parallel"),
        name="adaln_mod",
    )(cond, w_mod, b_mod.reshape(depth, 1, n))


def _norm_mod_kernel(x_ref, g_ref, sh_ref, sc_ref, o_ref):
    y = _rms(x_ref[...]) * g_ref[...]
    o_ref[...] = (y * (1.0 + sc_ref[...]) + sh_ref[...]).astype(o_ref.dtype)


def _norm_mod_call(x, g, mod, t_prompt, dec_seq, tm=512):
    t, d = x.shape
    seg = lambda i: _seg_of_row(i * tm, t_prompt, dec_seq)
    return pl.pallas_call(
        _norm_mod_kernel,
        grid=(t // tm,),
        in_specs=[pl.BlockSpec((tm, d), lambda i: (i, 0)),
                  pl.BlockSpec((1, d), lambda i: (0, 0)),
                  pl.BlockSpec((None, 1, d), lambda i: (seg(i), 0, 0)),
                  pl.BlockSpec((None, 1, d), lambda i: (seg(i), 0, 1))],
        out_specs=pl.BlockSpec((tm, d), lambda i: (i, 0)),
        out_shape=jax.ShapeDtypeStruct((t, d), BF16),
        compiler_params=_cparams("parallel"),
        name="norm1_modulate",
    )(x, g.reshape(1, d), mod, mod)


def _proj_kernel(a_ref, w_ref, o_ref):
    o_ref[...] = jnp.dot(a_ref[...], w_ref[...], preferred_element_type=F32).astype(o_ref.dtype)


def _proj_residual_kernel(a_ref, w_ref, x_ref, gate_ref, o_ref):
    o_ref[...] = x_ref[...] + gate_ref[...] * jnp.dot(a_ref[...], w_ref[...], preferred_element_type=F32)


def _proj_call(a, w, layer, name, out_dtype=BF16, residual=None, tm=1024, tn=1024):
    t, k = a.shape
    n = w.shape[2]
    in_specs = [pl.BlockSpec((tm, k), lambda i, j: (i, 0)),
                pl.BlockSpec((None, k, tn), lambda i, j: (layer, 0, j))]
    args = [a, w]
    body = _proj_kernel
    if residual is not None:
        x, mod, gate_block, t_prompt, dec_seq = residual
        seg = lambda i: _seg_of_row(i * tm, t_prompt, dec_seq)
        in_specs += [pl.BlockSpec((tm, tn), lambda i, j: (i, j)),
                     pl.BlockSpec((None, 1, tn), lambda i, j: (seg(i), 0, gate_block * (n // tn) + j))]
        args += [x, mod]
        body = _proj_residual_kernel
    return pl.pallas_call(
        body,
        grid=(t // tm, n // tn),
        in_specs=in_specs,
        out_specs=pl.BlockSpec((tm, tn), lambda i, j: (i, j)),
        out_shape=jax.ShapeDtypeStruct((t, n), out_dtype),
        compiler_params=_cparams("parallel", "parallel"),
        name=name,
    )(*args)


def _attn_prompt_kernel(q_ref, k_ref, v_ref, qg_ref, kg_ref, o_ref, kc_ref, vc_ref):
    kn = _rms(k_ref[...].astype(F32)) * kg_ref[...]
    vb = v_ref[...]
    kc_ref[...] = kn
    vc_ref[...] = vb.astype(F32)
    kb = kn.astype(BF16)
    scale = HEAD_DIM ** -0.5
    for m in range(Q_PER_KV):
        sl = slice(m * HEAD_DIM, (m + 1) * HEAD_DIM)
        qn = (_rms(q_ref[:, sl].astype(F32)) * qg_ref[...] * scale).astype(BF16)
        s = lax.dot_general(qn, kb, (((1,), (1,)), ((), ())), preferred_element_type=F32)
        p = jnp.exp(s - jnp.max(s, axis=-1, keepdims=True))
        l = jnp.sum(p, axis=-1, keepdims=True)
        o = jnp.dot(p.astype(BF16), vb, preferred_element_type=F32)
        o_ref[:, sl] = (o / l).astype(o_ref.dtype)


def _attn_prompt_call(z, qg, kg, batch, seq):
    gw = Q_PER_KV * HEAD_DIM
    tp = batch * seq
    return pl.pallas_call(
        _attn_prompt_kernel,
        grid=(batch, N_KV_HEADS),
        in_specs=[pl.BlockSpec((seq, gw), lambda b, g: (b, g)),
                  pl.BlockSpec((seq, HEAD_DIM), lambda b, g: (b, COL_K // HEAD_DIM + g)),
                  pl.BlockSpec((seq, HEAD_DIM), lambda b, g: (b, COL_V // HEAD_DIM + g)),
                  pl.BlockSpec((1, HEAD_DIM), lambda b, g: (0, 0)),
                  pl.BlockSpec((1, HEAD_DIM), lambda b, g: (0, 0))],
        out_specs=[pl.BlockSpec((seq, gw), lambda b, g: (b, g)),
                   pl.BlockSpec((seq, HEAD_DIM), lambda b, g: (b, g)),
                   pl.BlockSpec((seq, HEAD_DIM), lambda b, g: (b, g))],
        out_shape=[jax.ShapeDtypeStruct((tp, ATTN_DIM), BF16),
                   jax.ShapeDtypeStruct((tp, KV_DIM), F32),
                   jax.ShapeDtypeStruct((tp, KV_DIM), F32)],
        compiler_params=_cparams("parallel", "parallel"),
        name="attn_context",
    )(z, z, z, qg.reshape(1, HEAD_DIM), kg.reshape(1, HEAD_DIM))


def _rope(x, cos, sin_signed):
    lane = lax.broadcasted_iota(jnp.int32, x.shape, 1)
    quarter = HEAD_DIM // 4
    fwd = pltpu.roll(x, HEAD_DIM - quarter, 1)
    bwd = pltpu.roll(x, quarter, 1)
    swapped = jnp.where((lane % (2 * quarter)) < quarter, fwd, bwd)
    return x * cos + swapped * sin_signed


def _head_rope_kernel(x_ref, c_ref, s_ref, g_ref, o_ref, *, scale):
    for h in range(x_ref.shape[1] // HEAD_DIM):
        sl = slice(h * HEAD_DIM, (h + 1) * HEAD_DIM)
        xn = _rms(x_ref[:, sl].astype(F32)) * g_ref[...]
        o_ref[:, sl] = (_rope(xn, c_ref[...], s_ref[...]) * scale).astype(o_ref.dtype)


def _head_rope_call(z, col0, n_heads, cos, sin_signed, g, scale, t_prompt, dec_batch, dec_seq, tr=512):
    nr = dec_seq // tr
    width = n_heads * HEAD_DIM
    return pl.pallas_call(
        functools.partial(_head_rope_kernel, scale=scale),
        grid=(dec_batch, nr),
        in_specs=[pl.BlockSpec((tr, width), lambda b, r: (t_prompt // tr + b * nr + r, col0 // width)),
                  pl.BlockSpec((tr, HEAD_DIM), lambda b, r: (r, 0)),
                  pl.BlockSpec((tr, HEAD_DIM), lambda b, r: (r, 0)),
                  pl.BlockSpec((1, HEAD_DIM), lambda b, r: (0, 0))],
        out_specs=pl.BlockSpec((tr, width), lambda b, r: (b * nr + r, 0)),
        out_shape=jax.ShapeDtypeStruct((dec_batch * dec_seq, width), BF16),
        compiler_params=_cparams("parallel", "parallel"),
        name="attn_head_rope",
    )(z, cos, sin_signed, g.reshape(1, HEAD_DIM))


def _attn_sample_kernel(q_ref, k_ref, v_ref, ck_ref, cv_ref, o_ref, m_ref, acc_ref):
    j = pl.program_id(3)

    @pl.when(j == 0)
    def _():
        m_ref[...] = jnp.full(m_ref.shape, -jnp.inf, F32)
        acc_ref[...] = jnp.zeros(acc_ref.shape, F32)

    def step(kb, vb):
        reps = kb.shape[0] // LANES
        v1 = jnp.concatenate([vb, jnp.ones(vb.shape, BF16)], axis=1)
        for m in range(Q_PER_KV):
            qm = q_ref[:, m * HEAD_DIM:(m + 1) * HEAD_DIM]
            s = lax.dot_general(qm, kb, (((1,), (1,)), ((), ())), preferred_element_type=F32)
            m_prev = m_ref[m]
            m_next = jnp.maximum(m_prev, jnp.max(s, axis=-1, keepdims=True))
            alpha = jnp.exp(m_prev - m_next)
            p = jnp.exp(s - jnp.concatenate([m_next] * reps, axis=1))
            acc_ref[m] = (jnp.concatenate([alpha, alpha], axis=1) * acc_ref[m]
                          + jnp.dot(p.astype(BF16), v1, preferred_element_type=F32))
            m_ref[m] = m_next

    @pl.when(j == 0)
    def _():
        step(ck_ref[...].astype(BF16), cv_ref[...].astype(BF16))

    @pl.when(j > 0)
    def _():
        step(k_ref[...], v_ref[...])

    @pl.when(j == pl.num_programs(3) - 1)
    def _():
        for m in range(Q_PER_KV):
            o_ref[:, m * HEAD_DIM:(m + 1) * HEAD_DIM] = (
                acc_ref[m, :, :HEAD_DIM] / acc_ref[m, :, HEAD_DIM:]).astype(o_ref.dtype)


def _attn_sample_call(z, qr, kr, ctx_k, ctx_v, t_prompt, dec_batch, dec_seq, tq=512, tk=1024):
    gw = Q_PER_KV * HEAD_DIM
    past = ctx_k.shape[1]
    tk = min(tk, dec_seq)
    nq = dec_seq // tq
    nk = dec_seq // tk
    vrow = lambda b, j: (t_prompt + b * dec_seq) // tk + jnp.maximum(j - 1, 0)
    krow = lambda b, j: b * nk + jnp.maximum(j - 1, 0)
    return pl.pallas_call(
        _attn_sample_kernel,
        grid=(dec_batch, N_KV_HEADS, nq, nk + 1),
        in_specs=[pl.BlockSpec((tq, gw), lambda b, g, i, j: (b * nq + i, g)),
                  pl.BlockSpec((tk, HEAD_DIM), lambda b, g, i, j: (krow(b, j), g)),
                  pl.BlockSpec((tk, HEAD_DIM), lambda b, g, i, j: (vrow(b, j), COL_V // HEAD_DIM + g)),
                  pl.BlockSpec((None, past, HEAD_DIM), lambda b, g, i, j: (b, 0, g)),
                  pl.BlockSpec((None, past, HEAD_DIM), lambda b, g, i, j: (b, 0, g))],
        out_specs=pl.BlockSpec((tq, gw), lambda b, g, i, j: (b * nq + i, g)),
        out_shape=jax.ShapeDtypeStruct((dec_batch * dec_seq, ATTN_DIM), BF16),
        scratch_shapes=[pltpu.VMEM((Q_PER_KV, tq, LANES), F32),
                        pltpu.VMEM((Q_PER_KV, tq, 2 * HEAD_DIM), F32)],
        compiler_params=_cparams("parallel", "parallel", "parallel", "arbitrary"),
        name="attn_latent",
    )(qr, kr, z, ctx_k, ctx_v)


def _rope_tables(n_tokens):
    quarter = HEAD_DIM // 4
    n_rows = n_tokens // GRID_W
    row = jnp.repeat(jnp.arange(n_rows, dtype=F32), GRID_W)
    col = jnp.tile(jnp.arange(GRID_W, dtype=F32), n_rows)
    freqs = ROPE_THETA ** (-jnp.arange(quarter, dtype=F32) / quarter)
    ar = row[:, None] * freqs[None, :]
    ac = col[:, None] * freqs[None, :]
    cos = jnp.concatenate([jnp.cos(ar), jnp.cos(ar), jnp.cos(ac), jnp.cos(ac)], axis=-1)
    sin = jnp.concatenate([-jnp.sin(ar), jnp.sin(ar), -jnp.sin(ac), jnp.sin(ac)], axis=-1)
    return cos, sin


def _gelu_tanh(x):
    return 0.5 * x * (1.0 + jnp.tanh(math.sqrt(2.0 / math.pi) * (x + 0.044715 * (x * x * x))))


def _gmlp_kernel(z_ref, g_ref, b_ref, ws_ref, bs_ref, o_ref):
    hw = GM_DIM // GM_HEADS
    for ch in range(z_ref.shape[0] // CHUNK):
        rows = slice(ch * CHUNK, (ch + 1) * CHUNK)
        u = _gelu_tanh(z_ref[rows, :GM_DIM].astype(F32))
        v = _gelu_tanh(z_ref[rows, GM_DIM:].astype(F32))
        mu = jnp.mean(v, axis=-1, keepdims=True)
        vc = v - mu
        var = jnp.mean(vc * vc, axis=-1, keepdims=True)
        vn = (vc * lax.rsqrt(var + EPS) * g_ref[...] + b_ref[...]).astype(BF16)
        for h in range(GM_HEADS):
            cols = slice(h * hw, (h + 1) * hw)
            s = jnp.dot(ws_ref[h], vn[:, cols], preferred_element_type=F32) + bs_ref[h]
            o_ref[rows, cols] = (u[:, cols] * s).astype(o_ref.dtype)


def _gmlp_call(z, g, b, ws, bs, tr=512):
    t = z.shape[0]
    return pl.pallas_call(
        _gmlp_kernel,
        grid=(t // tr,),
        in_specs=[pl.BlockSpec((tr, 2 * GM_DIM), lambda i: (i, COL_GM // (2 * GM_DIM))),
                  pl.BlockSpec((1, GM_DIM), lambda i: (0, 0)),
                  pl.BlockSpec((1, GM_DIM), lambda i: (0, 0)),
                  pl.BlockSpec((GM_HEADS, CHUNK, CHUNK), lambda i: (0, 0, 0)),
                  pl.BlockSpec((GM_HEADS, CHUNK, 1), lambda i: (0, 0, 0))],
        out_specs=pl.BlockSpec((tr, GM_DIM), lambda i: (i, 0)),
        out_shape=jax.ShapeDtypeStruct((t, GM_DIM), BF16),
        compiler_params=_cparams("parallel"),
        name="chunk_gmlp",
    )(z, g.reshape(1, GM_DIM), b.reshape(1, GM_DIM), ws.astype(BF16), bs.reshape(GM_HEADS, CHUNK, 1))


def _sconv_kernel(z_ref, w_ref, b_ref, o_ref, *, seg_len):
    x = z_ref[...].astype(F32)
    n = x.shape[0]
    pos = lax.broadcasted_iota(jnp.int32, x.shape, 0) % seg_len
    prev = jnp.where(pos == 0, 0.0, pltpu.roll(x, 1, 0))
    nxt = jnp.where(pos == seg_len - 1, 0.0, pltpu.roll(x, n - 1, 0))
    o_ref[...] = b_ref[...] + prev * w_ref[0:1, :] + x * w_ref[1:2, :] + nxt * w_ref[2:3, :]


def _sconv_call(z, w, b, row0, n_rows, seg_len, block_rows, cw=256):
    width = 3 * HY_DIM
    return pl.pallas_call(
        functools.partial(_sconv_kernel, seg_len=seg_len),
        grid=(n_rows // block_rows, width // cw),
        in_specs=[pl.BlockSpec((block_rows, cw), lambda i, c: (row0 // block_rows + i, COL_HY // cw + c)),
                  pl.BlockSpec((3, cw), lambda i, c: (0, c)),
                  pl.BlockSpec((1, cw), lambda i, c: (0, c))],
        out_specs=pl.BlockSpec((block_rows, cw), lambda i, c: (i, c)),
        out_shape=jax.ShapeDtypeStruct((n_rows, width), F32),
        compiler_params=_cparams("parallel", "parallel"),
        name="hyena_short_conv",
    )(z, w, b.reshape(1, width))


def _filter_kernel(z_ref, t_ref, w1_ref, b1_ref, w2_ref, b2_ref, w3a_ref, b3a_ref, w3b_ref, b3b_ref,
                   fr_ref, dl_ref, o_ref, *, tr):
    fr = fr_ref[...]
    h = jnp.sin(fr * (jnp.dot(z_ref[...], w1_ref[...], precision=HI, preferred_element_type=F32) + b1_ref[...]))
    h = jnp.sin(fr * (jnp.dot(h, w2_ref[...], precision=HI, preferred_element_type=F32) + b2_ref[...]))
    decay = jnp.exp(-t_ref[...] * dl_ref[...]) + MOD_SHIFT
    row = lax.broadcasted_iota(jnp.int32, decay.shape, 0) + pl.program_id(0) * tr
    decay = jnp.where(row == 0, 0.0, decay)
    for o, (w3_ref, b3_ref) in enumerate(((w3a_ref, b3a_ref), (w3b_ref, b3b_ref))):
        taps = _dot_3pass(h, w3_ref[...]) + b3_ref[...]
        o_ref[:, o * HY_DIM:(o + 1) * HY_DIM] = taps * decay


def _filter_call(length, w1, b1, w2, b2, w3, b3, fr, tr=256):
    bands = (FILTER_EMB - 1) // 2
    t = np.linspace(0.0, 1.0, length)[:, None]
    wv = 2.0 * np.pi * np.arange(length)[:, None] / length
    f = np.linspace(1e-4, bands - 1, bands)[None, :]
    zf = np.concatenate([t, np.cos(f * wv), -np.sin(f * wv)], axis=-1)
    min_decay = math.log(DECAY_TARGET) / SLOW_DECAY_PCT
    max_decay = math.log(DECAY_TARGET) / FAST_DECAY_PCT
    deltas = np.abs(np.linspace(min_decay, max_decay, HY_DIM))[None, :]
    lag = np.minimum(np.abs(np.arange(2 * length) - length), length - 1)
    pad = LANES - FILTER_EMB
    hp = LANES - FILTER_HIDDEN
    z2 = jnp.asarray(np.pad(zf[lag], ((0, 0), (0, pad))).astype(np.float32))
    t2 = jnp.asarray(t[lag].astype(np.float32))
    dl = jnp.asarray(deltas.astype(np.float32))
    w1p = jnp.pad(w1, ((0, pad), (0, hp)))
    w2p = jnp.pad(w2, ((0, hp), (0, hp)))
    w3p = jnp.pad(w3, ((0, hp), (0, 0)))
    b1p = jnp.pad(b1, (0, hp)).reshape(1, LANES)
    b2p = jnp.pad(b2, (0, hp)).reshape(1, LANES)
    frp = jnp.pad(fr, (0, hp)).reshape(1, LANES)
    b3r = b3.reshape(1, -1)
    nblk = 2 * length // tr
    half = length // tr
    wcol = lambda r, o: o * 2 + jnp.where(r < half, 1, 0)
    small = pl.BlockSpec((1, LANES), lambda r: (0, 0))
    square = pl.BlockSpec((LANES, LANES), lambda r: (0, 0))
    return pl.pallas_call(
        functools.partial(_filter_kernel, tr=tr),
        grid=(nblk,),
        in_specs=[pl.BlockSpec((tr, LANES), lambda r: (r, 0)),
                  pl.BlockSpec((tr, 1), lambda r: (r, 0)),
                  square, small, square, small,
                  pl.BlockSpec((LANES, HY_DIM), lambda r: (0, wcol(r, 0))),
                  pl.BlockSpec((1, HY_DIM), lambda r: (0, wcol(r, 0))),
                  pl.BlockSpec((LANES, HY_DIM), lambda r: (0, wcol(r, 1))),
                  pl.BlockSpec((1, HY_DIM), lambda r: (0, wcol(r, 1))),
                  small,
                  pl.BlockSpec((1, HY_DIM), lambda r: (0, 0))],
        out_specs=pl.BlockSpec((tr, 2 * HY_DIM), lambda r: (r, 0)),
        out_shape=jax.ShapeDtypeStruct((2 * length, 2 * HY_DIM), F32),
        compiler_params=_cparams("parallel"),
        name="hyena_filter_mlp",
    )(z2, t2, w1p, b1p, w2p, b2p, w3p, b3r, w3p, b3r, frp, dl)


def _dft_constants(p):
    n = 2 * p
    idx = np.arange(p, dtype=np.float64)
    ang = 2.0 * np.pi * np.outer(idx, idx) / n
    re = np.cos(ang)
    im = -np.sin(ang)
    im[0, :] = np.cos(np.pi * idx)
    fwd = np.concatenate([re, im], axis=0)
    sign = np.where(np.arange(p) % 2 == 0, 1.0, -1.0)
    ar = (2.0 / n) * np.cos(ang)
    ar[:, 0] = 1.0 / n
    ai = -(2.0 / n) * np.sin(ang)
    ai[:, 0] = sign / n
    inv = np.concatenate([ar, ai], axis=1)
    as_bf16 = lambda a: jnp.asarray(a.astype(np.float32)).astype(BF16)
    return as_bf16(fwd), as_bf16(inv)


def _dft_kernel(f_ref, x_ref, o_ref):
    o_ref[...] = jnp.dot(f_ref[...], x_ref[...].astype(BF16), preferred_element_type=F32).astype(o_ref.dtype)


def _dft_call(x, fwd, p, col0, cw=256):
    nblk = x.shape[0] // p
    return pl.pallas_call(
        _dft_kernel,
        grid=(nblk, HY_DIM // cw),
        in_specs=[pl.BlockSpec((2 * p, p), lambda i, c: (0, 0)),
                  pl.BlockSpec((p, cw), lambda i, c: (i, col0 // cw + c))],
        out_specs=pl.BlockSpec((None, 2 * p, cw), lambda i, c: (i, 0, c)),
        out_shape=jax.ShapeDtypeStruct((nblk, 2 * p, HY_DIM), BF16),
        compiler_params=_cparams("parallel", "parallel"),
        name="hyena_block_dft",
    )(fwd, x)


def _filter_dft_kernel(f_ref, blk_ref, o_ref, tail_ref, *, p):
    b = pl.program_id(2)
    blk = blk_ref[...].astype(BF16)
    spec = jnp.dot(f_ref[...], blk, preferred_element_type=F32)

    @pl.when(b > 0)
    def _():
        o_ref[...] = (spec + tail_ref[...]).astype(o_ref.dtype)

    row = lax.broadcasted_iota(jnp.int32, spec.shape, 0)
    first = blk[0:1, :].astype(F32)
    real_row = jnp.logical_or(row < p, row == p)
    sign = (1 - 2 * (row % 2)).astype(F32)
    tail_ref[...] = sign * (spec - jnp.where(real_row, first, 0.0))


def _filter_dft_call(taps, fwd, p, cw=256):
    nb2 = taps.shape[0] // p
    nd = nb2 - 1
    ncb = HY_DIM // cw
    return pl.pallas_call(
        functools.partial(_filter_dft_kernel, p=p),
        grid=(2, ncb, nb2),
        in_specs=[pl.BlockSpec((2 * p, p), lambda o, c, b: (0, 0)),
                  pl.BlockSpec((p, cw), lambda o, c, b: (b, o * ncb + c))],
        out_specs=pl.BlockSpec((None, None, 2 * p, cw), lambda o, c, b: (o, jnp.maximum(b - 1, 0), 0, c)),
        out_shape=jax.ShapeDtypeStruct((2, nd, 2 * p, HY_DIM), BF16),
        scratch_shapes=[pltpu.VMEM((2 * p, cw), F32)],
        compiler_params=_cparams("parallel", "parallel", "arbitrary"),
        name="hyena_filter_dft",
    )(fwd, taps)


def _specconv_kernel(xs_ref, gs_ref, inv_ref, v_ref, gate_ref, bias_ref, o_ref, y_ref, *, nb, p, rc):
    i = pl.program_id(2)
    cw = o_ref.shape[1]

    for r in range(0, p, rc):
        def body(j, carry):
            yre, yim = carry
            d = i - j + nb - 1
            xre = xs_ref[j, r:r + rc, :].astype(F32)
            xim = xs_ref[j, p + r:p + r + rc, :].astype(F32)
            gre = gs_ref[d, r:r + rc, :].astype(F32)
            gim = gs_ref[d, p + r:p + r + rc, :].astype(F32)
            return yre + (xre * gre - xim * gim), yim + (xre * gim + xim * gre)

        zero = jnp.zeros((rc, cw), F32)
        yre, yim = lax.fori_loop(0, nb, body, (zero, zero))
        y_ref[r:r + rc, :] = yre
        y_ref[p + r:p + r + rc, :] = yim

    def edge(j, carry):
        y0, yn = carry
        d = i - j + nb - 1
        x0 = xs_ref[j, 0:16, :].astype(F32)
        xn = xs_ref[j, p:p + 16, :].astype(F32)
        g0 = gs_ref[d, 0:16, :].astype(F32)
        gn = gs_ref[d, p:p + 16, :].astype(F32)
        return y0 + x0 * g0, yn + xn * gn

    zero16 = jnp.zeros((16, cw), F32)
    y0, yn = lax.fori_loop(0, nb, edge, (zero16, zero16))
    y_ref[0:1, :] = y0[0:1, :]
    y_ref[p:p + 1, :] = yn[0:1, :]

    y = jnp.dot(inv_ref[...], y_ref[...].astype(BF16), preferred_element_type=F32)
    v = v_ref[...]
    o_ref[...] = (gate_ref[...] * (y + bias_ref[...] * v)).astype(o_ref.dtype)


def _specconv_call(xs, gs, inv, zc, bias, nseq, nb, p, v_col0, gate_col0, v_src=None, cw=256, rc=32,
                   out_dtype=F32):
    n_rows = nseq * nb * p
    xs4 = xs.reshape(nseq, nb, 2 * p, HY_DIM)
    if v_src is None:
        v_arr, v_spec = zc, pl.BlockSpec((p, cw), lambda c, s, i: (s * nb + i, v_col0 // cw + c))
    else:
        v_arr, v_spec = v_src, pl.BlockSpec((p, cw), lambda c, s, i: (s * nb + i, c))
    return pl.pallas_call(
        functools.partial(_specconv_kernel, nb=nb, p=p, rc=rc),
        grid=(HY_DIM // cw, nseq, nb),
        in_specs=[pl.BlockSpec((None, nb, 2 * p, cw), lambda c, s, i: (s, 0, 0, c)),
                  pl.BlockSpec((2 * nb - 1, 2 * p, cw), lambda c, s, i: (0, 0, c)),
                  pl.BlockSpec((p, 2 * p), lambda c, s, i: (0, 0)),
                  v_spec,
                  pl.BlockSpec((p, cw), lambda c, s, i: (s * nb + i, gate_col0 // cw + c)),
                  pl.BlockSpec((1, cw), lambda c, s, i: (0, c))],
        out_specs=pl.BlockSpec((p, cw), lambda c, s, i: (s * nb + i, c)),
        out_shape=jax.ShapeDtypeStruct((n_rows, HY_DIM), out_dtype),
        scratch_shapes=[pltpu.VMEM((2 * p, cw), F32)],
        compiler_params=_cparams("parallel", "parallel", "parallel"),
        name="hyena_spectral_conv",
    )(xs4, gs, inv, v_arr, zc, bias.reshape(1, HY_DIM))


def _hyena_group(z, row0, nseq, length, p, prm, sconv_rows):
    nb = length // p
    zc = _sconv_call(z, prm['hy_conv_w'], prm['hy_conv_b'], row0, nseq * length, length, sconv_rows)
    taps = _filter_call(length, prm['hf_w1'], prm['hf_b1'], prm['hf_w2'], prm['hf_b2'],
                        prm['hf_w3'], prm['hf_b3'], prm['hf_freq'], tr=min(512, length))
    fwd, inv = _dft_constants(p)
    gs = _filter_dft_call(taps, fwd, p, cw=512)
    cw = 512 if nb == 1 else 256
    vs = _dft_call(zc, fwd, p, 2 * HY_DIM, cw=512)
    u = _specconv_call(vs, gs[0], inv, zc, prm['hy_bias'][0], nseq, nb, p, 2 * HY_DIM, 0, cw=cw)
    us = _dft_call(u, fwd, p, 0, cw=512)
    return _specconv_call(us, gs[1], inv, zc, prm['hy_bias'][1], nseq, nb, p, 0, HY_DIM, v_src=u, cw=cw,
                          out_dtype=BF16)


def _mix_norm_kernel(ap_ref, as_ref, hp_ref, hs_ref, m_ref, g_ref, o_ref, *, ntp):
    i = pl.program_id(0)

    def fill(a_ref, h_ref):
        o_ref[:, :ATTN_DIM] = (_rms(a_ref[...].astype(F32)) * g_ref[:, :ATTN_DIM]).astype(BF16)
        o_ref[:, ATTN_DIM:ATTN_DIM + HY_DIM] = (
            _rms(h_ref[...].astype(F32)) * g_ref[:, ATTN_DIM:ATTN_DIM + HY_DIM]).astype(BF16)
        o_ref[:, ATTN_DIM + HY_DIM:] = (
            _rms(m_ref[...].astype(F32)) * g_ref[:, ATTN_DIM + HY_DIM:]).astype(BF16)

    @pl.when(i < ntp)
    def _():
        fill(ap_ref, hp_ref)

    @pl.when(i >= ntp)
    def _():
        fill(as_ref, hs_ref)


def _mix_norm_call(attn_p, attn_s, hy_p, hy_s, gm, g, t_prompt, tm=512):
    t = gm.shape[0]
    mix_dim = ATTN_DIM + HY_DIM + GM_DIM
    ntp = t_prompt // tm
    nts = t // tm - ntp
    prow = lambda i: (jnp.minimum(i, ntp - 1), 0)
    srow = lambda i: (jnp.clip(i - ntp, 0, nts - 1), 0)
    return pl.pallas_call(
        functools.partial(_mix_norm_kernel, ntp=ntp),
        grid=(t // tm,),
        in_specs=[pl.BlockSpec((tm, ATTN_DIM), prow),
                  pl.BlockSpec((tm, ATTN_DIM), srow),
                  pl.BlockSpec((tm, HY_DIM), prow),
                  pl.BlockSpec((tm, HY_DIM), srow),
                  pl.BlockSpec((tm, GM_DIM), lambda i: (i, 0)),
                  pl.BlockSpec((1, mix_dim), lambda i: (0, 0))],
        out_specs=pl.BlockSpec((tm, mix_dim), lambda i: (i, 0)),
        out_shape=jax.ShapeDtypeStruct((t, mix_dim), BF16),
        compiler_params=_cparams("parallel"),
        name="mix_group_norm",
    )(attn_p, attn_s, hy_p, hy_s, gm, g.reshape(1, mix_dim))


def _router_kernel(x_ref, g_ref, sh_ref, sc_ref, wr_ref, br_ref, h_ref, r_ref):
    y = _rms(x_ref[...]) * g_ref[...]
    h = y * (1.0 + sc_ref[...]) + sh_ref[...]
    h_ref[...] = _pack_bf16_pairs(h)
    logits = _dot_3pass(h, wr_ref[...]) + br_ref[...]
    col = lambda k: logits[:, k:k + 1]

    lg = [col(k) for k in range(N_GROUPS)]
    g_max = functools.reduce(jnp.maximum, lg)
    g_den = sum(jnp.exp(v - g_max) for v in lg)
    g_top = 1.0 / g_den
    g_idx = jnp.full(g_max.shape, N_GROUPS - 1, jnp.int32)
    for k in range(N_GROUPS - 2, -1, -1):
        g_idx = jnp.where(lg[k] == g_max, k, g_idx)

    le = []
    for e in range(EXPERTS_PER_GROUP):
        v = col(N_GROUPS + (N_GROUPS - 1) * EXPERTS_PER_GROUP + e)
        for k in range(N_GROUPS - 2, -1, -1):
            v = jnp.where(g_idx == k, col(N_GROUPS + k * EXPERTS_PER_GROUP + e), v)
        le.append(v)

    e_max = functools.reduce(jnp.maximum, le)
    e1 = jnp.full(e_max.shape, EXPERTS_PER_GROUP - 1, jnp.int32)
    for e in range(EXPERTS_PER_GROUP - 2, -1, -1):
        e1 = jnp.where(le[e] == e_max, e, e1)
    neg = jnp.float32(-jnp.inf)
    rest = [jnp.where(e1 == e, neg, le[e]) for e in range(EXPERTS_PER_GROUP)]
    e2_max = functools.reduce(jnp.maximum, rest)
    e2 = jnp.full(e_max.shape, EXPERTS_PER_GROUP - 1, jnp.int32)
    for e in range(EXPERTS_PER_GROUP - 2, -1, -1):
        e2 = jnp.where(rest[e] == e2_max, e, e2)
    ratio = jnp.exp(e2_max - e_max)
    w1 = g_top / (1.0 + ratio)
    w2 = g_top * ratio / (1.0 + ratio)
    id1 = (g_idx * EXPERTS_PER_GROUP + e1).astype(F32)
    id2 = (g_idx * EXPERTS_PER_GROUP + e2).astype(F32)

    lane = lax.broadcasted_iota(jnp.int32, r_ref.shape, 1)
    r_ref[...] = jnp.where(lane == 0, id1, jnp.where(lane == 1, id2, jnp.where(lane == 2, w1, jnp.where(lane == 3, w2, 0.0))))


def _router_call(x, g, mod, wr, br, t_prompt, dec_seq, tm=512):
    t, d = x.shape
    seg = lambda i: _seg_of_row(i * tm, t_prompt, dec_seq)
    return pl.pallas_call(
        _router_kernel,
        grid=(t // tm,),
        in_specs=[pl.BlockSpec((tm, d), lambda i: (i, 0)),
                  pl.BlockSpec((1, d), lambda i: (0, 0)),
                  pl.BlockSpec((None, 1, d), lambda i: (seg(i), 0, 3)),
                  pl.BlockSpec((None, 1, d), lambda i: (seg(i), 0, 4)),
                  pl.BlockSpec((d, LANES), lambda i: (0, 0)),
                  pl.BlockSpec((1, LANES), lambda i: (0, 0))],
        out_specs=[pl.BlockSpec((tm, d // 2), lambda i: (i, 0)),
                   pl.BlockSpec((tm, LANES), lambda i: (i, 0))],
        out_shape=[jax.ShapeDtypeStruct((t, d // 2), jnp.uint32),
                   jax.ShapeDtypeStruct((t, LANES), F32)],
        compiler_params=_cparams("parallel"),
        name="norm2_router",
    )(x, g.reshape(1, d), mod, mod, wr, br)


def _dispatch_kernel(dest_ref, zt_ref, h_ref, xs_ref, zero_ref, sem, zsem, *, tm, t, expert_tile):
    i = pl.program_id(0)

    @pl.when(i == 0)
    def _():
        zero_ref[...] = jnp.zeros(zero_ref.shape, zero_ref.dtype)

        def clear(q):
            row = pl.multiple_of(jnp.maximum(zt_ref[q], 0), expert_tile)
            return pltpu.make_async_copy(zero_ref, xs_ref.at[pl.ds(row, expert_tile), :], zsem)

        for q in range(2 * N_EXPERTS):
            @pl.when(zt_ref[q] >= 0)
            def _():
                clear(q).start()

        for q in range(2 * N_EXPERTS):
            @pl.when(zt_ref[q] >= 0)
            def _():
                clear(q).wait()

    base = i * tm

    def body(r, c):
        for k in range(2):
            row = dest_ref[k * t + base + r]
            pltpu.make_async_copy(h_ref.at[pl.ds(r, 1), :], xs_ref.at[pl.ds(row, 1), :], sem).start()
        return c

    lax.fori_loop(0, tm, body, 0, unroll=8)
    for k in range(2):
        pltpu.make_async_copy(h_ref, xs_ref.at[pl.ds(0, tm), :], sem).wait()


def _dispatch_call(dest, zero_tiles, h, n_rows, expert_tile, tm=256):
    t, d = h.shape
    return pl.pallas_call(
        functools.partial(_dispatch_kernel, tm=tm, t=t, expert_tile=expert_tile),
        grid_spec=pltpu.PrefetchScalarGridSpec(
            num_scalar_prefetch=2,
            grid=(t // tm,),
            in_specs=[pl.BlockSpec((tm, d), lambda i, dest, zt: (i, 0))],
            out_specs=pl.BlockSpec(memory_space=pl.ANY),
            scratch_shapes=[pltpu.VMEM((expert_tile, d), h.dtype),
                            pltpu.SemaphoreType.DMA(()),
                            pltpu.SemaphoreType.DMA(())]),
        out_shape=jax.ShapeDtypeStruct((n_rows, d), h.dtype),
        compiler_params=_cparams("arbitrary", disable_bounds_checks=True),
        name="moe_dispatch_rows",
    )(dest, zero_tiles, h)


def _stream_expert_weights(plan, sources, wbuf, targets, sem):
    te_ref, first_ref, run_ref, nxt_ref = plan
    m = pl.program_id(1)

    def copies(e, slot):
        return [pltpu.make_async_copy(src, wbuf.at[slot, k], sem.at[slot]) for k, src in enumerate(sources(e))]

    @pl.when(m == 0)
    def _():
        for cp in copies(te_ref[0], 0):
            cp.start()

    @pl.when(first_ref[m] == 1)
    def _():
        slot = run_ref[m] % 2
        for cp in copies(te_ref[m], slot):
            cp.wait()

        @pl.when(nxt_ref[m] >= 0)
        def _():
            for cp in copies(nxt_ref[m], 1 - slot):
                cp.start()

        for k, tgt in enumerate(targets):
            tgt[...] = wbuf[slot, k].astype(BF16)


def _expert_up_kernel(te_ref, nv_ref, first_ref, run_ref, nxt_ref, x_ref, wg_ref, wu_ref, o_ref,
                      wbuf, wgb_ref, wub_ref, sem, *, tm, tf, layer):
    valid = pl.program_id(1) * tm < nv_ref[0]
    cols = pl.ds(pl.multiple_of(pl.program_id(0) * tf, tf), tf)
    _stream_expert_weights((te_ref, first_ref, run_ref, nxt_ref),
                           lambda e: (wg_ref.at[layer, e, :, cols], wu_ref.at[layer, e, :, cols]),
                           wbuf, (wgb_ref, wub_ref), sem)

    @pl.when(valid)
    def _():
        x = _unpack_bf16_pairs(x_ref[...]).astype(BF16)
        hg = jnp.dot(x, wgb_ref[...], preferred_element_type=F32)
        hu = jnp.dot(x, wub_ref[...], preferred_element_type=F32)
        o_ref[...] = (hg * _sigmoid(hg) * hu).astype(o_ref.dtype)

    @pl.when(jnp.logical_not(valid))
    def _():
        o_ref[...] = jnp.zeros(o_ref.shape, o_ref.dtype)


def _expert_up_call(plan, xs, wg, wu, layer, tm, nf=2):
    n, dh = xs.shape
    d, ff = wg.shape[2], wg.shape[3]
    tf = ff // nf
    return pl.pallas_call(
        functools.partial(_expert_up_kernel, tm=tm, tf=tf, layer=layer),
        grid_spec=pltpu.PrefetchScalarGridSpec(
            num_scalar_prefetch=5,
            grid=(nf, n // tm),
            in_specs=[pl.BlockSpec((tm, dh), lambda f, m, *_: (m, 0)),
                      pl.BlockSpec(memory_space=pl.ANY),
                      pl.BlockSpec(memory_space=pl.ANY)],
            out_specs=pl.BlockSpec((tm, tf), lambda f, m, *_: (m, f)),
            scratch_shapes=[pltpu.VMEM((2, 2, d, tf), F32),
                            pltpu.VMEM((d, tf), BF16), pltpu.VMEM((d, tf), BF16),
                            pltpu.SemaphoreType.DMA((2,))]),
        out_shape=jax.ShapeDtypeStruct((n, ff), BF16),
        compiler_params=_cparams("arbitrary", "arbitrary"),
        name="moe_expert_up",
    )(*plan, xs, wg, wu)


def _expert_down_kernel(te_ref, nv_ref, first_ref, run_ref, nxt_ref, a_ref, wd_ref, o_ref,
                        wbuf, wdb_ref, sem, *, tm, layer):
    valid = pl.program_id(1) * tm < nv_ref[0]
    _stream_expert_weights((te_ref, first_ref, run_ref, nxt_ref), lambda e: (wd_ref.at[layer, e],),
                           wbuf, (wdb_ref,), sem)

    @pl.when(valid)
    def _():
        o_ref[...] = _pack_bf16_pairs(jnp.dot(a_ref[...], wdb_ref[...], preferred_element_type=F32))

    @pl.when(jnp.logical_not(valid))
    def _():
        o_ref[...] = jnp.zeros(o_ref.shape, o_ref.dtype)


def _expert_down_call(plan, a, wd, layer, tm):
    n, ff = a.shape
    d = wd.shape[3]
    return pl.pallas_call(
        functools.partial(_expert_down_kernel, tm=tm, layer=layer),
        grid_spec=pltpu.PrefetchScalarGridSpec(
            num_scalar_prefetch=5,
            grid=(1, n // tm),
            in_specs=[pl.BlockSpec((tm, ff), lambda c, m, *_: (m, 0)),
                      pl.BlockSpec(memory_space=pl.ANY)],
            out_specs=pl.BlockSpec((tm, d // 2), lambda c, m, *_: (m, 0)),
            scratch_shapes=[pltpu.VMEM((2, 1, ff, d), F32),
                            pltpu.VMEM((ff, d), BF16),
                            pltpu.SemaphoreType.DMA((2,))]),
        out_shape=jax.ShapeDtypeStruct((n, d // 2), jnp.uint32),
        compiler_params=_cparams("arbitrary", "arbitrary"),
        name="moe_expert_down",
    )(*plan, a, wd)


def _combine_kernel(dest_ref, x_ref, r_ref, gate_ref, fg_ref, ys_ref, *rest, tc, t, nt, ntp, final):
    out_refs, (ybuf, sem) = rest[:-2], rest[-2:]
    i = pl.program_id(0)

    def fetch(tile, slot):
        base = tile * tc

        def body(r, c):
            for k in range(2):
                row = dest_ref[k * t + base + r]
                pltpu.make_async_copy(ys_ref.at[pl.ds(row, 1), :], ybuf.at[slot, pl.ds(k * tc + r, 1), :],
                                      sem.at[slot]).start()
            return c

        lax.fori_loop(0, tc, body, 0, unroll=8)

    @pl.when(i == 0)
    def _():
        fetch(0, 0)

    @pl.when(i + 1 < nt)
    def _():
        fetch(i + 1, (i + 1) % 2)

    slot = i % 2
    pltpu.make_async_copy(ys_ref.at[pl.ds(0, 2 * tc), :], ybuf.at[slot], sem.at[slot]).wait()

    w1 = r_ref[:, 2:3]
    w2 = r_ref[:, 3:4]
    y = w1 * _unpack_bf16_pairs(ybuf[slot, 0:tc, :]) + w2 * _unpack_bf16_pairs(ybuf[slot, tc:2 * tc, :])
    x = x_ref[...] + gate_ref[...] * y
    if final:
        x = _rms(x) * fg_ref[...]
        op_ref, os_ref = out_refs

        @pl.when(i < ntp)
        def _():
            op_ref[...] = x

        @pl.when(i >= ntp)
        def _():
            os_ref[...] = x
    else:
        out_refs[0][...] = x


def _combine_call(dest, x, route, ys, mod, final_g, final, t_prompt, dec_seq, tc=128):
    t, d = x.shape
    seg = lambda i, *_: _seg_of_row(i * tc, t_prompt, dec_seq)
    nt = t // tc
    ntp = t_prompt // tc
    if final:
        out_specs = [pl.BlockSpec((tc, d), lambda i, dest: (jnp.minimum(i, ntp - 1), 0)),
                     pl.BlockSpec((tc, d), lambda i, dest: (jnp.maximum(i - ntp, 0), 0))]
        out_shape = [jax.ShapeDtypeStruct((t_prompt, d), F32), jax.ShapeDtypeStruct((t - t_prompt, d), F32)]
    else:
        out_specs = pl.BlockSpec((tc, d), lambda i, dest: (i, 0))
        out_shape = jax.ShapeDtypeStruct((t, d), F32)
    return pl.pallas_call(
        functools.partial(_combine_kernel, tc=tc, t=t, nt=nt, ntp=ntp, final=final),
        grid_spec=pltpu.PrefetchScalarGridSpec(
            num_scalar_prefetch=1,
            grid=(nt,),
            in_specs=[pl.BlockSpec((tc, d), lambda i, dest: (i, 0)),
                      pl.BlockSpec((tc, LANES), lambda i, dest: (i, 0)),
                      pl.BlockSpec((None, 1, d), lambda i, dest: (seg(i), 0, 5)),
                      pl.BlockSpec((1, d), lambda i, dest: (0, 0)),
                      pl.BlockSpec(memory_space=pl.ANY)],
            out_specs=out_specs,
            scratch_shapes=[pltpu.VMEM((2, 2 * tc, ys.shape[1]), ys.dtype),
                            pltpu.SemaphoreType.DMA((2,))]),
        out_shape=out_shape,
        compiler_params=_cparams("arbitrary", disable_bounds_checks=True),
        name="moe_combine_residual",
    )(dest, x, route, mod, final_g.reshape(1, d), ys)


def _dispatch_plan(route, tm):
    t = route.shape[0]
    flat_e = route[:, 0:2].astype(jnp.int32).T.reshape(-1)
    n_pairs = 2 * t
    n_tiles = n_pairs // tm + N_EXPERTS
    onehot = (flat_e[:, None] == jnp.arange(N_EXPERTS, dtype=jnp.int32)[None, :]).astype(jnp.int32)
    running = jnp.cumsum(onehot, axis=0)
    counts = running[-1]
    rank = jnp.sum((running - 1) * onehot, axis=1)
    padded = ((counts + tm - 1) // tm) * tm
    pad_end = jnp.cumsum(padded)
    pad_start = pad_end - padded
    dest = (jnp.sum(onehot * pad_start[None, :], axis=1) + rank).astype(jnp.int32)
    tile_start = jnp.arange(n_tiles, dtype=jnp.int32) * tm
    tile_e = jnp.sum((pad_end[None, :] <= tile_start[:, None]).astype(jnp.int32), axis=1)
    tile_e = jnp.minimum(tile_e, N_EXPERTS - 1).astype(jnp.int32)
    n_valid = pad_end[-1:].astype(jnp.int32)
    last = jnp.where(counts > 0, pad_end - tm, -1)
    tail = pad_end[-1] + jnp.arange(N_EXPERTS, dtype=jnp.int32) * tm
    tail = jnp.where(tail < n_tiles * tm, tail, -1)
    zero_tiles = jnp.concatenate([last, tail]).astype(jnp.int32)
    used = tile_start < pad_end[-1]
    first = jnp.logical_and(used, jnp.concatenate([jnp.ones((1,), bool), tile_e[1:] != tile_e[:-1]]))
    run = jnp.maximum(jnp.cumsum(first.astype(jnp.int32)) - 1, 0)
    ids = jnp.arange(N_EXPERTS, dtype=jnp.int32)
    later = jnp.logical_and(ids[None, :] > ids[:, None], (counts > 0)[None, :])
    next_e = jnp.min(jnp.where(later, ids[None, :], N_EXPERTS), axis=1)
    next_e = jnp.where(next_e == N_EXPERTS, -1, next_e)
    tile_plan = (tile_e, n_valid, first.astype(jnp.int32), run.astype(jnp.int32), next_e[tile_e].astype(jnp.int32))
    return dest, tile_plan, zero_tiles, n_tiles * tm


def _moe(h, route, wg, wu, wd, layer, tm=256):
    dest, tile_plan, zero_tiles, n_rows = _dispatch_plan(route, tm)
    xs = _dispatch_call(dest, zero_tiles, h, n_rows, tm)
    a = _expert_up_call(tile_plan, xs, wg, wu, layer, tm)
    ys = _expert_down_call(tile_plan, a, wd, layer, tm)
    return ys, dest


def _trunk(x_prompt, x_sample, cache_k, cache_v, c, c_ctx, prm, final_norm_g, hy_block_sample=512):
    batch, seq, d = x_prompt.shape
    dec_batch, dec_seq, _ = x_sample.shape
    depth = prm['w_in'].shape[0]
    past = cache_k.shape[2]
    t_prompt = batch * seq
    t_sample = dec_batch * dec_seq

    x = jnp.concatenate([x_prompt.reshape(t_prompt, d), x_sample.reshape(t_sample, d)], axis=0)
    n_cond = 1 + dec_batch
    cond = jnp.concatenate([c_ctx[None, :], c, jnp.zeros((8 - n_cond % 8, d), F32)], axis=0)
    mod_all = _mod_call(cond, prm['w_mod'], prm['b_mod'])
    cos, sin_signed = _rope_tables(dec_seq)
    sconv_rows_p = math.gcd(t_prompt, max(seq, 4096 // seq * seq))

    new_k, new_v = [], []
    big = ('w_mod', 'w_in', 'w_out', 'exp_w_gate', 'exp_w_up', 'exp_w_down')
    w_in, w_out = prm['w_in'].astype(BF16), prm['w_out'].astype(BF16)
    w_gate, w_up, w_down = prm['exp_w_gate'], prm['exp_w_up'], prm['exp_w_down']

    for l in range(depth):
        p = {name: w[l] for name, w in prm.items() if name not in big}
        mod = mod_all[l, :n_cond].reshape(n_cond, 1, 6 * d)

        h1 = _norm_mod_call(x, p['norm1_g'], mod, t_prompt, dec_seq)
        z = _proj_call(h1, w_in, l, "in_proj")

        attn_p, k_l, v_l = _attn_prompt_call(z, p['q_norm_g'], p['k_norm_g'], batch, seq)
        qr = _head_rope_call(z, 0, N_HEADS, cos, sin_signed, p['q_norm_g'], HEAD_DIM ** -0.5,
                             t_prompt, dec_batch, dec_seq)
        kr = _head_rope_call(z, COL_K, N_KV_HEADS, cos, sin_signed, p['k_norm_g'], 1.0,
                             t_prompt, dec_batch, dec_seq)
        attn_s = _attn_sample_call(z, qr, kr, cache_k[:, l].reshape(dec_batch, past, KV_DIM),
                                   cache_v[:, l].reshape(dec_batch, past, KV_DIM), t_prompt, dec_batch, dec_seq)
        new_k.append(k_l.reshape(batch, seq, N_KV_HEADS, HEAD_DIM))
        new_v.append(v_l.reshape(batch, seq, N_KV_HEADS, HEAD_DIM))

        hy_p = _hyena_group(z, 0, batch, seq, seq, p, sconv_rows_p)
        hy_s = _hyena_group(z, t_prompt, dec_batch, dec_seq, min(hy_block_sample, dec_seq), p, dec_seq)

        gm = _gmlp_call(z, p['gm_norm_g'], p['gm_norm_b'], p['gm_ws'], p['gm_bs'])

        mix = _mix_norm_call(attn_p, attn_s, hy_p, hy_s, gm, p['out_norm_g'], t_prompt)
        x = _proj_call(mix, w_out, l, "out_proj_residual", out_dtype=F32, residual=(x, mod, 2, t_prompt, dec_seq))

        wr = jnp.concatenate([p['router_g_w'],
                              p['router_e_w'].transpose(1, 0, 2).reshape(d, N_EXPERTS)], axis=1)
        wr = jnp.pad(wr, ((0, 0), (0, LANES - wr.shape[1])))
        br = jnp.pad(jnp.concatenate([p['router_g_b'], p['router_e_b'].reshape(-1)]),
                     (0, LANES - N_GROUPS - N_EXPERTS)).reshape(1, LANES)
        h2, route = _router_call(x, p['norm2_g'], mod, wr, br, t_prompt, dec_seq)
        ys, dest = _moe(h2, route, w_gate, w_up, w_down, l)
        x = _combine_call(dest, x, route, ys, mod, final_norm_g, l == depth - 1, t_prompt, dec_seq)

    y_prompt = x[0].reshape(batch, seq, d)
    y_sample = x[1].reshape(dec_batch, dec_seq, d)
    return y_prompt, y_sample, jnp.stack(new_k, axis=1), jnp.stack(new_v, axis=1)


_PARAM_NAMES = ('norm1_g', 'norm2_g', 'w_mod', 'b_mod', 'w_in', 'q_norm_g', 'k_norm_g', 'hy_conv_w', 'hy_conv_b',
                'hf_w1', 'hf_b1', 'hf_w2', 'hf_b2', 'hf_w3', 'hf_b3', 'hf_freq', 'hy_bias', 'gm_norm_g',
                'gm_norm_b', 'gm_ws', 'gm_bs', 'out_norm_g', 'w_out', 'router_g_w', 'router_g_b', 'router_e_w',
                'router_e_b', 'exp_w_gate', 'exp_w_up', 'exp_w_down')


def kernel(x_prompt, x_sample, cache_k, cache_v, c, c_ctx, norm1_g, norm2_g, w_mod, b_mod, w_in, q_norm_g, k_norm_g, hy_conv_w, hy_conv_b, hf_w1, hf_b1, hf_w2, hf_b2, hf_w3, hf_b3, hf_freq, hy_bias, gm_norm_g, gm_norm_b, gm_ws, gm_bs, out_norm_g, w_out, router_g_w, router_g_b, router_e_w, router_e_b, exp_w_gate, exp_w_up, exp_w_down, final_norm_g):
    values = (norm1_g, norm2_g, w_mod, b_mod, w_in, q_norm_g, k_norm_g, hy_conv_w, hy_conv_b, hf_w1, hf_b1, hf_w2,
              hf_b2, hf_w3, hf_b3, hf_freq, hy_bias, gm_norm_g, gm_norm_b, gm_ws, gm_bs, out_norm_g, w_out,
              router_g_w, router_g_b, router_e_w, router_e_b, exp_w_gate, exp_w_up, exp_w_down)
    prm = dict(zip(_PARAM_NAMES, values))
    return _trunk(x_prompt, x_sample, cache_k, cache_v, c, c_ctx, prm, final_norm_g)
```

```python
import functools
import math

import numpy as np
import jax
import jax.numpy as jnp
from jax import lax
from jax.experimental import pallas as pl
from jax.experimental.pallas import tpu as pltpu

F32 = jnp.float32
BF16 = jnp.bfloat16

D_MODEL = 4096
GRID_W = 64
HEAD_DIM = 128
N_HEADS = 16
N_KV_HEADS = 4
Q_PER_KV = N_HEADS // N_KV_HEADS
ATTN_DIM = N_HEADS * HEAD_DIM
KV_DIM = N_KV_HEADS * HEAD_DIM
HY_DIM = 1024
GM_DIM = 1024
GM_HEADS = 8
CHUNK = 128
PROJ_DIM = ATTN_DIM + 2 * KV_DIM + 3 * HY_DIM + 2 * GM_DIM
ROPE_THETA = 10000.0
FILTER_EMB = 33
FILTER_HIDDEN = 64
DECAY_TARGET = 1e-2
FAST_DECAY_PCT = 0.3
SLOW_DECAY_PCT = 1.5
MOD_SHIFT = 0.05
N_GROUPS = 4
EXPERTS_PER_GROUP = 4
N_EXPERTS = 16
EXPERT_FF = 1024
EPS = 1e-6

COL_K = ATTN_DIM
COL_V = ATTN_DIM + KV_DIM
COL_HY = ATTN_DIM + 2 * KV_DIM
COL_GM = COL_HY + 3 * HY_DIM

LANES = 128
VMEM_LIMIT = 56 * 1024 * 1024
HI = lax.Precision.HIGHEST


def _cparams(*sem, **kw):
    return pltpu.CompilerParams(dimension_semantics=sem, vmem_limit_bytes=VMEM_LIMIT, **kw)


def _rms(x):
    return x * lax.rsqrt(jnp.mean(x * x, axis=-1, keepdims=True) + EPS)


def _sigmoid(x):
    return 1.0 / (1.0 + jnp.exp(-x))


def _dot_3pass(a, b):
    a_hi = a.astype(BF16)
    a_lo = (a - a_hi.astype(F32)).astype(BF16)
    b_hi = b.astype(BF16)
    b_lo = (b - b_hi.astype(F32)).astype(BF16)
    return (jnp.dot(a_hi, b_hi, preferred_element_type=F32) + jnp.dot(a_hi, b_lo, preferred_element_type=F32)
            + jnp.dot(a_lo, b_hi, preferred_element_type=F32))


def _pack_bf16_pairs(x):
    n = x.shape[1] // 2
    lo = lax.bitcast_convert_type(x[:, :n].astype(BF16).astype(F32), jnp.uint32)
    hi = lax.bitcast_convert_type(x[:, n:].astype(BF16).astype(F32), jnp.uint32)
    return hi | (lo >> 16)


def _unpack_bf16_pairs(u):
    lo = lax.bitcast_convert_type(u << 16, F32)
    hi = lax.bitcast_convert_type(u & jnp.uint32(0xFFFF0000), F32)
    return jnp.concatenate([lo, hi], axis=1)


def _seg_of_row(row, t_prompt, dec_seq):
    return jnp.where(row < t_prompt, 0, 1 + (row - t_prompt) // dec_seq)


def _mod_kernel(c_ref, w_ref, b_ref, o_ref):
    c = c_ref[...]
    s = (c * _sigmoid(c)).astype(BF16)
    o_ref[...] = jnp.dot(s, w_ref[...].astype(BF16), preferred_element_type=F32) + b_ref[...]


def _mod_call(cond, w_mod, b_mod):
    depth, d, n = w_mod.shape
    r = cond.shape[0]
    tn = 512
    return pl.pallas_call(
        _mod_kernel,
        grid=(depth, n // tn),
        in_specs=[pl.BlockSpec((r, d), lambda l, j: (0, 0)),
                  pl.BlockSpec((None, d, tn), lambda l, j: (l, 0, j)),
                  pl.BlockSpec((None, 1, tn), lambda l, j: (l, 0, j))],
        out_specs=pl.BlockSpec((None, r, tn), lambda l, j: (l, 0, j)),
        out_shape=jax.ShapeDtypeStruct((depth, r, n), F32),
        compiler_params=_cparams("parallel", "parallel"),
        name="adaln_mod",
    )(cond, w_mod, b_mod.reshape(depth, 1, n))


def _norm_mod_kernel(x_ref, g_ref, sh_ref, sc_ref, o_ref):
    y = _rms(x_ref[...]) * g_ref[...]
    o_ref[...] = (y * (1.0 + sc_ref[...]) + sh_ref[...]).astype(o_ref.dtype)


def _norm_mod_call(x, g, mod, t_prompt, dec_seq, tm=512):
    t, d = x.shape
    seg = lambda i: _seg_of_row(i * tm, t_prompt, dec_seq)
    return pl.pallas_call(
        _norm_mod_kernel,
        grid=(t // tm,),
        in_specs=[pl.BlockSpec((tm, d), lambda i: (i, 0)),
                  pl.BlockSpec((1, d), lambda i: (0, 0)),
                  pl.BlockSpec((None, 1, d), lambda i: (seg(i), 0, 0)),
                  pl.BlockSpec((None, 1, d), lambda i: (seg(i), 0, 1))],
        out_specs=pl.BlockSpec((tm, d), lambda i: (i, 0)),
        out_shape=jax.ShapeDtypeStruct((t, d), BF16),
        compiler_params=_cparams("parallel"),
        name="norm1_modulate",
    )(x, g.reshape(1, d), mod, mod)


def _proj_kernel(a_ref, w_ref, o_ref):
    o_ref[...] = jnp.dot(a_ref[...], w_ref[...], preferred_element_type=F32).astype(o_ref.dtype)


def _proj_residual_kernel(a_ref, w_ref, x_ref, gate_ref, o_ref):
    o_ref[...] = x_ref[...] + gate_ref[...] * jnp.dot(a_ref[...], w_ref[...], preferred_element_type=F32)


def _proj_call(a, w, layer, name, out_dtype=BF16, residual=None, tm=1024, tn=1024):
    t, k = a.shape
    n = w.shape[2]
    in_specs = [pl.BlockSpec((tm, k), lambda i, j: (i, 0)),
                pl.BlockSpec((None, k, tn), lambda i, j: (layer, 0, j))]
    args = [a, w]
    body = _proj_kernel
    if residual is not None:
        x, mod, gate_block, t_prompt, dec_seq = residual
        seg = lambda i: _seg_of_row(i * tm, t_prompt, dec_seq)
        in_specs += [pl.BlockSpec((tm, tn), lambda i, j: (i, j)),
                     pl.BlockSpec((None, 1, tn), lambda i, j: (seg(i), 0, gate_block * (n // tn) + j))]
        args += [x, mod]
        body = _proj_residual_kernel
    return pl.pallas_call(
        body,
        grid=(t // tm, n // tn),
        in_specs=in_specs,
        out_specs=pl.BlockSpec((tm, tn), lambda i, j: (i, j)),
        out_shape=jax.ShapeDtypeStruct((t, n), out_dtype),
        compiler_params=_cparams("parallel", "parallel"),
        name=name,
    )(*args)


def _attn_prompt_kernel(q_ref, k_ref, v_ref, qg_ref, kg_ref, o_ref, kc_ref, vc_ref):
    kn = _rms(k_ref[...].astype(F32)) * kg_ref[...]
    vb = v_ref[...]
    kc_ref[...] = kn
    vc_ref[...] = vb.astype(F32)
    kb = kn.astype(BF16)
    scale = HEAD_DIM ** -0.5
    for m in range(Q_PER_KV):
        sl = slice(m * HEAD_DIM, (m + 1) * HEAD_DIM)
        qn = (_rms(q_ref[:, sl].astype(F32)) * qg_ref[...] * scale).astype(BF16)
        s = lax.dot_general(qn, kb, (((1,), (1,)), ((), ())), preferred_element_type=F32)
        p = jnp.exp(s - jnp.max(s, axis=-1, keepdims=True))
        l = jnp.sum(p, axis=-1, keepdims=True)
        o = jnp.dot(p.astype(BF16), vb, preferred_element_type=F32)
        o_ref[:, sl] = (o / l).astype(o_ref.dtype)


def _attn_prompt_call(z, qg, kg, batch, seq):
    gw = Q_PER_KV * HEAD_DIM
    tp = batch * seq
    return pl.pallas_call(
        _attn_prompt_kernel,
        grid=(batch, N_KV_HEADS),
        in_specs=[pl.BlockSpec((seq, gw), lambda b, g: (b, g)),
                  pl.BlockSpec((seq, HEAD_DIM), lambda b, g: (b, COL_K // HEAD_DIM + g)),
                  pl.BlockSpec((seq, HEAD_DIM), lambda b, g: (b, COL_V // HEAD_DIM + g)),
                  pl.BlockSpec((1, HEAD_DIM), lambda b, g: (0, 0)),
                  pl.BlockSpec((1, HEAD_DIM), lambda b, g: (0, 0))],
        out_specs=[pl.BlockSpec((seq, gw), lambda b, g: (b, g)),
                   pl.BlockSpec((seq, HEAD_DIM), lambda b, g: (b, g)),
                   pl.BlockSpec((seq, HEAD_DIM), lambda b, g: (b, g))],
        out_shape=[jax.ShapeDtypeStruct((tp, ATTN_DIM), BF16),
                   jax.ShapeDtypeStruct((tp, KV_DIM), F32),
                   jax.ShapeDtypeStruct((tp, KV_DIM), F32)],
        compiler_params=_cparams("parallel", "parallel"),
        name="attn_context",
    )(z, z, z, qg.reshape(1, HEAD_DIM), kg.reshape(1, HEAD_DIM))


def _rope(x, cos, sin_signed):
    lane = lax.broadcasted_iota(jnp.int32, x.shape, 1)
    quarter = HEAD_DIM // 4
    fwd = pltpu.roll(x, HEAD_DIM - quarter, 1)
    bwd = pltpu.roll(x, quarter, 1)
    swapped = jnp.where((lane % (2 * quarter)) < quarter, fwd, bwd)
    return x * cos + swapped * sin_signed


def _head_rope_kernel(x_ref, c_ref, s_ref, g_ref, o_ref, *, scale):
    for h in range(x_ref.shape[1] // HEAD_DIM):
        sl = slice(h * HEAD_DIM, (h + 1) * HEAD_DIM)
        xn = _rms(x_ref[:, sl].astype(F32)) * g_ref[...]
        o_ref[:, sl] = (_rope(xn, c_ref[...], s_ref[...]) * scale).astype(o_ref.dtype)


def _head_rope_call(z, col0, n_heads, cos, sin_signed, g, scale, t_prompt, dec_batch, dec_seq, tr=512):
    nr = dec_seq // tr
    width = n_heads * HEAD_DIM
    return pl.pallas_call(
        functools.partial(_head_rope_kernel, scale=scale),
        grid=(dec_batch, nr),
        in_specs=[pl.BlockSpec((tr, width), lambda b, r: (t_prompt // tr + b * nr + r, col0 // width)),
                  pl.BlockSpec((tr, HEAD_DIM), lambda b, r: (r, 0)),
                  pl.BlockSpec((tr, HEAD_DIM), lambda b, r: (r, 0)),
                  pl.BlockSpec((1, HEAD_DIM), lambda b, r: (0, 0))],
        out_specs=pl.BlockSpec((tr, width), lambda b, r: (b * nr + r, 0)),
        out_shape=jax.ShapeDtypeStruct((dec_batch * dec_seq, width), BF16),
        compiler_params=_cparams("parallel", "parallel"),
        name="attn_head_rope",
    )(z, cos, sin_signed, g.reshape(1, HEAD_DIM))


def _attn_sample_kernel(q_ref, k_ref, v_ref, ck_ref, cv_ref, o_ref, m_ref, acc_ref):
    j = pl.program_id(3)

    @pl.when(j == 0)
    def _():
        m_ref[...] = jnp.full(m_ref.shape, -jnp.inf, F32)
        acc_ref[...] = jnp.zeros(acc_ref.shape, F32)

    def step(kb, vb):
        reps = kb.shape[0] // LANES
        v1 = jnp.concatenate([vb, jnp.ones(vb.shape, BF16)], axis=1)
        scores = [lax.dot_general(q_ref[:, m * HEAD_DIM:(m + 1) * HEAD_DIM], kb, (((1,), (1,)), ((), ())),
                                  preferred_element_type=F32) for m in range(Q_PER_KV)]
        for m in range(Q_PER_KV):
            s = scores[m]
            m_prev = m_ref[m]
            m_next = jnp.maximum(m_prev, jnp.max(s, axis=-1, keepdims=True))
            alpha = jnp.exp(m_prev - m_next)
            p = jnp.exp(s - jnp.concatenate([m_next] * reps, axis=1))
            acc_ref[m] = (jnp.concatenate([alpha, alpha], axis=1) * acc_ref[m]
                          + jnp.dot(p.astype(BF16), v1, preferred_element_type=F32))
            m_ref[m] = m_next

    @pl.when(j == 0)
    def _():
        step(ck_ref[...].astype(BF16), cv_ref[...].astype(BF16))

    @pl.when(j > 0)
    def _():
        step(k_ref[...], v_ref[...])

    @pl.when(j == pl.num_programs(3) - 1)
    def _():
        for m in range(Q_PER_KV):
            o_ref[:, m * HEAD_DIM:(m + 1) * HEAD_DIM] = (
                acc_ref[m, :, :HEAD_DIM] / acc_ref[m, :, HEAD_DIM:]).astype(o_ref.dtype)


def _attn_sample_call(z, qr, kr, ctx_k, ctx_v, t_prompt, dec_batch, dec_seq, tq=512, tk=1024):
    gw = Q_PER_KV * HEAD_DIM
    past = ctx_k.shape[1]
    tk = min(tk, dec_seq)
    nq = dec_seq // tq
    nk = dec_seq // tk
    vrow = lambda b, j: (t_prompt + b * dec_seq) // tk + jnp.maximum(j - 1, 0)
    krow = lambda b, j: b * nk + jnp.maximum(j - 1, 0)
    return pl.pallas_call(
        _attn_sample_kernel,
        grid=(dec_batch, N_KV_HEADS, nq, nk + 1),
        in_specs=[pl.BlockSpec((tq, gw), lambda b, g, i, j: (b * nq + i, g)),
                  pl.BlockSpec((tk, HEAD_DIM), lambda b, g, i, j: (krow(b, j), g)),
                  pl.BlockSpec((tk, HEAD_DIM), lambda b, g, i, j: (vrow(b, j), COL_V // HEAD_DIM + g)),
                  pl.BlockSpec((None, past, HEAD_DIM), lambda b, g, i, j: (b, 0, g)),
                  pl.BlockSpec((None, past, HEAD_DIM), lambda b, g, i, j: (b, 0, g))],
        out_specs=pl.BlockSpec((tq, gw), lambda b, g, i, j: (b * nq + i, g)),
        out_shape=jax.ShapeDtypeStruct((dec_batch * dec_seq, ATTN_DIM), BF16),
        scratch_shapes=[pltpu.VMEM((Q_PER_KV, tq, LANES), F32),
                        pltpu.VMEM((Q_PER_KV, tq, 2 * HEAD_DIM), F32)],
        compiler_params=_cparams("parallel", "parallel", "parallel", "arbitrary"),
        name="attn_latent",
    )(qr, kr, z, ctx_k, ctx_v)


def _rope_tables(n_tokens):
    quarter = HEAD_DIM // 4
    n_rows = n_tokens // GRID_W
    row = jnp.repeat(jnp.arange(n_rows, dtype=F32), GRID_W)
    col = jnp.tile(jnp.arange(GRID_W, dtype=F32), n_rows)
    freqs = ROPE_THETA ** (-jnp.arange(quarter, dtype=F32) / quarter)
    ar = row[:, None] * freqs[None, :]
    ac = col[:, None] * freqs[None, :]
    cos = jnp.concatenate([jnp.cos(ar), jnp.cos(ar), jnp.cos(ac), jnp.cos(ac)], axis=-1)
    sin = jnp.concatenate([-jnp.sin(ar), jnp.sin(ar), -jnp.sin(ac), jnp.sin(ac)], axis=-1)
    return cos, sin


def _gelu_tanh(x):
    return 0.5 * x * (1.0 + jnp.tanh(math.sqrt(2.0 / math.pi) * (x + 0.044715 * (x * x * x))))


def _gmlp_kernel(z_ref, g_ref, b_ref, ws_ref, bs_ref, o_ref):
    hw = GM_DIM // GM_HEADS
    for ch in range(z_ref.shape[0] // CHUNK):
        rows = slice(ch * CHUNK, (ch + 1) * CHUNK)
        u = _gelu_tanh(z_ref[rows, :GM_DIM].astype(F32))
        v = _gelu_tanh(z_ref[rows, GM_DIM:].astype(F32))
        mu = jnp.mean(v, axis=-1, keepdims=True)
        vc = v - mu
        var = jnp.mean(vc * vc, axis=-1, keepdims=True)
        vn = (vc * lax.rsqrt(var + EPS) * g_ref[...] + b_ref[...]).astype(BF16)
        for h in range(GM_HEADS):
            cols = slice(h * hw, (h + 1) * hw)
            s = jnp.dot(ws_ref[h], vn[:, cols], preferred_element_type=F32) + bs_ref[h]
            o_ref[rows, cols] = (u[:, cols] * s).astype(o_ref.dtype)


def _gmlp_call(z, g, b, ws, bs, tr=512):
    t = z.shape[0]
    return pl.pallas_call(
        _gmlp_kernel,
        grid=(t // tr,),
        in_specs=[pl.BlockSpec((tr, 2 * GM_DIM), lambda i: (i, COL_GM // (2 * GM_DIM))),
                  pl.BlockSpec((1, GM_DIM), lambda i: (0, 0)),
                  pl.BlockSpec((1, GM_DIM), lambda i: (0, 0)),
                  pl.BlockSpec((GM_HEADS, CHUNK, CHUNK), lambda i: (0, 0, 0)),
                  pl.BlockSpec((GM_HEADS, CHUNK, 1), lambda i: (0, 0, 0))],
        out_specs=pl.BlockSpec((tr, GM_DIM), lambda i: (i, 0)),
        out_shape=jax.ShapeDtypeStruct((t, GM_DIM), BF16),
        compiler_params=_cparams("parallel"),
        name="chunk_gmlp",
    )(z, g.reshape(1, GM_DIM), b.reshape(1, GM_DIM), ws.astype(BF16), bs.reshape(GM_HEADS, CHUNK, 1))


def _sconv_kernel(z_ref, w_ref, b_ref, o_ref, *, seg_len):
    x = z_ref[...].astype(F32)
    n = x.shape[0]
    pos = lax.broadcasted_iota(jnp.int32, x.shape, 0) % seg_len
    prev = jnp.where(pos == 0, 0.0, pltpu.roll(x, 1, 0))
    nxt = jnp.where(pos == seg_len - 1, 0.0, pltpu.roll(x, n - 1, 0))
    o_ref[...] = (b_ref[...] + prev * w_ref[0:1, :] + x * w_ref[1:2, :] + nxt * w_ref[2:3, :]).astype(o_ref.dtype)


def _sconv_call(z, w, b, row0, n_rows, seg_len, block_rows, cw=256):
    width = 3 * HY_DIM
    return pl.pallas_call(
        functools.partial(_sconv_kernel, seg_len=seg_len),
        grid=(n_rows // block_rows, width // cw),
        in_specs=[pl.BlockSpec((block_rows, cw), lambda i, c: (row0 // block_rows + i, COL_HY // cw + c)),
                  pl.BlockSpec((3, cw), lambda i, c: (0, c)),
                  pl.BlockSpec((1, cw), lambda i, c: (0, c))],
        out_specs=pl.BlockSpec((block_rows, cw), lambda i, c: (i, c)),
        out_shape=jax.ShapeDtypeStruct((n_rows, width), BF16),
        compiler_params=_cparams("parallel", "parallel"),
        name="hyena_short_conv",
    )(z, w, b.reshape(1, width))


def _filter_kernel(z_ref, t_ref, w1_ref, b1_ref, w2_ref, b2_ref, w3a_ref, b3a_ref, w3b_ref, b3b_ref,
                   fr_ref, dl_ref, o_ref, *, tr):
    fr = fr_ref[...]
    h = jnp.sin(fr * (jnp.dot(z_ref[...], w1_ref[...], precision=HI, preferred_element_type=F32) + b1_ref[...]))
    h = jnp.sin(fr * (jnp.dot(h, w2_ref[...], precision=HI, preferred_element_type=F32) + b2_ref[...]))
    decay = jnp.exp(-t_ref[...] * dl_ref[...]) + MOD_SHIFT
    row = lax.broadcasted_iota(jnp.int32, decay.shape, 0) + pl.program_id(0) * tr
    decay = jnp.where(row == 0, 0.0, decay)
    for o, (w3_ref, b3_ref) in enumerate(((w3a_ref, b3a_ref), (w3b_ref, b3b_ref))):
        taps = _dot_3pass(h, w3_ref[...]) + b3_ref[...]
        o_ref[:, o * HY_DIM:(o + 1) * HY_DIM] = taps * decay


def _filter_call(length, w1, b1, w2, b2, w3, b3, fr, tr=256):
    bands = (FILTER_EMB - 1) // 2
    t = np.linspace(0.0, 1.0, length)[:, None]
    wv = 2.0 * np.pi * np.arange(length)[:, None] / length
    f = np.linspace(1e-4, bands - 1, bands)[None, :]
    zf = np.concatenate([t, np.cos(f * wv), -np.sin(f * wv)], axis=-1)
    min_decay = math.log(DECAY_TARGET) / SLOW_DECAY_PCT
    max_decay = math.log(DECAY_TARGET) / FAST_DECAY_PCT
    deltas = np.abs(np.linspace(min_decay, max_decay, HY_DIM))[None, :]
    lag = np.minimum(np.abs(np.arange(2 * length) - length), length - 1)
    pad = LANES - FILTER_EMB
    hp = LANES - FILTER_HIDDEN
    z2 = jnp.asarray(np.pad(zf[lag], ((0, 0), (0, pad))).astype(np.float32))
    t2 = jnp.asarray(t[lag].astype(np.float32))
    dl = jnp.asarray(deltas.astype(np.float32))
    w1p = jnp.pad(w1, ((0, pad), (0, hp)))
    w2p = jnp.pad(w2, ((0, hp), (0, hp)))
    w3p = jnp.pad(w3, ((0, hp), (0, 0)))
    b1p = jnp.pad(b1, (0, hp)).reshape(1, LANES)
    b2p = jnp.pad(b2, (0, hp)).reshape(1, LANES)
    frp = jnp.pad(fr, (0, hp)).reshape(1, LANES)
    b3r = b3.reshape(1, -1)
    nblk = 2 * length // tr
    half = length // tr
    wcol = lambda r, o: o * 2 + jnp.where(r < half, 1, 0)
    small = pl.BlockSpec((1, LANES), lambda r: (0, 0))
    square = pl.BlockSpec((LANES, LANES), lambda r: (0, 0))
    return pl.pallas_call(
        functools.partial(_filter_kernel, tr=tr),
        grid=(nblk,),
        in_specs=[pl.BlockSpec((tr, LANES), lambda r: (r, 0)),
                  pl.BlockSpec((tr, 1), lambda r: (r, 0)),
                  square, small, square, small,
                  pl.BlockSpec((LANES, HY_DIM), lambda r: (0, wcol(r, 0))),
                  pl.BlockSpec((1, HY_DIM), lambda r: (0, wcol(r, 0))),
                  pl.BlockSpec((LANES, HY_DIM), lambda r: (0, wcol(r, 1))),
                  pl.BlockSpec((1, HY_DIM), lambda r: (0, wcol(r, 1))),
                  small,
                  pl.BlockSpec((1, HY_DIM), lambda r: (0, 0))],
        out_specs=pl.BlockSpec((tr, 2 * HY_DIM), lambda r: (r, 0)),
        out_shape=jax.ShapeDtypeStruct((2 * length, 2 * HY_DIM), F32),
        compiler_params=_cparams("parallel"),
        name="hyena_filter_mlp",
    )(z2, t2, w1p, b1p, w2p, b2p, w3p, b3r, w3p, b3r, frp, dl)


def _dft_constants(p):
    n = 2 * p
    idx = np.arange(p, dtype=np.float64)
    ang = 2.0 * np.pi * np.outer(idx, idx) / n
    re = np.cos(ang)
    im = -np.sin(ang)
    im[0, :] = np.cos(np.pi * idx)
    fwd = np.concatenate([re, im], axis=0)
    sign = np.where(np.arange(p) % 2 == 0, 1.0, -1.0)
    ar = (2.0 / n) * np.cos(ang)
    ar[:, 0] = 1.0 / n
    ai = -(2.0 / n) * np.sin(ang)
    ai[:, 0] = sign / n
    inv = np.concatenate([ar, ai], axis=1)
    as_bf16 = lambda a: jnp.asarray(a.astype(np.float32)).astype(BF16)
    return as_bf16(fwd), as_bf16(inv)


def _dft_kernel(f_ref, x_ref, o_ref):
    o_ref[...] = jnp.dot(f_ref[...], x_ref[...].astype(BF16), preferred_element_type=F32).astype(o_ref.dtype)


def _dft_call(x, fwd, p, col0, cw=256):
    nblk = x.shape[0] // p
    return pl.pallas_call(
        _dft_kernel,
        grid=(nblk, HY_DIM // cw),
        in_specs=[pl.BlockSpec((2 * p, p), lambda i, c: (0, 0)),
                  pl.BlockSpec((p, cw), lambda i, c: (i, col0 // cw + c))],
        out_specs=pl.BlockSpec((None, 2 * p, cw), lambda i, c: (i, 0, c)),
        out_shape=jax.ShapeDtypeStruct((nblk, 2 * p, HY_DIM), BF16),
        compiler_params=_cparams("parallel", "parallel"),
        name="hyena_block_dft",
    )(fwd, x)


def _filter_dft_kernel(f_ref, blk_ref, o_ref, tail_ref, *, p):
    b = pl.program_id(2)
    blk = blk_ref[...].astype(BF16)
    spec = jnp.dot(f_ref[...], blk, preferred_element_type=F32)

    @pl.when(b > 0)
    def _():
        o_ref[...] = (spec + tail_ref[...]).astype(o_ref.dtype)

    row = lax.broadcasted_iota(jnp.int32, spec.shape, 0)
    first = blk[0:1, :].astype(F32)
    real_row = jnp.logical_or(row < p, row == p)
    sign = (1 - 2 * (row % 2)).astype(F32)
    tail_ref[...] = sign * (spec - jnp.where(real_row, first, 0.0))


def _filter_dft_call(taps, fwd, p, cw=256):
    nb2 = taps.shape[0] // p
    nd = nb2 - 1
    ncb = HY_DIM // cw
    return pl.pallas_call(
        functools.partial(_filter_dft_kernel, p=p),
        grid=(2, ncb, nb2),
        in_specs=[pl.BlockSpec((2 * p, p), lambda o, c, b: (0, 0)),
                  pl.BlockSpec((p, cw), lambda o, c, b: (b, o * ncb + c))],
        out_specs=pl.BlockSpec((None, None, 2 * p, cw), lambda o, c, b: (o, jnp.maximum(b - 1, 0), 0, c)),
        out_shape=jax.ShapeDtypeStruct((2, nd, 2 * p, HY_DIM), BF16),
        scratch_shapes=[pltpu.VMEM((2 * p, cw), F32)],
        compiler_params=_cparams("parallel", "parallel", "arbitrary"),
        name="hyena_filter_dft",
    )(fwd, taps)


def _specconv_kernel(xs_ref, gs_ref, inv_ref, v_ref, gate_ref, bias_ref, o_ref, y_ref, *, nb, p, rc):
    i = pl.program_id(2)
    cw = o_ref.shape[1]

    for r in range(0, p, rc):
        def body(j, carry):
            yre, yim = carry
            d = i - j + nb - 1
            xre = xs_ref[j, r:r + rc, :]
            xim = xs_ref[j, p + r:p + r + rc, :]
            gre = gs_ref[d, r:r + rc, :]
            gim = gs_ref[d, p + r:p + r + rc, :]
            return (yre + (xre * gre - xim * gim).astype(F32), yim + (xre * gim + xim * gre).astype(F32))

        zero = jnp.zeros((rc, cw), F32)
        yre, yim = lax.fori_loop(0, nb, body, (zero, zero), unroll=True)
        y_ref[r:r + rc, :] = yre
        y_ref[p + r:p + r + rc, :] = yim

    def edge(j, carry):
        y0, yn = carry
        d = i - j + nb - 1
        x0 = xs_ref[j, 0:16, :].astype(F32)
        xn = xs_ref[j, p:p + 16, :].astype(F32)
        g0 = gs_ref[d, 0:16, :].astype(F32)
        gn = gs_ref[d, p:p + 16, :].astype(F32)
        return y0 + x0 * g0, yn + xn * gn

    zero16 = jnp.zeros((16, cw), F32)
    y0, yn = lax.fori_loop(0, nb, edge, (zero16, zero16))
    y_ref[0:1, :] = y0[0:1, :]
    y_ref[p:p + 1, :] = yn[0:1, :]

    y = jnp.dot(inv_ref[...], y_ref[...].astype(BF16), preferred_element_type=F32)
    v = v_ref[...].astype(F32)
    o_ref[...] = (gate_ref[...].astype(F32) * (y + bias_ref[...] * v)).astype(o_ref.dtype)


def _specconv_call(xs, gs, inv, zc, bias, nseq, nb, p, v_col0, gate_col0, v_src=None, cw=256, rc=32,
                   out_dtype=F32):
    n_rows = nseq * nb * p
    xs4 = xs.reshape(nseq, nb, 2 * p, HY_DIM)
    if v_src is None:
        v_arr, v_spec = zc, pl.BlockSpec((p, cw), lambda c, s, i: (s * nb + i, v_col0 // cw + c))
    else:
        v_arr, v_spec = v_src, pl.BlockSpec((p, cw), lambda c, s, i: (s * nb + i, c))
    return pl.pallas_call(
        functools.partial(_specconv_kernel, nb=nb, p=p, rc=rc),
        grid=(HY_DIM // cw, nseq, nb),
        in_specs=[pl.BlockSpec((None, nb, 2 * p, cw), lambda c, s, i: (s, 0, 0, c)),
                  pl.BlockSpec((2 * nb - 1, 2 * p, cw), lambda c, s, i: (0, 0, c)),
                  pl.BlockSpec((p, 2 * p), lambda c, s, i: (0, 0)),
                  v_spec,
                  pl.BlockSpec((p, cw), lambda c, s, i: (s * nb + i, gate_col0 // cw + c)),
                  pl.BlockSpec((1, cw), lambda c, s, i: (0, c))],
        out_specs=pl.BlockSpec((p, cw), lambda c, s, i: (s * nb + i, c)),
        out_shape=jax.ShapeDtypeStruct((n_rows, HY_DIM), out_dtype),
        scratch_shapes=[pltpu.VMEM((2 * p, cw), F32)],
        compiler_params=_cparams("parallel", "parallel", "parallel"),
        name="hyena_spectral_conv",
    )(xs4, gs, inv, v_arr, zc, bias.reshape(1, HY_DIM))


def _hyena_group(z, row0, nseq, length, p, prm, sconv_rows):
    nb = length // p
    zc = _sconv_call(z, prm['hy_conv_w'], prm['hy_conv_b'], row0, nseq * length, length, sconv_rows)
    taps = _filter_call(length, prm['hf_w1'], prm['hf_b1'], prm['hf_w2'], prm['hf_b2'],
                        prm['hf_w3'], prm['hf_b3'], prm['hf_freq'], tr=min(512, length))
    fwd, inv = _dft_constants(p)
    gs = _filter_dft_call(taps, fwd, p, cw=512)
    cw = 512 if nb == 1 else 256
    vs = _dft_call(zc, fwd, p, 2 * HY_DIM, cw=512)
    u = _specconv_call(vs, gs[0], inv, zc, prm['hy_bias'][0], nseq, nb, p, 2 * HY_DIM, 0, cw=cw)
    us = _dft_call(u, fwd, p, 0, cw=512)
    return _specconv_call(us, gs[1], inv, zc, prm['hy_bias'][1], nseq, nb, p, 0, HY_DIM, v_src=u, cw=cw,
                          out_dtype=BF16)


def _mix_norm_kernel(ap_ref, as_ref, hp_ref, hs_ref, m_ref, g_ref, o_ref, *, ntp):
    i = pl.program_id(0)

    def fill(a_ref, h_ref):
        o_ref[:, :ATTN_DIM] = (_rms(a_ref[...].astype(F32)) * g_ref[:, :ATTN_DIM]).astype(BF16)
        o_ref[:, ATTN_DIM:ATTN_DIM + HY_DIM] = (
            _rms(h_ref[...].astype(F32)) * g_ref[:, ATTN_DIM:ATTN_DIM + HY_DIM]).astype(BF16)
        o_ref[:, ATTN_DIM + HY_DIM:] = (
            _rms(m_ref[...].astype(F32)) * g_ref[:, ATTN_DIM + HY_DIM:]).astype(BF16)

    @pl.when(i < ntp)
    def _():
        fill(ap_ref, hp_ref)

    @pl.when(i >= ntp)
    def _():
        fill(as_ref, hs_ref)


def _mix_norm_call(attn_p, attn_s, hy_p, hy_s, gm, g, t_prompt, tm=512):
    t = gm.shape[0]
    mix_dim = ATTN_DIM + HY_DIM + GM_DIM
    ntp = t_prompt // tm
    nts = t // tm - ntp
    prow = lambda i: (jnp.minimum(i, ntp - 1), 0)
    srow = lambda i: (jnp.clip(i - ntp, 0, nts - 1), 0)
    return pl.pallas_call(
        functools.partial(_mix_norm_kernel, ntp=ntp),
        grid=(t // tm,),
        in_specs=[pl.BlockSpec((tm, ATTN_DIM), prow),
                  pl.BlockSpec((tm, ATTN_DIM), srow),
                  pl.BlockSpec((tm, HY_DIM), prow),
                  pl.BlockSpec((tm, HY_DIM), srow),
                  pl.BlockSpec((tm, GM_DIM), lambda i: (i, 0)),
                  pl.BlockSpec((1, mix_dim), lambda i: (0, 0))],
        out_specs=pl.BlockSpec((tm, mix_dim), lambda i: (i, 0)),
        out_shape=jax.ShapeDtypeStruct((t, mix_dim), BF16),
        compiler_params=_cparams("parallel"),
        name="mix_group_norm",
    )(attn_p, attn_s, hy_p, hy_s, gm, g.reshape(1, mix_dim))


def _router_kernel(x_ref, g_ref, sh_ref, sc_ref, wr_ref, br_ref, h_ref, r_ref):
    y = _rms(x_ref[...]) * g_ref[...]
    h = y * (1.0 + sc_ref[...]) + sh_ref[...]
    h_ref[...] = _pack_bf16_pairs(h)
    logits = _dot_3pass(h, wr_ref[...]) + br_ref[...]
    col = lambda k: logits[:, k:k + 1]

    lg = [col(k) for k in range(N_GROUPS)]
    g_max = functools.reduce(jnp.maximum, lg)
    g_den = sum(jnp.exp(v - g_max) for v in lg)
    g_top = 1.0 / g_den
    g_idx = jnp.full(g_max.shape, N_GROUPS - 1, jnp.int32)
    for k in range(N_GROUPS - 2, -1, -1):
        g_idx = jnp.where(lg[k] == g_max, k, g_idx)

    le = []
    for e in range(EXPERTS_PER_GROUP):
        v = col(N_GROUPS + (N_GROUPS - 1) * EXPERTS_PER_GROUP + e)
        for k in range(N_GROUPS - 2, -1, -1):
            v = jnp.where(g_idx == k, col(N_GROUPS + k * EXPERTS_PER_GROUP + e), v)
        le.append(v)

    e_max = functools.reduce(jnp.maximum, le)
    e1 = jnp.full(e_max.shape, EXPERTS_PER_GROUP - 1, jnp.int32)
    for e in range(EXPERTS_PER_GROUP - 2, -1, -1):
        e1 = jnp.where(le[e] == e_max, e, e1)
    neg = jnp.float32(-jnp.inf)
    rest = [jnp.where(e1 == e, neg, le[e]) for e in range(EXPERTS_PER_GROUP)]
    e2_max = functools.reduce(jnp.maximum, rest)
    e2 = jnp.full(e_max.shape, EXPERTS_PER_GROUP - 1, jnp.int32)
    for e in range(EXPERTS_PER_GROUP - 2, -1, -1):
        e2 = jnp.where(rest[e] == e2_max, e, e2)
    ratio = jnp.exp(e2_max - e_max)
    w1 = g_top / (1.0 + ratio)
    w2 = g_top * ratio / (1.0 + ratio)
    id1 = (g_idx * EXPERTS_PER_GROUP + e1).astype(F32)
    id2 = (g_idx * EXPERTS_PER_GROUP + e2).astype(F32)

    lane = lax.broadcasted_iota(jnp.int32, r_ref.shape, 1)
    r_ref[...] = jnp.where(lane == 0, id1, jnp.where(lane == 1, id2, jnp.where(lane == 2, w1, jnp.where(lane == 3, w2, 0.0))))


def _router_call(x, g, mod, wr, br, t_prompt, dec_seq, tm=512):
    t, d = x.shape
    seg = lambda i: _seg_of_row(i * tm, t_prompt, dec_seq)
    return pl.pallas_call(
        _router_kernel,
        grid=(t // tm,),
        in_specs=[pl.BlockSpec((tm, d), lambda i: (i, 0)),
                  pl.BlockSpec((1, d), lambda i: (0, 0)),
                  pl.BlockSpec((None, 1, d), lambda i: (seg(i), 0, 3)),
                  pl.BlockSpec((None, 1, d), lambda i: (seg(i), 0, 4)),
                  pl.BlockSpec((d, LANES), lambda i: (0, 0)),
                  pl.BlockSpec((1, LANES), lambda i: (0, 0))],
        out_specs=[pl.BlockSpec((tm, d // 2), lambda i: (i, 0)),
                   pl.BlockSpec((tm, LANES), lambda i: (i, 0))],
        out_shape=[jax.ShapeDtypeStruct((t, d // 2), jnp.uint32),
                   jax.ShapeDtypeStruct((t, LANES), F32)],
        compiler_params=_cparams("parallel"),
        name="norm2_router",
    )(x, g.reshape(1, d), mod, mod, wr, br)


def _dispatch_kernel(dest_ref, zt_ref, h_ref, xs_ref, zero_ref, sem, zsem, *, tm, t, expert_tile):
    i = pl.program_id(0)

    @pl.when(i == 0)
    def _():
        zero_ref[...] = jnp.zeros(zero_ref.shape, zero_ref.dtype)

        def clear(q):
            row = pl.multiple_of(jnp.maximum(zt_ref[q], 0), expert_tile)
            return pltpu.make_async_copy(zero_ref, xs_ref.at[pl.ds(row, expert_tile), :], zsem)

        for q in range(2 * N_EXPERTS):
            @pl.when(zt_ref[q] >= 0)
            def _():
                clear(q).start()

        for q in range(2 * N_EXPERTS):
            @pl.when(zt_ref[q] >= 0)
            def _():
                clear(q).wait()

    base = i * tm

    def body(r, c):
        for k in range(2):
            row = dest_ref[k * t + base + r]
            pltpu.make_async_copy(h_ref.at[pl.ds(r, 1), :], xs_ref.at[pl.ds(row, 1), :], sem).start()
        return c

    lax.fori_loop(0, tm, body, 0, unroll=8)
    for k in range(2):
        pltpu.make_async_copy(h_ref, xs_ref.at[pl.ds(0, tm), :], sem).wait()


def _dispatch_call(dest, zero_tiles, h, n_rows, expert_tile, tm=256):
    t, d = h.shape
    return pl.pallas_call(
        functools.partial(_dispatch_kernel, tm=tm, t=t, expert_tile=expert_tile),
        grid_spec=pltpu.PrefetchScalarGridSpec(
            num_scalar_prefetch=2,
            grid=(t // tm,),
            in_specs=[pl.BlockSpec((tm, d), lambda i, dest, zt: (i, 0))],
            out_specs=pl.BlockSpec(memory_space=pl.ANY),
            scratch_shapes=[pltpu.VMEM((expert_tile, d), h.dtype),
                            pltpu.SemaphoreType.DMA(()),
                            pltpu.SemaphoreType.DMA(())]),
        out_shape=jax.ShapeDtypeStruct((n_rows, d), h.dtype),
        compiler_params=_cparams("arbitrary", disable_bounds_checks=True),
        name="moe_dispatch_rows",
    )(dest, zero_tiles, h)


def _stream_expert_weights(plan, sources, wbuf, targets, sem):
    te_ref, first_ref, run_ref, nxt_ref = plan
    m = pl.program_id(1)

    def copies(e, slot):
        return [pltpu.make_async_copy(src, wbuf.at[slot, k], sem.at[slot]) for k, src in enumerate(sources(e))]

    @pl.when(m == 0)
    def _():
        for cp in copies(te_ref[0], 0):
            cp.start()

    @pl.when(first_ref[m] == 1)
    def _():
        slot = run_ref[m] % 2
        for cp in copies(te_ref[m], slot):
            cp.wait()

        @pl.when(nxt_ref[m] >= 0)
        def _():
            for cp in copies(nxt_ref[m], 1 - slot):
                cp.start()

        for k, tgt in enumerate(targets):
            tgt[...] = wbuf[slot, k].astype(BF16)


def _expert_up_kernel(te_ref, nv_ref, first_ref, run_ref, nxt_ref, x_ref, wg_ref, wu_ref, o_ref,
                      wbuf, wgb_ref, wub_ref, sem, *, tm, tf, layer):
    valid = pl.program_id(1) * tm < nv_ref[0]
    cols = pl.ds(pl.multiple_of(pl.program_id(0) * tf, tf), tf)
    _stream_expert_weights((te_ref, first_ref, run_ref, nxt_ref),
                           lambda e: (wg_ref.at[layer, e, :, cols], wu_ref.at[layer, e, :, cols]),
                           wbuf, (wgb_ref, wub_ref), sem)

    @pl.when(valid)
    def _():
        x = _unpack_bf16_pairs(x_ref[...]).astype(BF16)
        hg = jnp.dot(x, wgb_ref[...], preferred_element_type=F32)
        hu = jnp.dot(x, wub_ref[...], preferred_element_type=F32)
        o_ref[...] = (hg * _sigmoid(hg) * hu).astype(o_ref.dtype)

    @pl.when(jnp.logical_not(valid))
    def _():
        o_ref[...] = jnp.zeros(o_ref.shape, o_ref.dtype)


def _expert_up_call(plan, xs, wg, wu, layer, tm, nf=2):
    n, dh = xs.shape
    d, ff = wg.shape[2], wg.shape[3]
    tf = ff // nf
    return pl.pallas_call(
        functools.partial(_expert_up_kernel, tm=tm, tf=tf, layer=layer),
        grid_spec=pltpu.PrefetchScalarGridSpec(
            num_scalar_prefetch=5,
            grid=(nf, n // tm),
            in_specs=[pl.BlockSpec((tm, dh), lambda f, m, *_: (m, 0)),
                      pl.BlockSpec(memory_space=pl.ANY),
                      pl.BlockSpec(memory_space=pl.ANY)],
            out_specs=pl.BlockSpec((tm, tf), lambda f, m, *_: (m, f)),
            scratch_shapes=[pltpu.VMEM((2, 2, d, tf), F32),
                            pltpu.VMEM((d, tf), BF16), pltpu.VMEM((d, tf), BF16),
                            pltpu.SemaphoreType.DMA((2,))]),
        out_shape=jax.ShapeDtypeStruct((n, ff), BF16),
        compiler_params=_cparams("arbitrary", "arbitrary"),
        name="moe_expert_up",
    )(*plan, xs, wg, wu)


def _expert_down_kernel(te_ref, nv_ref, first_ref, run_ref, nxt_ref, a_ref, wd_ref, o_ref,
                        wbuf, wdb_ref, sem, *, tm, layer):
    valid = pl.program_id(1) * tm < nv_ref[0]
    _stream_expert_weights((te_ref, first_ref, run_ref, nxt_ref), lambda e: (wd_ref.at[layer, e],),
                           wbuf, (wdb_ref,), sem)

    @pl.when(valid)
    def _():
        o_ref[...] = _pack_bf16_pairs(jnp.dot(a_ref[...], wdb_ref[...], preferred_element_type=F32))

    @pl.when(jnp.logical_not(valid))
    def _():
        o_ref[...] = jnp.zeros(o_ref.shape, o_ref.dtype)


def _expert_down_call(plan, a, wd, layer, tm):
    n, ff = a.shape
    d = wd.shape[3]
    return pl.pallas_call(
        functools.partial(_expert_down_kernel, tm=tm, layer=layer),
        grid_spec=pltpu.PrefetchScalarGridSpec(
            num_scalar_prefetch=5,
            grid=(1, n // tm),
            in_specs=[pl.BlockSpec((tm, ff), lambda c, m, *_: (m, 0)),
                      pl.BlockSpec(memory_space=pl.ANY)],
            out_specs=pl.BlockSpec((tm, d // 2), lambda c, m, *_: (m, 0)),
            scratch_shapes=[pltpu.VMEM((2, 1, ff, d), F32),
                            pltpu.VMEM((ff, d), BF16),
                            pltpu.SemaphoreType.DMA((2,))]),
        out_shape=jax.ShapeDtypeStruct((n, d // 2), jnp.uint32),
        compiler_params=_cparams("arbitrary", "arbitrary"),
        name="moe_expert_down",
    )(*plan, a, wd)


def _combine_kernel(dest_ref, x_ref, r_ref, gate_ref, fg_ref, ys_ref, *rest, tc, t, nt, ntp, final):
    out_refs, (ybuf, sem) = rest[:-2], rest[-2:]
    i = pl.program_id(0)

    def fetch(tile, slot):
        base = tile * tc

        def body(r, c):
            for k in range(2):
                row = dest_ref[k * t + base + r]
                pltpu.make_async_copy(ys_ref.at[pl.ds(row, 1), :], ybuf.at[slot, pl.ds(k * tc + r, 1), :],
                                      sem.at[slot]).start()
            return c

        lax.fori_loop(0, tc, body, 0, unroll=8)

    @pl.when(i == 0)
    def _():
        fetch(0, 0)

    @pl.when(i + 1 < nt)
    def _():
        fetch(i + 1, (i + 1) % 2)

    slot = i % 2
    pltpu.make_async_copy(ys_ref.at[pl.ds(0, 2 * tc), :], ybuf.at[slot], sem.at[slot]).wait()

    w1 = r_ref[:, 2:3]
    w2 = r_ref[:, 3:4]
    y = w1 * _unpack_bf16_pairs(ybuf[slot, 0:tc, :]) + w2 * _unpack_bf16_pairs(ybuf[slot, tc:2 * tc, :])
    x = x_ref[...] + gate_ref[...] * y
    if final:
        x = _rms(x) * fg_ref[...]
        op_ref, os_ref = out_refs

        @pl.when(i < ntp)
        def _():
            op_ref[...] = x

        @pl.when(i >= ntp)
        def _():
            os_ref[...] = x
    else:
        out_refs[0][...] = x


def _combine_call(dest, x, route, ys, mod, final_g, final, t_prompt, dec_seq, tc=128):
    t, d = x.shape
    seg = lambda i, *_: _seg_of_row(i * tc, t_prompt, dec_seq)
    nt = t // tc
    ntp = t_prompt // tc
    if final:
        out_specs = [pl.BlockSpec((tc, d), lambda i, dest: (jnp.minimum(i, ntp - 1), 0)),
                     pl.BlockSpec((tc, d), lambda i, dest: (jnp.maximum(i - ntp, 0), 0))]
        out_shape = [jax.ShapeDtypeStruct((t_prompt, d), F32), jax.ShapeDtypeStruct((t - t_prompt, d), F32)]
    else:
        out_specs = pl.BlockSpec((tc, d), lambda i, dest: (i, 0))
        out_shape = jax.ShapeDtypeStruct((t, d), F32)
    return pl.pallas_call(
        functools.partial(_combine_kernel, tc=tc, t=t, nt=nt, ntp=ntp, final=final),
        grid_spec=pltpu.PrefetchScalarGridSpec(
            num_scalar_prefetch=1,
            grid=(nt,),
            in_specs=[pl.BlockSpec((tc, d), lambda i, dest: (i, 0)),
                      pl.BlockSpec((tc, LANES), lambda i, dest: (i, 0)),
                      pl.BlockSpec((None, 1, d), lambda i, dest: (seg(i), 0, 5)),
                      pl.BlockSpec((1, d), lambda i, dest: (0, 0)),
                      pl.BlockSpec(memory_space=pl.ANY)],
            out_specs=out_specs,
            scratch_shapes=[pltpu.VMEM((2, 2 * tc, ys.shape[1]), ys.dtype),
                            pltpu.SemaphoreType.DMA((2,))]),
        out_shape=out_shape,
        compiler_params=_cparams("arbitrary", disable_bounds_checks=True),
        name="moe_combine_residual",
    )(dest, x, route, mod, final_g.reshape(1, d), ys)


def _dispatch_plan(route, tm):
    t = route.shape[0]
    flat_e = route[:, 0:2].astype(jnp.int32).T.reshape(-1)
    n_pairs = 2 * t
    n_tiles = n_pairs // tm + N_EXPERTS
    onehot = (flat_e[:, None] == jnp.arange(N_EXPERTS, dtype=jnp.int32)[None, :]).astype(jnp.int32)
    running = jnp.cumsum(onehot, axis=0)
    counts = running[-1]
    rank = jnp.sum((running - 1) * onehot, axis=1)
    padded = ((counts + tm - 1) // tm) * tm
    pad_end = jnp.cumsum(padded)
    pad_start = pad_end - padded
    dest = (jnp.sum(onehot * pad_start[None, :], axis=1) + rank).astype(jnp.int32)
    tile_start = jnp.arange(n_tiles, dtype=jnp.int32) * tm
    tile_e = jnp.sum((pad_end[None, :] <= tile_start[:, None]).astype(jnp.int32), axis=1)
    tile_e = jnp.minimum(tile_e, N_EXPERTS - 1).astype(jnp.int32)
    n_valid = pad_end[-1:].astype(jnp.int32)
    last = jnp.where(counts > 0, pad_end - tm, -1)
    tail = pad_end[-1] + jnp.arange(N_EXPERTS, dtype=jnp.int32) * tm
    tail = jnp.where(tail < n_tiles * tm, tail, -1)
    zero_tiles = jnp.concatenate([last, tail]).astype(jnp.int32)
    used = tile_start < pad_end[-1]
    first = jnp.logical_and(used, jnp.concatenate([jnp.ones((1,), bool), tile_e[1:] != tile_e[:-1]]))
    run = jnp.maximum(jnp.cumsum(first.astype(jnp.int32)) - 1, 0)
    ids = jnp.arange(N_EXPERTS, dtype=jnp.int32)
    later = jnp.logical_and(ids[None, :] > ids[:, None], (counts > 0)[None, :])
    next_e = jnp.min(jnp.where(later, ids[None, :], N_EXPERTS), axis=1)
    next_e = jnp.where(next_e == N_EXPERTS, -1, next_e)
    tile_plan = (tile_e, n_valid, first.astype(jnp.int32), run.astype(jnp.int32), next_e[tile_e].astype(jnp.int32))
    return dest, tile_plan, zero_tiles, n_tiles * tm


def _moe(h, route, wg, wu, wd, layer, tm=256):
    dest, tile_plan, zero_tiles, n_rows = _dispatch_plan(route, tm)
    xs = _dispatch_call(dest, zero_tiles, h, n_rows, tm)
    a = _expert_up_call(tile_plan, xs, wg, wu, layer, tm)
    ys = _expert_down_call(tile_plan, a, wd, layer, tm)
    return ys, dest


def _trunk(x_prompt, x_sample, cache_k, cache_v, c, c_ctx, prm, final_norm_g, hy_block_sample=512):
    batch, seq, d = x_prompt.shape
    dec_batch, dec_seq, _ = x_sample.shape
    depth = prm['w_in'].shape[0]
    past = cache_k.shape[2]
    t_prompt = batch * seq
    t_sample = dec_batch * dec_seq

    x = jnp.concatenate([x_prompt.reshape(t_prompt, d), x_sample.reshape(t_sample, d)], axis=0)
    n_cond = 1 + dec_batch
    cond = jnp.concatenate([c_ctx[None, :], c, jnp.zeros((8 - n_cond % 8, d), F32)], axis=0)
    mod_all = _mod_call(cond, prm['w_mod'], prm['b_mod'])
    cos, sin_signed = _rope_tables(dec_seq)
    sconv_rows_p = math.gcd(t_prompt, max(seq, 4096 // seq * seq))

    new_k, new_v = [], []
    big = ('w_mod', 'w_in', 'w_out', 'exp_w_gate', 'exp_w_up', 'exp_w_down')
    w_in, w_out = prm['w_in'].astype(BF16), prm['w_out'].astype(BF16)
    w_gate, w_up, w_down = prm['exp_w_gate'], prm['exp_w_up'], prm['exp_w_down']

    for l in range(depth):
        p = {name: w[l] for name, w in prm.items() if name not in big}
        mod = mod_all[l, :n_cond].reshape(n_cond, 1, 6 * d)

        h1 = _norm_mod_call(x, p['norm1_g'], mod, t_prompt, dec_seq)
        z = _proj_call(h1, w_in, l, "in_proj")

        attn_p, k_l, v_l = _attn_prompt_call(z, p['q_norm_g'], p['k_norm_g'], batch, seq)
        qr = _head_rope_call(z, 0, N_HEADS, cos, sin_signed, p['q_norm_g'], HEAD_DIM ** -0.5,
                             t_prompt, dec_batch, dec_seq)
        kr = _head_rope_call(z, COL_K, N_KV_HEADS, cos, sin_signed, p['k_norm_g'], 1.0,
                             t_prompt, dec_batch, dec_seq)
        attn_s = _attn_sample_call(z, qr, kr, cache_k[:, l].reshape(dec_batch, past, KV_DIM),
                                   cache_v[:, l].reshape(dec_batch, past, KV_DIM), t_prompt, dec_batch, dec_seq)
        new_k.append(k_l.reshape(batch, seq, N_KV_HEADS, HEAD_DIM))
        new_v.append(v_l.reshape(batch, seq, N_KV_HEADS, HEAD_DIM))

        hy_p = _hyena_group(z, 0, batch, seq, seq, p, sconv_rows_p)
        hy_s = _hyena_group(z, t_prompt, dec_batch, dec_seq, min(hy_block_sample, dec_seq), p, dec_seq)

        gm = _gmlp_call(z, p['gm_norm_g'], p['gm_norm_b'], p['gm_ws'], p['gm_bs'])

        mix = _mix_norm_call(attn_p, attn_s, hy_p, hy_s, gm, p['out_norm_g'], t_prompt)
        x = _proj_call(mix, w_out, l, "out_proj_residual", out_dtype=F32, residual=(x, mod, 2, t_prompt, dec_seq))

        wr = jnp.concatenate([p['router_g_w'],
                              p['router_e_w'].transpose(1, 0, 2).reshape(d, N_EXPERTS)], axis=1)
        wr = jnp.pad(wr, ((0, 0), (0, LANES - wr.shape[1])))
        br = jnp.pad(jnp.concatenate([p['router_g_b'], p['router_e_b'].reshape(-1)]),
                     (0, LANES - N_GROUPS - N_EXPERTS)).reshape(1, LANES)
        h2, route = _router_call(x, p['norm2_g'], mod, wr, br, t_prompt, dec_seq)
        ys, dest = _moe(h2, route, w_gate, w_up, w_down, l)
        x = _combine_call(dest, x, route, ys, mod, final_norm_g, l == depth - 1, t_prompt, dec_seq)

    y_prompt = x[0].reshape(batch, seq, d)
    y_sample = x[1].reshape(dec_batch, dec_seq, d)
    return y_prompt, y_sample, jnp.stack(new_k, axis=1), jnp.stack(new_v, axis=1)


_PARAM_NAMES = ('norm1_g', 'norm2_g', 'w_mod', 'b_mod', 'w_in', 'q_norm_g', 'k_norm_g', 'hy_conv_w', 'hy_conv_b',
                'hf_w1', 'hf_b1', 'hf_w2', 'hf_b2', 'hf_w3', 'hf_b3', 'hf_freq', 'hy_bias', 'gm_norm_g',
                'gm_norm_b', 'gm_ws', 'gm_bs', 'out_norm_g', 'w_out', 'router_g_w', 'router_g_b', 'router_e_w',
                'router_e_b', 'exp_w_gate', 'exp_w_up', 'exp_w_down')


def kernel(x_prompt, x_sample, cache_k, cache_v, c, c_ctx, norm1_g, norm2_g, w_mod, b_mod, w_in, q_norm_g, k_norm_g, hy_conv_w, hy_conv_b, hf_w1, hf_b1, hf_w2, hf_b2, hf_w3, hf_b3, hf_freq, hy_bias, gm_norm_g, gm_norm_b, gm_ws, gm_bs, out_norm_g, w_out, router_g_w, router_g_b, router_e_w, router_e_b, exp_w_gate, exp_w_up, exp_w_down, final_norm_g):
    values = (norm1_g, norm2_g, w_mod, b_mod, w_in, q_norm_g, k_norm_g, hy_conv_w, hy_conv_b, hf_w1, hf_b1, hf_w2,
              hf_b2, hf_w3, hf_b3, hf_freq, hy_bias, gm_norm_g, gm_norm_b, gm_ws, gm_bs, out_norm_g, w_out,
              router_g_w, router_g_b, router_e_w, router_e_b, exp_w_gate, exp_w_up, exp_w_down)
    prm = dict(zip(_PARAM_NAMES, values))
    return _trunk(x_prompt, x_sample, cache_k, cache_v, c, c_ctx, prm, final_norm_g)
```

```python
import functools
import math

import numpy as np
import jax
import jax.numpy as jnp
from jax import lax
from jax.experimental import pallas as pl
from jax.experimental.pallas import tpu as pltpu

F32 = jnp.float32
BF16 = jnp.bfloat16

D_MODEL = 4096
GRID_W = 64
HEAD_DIM = 128
N_HEADS = 16
N_KV_HEADS = 4
Q_PER_KV = N_HEADS // N_KV_HEADS
ATTN_DIM = N_HEADS * HEAD_DIM
KV_DIM = N_KV_HEADS * HEAD_DIM
HY_DIM = 1024
GM_DIM = 1024
GM_HEADS = 8
CHUNK = 128
PROJ_DIM = ATTN_DIM + 2 * KV_DIM + 3 * HY_DIM + 2 * GM_DIM
ROPE_THETA = 10000.0
FILTER_EMB = 33
FILTER_HIDDEN = 64
DECAY_TARGET = 1e-2
FAST_DECAY_PCT = 0.3
SLOW_DECAY_PCT = 1.5
MOD_SHIFT = 0.05
N_GROUPS = 4
EXPERTS_PER_GROUP = 4
N_EXPERTS = 16
EXPERT_FF = 1024
EPS = 1e-6

COL_K = ATTN_DIM
COL_V = ATTN_DIM + KV_DIM
COL_HY = ATTN_DIM + 2 * KV_DIM
COL_GM = COL_HY + 3 * HY_DIM

LANES = 128
VMEM_LIMIT = 56 * 1024 * 1024
HI = lax.Precision.HIGHEST


def _cparams(*sem, **kw):
    return pltpu.CompilerParams(dimension_semantics=sem, vmem_limit_bytes=VMEM_LIMIT, **kw)


def _rms(x):
    return x * lax.rsqrt(jnp.mean(x * x, axis=-1, keepdims=True) + EPS)


def _sigmoid(x):
    return 1.0 / (1.0 + jnp.exp(-x))


def _dot_3pass(a, b):
    a_hi = a.astype(BF16)
    a_lo = (a - a_hi.astype(F32)).astype(BF16)
    b_hi = b.astype(BF16)
    b_lo = (b - b_hi.astype(F32)).astype(BF16)
    return (jnp.dot(a_hi, b_hi, preferred_element_type=F32) + jnp.dot(a_hi, b_lo, preferred_element_type=F32)
            + jnp.dot(a_lo, b_hi, preferred_element_type=F32))


def _pack_bf16_pairs(x):
    n = x.shape[1] // 2
    lo = lax.bitcast_convert_type(x[:, :n].astype(BF16).astype(F32), jnp.uint32)
    hi = lax.bitcast_convert_type(x[:, n:].astype(BF16).astype(F32), jnp.uint32)
    return hi | (lo >> 16)


def _unpack_bf16_pairs(u):
    lo = lax.bitcast_convert_type(u << 16, F32)
    hi = lax.bitcast_convert_type(u & jnp.uint32(0xFFFF0000), F32)
    return jnp.concatenate([lo, hi], axis=1)


def _seg_of_row(row, t_prompt, dec_seq):
    return jnp.where(row < t_prompt, 0, 1 + (row - t_prompt) // dec_seq)


def _mod_kernel(c_ref, w_ref, b_ref, o_ref):
    c = c_ref[...]
    s = (c * _sigmoid(c)).astype(BF16)
    o_ref[...] = jnp.dot(s, w_ref[...].astype(BF16), preferred_element_type=F32) + b_ref[...]


def _mod_call(cond, w_mod, b_mod):
    depth, d, n = w_mod.shape
    r = cond.shape[0]
    tn = 512
    return pl.pallas_call(
        _mod_kernel,
        grid=(depth, n // tn),
        in_specs=[pl.BlockSpec((r, d), lambda l, j: (0, 0)),
                  pl.BlockSpec((None, d, tn), lambda l, j: (l, 0, j)),
                  pl.BlockSpec((None, 1, tn), lambda l, j: (l, 0, j))],
        out_specs=pl.BlockSpec((None, r, tn), lambda l, j: (l, 0, j)),
        out_shape=jax.ShapeDtypeStruct((depth, r, n), F32),
        compiler_params=_cparams("parallel", "parallel"),
        name="adaln_mod",
    )(cond, w_mod, b_mod.reshape(depth, 1, n))


def _norm_mod_kernel(x_ref, g_ref, sh_ref, sc_ref, o_ref):
    y = _rms(x_ref[...]) * g_ref[...]
    o_ref[...] = (y * (1.0 + sc_ref[...]) + sh_ref[...]).astype(o_ref.dtype)


def _norm_mod_call(x, g, mod, t_prompt, dec_seq, tm=512):
    t, d = x.shape
    seg = lambda i: _seg_of_row(i * tm, t_prompt, dec_seq)
    return pl.pallas_call(
        _norm_mod_kernel,
        grid=(t // tm,),
        in_specs=[pl.BlockSpec((tm, d), lambda i: (i, 0)),
                  pl.BlockSpec((1, d), lambda i: (0, 0)),
                  pl.BlockSpec((None, 1, d), lambda i: (seg(i), 0, 0)),
                  pl.BlockSpec((None, 1, d), lambda i: (seg(i), 0, 1))],
        out_specs=pl.BlockSpec((tm, d), lambda i: (i, 0)),
        out_shape=jax.ShapeDtypeStruct((t, d), BF16),
        compiler_params=_cparams("parallel"),
        name="norm1_modulate",
    )(x, g.reshape(1, d), mod, mod)


def _proj_kernel(a_ref, w_ref, o_ref):
    o_ref[...] = jnp.dot(a_ref[...], w_ref[...], preferred_element_type=F32).astype(o_ref.dtype)


def _proj_residual_kernel(a_ref, w_ref, x_ref, gate_ref, o_ref):
    o_ref[...] = x_ref[...] + gate_ref[...] * jnp.dot(a_ref[...], w_ref[...], preferred_element_type=F32)


def _proj_call(a, w, layer, name, out_dtype=BF16, residual=None, tm=1024, tn=1024):
    t, k = a.shape
    n = w.shape[2]
    in_specs = [pl.BlockSpec((tm, k), lambda i, j: (i, 0)),
                pl.BlockSpec((None, k, tn), lambda i, j: (layer, 0, j))]
    args = [a, w]
    body = _proj_kernel
    if residual is not None:
        x, mod, gate_block, t_prompt, dec_seq = residual
        seg = lambda i: _seg_of_row(i * tm, t_prompt, dec_seq)
        in_specs += [pl.BlockSpec((tm, tn), lambda i, j: (i, j)),
                     pl.BlockSpec((None, 1, tn), lambda i, j: (seg(i), 0, gate_block * (n // tn) + j))]
        args += [x, mod]
        body = _proj_residual_kernel
    return pl.pallas_call(
        body,
        grid=(t // tm, n // tn),
        in_specs=in_specs,
        out_specs=pl.BlockSpec((tm, tn), lambda i, j: (i, j)),
        out_shape=jax.ShapeDtypeStruct((t, n), out_dtype),
        compiler_params=_cparams("parallel", "parallel"),
        name=name,
    )(*args)


def _attn_prompt_kernel(q_ref, k_ref, v_ref, qg_ref, kg_ref, o_ref, kc_ref, vc_ref):
    kn = _rms(k_ref[...].astype(F32)) * kg_ref[...]
    vb = v_ref[...]
    kc_ref[...] = kn
    vc_ref[...] = vb.astype(F32)
    kb = kn.astype(BF16)
    scale = HEAD_DIM ** -0.5
    for m in range(Q_PER_KV):
        sl = slice(m * HEAD_DIM, (m + 1) * HEAD_DIM)
        qn = (_rms(q_ref[:, sl].astype(F32)) * qg_ref[...] * scale).astype(BF16)
        s = lax.dot_general(qn, kb, (((1,), (1,)), ((), ())), preferred_element_type=F32)
        p = jnp.exp(s - jnp.max(s, axis=-1, keepdims=True))
        l = jnp.sum(p, axis=-1, keepdims=True)
        o = jnp.dot(p.astype(BF16), vb, preferred_element_type=F32)
        o_ref[:, sl] = (o / l).astype(o_ref.dtype)


def _attn_prompt_call(z, qg, kg, batch, seq):
    gw = Q_PER_KV * HEAD_DIM
    tp = batch * seq
    return pl.pallas_call(
        _attn_prompt_kernel,
        grid=(batch, N_KV_HEADS),
        in_specs=[pl.BlockSpec((seq, gw), lambda b, g: (b, g)),
                  pl.BlockSpec((seq, HEAD_DIM), lambda b, g: (b, COL_K // HEAD_DIM + g)),
                  pl.BlockSpec((seq, HEAD_DIM), lambda b, g: (b, COL_V // HEAD_DIM + g)),
                  pl.BlockSpec((1, HEAD_DIM), lambda b, g: (0, 0)),
                  pl.BlockSpec((1, HEAD_DIM), lambda b, g: (0, 0))],
        out_specs=[pl.BlockSpec((seq, gw), lambda b, g: (b, g)),
                   pl.BlockSpec((seq, HEAD_DIM), lambda b, g: (b, g)),
                   pl.BlockSpec((seq, HEAD_DIM), lambda b, g: (b, g))],
        out_shape=[jax.ShapeDtypeStruct((tp, ATTN_DIM), BF16),
                   jax.ShapeDtypeStruct((tp, KV_DIM), F32),
                   jax.ShapeDtypeStruct((tp, KV_DIM), F32)],
        compiler_params=_cparams("parallel", "parallel"),
        name="attn_context",
    )(z, z, z, qg.reshape(1, HEAD_DIM), kg.reshape(1, HEAD_DIM))


def _rope(x, cos, sin_signed):
    lane = lax.broadcasted_iota(jnp.int32, x.shape, 1)
    quarter = HEAD_DIM // 4
    fwd = pltpu.roll(x, HEAD_DIM - quarter, 1)
    bwd = pltpu.roll(x, quarter, 1)
    swapped = jnp.where((lane % (2 * quarter)) < quarter, fwd, bwd)
    return x * cos + swapped * sin_signed


def _head_rope_kernel(x_ref, c_ref, s_ref, g_ref, o_ref, *, scale):
    for h in range(x_ref.shape[1] // HEAD_DIM):
        sl = slice(h * HEAD_DIM, (h + 1) * HEAD_DIM)
        xn = _rms(x_ref[:, sl].astype(F32)) * g_ref[...]
        o_ref[:, sl] = (_rope(xn, c_ref[...], s_ref[...]) * scale).astype(o_ref.dtype)


def _head_rope_call(z, col0, n_heads, cos, sin_signed, g, scale, t_prompt, dec_batch, dec_seq, tr=512):
    nr = dec_seq // tr
    width = n_heads * HEAD_DIM
    return pl.pallas_call(
        functools.partial(_head_rope_kernel, scale=scale),
        grid=(dec_batch, nr),
        in_specs=[pl.BlockSpec((tr, width), lambda b, r: (t_prompt // tr + b * nr + r, col0 // width)),
                  pl.BlockSpec((tr, HEAD_DIM), lambda b, r: (r, 0)),
                  pl.BlockSpec((tr, HEAD_DIM), lambda b, r: (r, 0)),
                  pl.BlockSpec((1, HEAD_DIM), lambda b, r: (0, 0))],
        out_specs=pl.BlockSpec((tr, width), lambda b, r: (b * nr + r, 0)),
        out_shape=jax.ShapeDtypeStruct((dec_batch * dec_seq, width), BF16),
        compiler_params=_cparams("parallel", "parallel"),
        name="attn_head_rope",
    )(z, cos, sin_signed, g.reshape(1, HEAD_DIM))


def _attn_sample_kernel(q_ref, k_ref, v_ref, ck_ref, cv_ref, o_ref, m_ref, acc_ref, *, tk):
    m_ref[...] = jnp.full(m_ref.shape, -jnp.inf, F32)
    acc_ref[...] = jnp.zeros(acc_ref.shape, F32)

    def step(kb, vb):
        reps = kb.shape[0] // LANES
        v1 = jnp.concatenate([vb, jnp.ones(vb.shape, BF16)], axis=1)
        scores = [lax.dot_general(q_ref[:, m * HEAD_DIM:(m + 1) * HEAD_DIM], kb, (((1,), (1,)), ((), ())),
                                  preferred_element_type=F32) for m in range(Q_PER_KV)]
        for m in range(Q_PER_KV):
            s = scores[m]
            m_prev = m_ref[m]
            m_next = jnp.maximum(m_prev, jnp.max(s, axis=-1, keepdims=True))
            alpha = jnp.exp(m_prev - m_next)
            p = jnp.exp(s - jnp.concatenate([m_next] * reps, axis=1))
            acc_ref[m] = (jnp.concatenate([alpha, alpha], axis=1) * acc_ref[m]
                          + jnp.dot(p.astype(BF16), v1, preferred_element_type=F32))
            m_ref[m] = m_next

    step(ck_ref[...].astype(BF16), cv_ref[...].astype(BF16))

    def latent_tile(j, carry):
        rows = pl.ds(pl.multiple_of(j * tk, tk), tk)
        step(k_ref[rows, :], v_ref[rows, :])
        return carry

    lax.fori_loop(0, k_ref.shape[0] // tk, latent_tile, 0)

    for m in range(Q_PER_KV):
        o_ref[:, m * HEAD_DIM:(m + 1) * HEAD_DIM] = (
            acc_ref[m, :, :HEAD_DIM] / acc_ref[m, :, HEAD_DIM:]).astype(o_ref.dtype)


def _attn_sample_call(z, qr, kr, ctx_k, ctx_v, t_prompt, dec_batch, dec_seq, tq=512, tk=1024):
    gw = Q_PER_KV * HEAD_DIM
    past = ctx_k.shape[1]
    tk = min(tk, dec_seq)
    nq = dec_seq // tq
    return pl.pallas_call(
        functools.partial(_attn_sample_kernel, tk=tk),
        grid=(dec_batch, N_KV_HEADS, nq),
        in_specs=[pl.BlockSpec((tq, gw), lambda b, g, i: (b * nq + i, g)),
                  pl.BlockSpec((dec_seq, HEAD_DIM), lambda b, g, i: (b, g)),
                  pl.BlockSpec((dec_seq, HEAD_DIM),
                               lambda b, g, i: (t_prompt // dec_seq + b, COL_V // HEAD_DIM + g)),
                  pl.BlockSpec((None, past, HEAD_DIM), lambda b, g, i: (b, 0, g)),
                  pl.BlockSpec((None, past, HEAD_DIM), lambda b, g, i: (b, 0, g))],
        out_specs=pl.BlockSpec((tq, gw), lambda b, g, i: (b * nq + i, g)),
        out_shape=jax.ShapeDtypeStruct((dec_batch * dec_seq, ATTN_DIM), BF16),
        scratch_shapes=[pltpu.VMEM((Q_PER_KV, tq, LANES), F32),
                        pltpu.VMEM((Q_PER_KV, tq, 2 * HEAD_DIM), F32)],
        compiler_params=_cparams("parallel", "parallel", "parallel"),
        name="attn_latent",
    )(qr, kr, z, ctx_k, ctx_v)


def _rope_tables(n_tokens):
    quarter = HEAD_DIM // 4
    n_rows = n_tokens // GRID_W
    row = jnp.repeat(jnp.arange(n_rows, dtype=F32), GRID_W)
    col = jnp.tile(jnp.arange(GRID_W, dtype=F32), n_rows)
    freqs = ROPE_THETA ** (-jnp.arange(quarter, dtype=F32) / quarter)
    ar = row[:, None] * freqs[None, :]
    ac = col[:, None] * freqs[None, :]
    cos = jnp.concatenate([jnp.cos(ar), jnp.cos(ar), jnp.cos(ac), jnp.cos(ac)], axis=-1)
    sin = jnp.concatenate([-jnp.sin(ar), jnp.sin(ar), -jnp.sin(ac), jnp.sin(ac)], axis=-1)
    return cos, sin


def _gelu_tanh(x):
    return 0.5 * x * (1.0 + jnp.tanh(math.sqrt(2.0 / math.pi) * (x + 0.044715 * (x * x * x))))


def _gmlp_kernel(z_ref, g_ref, b_ref, ws_ref, bs_ref, o_ref):
    hw = GM_DIM // GM_HEADS
    for ch in range(z_ref.shape[0] // CHUNK):
        rows = slice(ch * CHUNK, (ch + 1) * CHUNK)
        u = _gelu_tanh(z_ref[rows, :GM_DIM].astype(F32))
        v = _gelu_tanh(z_ref[rows, GM_DIM:].astype(F32))
        mu = jnp.mean(v, axis=-1, keepdims=True)
        vc = v - mu
        var = jnp.mean(vc * vc, axis=-1, keepdims=True)
        vn = (vc * lax.rsqrt(var + EPS) * g_ref[...] + b_ref[...]).astype(BF16)
        for h in range(GM_HEADS):
            cols = slice(h * hw, (h + 1) * hw)
            s = jnp.dot(ws_ref[h], vn[:, cols], preferred_element_type=F32) + bs_ref[h]
            o_ref[rows, cols] = (u[:, cols] * s).astype(o_ref.dtype)


def _gmlp_call(z, g, b, ws, bs, tr=512):
    t = z.shape[0]
    return pl.pallas_call(
        _gmlp_kernel,
        grid=(t // tr,),
        in_specs=[pl.BlockSpec((tr, 2 * GM_DIM), lambda i: (i, COL_GM // (2 * GM_DIM))),
                  pl.BlockSpec((1, GM_DIM), lambda i: (0, 0)),
                  pl.BlockSpec((1, GM_DIM), lambda i: (0, 0)),
                  pl.BlockSpec((GM_HEADS, CHUNK, CHUNK), lambda i: (0, 0, 0)),
                  pl.BlockSpec((GM_HEADS, CHUNK, 1), lambda i: (0, 0, 0))],
        out_specs=pl.BlockSpec((tr, GM_DIM), lambda i: (i, 0)),
        out_shape=jax.ShapeDtypeStruct((t, GM_DIM), BF16),
        compiler_params=_cparams("parallel"),
        name="chunk_gmlp",
    )(z, g.reshape(1, GM_DIM), b.reshape(1, GM_DIM), ws.astype(BF16), bs.reshape(GM_HEADS, CHUNK, 1))


def _sconv_kernel(z_ref, w_ref, b_ref, o_ref, *, seg_len):
    x = z_ref[...].astype(F32)
    n = x.shape[0]
    pos = lax.broadcasted_iota(jnp.int32, x.shape, 0) % seg_len
    prev = jnp.where(pos == 0, 0.0, pltpu.roll(x, 1, 0))
    nxt = jnp.where(pos == seg_len - 1, 0.0, pltpu.roll(x, n - 1, 0))
    o_ref[...] = (b_ref[...] + prev * w_ref[0:1, :] + x * w_ref[1:2, :] + nxt * w_ref[2:3, :]).astype(o_ref.dtype)


def _sconv_call(z, w, b, row0, n_rows, seg_len, block_rows, cw=256):
    width = 3 * HY_DIM
    return pl.pallas_call(
        functools.partial(_sconv_kernel, seg_len=seg_len),
        grid=(n_rows // block_rows, width // cw),
        in_specs=[pl.BlockSpec((block_rows, cw), lambda i, c: (row0 // block_rows + i, COL_HY // cw + c)),
                  pl.BlockSpec((3, cw), lambda i, c: (0, c)),
                  pl.BlockSpec((1, cw), lambda i, c: (0, c))],
        out_specs=pl.BlockSpec((block_rows, cw), lambda i, c: (i, c)),
        out_shape=jax.ShapeDtypeStruct((n_rows, width), BF16),
        compiler_params=_cparams("parallel", "parallel"),
        name="hyena_short_conv",
    )(z, w, b.reshape(1, width))


def _filter_kernel(z_ref, t_ref, w1_ref, b1_ref, w2_ref, b2_ref, w3a_ref, b3a_ref, w3b_ref, b3b_ref,
                   fr_ref, dl_ref, o_ref, *, tr):
    fr = fr_ref[...]
    h = jnp.sin(fr * (jnp.dot(z_ref[...], w1_ref[...], precision=HI, preferred_element_type=F32) + b1_ref[...]))
    h = jnp.sin(fr * (jnp.dot(h, w2_ref[...], precision=HI, preferred_element_type=F32) + b2_ref[...]))
    decay = jnp.exp(-t_ref[...] * dl_ref[...]) + MOD_SHIFT
    row = lax.broadcasted_iota(jnp.int32, decay.shape, 0) + pl.program_id(0) * tr
    decay = jnp.where(row == 0, 0.0, decay)
    for o, (w3_ref, b3_ref) in enumerate(((w3a_ref, b3a_ref), (w3b_ref, b3b_ref))):
        taps = _dot_3pass(h, w3_ref[...]) + b3_ref[...]
        o_ref[:, o * HY_DIM:(o + 1) * HY_DIM] = taps * decay


def _filter_call(length, w1, b1, w2, b2, w3, b3, fr, tr=256):
    bands = (FILTER_EMB - 1) // 2
    t = np.linspace(0.0, 1.0, length)[:, None]
    wv = 2.0 * np.pi * np.arange(length)[:, None] / length
    f = np.linspace(1e-4, bands - 1, bands)[None, :]
    zf = np.concatenate([t, np.cos(f * wv), -np.sin(f * wv)], axis=-1)
    min_decay = math.log(DECAY_TARGET) / SLOW_DECAY_PCT
    max_decay = math.log(DECAY_TARGET) / FAST_DECAY_PCT
    deltas = np.abs(np.linspace(min_decay, max_decay, HY_DIM))[None, :]
    lag = np.minimum(np.abs(np.arange(2 * length) - length), length - 1)
    pad = LANES - FILTER_EMB
    hp = LANES - FILTER_HIDDEN
    z2 = jnp.asarray(np.pad(zf[lag], ((0, 0), (0, pad))).astype(np.float32))
    t2 = jnp.asarray(t[lag].astype(np.float32))
    dl = jnp.asarray(deltas.astype(np.float32))
    w1p = jnp.pad(w1, ((0, pad), (0, hp)))
    w2p = jnp.pad(w2, ((0, hp), (0, hp)))
    w3p = jnp.pad(w3, ((0, hp), (0, 0)))
    b1p = jnp.pad(b1, (0, hp)).reshape(1, LANES)
    b2p = jnp.pad(b2, (0, hp)).reshape(1, LANES)
    frp = jnp.pad(fr, (0, hp)).reshape(1, LANES)
    b3r = b3.reshape(1, -1)
    nblk = 2 * length // tr
    half = length // tr
    wcol = lambda r, o: o * 2 + jnp.where(r < half, 1, 0)
    small = pl.BlockSpec((1, LANES), lambda r: (0, 0))
    square = pl.BlockSpec((LANES, LANES), lambda r: (0, 0))
    return pl.pallas_call(
        functools.partial(_filter_kernel, tr=tr),
        grid=(nblk,),
        in_specs=[pl.BlockSpec((tr, LANES), lambda r: (r, 0)),
                  pl.BlockSpec((tr, 1), lambda r: (r, 0)),
                  square, small, square, small,
                  pl.BlockSpec((LANES, HY_DIM), lambda r: (0, wcol(r, 0))),
                  pl.BlockSpec((1, HY_DIM), lambda r: (0, wcol(r, 0))),
                  pl.BlockSpec((LANES, HY_DIM), lambda r: (0, wcol(r, 1))),
                  pl.BlockSpec((1, HY_DIM), lambda r: (0, wcol(r, 1))),
                  small,
                  pl.BlockSpec((1, HY_DIM), lambda r: (0, 0))],
        out_specs=pl.BlockSpec((tr, 2 * HY_DIM), lambda r: (r, 0)),
        out_shape=jax.ShapeDtypeStruct((2 * length, 2 * HY_DIM), F32),
        compiler_params=_cparams("parallel"),
        name="hyena_filter_mlp",
    )(z2, t2, w1p, b1p, w2p, b2p, w3p, b3r, w3p, b3r, frp, dl)


def _dft_constants(p):
    n = 2 * p
    idx = np.arange(p, dtype=np.float64)
    ang = 2.0 * np.pi * np.outer(idx, idx) / n
    re = np.cos(ang)
    im = -np.sin(ang)
    im[0, :] = np.cos(np.pi * idx)
    fwd = np.concatenate([re, im], axis=0)
    sign = np.where(np.arange(p) % 2 == 0, 1.0, -1.0)
    ar = (2.0 / n) * np.cos(ang)
    ar[:, 0] = 1.0 / n
    ai = -(2.0 / n) * np.sin(ang)
    ai[:, 0] = sign / n
    inv = np.concatenate([ar, ai], axis=1)
    as_bf16 = lambda a: jnp.asarray(a.astype(np.float32)).astype(BF16)
    return as_bf16(fwd), as_bf16(inv)


def _dft_kernel(f_ref, x_ref, o_ref):
    o_ref[...] = jnp.dot(f_ref[...], x_ref[...].astype(BF16), preferred_element_type=F32).astype(o_ref.dtype)


def _dft_call(x, fwd, p, col0, cw=256):
    nblk = x.shape[0] // p
    return pl.pallas_call(
        _dft_kernel,
        grid=(nblk, HY_DIM // cw),
        in_specs=[pl.BlockSpec((2 * p, p), lambda i, c: (0, 0)),
                  pl.BlockSpec((p, cw), lambda i, c: (i, col0 // cw + c))],
        out_specs=pl.BlockSpec((None, 2 * p, cw), lambda i, c: (i, 0, c)),
        out_shape=jax.ShapeDtypeStruct((nblk, 2 * p, HY_DIM), BF16),
        compiler_params=_cparams("parallel", "parallel"),
        name="hyena_block_dft",
    )(fwd, x)


def _filter_dft_kernel(f_ref, blk_ref, o_ref, tail_ref, *, p):
    b = pl.program_id(2)
    blk = blk_ref[...].astype(BF16)
    spec = jnp.dot(f_ref[...], blk, preferred_element_type=F32)

    @pl.when(b > 0)
    def _():
        o_ref[...] = (spec + tail_ref[...]).astype(o_ref.dtype)

    row = lax.broadcasted_iota(jnp.int32, spec.shape, 0)
    first = blk[0:1, :].astype(F32)
    real_row = jnp.logical_or(row < p, row == p)
    sign = (1 - 2 * (row % 2)).astype(F32)
    tail_ref[...] = sign * (spec - jnp.where(real_row, first, 0.0))


def _filter_dft_call(taps, fwd, p, cw=256):
    nb2 = taps.shape[0] // p
    nd = nb2 - 1
    ncb = HY_DIM // cw
    return pl.pallas_call(
        functools.partial(_filter_dft_kernel, p=p),
        grid=(2, ncb, nb2),
        in_specs=[pl.BlockSpec((2 * p, p), lambda o, c, b: (0, 0)),
                  pl.BlockSpec((p, cw), lambda o, c, b: (b, o * ncb + c))],
        out_specs=pl.BlockSpec((None, None, 2 * p, cw), lambda o, c, b: (o, jnp.maximum(b - 1, 0), 0, c)),
        out_shape=jax.ShapeDtypeStruct((2, nd, 2 * p, HY_DIM), BF16),
        scratch_shapes=[pltpu.VMEM((2 * p, cw), F32)],
        compiler_params=_cparams("parallel", "parallel", "arbitrary"),
        name="hyena_filter_dft",
    )(fwd, taps)


def _specconv_kernel(xs_ref, gs_ref, inv_ref, v_ref, gate_ref, bias_ref, o_ref, y_ref, *, nb, p, rc):
    i = pl.program_id(2)
    cw = o_ref.shape[1]

    for r in range(0, p, rc):
        def body(j, carry):
            yre, yim = carry
            d = i - j + nb - 1
            xre = xs_ref[j, r:r + rc, :]
            xim = xs_ref[j, p + r:p + r + rc, :]
            gre = gs_ref[d, r:r + rc, :]
            gim = gs_ref[d, p + r:p + r + rc, :]
            return (yre + (xre * gre - xim * gim).astype(F32), yim + (xre * gim + xim * gre).astype(F32))

        zero = jnp.zeros((rc, cw), F32)
        yre, yim = lax.fori_loop(0, nb, body, (zero, zero), unroll=True)
        y_ref[r:r + rc, :] = yre
        y_ref[p + r:p + r + rc, :] = yim

    def edge(j, carry):
        y0, yn = carry
        d = i - j + nb - 1
        x0 = xs_ref[j, 0:16, :].astype(F32)
        xn = xs_ref[j, p:p + 16, :].astype(F32)
        g0 = gs_ref[d, 0:16, :].astype(F32)
        gn = gs_ref[d, p:p + 16, :].astype(F32)
        return y0 + x0 * g0, yn + xn * gn

    zero16 = jnp.zeros((16, cw), F32)
    y0, yn = lax.fori_loop(0, nb, edge, (zero16, zero16))
    y_ref[0:1, :] = y0[0:1, :]
    y_ref[p:p + 1, :] = yn[0:1, :]

    y = jnp.dot(inv_ref[...], y_ref[...].astype(BF16), preferred_element_type=F32)
    v = v_ref[...].astype(F32)
    o_ref[...] = (gate_ref[...].astype(F32) * (y + bias_ref[...] * v)).astype(o_ref.dtype)


def _specconv_call(xs, gs, inv, zc, bias, nseq, nb, p, v_col0, gate_col0, v_src=None, cw=256, rc=32,
                   out_dtype=F32):
    n_rows = nseq * nb * p
    xs4 = xs.reshape(nseq, nb, 2 * p, HY_DIM)
    if v_src is None:
        v_arr, v_spec = zc, pl.BlockSpec((p, cw), lambda c, s, i: (s * nb + i, v_col0 // cw + c))
    else:
        v_arr, v_spec = v_src, pl.BlockSpec((p, cw), lambda c, s, i: (s * nb + i, c))
    return pl.pallas_call(
        functools.partial(_specconv_kernel, nb=nb, p=p, rc=rc),
        grid=(HY_DIM // cw, nseq, nb),
        in_specs=[pl.BlockSpec((None, nb, 2 * p, cw), lambda c, s, i: (s, 0, 0, c)),
                  pl.BlockSpec((2 * nb - 1, 2 * p, cw), lambda c, s, i: (0, 0, c)),
                  pl.BlockSpec((p, 2 * p), lambda c, s, i: (0, 0)),
                  v_spec,
                  pl.BlockSpec((p, cw), lambda c, s, i: (s * nb + i, gate_col0 // cw + c)),
                  pl.BlockSpec((1, cw), lambda c, s, i: (0, c))],
        out_specs=pl.BlockSpec((p, cw), lambda c, s, i: (s * nb + i, c)),
        out_shape=jax.ShapeDtypeStruct((n_rows, HY_DIM), out_dtype),
        scratch_shapes=[pltpu.VMEM((2 * p, cw), F32)],
        compiler_params=_cparams("parallel", "parallel", "parallel"),
        name="hyena_spectral_conv",
    )(xs4, gs, inv, v_arr, zc, bias.reshape(1, HY_DIM))


def _hyena_group(z, row0, nseq, length, p, prm, sconv_rows):
    nb = length // p
    zc = _sconv_call(z, prm['hy_conv_w'], prm['hy_conv_b'], row0, nseq * length, length, sconv_rows)
    taps = _filter_call(length, prm['hf_w1'], prm['hf_b1'], prm['hf_w2'], prm['hf_b2'],
                        prm['hf_w3'], prm['hf_b3'], prm['hf_freq'], tr=min(512, length))
    fwd, inv = _dft_constants(p)
    gs = _filter_dft_call(taps, fwd, p, cw=512)
    cw = 512 if nb == 1 else 256
    vs = _dft_call(zc, fwd, p, 2 * HY_DIM, cw=512)
    u = _specconv_call(vs, gs[0], inv, zc, prm['hy_bias'][0], nseq, nb, p, 2 * HY_DIM, 0, cw=cw)
    us = _dft_call(u, fwd, p, 0, cw=512)
    return _specconv_call(us, gs[1], inv, zc, prm['hy_bias'][1], nseq, nb, p, 0, HY_DIM, v_src=u, cw=cw,
                          out_dtype=BF16)


def _mix_norm_kernel(ap_ref, as_ref, hp_ref, hs_ref, m_ref, g_ref, o_ref, *, ntp):
    i = pl.program_id(0)

    def fill(a_ref, h_ref):
        o_ref[:, :ATTN_DIM] = (_rms(a_ref[...].astype(F32)) * g_ref[:, :ATTN_DIM]).astype(BF16)
        o_ref[:, ATTN_DIM:ATTN_DIM + HY_DIM] = (
            _rms(h_ref[...].astype(F32)) * g_ref[:, ATTN_DIM:ATTN_DIM + HY_DIM]).astype(BF16)
        o_ref[:, ATTN_DIM + HY_DIM:] = (
            _rms(m_ref[...].astype(F32)) * g_ref[:, ATTN_DIM + HY_DIM:]).astype(BF16)

    @pl.when(i < ntp)
    def _():
        fill(ap_ref, hp_ref)

    @pl.when(i >= ntp)
    def _():
        fill(as_ref, hs_ref)


def _mix_norm_call(attn_p, attn_s, hy_p, hy_s, gm, g, t_prompt, tm=512):
    t = gm.shape[0]
    mix_dim = ATTN_DIM + HY_DIM + GM_DIM
    ntp = t_prompt // tm
    nts = t // tm - ntp
    prow = lambda i: (jnp.minimum(i, ntp - 1), 0)
    srow = lambda i: (jnp.clip(i - ntp, 0, nts - 1), 0)
    return pl.pallas_call(
        functools.partial(_mix_norm_kernel, ntp=ntp),
        grid=(t // tm,),
        in_specs=[pl.BlockSpec((tm, ATTN_DIM), prow),
                  pl.BlockSpec((tm, ATTN_DIM), srow),
                  pl.BlockSpec((tm, HY_DIM), prow),
                  pl.BlockSpec((tm, HY_DIM), srow),
                  pl.BlockSpec((tm, GM_DIM), lambda i: (i, 0)),
                  pl.BlockSpec((1, mix_dim), lambda i: (0, 0))],
        out_specs=pl.BlockSpec((tm, mix_dim), lambda i: (i, 0)),
        out_shape=jax.ShapeDtypeStruct((t, mix_dim), BF16),
        compiler_params=_cparams("parallel"),
        name="mix_group_norm",
    )(attn_p, attn_s, hy_p, hy_s, gm, g.reshape(1, mix_dim))


def _router_kernel(x_ref, g_ref, sh_ref, sc_ref, wr_ref, br_ref, h_ref, r_ref):
    y = _rms(x_ref[...]) * g_ref[...]
    h = y * (1.0 + sc_ref[...]) + sh_ref[...]
    h_ref[...] = _pack_bf16_pairs(h)
    logits = _dot_3pass(h, wr_ref[...]) + br_ref[...]
    col = lambda k: logits[:, k:k + 1]

    lg = [col(k) for k in range(N_GROUPS)]
    g_max = functools.reduce(jnp.maximum, lg)
    g_den = sum(jnp.exp(v - g_max) for v in lg)
    g_top = 1.0 / g_den
    g_idx = jnp.full(g_max.shape, N_GROUPS - 1, jnp.int32)
    for k in range(N_GROUPS - 2, -1, -1):
        g_idx = jnp.where(lg[k] == g_max, k, g_idx)

    le = []
    for e in range(EXPERTS_PER_GROUP):
        v = col(N_GROUPS + (N_GROUPS - 1) * EXPERTS_PER_GROUP + e)
        for k in range(N_GROUPS - 2, -1, -1):
            v = jnp.where(g_idx == k, col(N_GROUPS + k * EXPERTS_PER_GROUP + e), v)
        le.append(v)

    e_max = functools.reduce(jnp.maximum, le)
    e1 = jnp.full(e_max.shape, EXPERTS_PER_GROUP - 1, jnp.int32)
    for e in range(EXPERTS_PER_GROUP - 2, -1, -1):
        e1 = jnp.where(le[e] == e_max, e, e1)
    neg = jnp.float32(-jnp.inf)
    rest = [jnp.where(e1 == e, neg, le[e]) for e in range(EXPERTS_PER_GROUP)]
    e2_max = functools.reduce(jnp.maximum, rest)
    e2 = jnp.full(e_max.shape, EXPERTS_PER_GROUP - 1, jnp.int32)
    for e in range(EXPERTS_PER_GROUP - 2, -1, -1):
        e2 = jnp.where(rest[e] == e2_max, e, e2)
    ratio = jnp.exp(e2_max - e_max)
    w1 = g_top / (1.0 + ratio)
    w2 = g_top * ratio / (1.0 + ratio)
    id1 = (g_idx * EXPERTS_PER_GROUP + e1).astype(F32)
    id2 = (g_idx * EXPERTS_PER_GROUP + e2).astype(F32)

    lane = lax.broadcasted_iota(jnp.int32, r_ref.shape, 1)
    r_ref[...] = jnp.where(lane == 0, id1, jnp.where(lane == 1, id2, jnp.where(lane == 2, w1, jnp.where(lane == 3, w2, 0.0))))


def _router_call(x, g, mod, wr, br, t_prompt, dec_seq, tm=512):
    t, d = x.shape
    seg = lambda i: _seg_of_row(i * tm, t_prompt, dec_seq)
    return pl.pallas_call(
        _router_kernel,
        grid=(t // tm,),
        in_specs=[pl.BlockSpec((tm, d), lambda i: (i, 0)),
                  pl.BlockSpec((1, d), lambda i: (0, 0)),
                  pl.BlockSpec((None, 1, d), lambda i: (seg(i), 0, 3)),
                  pl.BlockSpec((None, 1, d), lambda i: (seg(i), 0, 4)),
                  pl.BlockSpec((d, LANES), lambda i: (0, 0)),
                  pl.BlockSpec((1, LANES), lambda i: (0, 0))],
        out_specs=[pl.BlockSpec((tm, d // 2), lambda i: (i, 0)),
                   pl.BlockSpec((tm, LANES), lambda i: (i, 0))],
        out_shape=[jax.ShapeDtypeStruct((t, d // 2), jnp.uint32),
                   jax.ShapeDtypeStruct((t, LANES), F32)],
        compiler_params=_cparams("parallel"),
        name="norm2_router",
    )(x, g.reshape(1, d), mod, mod, wr, br)


def _dispatch_kernel(dest_ref, zt_ref, h_ref, xs_ref, zero_ref, sem, zsem, *, tm, t, expert_tile):
    i = pl.program_id(0)

    @pl.when(i == 0)
    def _():
        zero_ref[...] = jnp.zeros(zero_ref.shape, zero_ref.dtype)

        def clear(q):
            row = pl.multiple_of(jnp.maximum(zt_ref[q], 0), expert_tile)
            return pltpu.make_async_copy(zero_ref, xs_ref.at[pl.ds(row, expert_tile), :], zsem)

        for q in range(2 * N_EXPERTS):
            @pl.when(zt_ref[q] >= 0)
            def _():
                clear(q).start()

        for q in range(2 * N_EXPERTS):
            @pl.when(zt_ref[q] >= 0)
            def _():
                clear(q).wait()

    base = i * tm

    def body(r, c):
        for k in range(2):
            row = dest_ref[k * t + base + r]
            pltpu.make_async_copy(h_ref.at[pl.ds(r, 1), :], xs_ref.at[pl.ds(row, 1), :], sem).start()
        return c

    lax.fori_loop(0, tm, body, 0, unroll=8)
    for k in range(2):
        pltpu.make_async_copy(h_ref, xs_ref.at[pl.ds(0, tm), :], sem).wait()


def _dispatch_call(dest, zero_tiles, h, n_rows, expert_tile, tm=256):
    t, d = h.shape
    return pl.pallas_call(
        functools.partial(_dispatch_kernel, tm=tm, t=t, expert_tile=expert_tile),
        grid_spec=pltpu.PrefetchScalarGridSpec(
            num_scalar_prefetch=2,
            grid=(t // tm,),
            in_specs=[pl.BlockSpec((tm, d), lambda i, dest, zt: (i, 0))],
            out_specs=pl.BlockSpec(memory_space=pl.ANY),
            scratch_shapes=[pltpu.VMEM((expert_tile, d), h.dtype),
                            pltpu.SemaphoreType.DMA(()),
                            pltpu.SemaphoreType.DMA(())]),
        out_shape=jax.ShapeDtypeStruct((n_rows, d), h.dtype),
        compiler_params=_cparams("arbitrary", disable_bounds_checks=True),
        name="moe_dispatch_rows",
    )(dest, zero_tiles, h)


def _stream_expert_weights(plan, sources, wbuf, targets, sem):
    te_ref, first_ref, run_ref, nxt_ref = plan
    m = pl.program_id(1)

    def copies(e, slot):
        return [pltpu.make_async_copy(src, wbuf.at[slot, k], sem.at[slot]) for k, src in enumerate(sources(e))]

    @pl.when(m == 0)
    def _():
        for cp in copies(te_ref[0], 0):
            cp.start()

    @pl.when(first_ref[m] == 1)
    def _():
        slot = run_ref[m] % 2
        for cp in copies(te_ref[m], slot):
            cp.wait()

        @pl.when(nxt_ref[m] >= 0)
        def _():
            for cp in copies(nxt_ref[m], 1 - slot):
                cp.start()

        for k, tgt in enumerate(targets):
            tgt[...] = wbuf[slot, k].astype(BF16)


def _expert_up_kernel(te_ref, nv_ref, first_ref, run_ref, nxt_ref, x_ref, wg_ref, wu_ref, o_ref,
                      wbuf, wgb_ref, wub_ref, sem, *, tm, tf, layer):
    valid = pl.program_id(1) * tm < nv_ref[0]
    cols = pl.ds(pl.multiple_of(pl.program_id(0) * tf, tf), tf)
    _stream_expert_weights((te_ref, first_ref, run_ref, nxt_ref),
                           lambda e: (wg_ref.at[layer, e, :, cols], wu_ref.at[layer, e, :, cols]),
                           wbuf, (wgb_ref, wub_ref), sem)

    @pl.when(valid)
    def _():
        x = _unpack_bf16_pairs(x_ref[...]).astype(BF16)
        hg = jnp.dot(x, wgb_ref[...], preferred_element_type=F32)
        hu = jnp.dot(x, wub_ref[...], preferred_element_type=F32)
        o_ref[...] = (hg * _sigmoid(hg) * hu).astype(o_ref.dtype)

    @pl.when(jnp.logical_not(valid))
    def _():
        o_ref[...] = jnp.zeros(o_ref.shape, o_ref.dtype)


def _expert_up_call(plan, xs, wg, wu, layer, tm, nf=2):
    n, dh = xs.shape
    d, ff = wg.shape[2], wg.shape[3]
    tf = ff // nf
    return pl.pallas_call(
        functools.partial(_expert_up_kernel, tm=tm, tf=tf, layer=layer),
        grid_spec=pltpu.PrefetchScalarGridSpec(
            num_scalar_prefetch=5,
            grid=(nf, n // tm),
            in_specs=[pl.BlockSpec((tm, dh), lambda f, m, *_: (m, 0)),
                      pl.BlockSpec(memory_space=pl.ANY),
                      pl.BlockSpec(memory_space=pl.ANY)],
            out_specs=pl.BlockSpec((tm, tf), lambda f, m, *_: (m, f)),
            scratch_shapes=[pltpu.VMEM((2, 2, d, tf), F32),
                            pltpu.VMEM((d, tf), BF16), pltpu.VMEM((d, tf), BF16),
                            pltpu.SemaphoreType.DMA((2,))]),
        out_shape=jax.ShapeDtypeStruct((n, ff), BF16),
        compiler_params=_cparams("arbitrary", "arbitrary"),
        name="moe_expert_up",
    )(*plan, xs, wg, wu)


def _expert_down_kernel(te_ref, nv_ref, first_ref, run_ref, nxt_ref, a_ref, wd_ref, o_ref,
                        wbuf, wdb_ref, sem, *, tm, layer):
    valid = pl.program_id(1) * tm < nv_ref[0]
    _stream_expert_weights((te_ref, first_ref, run_ref, nxt_ref), lambda e: (wd_ref.at[layer, e],),
                           wbuf, (wdb_ref,), sem)

    @pl.when(valid)
    def _():
        o_ref[...] = _pack_bf16_pairs(jnp.dot(a_ref[...], wdb_ref[...], preferred_element_type=F32))

    @pl.when(jnp.logical_not(valid))
    def _():
        o_ref[...] = jnp.zeros(o_ref.shape, o_ref.dtype)


def _expert_down_call(plan, a, wd, layer, tm):
    n, ff = a.shape
    d = wd.shape[3]
    return pl.pallas_call(
        functools.partial(_expert_down_kernel, tm=tm, layer=layer),
        grid_spec=pltpu.PrefetchScalarGridSpec(
            num_scalar_prefetch=5,
            grid=(1, n // tm),
            in_specs=[pl.BlockSpec((tm, ff), lambda c, m, *_: (m, 0)),
                      pl.BlockSpec(memory_space=pl.ANY)],
            out_specs=pl.BlockSpec((tm, d // 2), lambda c, m, *_: (m, 0)),
            scratch_shapes=[pltpu.VMEM((2, 1, ff, d), F32),
                            pltpu.VMEM((ff, d), BF16),
                            pltpu.SemaphoreType.DMA((2,))]),
        out_shape=jax.ShapeDtypeStruct((n, d // 2), jnp.uint32),
        compiler_params=_cparams("arbitrary", "arbitrary"),
        name="moe_expert_down",
    )(*plan, a, wd)


def _combine_kernel(dest_ref, x_ref, r_ref, gate_ref, fg_ref, ys_ref, *rest, tc, t, nt, ntp, final):
    out_refs, (ybuf, sem) = rest[:-2], rest[-2:]
    i = pl.program_id(0)

    def fetch(tile, slot):
        base = tile * tc

        def body(r, c):
            for k in range(2):
                row = dest_ref[k * t + base + r]
                pltpu.make_async_copy(ys_ref.at[pl.ds(row, 1), :], ybuf.at[slot, pl.ds(k * tc + r, 1), :],
                                      sem.at[slot]).start()
            return c

        lax.fori_loop(0, tc, body, 0, unroll=8)

    @pl.when(i == 0)
    def _():
        fetch(0, 0)

    @pl.when(i + 1 < nt)
    def _():
        fetch(i + 1, (i + 1) % 2)

    slot = i % 2
    pltpu.make_async_copy(ys_ref.at[pl.ds(0, 2 * tc), :], ybuf.at[slot], sem.at[slot]).wait()

    w1 = r_ref[:, 2:3]
    w2 = r_ref[:, 3:4]
    y = w1 * _unpack_bf16_pairs(ybuf[slot, 0:tc, :]) + w2 * _unpack_bf16_pairs(ybuf[slot, tc:2 * tc, :])
    x = x_ref[...] + gate_ref[...] * y
    if final:
        x = _rms(x) * fg_ref[...]
        op_ref, os_ref = out_refs

        @pl.when(i < ntp)
        def _():
            op_ref[...] = x

        @pl.when(i >= ntp)
        def _():
            os_ref[...] = x
    else:
        out_refs[0][...] = x


def _combine_call(dest, x, route, ys, mod, final_g, final, t_prompt, dec_seq, tc=128):
    t, d = x.shape
    seg = lambda i, *_: _seg_of_row(i * tc, t_prompt, dec_seq)
    nt = t // tc
    ntp = t_prompt // tc
    if final:
        out_specs = [pl.BlockSpec((tc, d), lambda i, dest: (jnp.minimum(i, ntp - 1), 0)),
                     pl.BlockSpec((tc, d), lambda i, dest: (jnp.maximum(i - ntp, 0), 0))]
        out_shape = [jax.ShapeDtypeStruct((t_prompt, d), F32), jax.ShapeDtypeStruct((t - t_prompt, d), F32)]
    else:
        out_specs = pl.BlockSpec((tc, d), lambda i, dest: (i, 0))
        out_shape = jax.ShapeDtypeStruct((t, d), F32)
    return pl.pallas_call(
        functools.partial(_combine_kernel, tc=tc, t=t, nt=nt, ntp=ntp, final=final),
        grid_spec=pltpu.PrefetchScalarGridSpec(
            num_scalar_prefetch=1,
            grid=(nt,),
            in_specs=[pl.BlockSpec((tc, d), lambda i, dest: (i, 0)),
                      pl.BlockSpec((tc, LANES), lambda i, dest: (i, 0)),
                      pl.BlockSpec((None, 1, d), lambda i, dest: (seg(i), 0, 5)),
                      pl.BlockSpec((1, d), lambda i, dest: (0, 0)),
                      pl.BlockSpec(memory_space=pl.ANY)],
            out_specs=out_specs,
            scratch_shapes=[pltpu.VMEM((2, 2 * tc, ys.shape[1]), ys.dtype),
                            pltpu.SemaphoreType.DMA((2,))]),
        out_shape=out_shape,
        compiler_params=_cparams("arbitrary", disable_bounds_checks=True),
        name="moe_combine_residual",
    )(dest, x, route, mod, final_g.reshape(1, d), ys)


def _dispatch_plan(route, tm):
    t = route.shape[0]
    flat_e = route[:, 0:2].astype(jnp.int32).T.reshape(-1)
    n_pairs = 2 * t
    n_tiles = n_pairs // tm + N_EXPERTS
    onehot = (flat_e[:, None] == jnp.arange(N_EXPERTS, dtype=jnp.int32)[None, :]).astype(jnp.int32)
    running = jnp.cumsum(onehot, axis=0)
    counts = running[-1]
    rank = jnp.sum((running - 1) * onehot, axis=1)
    padded = ((counts + tm - 1) // tm) * tm
    pad_end = jnp.cumsum(padded)
    pad_start = pad_end - padded
    dest = (jnp.sum(onehot * pad_start[None, :], axis=1) + rank).astype(jnp.int32)
    tile_start = jnp.arange(n_tiles, dtype=jnp.int32) * tm
    tile_e = jnp.sum((pad_end[None, :] <= tile_start[:, None]).astype(jnp.int32), axis=1)
    tile_e = jnp.minimum(tile_e, N_EXPERTS - 1).astype(jnp.int32)
    n_valid = pad_end[-1:].astype(jnp.int32)
    last = jnp.where(counts > 0, pad_end - tm, -1)
    tail = pad_end[-1] + jnp.arange(N_EXPERTS, dtype=jnp.int32) * tm
    tail = jnp.where(tail < n_tiles * tm, tail, -1)
    zero_tiles = jnp.concatenate([last, tail]).astype(jnp.int32)
    used = tile_start < pad_end[-1]
    first = jnp.logical_and(used, jnp.concatenate([jnp.ones((1,), bool), tile_e[1:] != tile_e[:-1]]))
    run = jnp.maximum(jnp.cumsum(first.astype(jnp.int32)) - 1, 0)
    ids = jnp.arange(N_EXPERTS, dtype=jnp.int32)
    later = jnp.logical_and(ids[None, :] > ids[:, None], (counts > 0)[None, :])
    next_e = jnp.min(jnp.where(later, ids[None, :], N_EXPERTS), axis=1)
    next_e = jnp.where(next_e == N_EXPERTS, -1, next_e)
    tile_plan = (tile_e, n_valid, first.astype(jnp.int32), run.astype(jnp.int32), next_e[tile_e].astype(jnp.int32))
    return dest, tile_plan, zero_tiles, n_tiles * tm


def _moe(h, route, wg, wu, wd, layer, tm=256):
    dest, tile_plan, zero_tiles, n_rows = _dispatch_plan(route, tm)
    xs = _dispatch_call(dest, zero_tiles, h, n_rows, tm)
    a = _expert_up_call(tile_plan, xs, wg, wu, layer, tm)
    ys = _expert_down_call(tile_plan, a, wd, layer, tm)
    return ys, dest


def _trunk(x_prompt, x_sample, cache_k, cache_v, c, c_ctx, prm, final_norm_g, hy_block_sample=512):
    batch, seq, d = x_prompt.shape
    dec_batch, dec_seq, _ = x_sample.shape
    depth = prm['w_in'].shape[0]
    past = cache_k.shape[2]
    t_prompt = batch * seq
    t_sample = dec_batch * dec_seq

    x = jnp.concatenate([x_prompt.reshape(t_prompt, d), x_sample.reshape(t_sample, d)], axis=0)
    n_cond = 1 + dec_batch
    cond = jnp.concatenate([c_ctx[None, :], c, jnp.zeros((8 - n_cond % 8, d), F32)], axis=0)
    mod_all = _mod_call(cond, prm['w_mod'], prm['b_mod'])
    cos, sin_signed = _rope_tables(dec_seq)
    sconv_rows_p = math.gcd(t_prompt, max(seq, 4096 // seq * seq))

    new_k, new_v = [], []
    big = ('w_mod', 'w_in', 'w_out', 'exp_w_gate', 'exp_w_up', 'exp_w_down')
    w_in, w_out = prm['w_in'].astype(BF16), prm['w_out'].astype(BF16)
    w_gate, w_up, w_down = prm['exp_w_gate'], prm['exp_w_up'], prm['exp_w_down']

    for l in range(depth):
        p = {name: w[l] for name, w in prm.items() if name not in big}
        mod = mod_all[l, :n_cond].reshape(n_cond, 1, 6 * d)

        h1 = _norm_mod_call(x, p['norm1_g'], mod, t_prompt, dec_seq)
        z = _proj_call(h1, w_in, l, "in_proj")

        attn_p, k_l, v_l = _attn_prompt_call(z, p['q_norm_g'], p['k_norm_g'], batch, seq)
        qr = _head_rope_call(z, 0, N_HEADS, cos, sin_signed, p['q_norm_g'], HEAD_DIM ** -0.5,
                             t_prompt, dec_batch, dec_seq)
        kr = _head_rope_call(z, COL_K, N_KV_HEADS, cos, sin_signed, p['k_norm_g'], 1.0,
                             t_prompt, dec_batch, dec_seq)
        attn_s = _attn_sample_call(z, qr, kr, cache_k[:, l].reshape(dec_batch, past, KV_DIM),
                                   cache_v[:, l].reshape(dec_batch, past, KV_DIM), t_prompt, dec_batch, dec_seq)
        new_k.append(k_l.reshape(batch, seq, N_KV_HEADS, HEAD_DIM))
        new_v.append(v_l.reshape(batch, seq, N_KV_HEADS, HEAD_DIM))

        hy_p = _hyena_group(z, 0, batch, seq, seq, p, sconv_rows_p)
        hy_s = _hyena_group(z, t_prompt, dec_batch, dec_seq, min(hy_block_sample, dec_seq), p, dec_seq)

        gm = _gmlp_call(z, p['gm_norm_g'], p['gm_norm_b'], p['gm_ws'], p['gm_bs'])

        mix = _mix_norm_call(attn_p, attn_s, hy_p, hy_s, gm, p['out_norm_g'], t_prompt)
        x = _proj_call(mix, w_out, l, "out_proj_residual", out_dtype=F32, residual=(x, mod, 2, t_prompt, dec_seq))

        wr = jnp.concatenate([p['router_g_w'],
                              p['router_e_w'].transpose(1, 0, 2).reshape(d, N_EXPERTS)], axis=1)
        wr = jnp.pad(wr, ((0, 0), (0, LANES - wr.shape[1])))
        br = jnp.pad(jnp.concatenate([p['router_g_b'], p['router_e_b'].reshape(-1)]),
                     (0, LANES - N_GROUPS - N_EXPERTS)).reshape(1, LANES)
        h2, route = _router_call(x, p['norm2_g'], mod, wr, br, t_prompt, dec_seq)
        ys, dest = _moe(h2, route, w_gate, w_up, w_down, l)
        x = _combine_call(dest, x, route, ys, mod, final_norm_g, l == depth - 1, t_prompt, dec_seq)

    y_prompt = x[0].reshape(batch, seq, d)
    y_sample = x[1].reshape(dec_batch, dec_seq, d)
    return y_prompt, y_sample, jnp.stack(new_k, axis=1), jnp.stack(new_v, axis=1)


_PARAM_NAMES = ('norm1_g', 'norm2_g', 'w_mod', 'b_mod', 'w_in', 'q_norm_g', 'k_norm_g', 'hy_conv_w', 'hy_conv_b',
                'hf_w1', 'hf_b1', 'hf_w2', 'hf_b2', 'hf_w3', 'hf_b3', 'hf_freq', 'hy_bias', 'gm_norm_g',
                'gm_norm_b', 'gm_ws', 'gm_bs', 'out_norm_g', 'w_out', 'router_g_w', 'router_g_b', 'router_e_w',
                'router_e_b', 'exp_w_gate', 'exp_w_up', 'exp_w_down')


def kernel(x_prompt, x_sample, cache_k, cache_v, c, c_ctx, norm1_g, norm2_g, w_mod, b_mod, w_in, q_norm_g, k_norm_g, hy_conv_w, hy_conv_b, hf_w1, hf_b1, hf_w2, hf_b2, hf_w3, hf_b3, hf_freq, hy_bias, gm_norm_g, gm_norm_b, gm_ws, gm_bs, out_norm_g, w_out, router_g_w, router_g_b, router_e_w, router_e_b, exp_w_gate, exp_w_up, exp_w_down, final_norm_g):
    values = (norm1_g, norm2_g, w_mod, b_mod, w_in, q_norm_g, k_norm_g, hy_conv_w, hy_conv_b, hf_w1, hf_b1, hf_w2,
              hf_b2, hf_w3, hf_b3, hf_freq, hy_bias, gm_norm_g, gm_norm_b, gm_ws, gm_bs, out_norm_g, w_out,
              router_g_w, router_g_b, router_e_w, router_e_b, exp_w_gate, exp_w_up, exp_w_down)
    prm = dict(zip(_PARAM_NAMES, values))
    return _trunk(x_prompt, x_sample, cache_k, cache_v, c, c_ctx, prm, final_norm_g)
```

```python
import functools
import math

import numpy as np
import jax
import jax.numpy as jnp
from jax import lax
from jax.experimental import pallas as pl
from jax.experimental.pallas import tpu as pltpu

F32 = jnp.float32
BF16 = jnp.bfloat16

D_MODEL = 4096
GRID_W = 64
HEAD_DIM = 128
N_HEADS = 16
N_KV_HEADS = 4
Q_PER_KV = N_HEADS // N_KV_HEADS
ATTN_DIM = N_HEADS * HEAD_DIM
KV_DIM = N_KV_HEADS * HEAD_DIM
HY_DIM = 1024
GM_DIM = 1024
GM_HEADS = 8
CHUNK = 128
PROJ_DIM = ATTN_DIM + 2 * KV_DIM + 3 * HY_DIM + 2 * GM_DIM
ROPE_THETA = 10000.0
FILTER_EMB = 33
FILTER_HIDDEN = 64
DECAY_TARGET = 1e-2
FAST_DECAY_PCT = 0.3
SLOW_DECAY_PCT = 1.5
MOD_SHIFT = 0.05
N_GROUPS = 4
EXPERTS_PER_GROUP = 4
N_EXPERTS = 16
EXPERT_FF = 1024
EPS = 1e-6

COL_K = ATTN_DIM
COL_V = ATTN_DIM + KV_DIM
COL_HY = ATTN_DIM + 2 * KV_DIM
COL_GM = COL_HY + 3 * HY_DIM

LANES = 128
VMEM_LIMIT = 56 * 1024 * 1024
HI = lax.Precision.HIGHEST


def _cparams(*sem, **kw):
    return pltpu.CompilerParams(dimension_semantics=sem, vmem_limit_bytes=VMEM_LIMIT, **kw)


def _rms(x):
    return x * lax.rsqrt(jnp.mean(x * x, axis=-1, keepdims=True) + EPS)


def _sigmoid(x):
    return 1.0 / (1.0 + jnp.exp(-x))


def _dot_3pass(a, b):
    a_hi = a.astype(BF16)
    a_lo = (a - a_hi.astype(F32)).astype(BF16)
    b_hi = b.astype(BF16)
    b_lo = (b - b_hi.astype(F32)).astype(BF16)
    return (jnp.dot(a_hi, b_hi, preferred_element_type=F32) + jnp.dot(a_hi, b_lo, preferred_element_type=F32)
            + jnp.dot(a_lo, b_hi, preferred_element_type=F32))


def _pack_bf16_pairs(x):
    n = x.shape[1] // 2
    lo = lax.bitcast_convert_type(x[:, :n].astype(BF16).astype(F32), jnp.uint32)
    hi = lax.bitcast_convert_type(x[:, n:].astype(BF16).astype(F32), jnp.uint32)
    return hi | (lo >> 16)


def _unpack_bf16_pairs(u):
    lo = lax.bitcast_convert_type(u << 16, F32)
    hi = lax.bitcast_convert_type(u & jnp.uint32(0xFFFF0000), F32)
    return jnp.concatenate([lo, hi], axis=1)


def _seg_of_row(row, t_prompt, dec_seq):
    return jnp.where(row < t_prompt, 0, 1 + (row - t_prompt) // dec_seq)


def _mod_kernel(c_ref, w_ref, b_ref, o_ref):
    c = c_ref[...]
    s = (c * _sigmoid(c)).astype(BF16)
    o_ref[...] = jnp.dot(s, w_ref[...].astype(BF16), preferred_element_type=F32) + b_ref[...]


def _mod_call(cond, w_mod, b_mod):
    depth, d, n = w_mod.shape
    r = cond.shape[0]
    tn = 512
    return pl.pallas_call(
        _mod_kernel,
        grid=(depth, n // tn),
        in_specs=[pl.BlockSpec((r, d), lambda l, j: (0, 0)),
                  pl.BlockSpec((None, d, tn), lambda l, j: (l, 0, j)),
                  pl.BlockSpec((None, 1, tn), lambda l, j: (l, 0, j))],
        out_specs=pl.BlockSpec((None, r, tn), lambda l, j: (l, 0, j)),
        out_shape=jax.ShapeDtypeStruct((depth, r, n), F32),
        compiler_params=_cparams("parallel", "parallel"),
        name="adaln_mod",
    )(cond, w_mod, b_mod.reshape(depth, 1, n))


def _row_split_specs(x, tm, width, col):
    if not isinstance(x, tuple):
        return [x], [pl.BlockSpec((tm, width), lambda i, *r: (i, col(i, *r)))], None
    xp, xs = x
    ntp, nts = xp.shape[0] // tm, xs.shape[0] // tm
    return ([xp, xs],
            [pl.BlockSpec((tm, width), lambda i, *r: (jnp.minimum(i, ntp - 1), col(i, *r))),
             pl.BlockSpec((tm, width), lambda i, *r: (jnp.clip(i - ntp, 0, nts - 1), col(i, *r)))],
            ntp)


def _row_split_pick(refs, ntp):
    if len(refs) == 1:
        return refs[0][...]
    return jnp.where(pl.program_id(0) < ntp, refs[0][...], refs[1][...])


def _norm_mod_kernel(*refs, ntp):
    *x_refs, g_ref, sh_ref, sc_ref, o_ref = refs
    y = _rms(_row_split_pick(x_refs, ntp)) * g_ref[...]
    o_ref[...] = (y * (1.0 + sc_ref[...]) + sh_ref[...]).astype(o_ref.dtype)


def _norm_mod_call(x, g, mod, t_prompt, dec_seq, tm=512):
    d = g.shape[0]
    xs, x_specs, ntp = _row_split_specs(x, tm, d, lambda i: 0)
    t = sum(a.shape[0] for a in xs)
    seg = lambda i: _seg_of_row(i * tm, t_prompt, dec_seq)
    return pl.pallas_call(
        functools.partial(_norm_mod_kernel, ntp=ntp),
        grid=(t // tm,),
        in_specs=x_specs + [pl.BlockSpec((1, d), lambda i: (0, 0)),
                            pl.BlockSpec((None, 1, d), lambda i: (seg(i), 0, 0)),
                            pl.BlockSpec((None, 1, d), lambda i: (seg(i), 0, 1))],
        out_specs=pl.BlockSpec((tm, d), lambda i: (i, 0)),
        out_shape=jax.ShapeDtypeStruct((t, d), BF16),
        compiler_params=_cparams("parallel"),
        name="norm1_modulate",
    )(*xs, g.reshape(1, d), mod, mod)


def _proj_kernel(a_ref, w_ref, o_ref):
    o_ref[...] = jnp.dot(a_ref[...], w_ref[...], preferred_element_type=F32).astype(o_ref.dtype)


def _proj_residual_kernel(a_ref, w_ref, *refs, ntp):
    *x_refs, gate_ref, o_ref = refs
    y = jnp.dot(a_ref[...], w_ref[...], preferred_element_type=F32)
    o_ref[...] = _row_split_pick(x_refs, ntp) + gate_ref[...] * y


def _proj_call(a, w, layer, name, out_dtype=BF16, residual=None, tm=1024, tn=1024):
    t, k = a.shape
    n = w.shape[2]
    in_specs = [pl.BlockSpec((tm, k), lambda i, j: (i, 0)),
                pl.BlockSpec((None, k, tn), lambda i, j: (layer, 0, j))]
    args = [a, w]
    body = _proj_kernel
    if residual is not None:
        x, mod, gate_block, t_prompt, dec_seq = residual
        seg = lambda i: _seg_of_row(i * tm, t_prompt, dec_seq)
        xs, x_specs, ntp = _row_split_specs(x, tm, tn, lambda i, j: j)
        in_specs += x_specs + [pl.BlockSpec((None, 1, tn),
                                            lambda i, j: (seg(i), 0, gate_block * (n // tn) + j))]
        args += xs + [mod]
        body = functools.partial(_proj_residual_kernel, ntp=ntp)
    return pl.pallas_call(
        body,
        grid=(t // tm, n // tn),
        in_specs=in_specs,
        out_specs=pl.BlockSpec((tm, tn), lambda i, j: (i, j)),
        out_shape=jax.ShapeDtypeStruct((t, n), out_dtype),
        compiler_params=_cparams("parallel", "parallel"),
        name=name,
    )(*args)


def _attn_prompt_kernel(q_ref, k_ref, v_ref, qg_ref, kg_ref, o_ref, kc_ref, vc_ref):
    kn = _rms(k_ref[...].astype(F32)) * kg_ref[...]
    vb = v_ref[...]
    kc_ref[...] = kn
    vc_ref[...] = vb.astype(F32)
    kb = kn.astype(BF16)
    scale = HEAD_DIM ** -0.5
    for m in range(Q_PER_KV):
        sl = slice(m * HEAD_DIM, (m + 1) * HEAD_DIM)
        qn = (_rms(q_ref[:, sl].astype(F32)) * qg_ref[...] * scale).astype(BF16)
        s = lax.dot_general(qn, kb, (((1,), (1,)), ((), ())), preferred_element_type=F32)
        p = jnp.exp(s - jnp.max(s, axis=-1, keepdims=True))
        l = jnp.sum(p, axis=-1, keepdims=True)
        o = jnp.dot(p.astype(BF16), vb, preferred_element_type=F32)
        o_ref[:, sl] = (o / l).astype(o_ref.dtype)


def _attn_prompt_call(z, qg, kg, batch, seq):
    gw = Q_PER_KV * HEAD_DIM
    tp = batch * seq
    return pl.pallas_call(
        _attn_prompt_kernel,
        grid=(batch, N_KV_HEADS),
        in_specs=[pl.BlockSpec((seq, gw), lambda b, g: (b, g)),
                  pl.BlockSpec((seq, HEAD_DIM), lambda b, g: (b, COL_K // HEAD_DIM + g)),
                  pl.BlockSpec((seq, HEAD_DIM), lambda b, g: (b, COL_V // HEAD_DIM + g)),
                  pl.BlockSpec((1, HEAD_DIM), lambda b, g: (0, 0)),
                  pl.BlockSpec((1, HEAD_DIM), lambda b, g: (0, 0))],
        out_specs=[pl.BlockSpec((seq, gw), lambda b, g: (b, g)),
                   pl.BlockSpec((seq, HEAD_DIM), lambda b, g: (b, g)),
                   pl.BlockSpec((seq, HEAD_DIM), lambda b, g: (b, g))],
        out_shape=[jax.ShapeDtypeStruct((tp, ATTN_DIM), BF16),
                   jax.ShapeDtypeStruct((tp, KV_DIM), F32),
                   jax.ShapeDtypeStruct((tp, KV_DIM), F32)],
        compiler_params=_cparams("parallel", "parallel"),
        name="attn_context",
    )(z, z, z, qg.reshape(1, HEAD_DIM), kg.reshape(1, HEAD_DIM))


def _rope(x, cos, sin_signed):
    lane = lax.broadcasted_iota(jnp.int32, x.shape, 1)
    quarter = HEAD_DIM // 4
    fwd = pltpu.roll(x, HEAD_DIM - quarter, 1)
    bwd = pltpu.roll(x, quarter, 1)
    swapped = jnp.where((lane % (2 * quarter)) < quarter, fwd, bwd)
    return x * cos + swapped * sin_signed


def _head_rope_kernel(x_ref, c_ref, s_ref, g_ref, o_ref, *, scale):
    for h in range(x_ref.shape[1] // HEAD_DIM):
        sl = slice(h * HEAD_DIM, (h + 1) * HEAD_DIM)
        xn = _rms(x_ref[:, sl].astype(F32)) * g_ref[...]
        o_ref[:, sl] = (_rope(xn, c_ref[...], s_ref[...]) * scale).astype(o_ref.dtype)


def _head_rope_call(z, col0, n_heads, cos, sin_signed, g, scale, t_prompt, dec_batch, dec_seq, tr=512):
    nr = dec_seq // tr
    width = n_heads * HEAD_DIM
    return pl.pallas_call(
        functools.partial(_head_rope_kernel, scale=scale),
        grid=(dec_batch, nr),
        in_specs=[pl.BlockSpec((tr, width), lambda b, r: (t_prompt // tr + b * nr + r, col0 // width)),
                  pl.BlockSpec((tr, HEAD_DIM), lambda b, r: (r, 0)),
                  pl.BlockSpec((tr, HEAD_DIM), lambda b, r: (r, 0)),
                  pl.BlockSpec((1, HEAD_DIM), lambda b, r: (0, 0))],
        out_specs=pl.BlockSpec((tr, width), lambda b, r: (b * nr + r, 0)),
        out_shape=jax.ShapeDtypeStruct((dec_batch * dec_seq, width), BF16),
        compiler_params=_cparams("parallel", "parallel"),
        name="attn_head_rope",
    )(z, cos, sin_signed, g.reshape(1, HEAD_DIM))


def _attn_sample_kernel(q_ref, k_ref, v_ref, ck_ref, cv_ref, o_ref, m_ref, acc_ref, *, tk):
    m_ref[...] = jnp.full(m_ref.shape, -jnp.inf, F32)
    acc_ref[...] = jnp.zeros(acc_ref.shape, F32)

    def step(kb, vb):
        reps = kb.shape[0] // LANES
        v1 = jnp.concatenate([vb, jnp.ones(vb.shape, BF16)], axis=1)
        scores = [lax.dot_general(q_ref[:, m * HEAD_DIM:(m + 1) * HEAD_DIM], kb, (((1,), (1,)), ((), ())),
                                  preferred_element_type=F32) for m in range(Q_PER_KV)]
        for m in range(Q_PER_KV):
            s = scores[m]
            m_prev = m_ref[m]
            m_next = jnp.maximum(m_prev, jnp.max(s, axis=-1, keepdims=True))
            alpha = jnp.exp(m_prev - m_next)
            p = jnp.exp(s - jnp.concatenate([m_next] * reps, axis=1))
            acc_ref[m] = (jnp.concatenate([alpha, alpha], axis=1) * acc_ref[m]
                          + jnp.dot(p.astype(BF16), v1, preferred_element_type=F32))
            m_ref[m] = m_next

    step(ck_ref[...].astype(BF16), cv_ref[...].astype(BF16))

    def latent_tile(j, carry):
        rows = pl.ds(pl.multiple_of(j * tk, tk), tk)
        step(k_ref[rows, :], v_ref[rows, :])
        return carry

    lax.fori_loop(0, k_ref.shape[0] // tk, latent_tile, 0)

    for m in range(Q_PER_KV):
        o_ref[:, m * HEAD_DIM:(m + 1) * HEAD_DIM] = (
            acc_ref[m, :, :HEAD_DIM] / acc_ref[m, :, HEAD_DIM:]).astype(o_ref.dtype)


def _attn_sample_call(z, qr, kr, ctx_k, ctx_v, t_prompt, dec_batch, dec_seq, tq=512, tk=1024):
    gw = Q_PER_KV * HEAD_DIM
    past = ctx_k.shape[1]
    tk = min(tk, dec_seq)
    nq = dec_seq // tq
    return pl.pallas_call(
        functools.partial(_attn_sample_kernel, tk=tk),
        grid=(dec_batch, N_KV_HEADS, nq),
        in_specs=[pl.BlockSpec((tq, gw), lambda b, g, i: (b * nq + i, g)),
                  pl.BlockSpec((dec_seq, HEAD_DIM), lambda b, g, i: (b, g)),
                  pl.BlockSpec((dec_seq, HEAD_DIM),
                               lambda b, g, i: (t_prompt // dec_seq + b, COL_V // HEAD_DIM + g)),
                  pl.BlockSpec((None, past, HEAD_DIM), lambda b, g, i: (b, 0, g)),
                  pl.BlockSpec((None, past, HEAD_DIM), lambda b, g, i: (b, 0, g))],
        out_specs=pl.BlockSpec((tq, gw), lambda b, g, i: (b * nq + i, g)),
        out_shape=jax.ShapeDtypeStruct((dec_batch * dec_seq, ATTN_DIM), BF16),
        scratch_shapes=[pltpu.VMEM((Q_PER_KV, tq, LANES), F32),
                        pltpu.VMEM((Q_PER_KV, tq, 2 * HEAD_DIM), F32)],
        compiler_params=_cparams("parallel", "parallel", "parallel"),
        name="attn_latent",
    )(qr, kr, z, ctx_k, ctx_v)


def _rope_tables(n_tokens):
    quarter = HEAD_DIM // 4
    n_rows = n_tokens // GRID_W
    row = jnp.repeat(jnp.arange(n_rows, dtype=F32), GRID_W)
    col = jnp.tile(jnp.arange(GRID_W, dtype=F32), n_rows)
    freqs = ROPE_THETA ** (-jnp.arange(quarter, dtype=F32) / quarter)
    ar = row[:, None] * freqs[None, :]
    ac = col[:, None] * freqs[None, :]
    cos = jnp.concatenate([jnp.cos(ar), jnp.cos(ar), jnp.cos(ac), jnp.cos(ac)], axis=-1)
    sin = jnp.concatenate([-jnp.sin(ar), jnp.sin(ar), -jnp.sin(ac), jnp.sin(ac)], axis=-1)
    return cos, sin


def _gelu_tanh(x):
    return 0.5 * x * (1.0 + jnp.tanh(math.sqrt(2.0 / math.pi) * (x + 0.044715 * (x * x * x))))


def _gmlp_kernel(z_ref, g_ref, b_ref, ws_ref, bs_ref, o_ref):
    hw = GM_DIM // GM_HEADS
    for ch in range(z_ref.shape[0] // CHUNK):
        rows = slice(ch * CHUNK, (ch + 1) * CHUNK)
        u = _gelu_tanh(z_ref[rows, :GM_DIM].astype(F32))
        v = _gelu_tanh(z_ref[rows, GM_DIM:].astype(F32))
        mu = jnp.mean(v, axis=-1, keepdims=True)
        vc = v - mu
        var = jnp.mean(vc * vc, axis=-1, keepdims=True)
        vn = (vc * lax.rsqrt(var + EPS) * g_ref[...] + b_ref[...]).astype(BF16)
        for h in range(GM_HEADS):
            cols = slice(h * hw, (h + 1) * hw)
            s = jnp.dot(ws_ref[h], vn[:, cols], preferred_element_type=F32) + bs_ref[h]
            o_ref[rows, cols] = (u[:, cols] * s).astype(o_ref.dtype)


def _gmlp_call(z, g, b, ws, bs, tr=512):
    t = z.shape[0]
    return pl.pallas_call(
        _gmlp_kernel,
        grid=(t // tr,),
        in_specs=[pl.BlockSpec((tr, 2 * GM_DIM), lambda i: (i, COL_GM // (2 * GM_DIM))),
                  pl.BlockSpec((1, GM_DIM), lambda i: (0, 0)),
                  pl.BlockSpec((1, GM_DIM), lambda i: (0, 0)),
                  pl.BlockSpec((GM_HEADS, CHUNK, CHUNK), lambda i: (0, 0, 0)),
                  pl.BlockSpec((GM_HEADS, CHUNK, 1), lambda i: (0, 0, 0))],
        out_specs=pl.BlockSpec((tr, GM_DIM), lambda i: (i, 0)),
        out_shape=jax.ShapeDtypeStruct((t, GM_DIM), BF16),
        compiler_params=_cparams("parallel"),
        name="chunk_gmlp",
    )(z, g.reshape(1, GM_DIM), b.reshape(1, GM_DIM), ws.astype(BF16), bs.reshape(GM_HEADS, CHUNK, 1))


def _sconv_kernel(z_ref, w_ref, b_ref, o_ref, *, seg_len):
    x = z_ref[...].astype(F32)
    n = x.shape[0]
    pos = lax.broadcasted_iota(jnp.int32, x.shape, 0) % seg_len
    prev = jnp.where(pos == 0, 0.0, pltpu.roll(x, 1, 0))
    nxt = jnp.where(pos == seg_len - 1, 0.0, pltpu.roll(x, n - 1, 0))
    o_ref[...] = (b_ref[...] + prev * w_ref[0:1, :] + x * w_ref[1:2, :] + nxt * w_ref[2:3, :]).astype(o_ref.dtype)


def _sconv_call(z, w, b, row0, n_rows, seg_len, block_rows, cw=256):
    width = 3 * HY_DIM
    return pl.pallas_call(
        functools.partial(_sconv_kernel, seg_len=seg_len),
        grid=(n_rows // block_rows, width // cw),
        in_specs=[pl.BlockSpec((block_rows, cw), lambda i, c: (row0 // block_rows + i, COL_HY // cw + c)),
                  pl.BlockSpec((3, cw), lambda i, c: (0, c)),
                  pl.BlockSpec((1, cw), lambda i, c: (0, c))],
        out_specs=pl.BlockSpec((block_rows, cw), lambda i, c: (i, c)),
        out_shape=jax.ShapeDtypeStruct((n_rows, width), BF16),
        compiler_params=_cparams("parallel", "parallel"),
        name="hyena_short_conv",
    )(z, w, b.reshape(1, width))


def _filter_kernel(z_ref, t_ref, w1_ref, b1_ref, w2_ref, b2_ref, w3a_ref, b3a_ref, w3b_ref, b3b_ref,
                   fr_ref, dl_ref, o_ref, *, tr):
    fr = fr_ref[...]
    h = jnp.sin(fr * (jnp.dot(z_ref[...], w1_ref[...], precision=HI, preferred_element_type=F32) + b1_ref[...]))
    h = jnp.sin(fr * (jnp.dot(h, w2_ref[...], precision=HI, preferred_element_type=F32) + b2_ref[...]))
    decay = jnp.exp(-t_ref[...] * dl_ref[...]) + MOD_SHIFT
    row = lax.broadcasted_iota(jnp.int32, decay.shape, 0) + pl.program_id(0) * tr
    decay = jnp.where(row == 0, 0.0, decay)
    for o, (w3_ref, b3_ref) in enumerate(((w3a_ref, b3a_ref), (w3b_ref, b3b_ref))):
        taps = _dot_3pass(h, w3_ref[...]) + b3_ref[...]
        o_ref[:, o * HY_DIM:(o + 1) * HY_DIM] = taps * decay


def _filter_call(length, w1, b1, w2, b2, w3, b3, fr, tr=256):
    bands = (FILTER_EMB - 1) // 2
    t = np.linspace(0.0, 1.0, length)[:, None]
    wv = 2.0 * np.pi * np.arange(length)[:, None] / length
    f = np.linspace(1e-4, bands - 1, bands)[None, :]
    zf = np.concatenate([t, np.cos(f * wv), -np.sin(f * wv)], axis=-1)
    min_decay = math.log(DECAY_TARGET) / SLOW_DECAY_PCT
    max_decay = math.log(DECAY_TARGET) / FAST_DECAY_PCT
    deltas = np.abs(np.linspace(min_decay, max_decay, HY_DIM))[None, :]
    lag = np.minimum(np.abs(np.arange(2 * length) - length), length - 1)
    pad = LANES - FILTER_EMB
    hp = LANES - FILTER_HIDDEN
    z2 = jnp.asarray(np.pad(zf[lag], ((0, 0), (0, pad))).astype(np.float32))
    t2 = jnp.asarray(t[lag].astype(np.float32))
    dl = jnp.asarray(deltas.astype(np.float32))
    w1p = jnp.pad(w1, ((0, pad), (0, hp)))
    w2p = jnp.pad(w2, ((0, hp), (0, hp)))
    w3p = jnp.pad(w3, ((0, hp), (0, 0)))
    b1p = jnp.pad(b1, (0, hp)).reshape(1, LANES)
    b2p = jnp.pad(b2, (0, hp)).reshape(1, LANES)
    frp = jnp.pad(fr, (0, hp)).reshape(1, LANES)
    b3r = b3.reshape(1, -1)
    nblk = 2 * length // tr
    half = length // tr
    wcol = lambda r, o: o * 2 + jnp.where(r < half, 1, 0)
    small = pl.BlockSpec((1, LANES), lambda r: (0, 0))
    square = pl.BlockSpec((LANES, LANES), lambda r: (0, 0))
    return pl.pallas_call(
        functools.partial(_filter_kernel, tr=tr),
        grid=(nblk,),
        in_specs=[pl.BlockSpec((tr, LANES), lambda r: (r, 0)),
                  pl.BlockSpec((tr, 1), lambda r: (r, 0)),
                  square, small, square, small,
                  pl.BlockSpec((LANES, HY_DIM), lambda r: (0, wcol(r, 0))),
                  pl.BlockSpec((1, HY_DIM), lambda r: (0, wcol(r, 0))),
                  pl.BlockSpec((LANES, HY_DIM), lambda r: (0, wcol(r, 1))),
                  pl.BlockSpec((1, HY_DIM), lambda r: (0, wcol(r, 1))),
                  small,
                  pl.BlockSpec((1, HY_DIM), lambda r: (0, 0))],
        out_specs=pl.BlockSpec((tr, 2 * HY_DIM), lambda r: (r, 0)),
        out_shape=jax.ShapeDtypeStruct((2 * length, 2 * HY_DIM), F32),
        compiler_params=_cparams("parallel"),
        name="hyena_filter_mlp",
    )(z2, t2, w1p, b1p, w2p, b2p, w3p, b3r, w3p, b3r, frp, dl)


def _dft_constants(p):
    n = 2 * p
    idx = np.arange(p, dtype=np.float64)
    ang = 2.0 * np.pi * np.outer(idx, idx) / n
    re = np.cos(ang)
    im = -np.sin(ang)
    im[0, :] = np.cos(np.pi * idx)
    fwd = np.concatenate([re, im], axis=0)
    sign = np.where(np.arange(p) % 2 == 0, 1.0, -1.0)
    ar = (2.0 / n) * np.cos(ang)
    ar[:, 0] = 1.0 / n
    ai = -(2.0 / n) * np.sin(ang)
    ai[:, 0] = sign / n
    inv = np.concatenate([ar, ai], axis=1)
    as_bf16 = lambda a: jnp.asarray(a.astype(np.float32)).astype(BF16)
    return as_bf16(fwd), as_bf16(inv)


def _dft_kernel(f_ref, x_ref, o_ref):
    o_ref[...] = jnp.dot(f_ref[...], x_ref[...].astype(BF16), preferred_element_type=F32).astype(o_ref.dtype)


def _dft_call(x, fwd, p, col0, cw=256):
    nblk = x.shape[0] // p
    return pl.pallas_call(
        _dft_kernel,
        grid=(nblk, HY_DIM // cw),
        in_specs=[pl.BlockSpec((2 * p, p), lambda i, c: (0, 0)),
                  pl.BlockSpec((p, cw), lambda i, c: (i, col0 // cw + c))],
        out_specs=pl.BlockSpec((None, 2 * p, cw), lambda i, c: (i, 0, c)),
        out_shape=jax.ShapeDtypeStruct((nblk, 2 * p, HY_DIM), BF16),
        compiler_params=_cparams("parallel", "parallel"),
        name="hyena_block_dft",
    )(fwd, x)


def _filter_dft_kernel(f_ref, blk_ref, o_ref, tail_ref, *, p):
    b = pl.program_id(2)
    blk = blk_ref[...].astype(BF16)
    spec = jnp.dot(f_ref[...], blk, preferred_element_type=F32)

    @pl.when(b > 0)
    def _():
        o_ref[...] = (spec + tail_ref[...]).astype(o_ref.dtype)

    row = lax.broadcasted_iota(jnp.int32, spec.shape, 0)
    first = blk[0:1, :].astype(F32)
    real_row = jnp.logical_or(row < p, row == p)
    sign = (1 - 2 * (row % 2)).astype(F32)
    tail_ref[...] = sign * (spec - jnp.where(real_row, first, 0.0))


def _filter_dft_call(taps, fwd, p, cw=256):
    nb2 = taps.shape[0] // p
    nd = nb2 - 1
    ncb = HY_DIM // cw
    return pl.pallas_call(
        functools.partial(_filter_dft_kernel, p=p),
        grid=(2, ncb, nb2),
        in_specs=[pl.BlockSpec((2 * p, p), lambda o, c, b: (0, 0)),
                  pl.BlockSpec((p, cw), lambda o, c, b: (b, o * ncb + c))],
        out_specs=pl.BlockSpec((None, None, 2 * p, cw), lambda o, c, b: (o, jnp.maximum(b - 1, 0), 0, c)),
        out_shape=jax.ShapeDtypeStruct((2, nd, 2 * p, HY_DIM), BF16),
        scratch_shapes=[pltpu.VMEM((2 * p, cw), F32)],
        compiler_params=_cparams("parallel", "parallel", "arbitrary"),
        name="hyena_filter_dft",
    )(fwd, taps)


def _specconv_kernel(xs_ref, gs_ref, inv_ref, v_ref, gate_ref, bias_ref, o_ref, y_ref, *, nb, p, rc):
    i = pl.program_id(2)
    cw = o_ref.shape[1]

    for r in range(0, p, rc):
        def body(j, carry):
            yre, yim = carry
            d = i - j + nb - 1
            xre = xs_ref[j, r:r + rc, :]
            xim = xs_ref[j, p + r:p + r + rc, :]
            gre = gs_ref[d, r:r + rc, :]
            gim = gs_ref[d, p + r:p + r + rc, :]
            return (yre + (xre * gre - xim * gim).astype(F32), yim + (xre * gim + xim * gre).astype(F32))

        zero = jnp.zeros((rc, cw), F32)
        yre, yim = lax.fori_loop(0, nb, body, (zero, zero), unroll=True)
        y_ref[r:r + rc, :] = yre
        y_ref[p + r:p + r + rc, :] = yim

    def edge(j, carry):
        y0, yn = carry
        d = i - j + nb - 1
        x0 = xs_ref[j, 0:16, :].astype(F32)
        xn = xs_ref[j, p:p + 16, :].astype(F32)
        g0 = gs_ref[d, 0:16, :].astype(F32)
        gn = gs_ref[d, p:p + 16, :].astype(F32)
        return y0 + x0 * g0, yn + xn * gn

    zero16 = jnp.zeros((16, cw), F32)
    y0, yn = lax.fori_loop(0, nb, edge, (zero16, zero16))
    y_ref[0:1, :] = y0[0:1, :]
    y_ref[p:p + 1, :] = yn[0:1, :]

    y = jnp.dot(inv_ref[...], y_ref[...].astype(BF16), preferred_element_type=F32)
    v = v_ref[...].astype(F32)
    o_ref[...] = (gate_ref[...].astype(F32) * (y + bias_ref[...] * v)).astype(o_ref.dtype)


def _specconv_call(xs, gs, inv, zc, bias, nseq, nb, p, v_col0, gate_col0, v_src=None, cw=256, rc=32,
                   out_dtype=F32):
    n_rows = nseq * nb * p
    xs4 = xs.reshape(nseq, nb, 2 * p, HY_DIM)
    if v_src is None:
        v_arr, v_spec = zc, pl.BlockSpec((p, cw), lambda c, s, i: (s * nb + i, v_col0 // cw + c))
    else:
        v_arr, v_spec = v_src, pl.BlockSpec((p, cw), lambda c, s, i: (s * nb + i, c))
    return pl.pallas_call(
        functools.partial(_specconv_kernel, nb=nb, p=p, rc=rc),
        grid=(HY_DIM // cw, nseq, nb),
        in_specs=[pl.BlockSpec((None, nb, 2 * p, cw), lambda c, s, i: (s, 0, 0, c)),
                  pl.BlockSpec((2 * nb - 1, 2 * p, cw), lambda c, s, i: (0, 0, c)),
                  pl.BlockSpec((p, 2 * p), lambda c, s, i: (0, 0)),
                  v_spec,
                  pl.BlockSpec((p, cw), lambda c, s, i: (s * nb + i, gate_col0 // cw + c)),
                  pl.BlockSpec((1, cw), lambda c, s, i: (0, c))],
        out_specs=pl.BlockSpec((p, cw), lambda c, s, i: (s * nb + i, c)),
        out_shape=jax.ShapeDtypeStruct((n_rows, HY_DIM), out_dtype),
        scratch_shapes=[pltpu.VMEM((2 * p, cw), F32)],
        compiler_params=_cparams("parallel", "parallel", "parallel"),
        name="hyena_spectral_conv",
    )(xs4, gs, inv, v_arr, zc, bias.reshape(1, HY_DIM))


def _hyena_group(z, row0, nseq, length, p, prm, sconv_rows):
    nb = length // p
    zc = _sconv_call(z, prm['hy_conv_w'], prm['hy_conv_b'], row0, nseq * length, length, sconv_rows)
    taps = _filter_call(length, prm['hf_w1'], prm['hf_b1'], prm['hf_w2'], prm['hf_b2'],
                        prm['hf_w3'], prm['hf_b3'], prm['hf_freq'], tr=min(512, length))
    fwd, inv = _dft_constants(p)
    gs = _filter_dft_call(taps, fwd, p, cw=512)
    cw = 512 if nb == 1 else 256
    vs = _dft_call(zc, fwd, p, 2 * HY_DIM, cw=512)
    u = _specconv_call(vs, gs[0], inv, zc, prm['hy_bias'][0], nseq, nb, p, 2 * HY_DIM, 0, cw=cw)
    us = _dft_call(u, fwd, p, 0, cw=512)
    return _specconv_call(us, gs[1], inv, zc, prm['hy_bias'][1], nseq, nb, p, 0, HY_DIM, v_src=u, cw=cw,
                          out_dtype=BF16)


def _mix_norm_kernel(ap_ref, as_ref, hp_ref, hs_ref, m_ref, g_ref, o_ref, *, ntp):
    i = pl.program_id(0)

    def fill(a_ref, h_ref):
        o_ref[:, :ATTN_DIM] = (_rms(a_ref[...].astype(F32)) * g_ref[:, :ATTN_DIM]).astype(BF16)
        o_ref[:, ATTN_DIM:ATTN_DIM + HY_DIM] = (
            _rms(h_ref[...].astype(F32)) * g_ref[:, ATTN_DIM:ATTN_DIM + HY_DIM]).astype(BF16)
        o_ref[:, ATTN_DIM + HY_DIM:] = (
            _rms(m_ref[...].astype(F32)) * g_ref[:, ATTN_DIM + HY_DIM:]).astype(BF16)

    @pl.when(i < ntp)
    def _():
        fill(ap_ref, hp_ref)

    @pl.when(i >= ntp)
    def _():
        fill(as_ref, hs_ref)


def _mix_norm_call(attn_p, attn_s, hy_p, hy_s, gm, g, t_prompt, tm=512):
    t = gm.shape[0]
    mix_dim = ATTN_DIM + HY_DIM + GM_DIM
    ntp = t_prompt // tm
    nts = t // tm - ntp
    prow = lambda i: (jnp.minimum(i, ntp - 1), 0)
    srow = lambda i: (jnp.clip(i - ntp, 0, nts - 1), 0)
    return pl.pallas_call(
        functools.partial(_mix_norm_kernel, ntp=ntp),
        grid=(t // tm,),
        in_specs=[pl.BlockSpec((tm, ATTN_DIM), prow),
                  pl.BlockSpec((tm, ATTN_DIM), srow),
                  pl.BlockSpec((tm, HY_DIM), prow),
                  pl.BlockSpec((tm, HY_DIM), srow),
                  pl.BlockSpec((tm, GM_DIM), lambda i: (i, 0)),
                  pl.BlockSpec((1, mix_dim), lambda i: (0, 0))],
        out_specs=pl.BlockSpec((tm, mix_dim), lambda i: (i, 0)),
        out_shape=jax.ShapeDtypeStruct((t, mix_dim), BF16),
        compiler_params=_cparams("parallel"),
        name="mix_group_norm",
    )(attn_p, attn_s, hy_p, hy_s, gm, g.reshape(1, mix_dim))


def _router_kernel(x_ref, g_ref, sh_ref, sc_ref, wr_ref, br_ref, h_ref, r_ref):
    y = _rms(x_ref[...]) * g_ref[...]
    h = y * (1.0 + sc_ref[...]) + sh_ref[...]
    h_ref[...] = _pack_bf16_pairs(h)
    w = wr_ref[...]
    w_hi = w.astype(BF16)
    w_lo = (w - w_hi.astype(F32)).astype(BF16)
    h_hi = h.astype(BF16)
    h_lo = (h - h_hi.astype(F32)).astype(BF16)
    both = jnp.dot(h_hi, jnp.concatenate([w_hi, w_lo], axis=1), preferred_element_type=F32)
    logits = (both[:, :LANES] + both[:, LANES:]) + jnp.dot(h_lo, w_hi, preferred_element_type=F32) + br_ref[...]
    col = lambda k: logits[:, k:k + 1]

    lg = [col(k) for k in range(N_GROUPS)]
    g_max = functools.reduce(jnp.maximum, lg)
    g_den = sum(jnp.exp(v - g_max) for v in lg)
    g_top = 1.0 / g_den
    g_idx = jnp.full(g_max.shape, N_GROUPS - 1, jnp.int32)
    for k in range(N_GROUPS - 2, -1, -1):
        g_idx = jnp.where(lg[k] == g_max, k, g_idx)

    le = []
    for e in range(EXPERTS_PER_GROUP):
        v = col(N_GROUPS + (N_GROUPS - 1) * EXPERTS_PER_GROUP + e)
        for k in range(N_GROUPS - 2, -1, -1):
            v = jnp.where(g_idx == k, col(N_GROUPS + k * EXPERTS_PER_GROUP + e), v)
        le.append(v)

    e_max = functools.reduce(jnp.maximum, le)
    e1 = jnp.full(e_max.shape, EXPERTS_PER_GROUP - 1, jnp.int32)
    for e in range(EXPERTS_PER_GROUP - 2, -1, -1):
        e1 = jnp.where(le[e] == e_max, e, e1)
    neg = jnp.float32(-jnp.inf)
    rest = [jnp.where(e1 == e, neg, le[e]) for e in range(EXPERTS_PER_GROUP)]
    e2_max = functools.reduce(jnp.maximum, rest)
    e2 = jnp.full(e_max.shape, EXPERTS_PER_GROUP - 1, jnp.int32)
    for e in range(EXPERTS_PER_GROUP - 2, -1, -1):
        e2 = jnp.where(rest[e] == e2_max, e, e2)
    ratio = jnp.exp(e2_max - e_max)
    w1 = g_top / (1.0 + ratio)
    w2 = g_top * ratio / (1.0 + ratio)
    id1 = (g_idx * EXPERTS_PER_GROUP + e1).astype(F32)
    id2 = (g_idx * EXPERTS_PER_GROUP + e2).astype(F32)

    lane = lax.broadcasted_iota(jnp.int32, r_ref.shape, 1)
    r_ref[...] = jnp.where(lane == 0, id1, jnp.where(lane == 1, id2, jnp.where(lane == 2, w1, jnp.where(lane == 3, w2, 0.0))))


def _router_call(x, g, mod, wr, br, t_prompt, dec_seq, tm=512):
    t, d = x.shape
    seg = lambda i: _seg_of_row(i * tm, t_prompt, dec_seq)
    return pl.pallas_call(
        _router_kernel,
        grid=(t // tm,),
        in_specs=[pl.BlockSpec((tm, d), lambda i: (i, 0)),
                  pl.BlockSpec((1, d), lambda i: (0, 0)),
                  pl.BlockSpec((None, 1, d), lambda i: (seg(i), 0, 3)),
                  pl.BlockSpec((None, 1, d), lambda i: (seg(i), 0, 4)),
                  pl.BlockSpec((d, LANES), lambda i: (0, 0)),
                  pl.BlockSpec((1, LANES), lambda i: (0, 0))],
        out_specs=[pl.BlockSpec((tm, d // 2), lambda i: (i, 0)),
                   pl.BlockSpec((tm, LANES), lambda i: (i, 0))],
        out_shape=[jax.ShapeDtypeStruct((t, d // 2), jnp.uint32),
                   jax.ShapeDtypeStruct((t, LANES), F32)],
        compiler_params=_cparams("parallel"),
        name="norm2_router",
    )(x, g.reshape(1, d), mod, mod, wr, br)


def _dispatch_kernel(dest_ref, zt_ref, h_ref, xs_ref, zero_ref, sem, zsem, *, tm, t, expert_tile):
    i = pl.program_id(0)

    @pl.when(i == 0)
    def _():
        zero_ref[...] = jnp.zeros(zero_ref.shape, zero_ref.dtype)

        def clear(q):
            row = pl.multiple_of(jnp.maximum(zt_ref[q], 0), expert_tile)
            return pltpu.make_async_copy(zero_ref, xs_ref.at[pl.ds(row, expert_tile), :], zsem)

        for q in range(2 * N_EXPERTS):
            @pl.when(zt_ref[q] >= 0)
            def _():
                clear(q).start()

        for q in range(2 * N_EXPERTS):
            @pl.when(zt_ref[q] >= 0)
            def _():
                clear(q).wait()

    base = i * tm

    def body(r, c):
        for k in range(2):
            row = dest_ref[k * t + base + r]
            pltpu.make_async_copy(h_ref.at[pl.ds(r, 1), :], xs_ref.at[pl.ds(row, 1), :], sem).start()
        return c

    lax.fori_loop(0, tm, body, 0, unroll=8)
    for k in range(2):
        pltpu.make_async_copy(h_ref, xs_ref.at[pl.ds(0, tm), :], sem).wait()


def _dispatch_call(dest, zero_tiles, h, n_rows, expert_tile, tm=256):
    t, d = h.shape
    return pl.pallas_call(
        functools.partial(_dispatch_kernel, tm=tm, t=t, expert_tile=expert_tile),
        grid_spec=pltpu.PrefetchScalarGridSpec(
            num_scalar_prefetch=2,
            grid=(t // tm,),
            in_specs=[pl.BlockSpec((tm, d), lambda i, dest, zt: (i, 0))],
            out_specs=pl.BlockSpec(memory_space=pl.ANY),
            scratch_shapes=[pltpu.VMEM((expert_tile, d), h.dtype),
                            pltpu.SemaphoreType.DMA(()),
                            pltpu.SemaphoreType.DMA(())]),
        out_shape=jax.ShapeDtypeStruct((n_rows, d), h.dtype),
        compiler_params=_cparams("arbitrary", disable_bounds_checks=True),
        name="moe_dispatch_rows",
    )(dest, zero_tiles, h)


def _stream_expert_weights(plan, sources, wbuf, targets, sem):
    te_ref, first_ref, run_ref, nxt_ref = plan
    m = pl.program_id(1)

    def copies(e, slot):
        return [pltpu.make_async_copy(src, wbuf.at[slot, k], sem.at[slot]) for k, src in enumerate(sources(e))]

    @pl.when(m == 0)
    def _():
        for cp in copies(te_ref[0], 0):
            cp.start()

    @pl.when(first_ref[m] == 1)
    def _():
        slot = run_ref[m] % 2
        for cp in copies(te_ref[m], slot):
            cp.wait()

        @pl.when(nxt_ref[m] >= 0)
        def _():
            for cp in copies(nxt_ref[m], 1 - slot):
                cp.start()

        for k, tgt in enumerate(targets):
            tgt[...] = wbuf[slot, k].astype(BF16)


def _expert_up_kernel(te_ref, nv_ref, first_ref, run_ref, nxt_ref, x_ref, wg_ref, wu_ref, o_ref,
                      wbuf, wgb_ref, wub_ref, sem, *, tm, tf, layer):
    valid = pl.program_id(1) * tm < nv_ref[0]
    cols = pl.ds(pl.multiple_of(pl.program_id(0) * tf, tf), tf)
    _stream_expert_weights((te_ref, first_ref, run_ref, nxt_ref),
                           lambda e: (wg_ref.at[layer, e, :, cols], wu_ref.at[layer, e, :, cols]),
                           wbuf, (wgb_ref, wub_ref), sem)

    @pl.when(valid)
    def _():
        x = _unpack_bf16_pairs(x_ref[...]).astype(BF16)
        hg = jnp.dot(x, wgb_ref[...], preferred_element_type=F32)
        hu = jnp.dot(x, wub_ref[...], preferred_element_type=F32)
        o_ref[...] = (hg * _sigmoid(hg) * hu).astype(o_ref.dtype)

    @pl.when(jnp.logical_not(valid))
    def _():
        o_ref[...] = jnp.zeros(o_ref.shape, o_ref.dtype)


def _expert_up_call(plan, xs, wg, wu, layer, tm, nf=2):
    n, dh = xs.shape
    d, ff = wg.shape[2], wg.shape[3]
    tf = ff // nf
    return pl.pallas_call(
        functools.partial(_expert_up_kernel, tm=tm, tf=tf, layer=layer),
        grid_spec=pltpu.PrefetchScalarGridSpec(
            num_scalar_prefetch=5,
            grid=(nf, n // tm),
            in_specs=[pl.BlockSpec((tm, dh), lambda f, m, *_: (m, 0)),
                      pl.BlockSpec(memory_space=pl.ANY),
                      pl.BlockSpec(memory_space=pl.ANY)],
            out_specs=pl.BlockSpec((tm, tf), lambda f, m, *_: (m, f)),
            scratch_shapes=[pltpu.VMEM((2, 2, d, tf), F32),
                            pltpu.VMEM((d, tf), BF16), pltpu.VMEM((d, tf), BF16),
                            pltpu.SemaphoreType.DMA((2,))]),
        out_shape=jax.ShapeDtypeStruct((n, ff), BF16),
        compiler_params=_cparams("arbitrary", "arbitrary"),
        name="moe_expert_up",
    )(*plan, xs, wg, wu)


def _expert_down_kernel(te_ref, nv_ref, first_ref, run_ref, nxt_ref, a_ref, wd_ref, o_ref,
                        wbuf, wdb_ref, sem, *, tm, layer):
    valid = pl.program_id(1) * tm < nv_ref[0]
    _stream_expert_weights((te_ref, first_ref, run_ref, nxt_ref), lambda e: (wd_ref.at[layer, e],),
                           wbuf, (wdb_ref,), sem)

    @pl.when(valid)
    def _():
        o_ref[...] = _pack_bf16_pairs(jnp.dot(a_ref[...], wdb_ref[...], preferred_element_type=F32))

    @pl.when(jnp.logical_not(valid))
    def _():
        o_ref[...] = jnp.zeros(o_ref.shape, o_ref.dtype)


def _expert_down_call(plan, a, wd, layer, tm):
    n, ff = a.shape
    d = wd.shape[3]
    return pl.pallas_call(
        functools.partial(_expert_down_kernel, tm=tm, layer=layer),
        grid_spec=pltpu.PrefetchScalarGridSpec(
            num_scalar_prefetch=5,
            grid=(1, n // tm),
            in_specs=[pl.BlockSpec((tm, ff), lambda c, m, *_: (m, 0)),
                      pl.BlockSpec(memory_space=pl.ANY)],
            out_specs=pl.BlockSpec((tm, d // 2), lambda c, m, *_: (m, 0)),
            scratch_shapes=[pltpu.VMEM((2, 1, ff, d), F32),
                            pltpu.VMEM((ff, d), BF16),
                            pltpu.SemaphoreType.DMA((2,))]),
        out_shape=jax.ShapeDtypeStruct((n, d // 2), jnp.uint32),
        compiler_params=_cparams("arbitrary", "arbitrary"),
        name="moe_expert_down",
    )(*plan, a, wd)


def _combine_kernel(dest_ref, x_ref, r_ref, gate_ref, fg_ref, ys_ref, *rest, tc, t, nt, ntp, final):
    out_refs, (ybuf, sem) = rest[:-2], rest[-2:]
    i = pl.program_id(0)

    def fetch(tile, slot):
        base = tile * tc

        def body(r, c):
            for k in range(2):
                row = dest_ref[k * t + base + r]
                pltpu.make_async_copy(ys_ref.at[pl.ds(row, 1), :], ybuf.at[slot, pl.ds(k * tc + r, 1), :],
                                      sem.at[slot]).start()
            return c

        lax.fori_loop(0, tc, body, 0, unroll=8)

    @pl.when(i == 0)
    def _():
        fetch(0, 0)

    @pl.when(i + 1 < nt)
    def _():
        fetch(i + 1, (i + 1) % 2)

    slot = i % 2
    pltpu.make_async_copy(ys_ref.at[pl.ds(0, 2 * tc), :], ybuf.at[slot], sem.at[slot]).wait()

    w1 = r_ref[:, 2:3]
    w2 = r_ref[:, 3:4]
    y = w1 * _unpack_bf16_pairs(ybuf[slot, 0:tc, :]) + w2 * _unpack_bf16_pairs(ybuf[slot, tc:2 * tc, :])
    x = x_ref[...] + gate_ref[...] * y
    if final:
        x = _rms(x) * fg_ref[...]
        op_ref, os_ref = out_refs

        @pl.when(i < ntp)
        def _():
            op_ref[...] = x

        @pl.when(i >= ntp)
        def _():
            os_ref[...] = x
    else:
        out_refs[0][...] = x


def _combine_call(dest, x, route, ys, mod, final_g, final, t_prompt, dec_seq, tc=128):
    t, d = x.shape
    seg = lambda i, *_: _seg_of_row(i * tc, t_prompt, dec_seq)
    nt = t // tc
    ntp = t_prompt // tc
    if final:
        out_specs = [pl.BlockSpec((tc, d), lambda i, dest: (jnp.minimum(i, ntp - 1), 0)),
                     pl.BlockSpec((tc, d), lambda i, dest: (jnp.maximum(i - ntp, 0), 0))]
        out_shape = [jax.ShapeDtypeStruct((t_prompt, d), F32), jax.ShapeDtypeStruct((t - t_prompt, d), F32)]
    else:
        out_specs = pl.BlockSpec((tc, d), lambda i, dest: (i, 0))
        out_shape = jax.ShapeDtypeStruct((t, d), F32)
    return pl.pallas_call(
        functools.partial(_combine_kernel, tc=tc, t=t, nt=nt, ntp=ntp, final=final),
        grid_spec=pltpu.PrefetchScalarGridSpec(
            num_scalar_prefetch=1,
            grid=(nt,),
            in_specs=[pl.BlockSpec((tc, d), lambda i, dest: (i, 0)),
                      pl.BlockSpec((tc, LANES), lambda i, dest: (i, 0)),
                      pl.BlockSpec((None, 1, d), lambda i, dest: (seg(i), 0, 5)),
                      pl.BlockSpec((1, d), lambda i, dest: (0, 0)),
                      pl.BlockSpec(memory_space=pl.ANY)],
            out_specs=out_specs,
            scratch_shapes=[pltpu.VMEM((2, 2 * tc, ys.shape[1]), ys.dtype),
                            pltpu.SemaphoreType.DMA((2,))]),
        out_shape=out_shape,
        compiler_params=_cparams("arbitrary", disable_bounds_checks=True),
        name="moe_combine_residual",
    )(dest, x, route, mod, final_g.reshape(1, d), ys)


def _dispatch_plan(route, tm):
    t = route.shape[0]
    flat_e = route[:, 0:2].astype(jnp.int32).T.reshape(-1)
    n_pairs = 2 * t
    n_tiles = n_pairs // tm + N_EXPERTS
    onehot = (flat_e[:, None] == jnp.arange(N_EXPERTS, dtype=jnp.int32)[None, :]).astype(jnp.int32)
    running = jnp.cumsum(onehot, axis=0)
    counts = running[-1]
    rank = jnp.sum((running - 1) * onehot, axis=1)
    padded = ((counts + tm - 1) // tm) * tm
    pad_end = jnp.cumsum(padded)
    pad_start = pad_end - padded
    dest = (jnp.sum(onehot * pad_start[None, :], axis=1) + rank).astype(jnp.int32)
    tile_start = jnp.arange(n_tiles, dtype=jnp.int32) * tm
    tile_e = jnp.sum((pad_end[None, :] <= tile_start[:, None]).astype(jnp.int32), axis=1)
    tile_e = jnp.minimum(tile_e, N_EXPERTS - 1).astype(jnp.int32)
    n_valid = pad_end[-1:].astype(jnp.int32)
    last = jnp.where(counts > 0, pad_end - tm, -1)
    tail = pad_end[-1] + jnp.arange(N_EXPERTS, dtype=jnp.int32) * tm
    tail = jnp.where(tail < n_tiles * tm, tail, -1)
    zero_tiles = jnp.concatenate([last, tail]).astype(jnp.int32)
    used = tile_start < pad_end[-1]
    first = jnp.logical_and(used, jnp.concatenate([jnp.ones((1,), bool), tile_e[1:] != tile_e[:-1]]))
    run = jnp.maximum(jnp.cumsum(first.astype(jnp.int32)) - 1, 0)
    ids = jnp.arange(N_EXPERTS, dtype=jnp.int32)
    later = jnp.logical_and(ids[None, :] > ids[:, None], (counts > 0)[None, :])
    next_e = jnp.min(jnp.where(later, ids[None, :], N_EXPERTS), axis=1)
    next_e = jnp.where(next_e == N_EXPERTS, -1, next_e)
    tile_plan = (tile_e, n_valid, first.astype(jnp.int32), run.astype(jnp.int32), next_e[tile_e].astype(jnp.int32))
    return dest, tile_plan, zero_tiles, n_tiles * tm


def _moe(h, route, wg, wu, wd, layer, tm=256):
    dest, tile_plan, zero_tiles, n_rows = _dispatch_plan(route, tm)
    xs = _dispatch_call(dest, zero_tiles, h, n_rows, tm)
    a = _expert_up_call(tile_plan, xs, wg, wu, layer, tm)
    ys = _expert_down_call(tile_plan, a, wd, layer, tm)
    return ys, dest


def _trunk(x_prompt, x_sample, cache_k, cache_v, c, c_ctx, prm, final_norm_g, hy_block_sample=512):
    batch, seq, d = x_prompt.shape
    dec_batch, dec_seq, _ = x_sample.shape
    depth = prm['w_in'].shape[0]
    past = cache_k.shape[2]
    t_prompt = batch * seq
    t_sample = dec_batch * dec_seq

    x = (x_prompt.reshape(t_prompt, d), x_sample.reshape(t_sample, d))
    n_cond = 1 + dec_batch
    cond = jnp.concatenate([c_ctx[None, :], c, jnp.zeros((8 - n_cond % 8, d), F32)], axis=0)
    mod_all = _mod_call(cond, prm['w_mod'], prm['b_mod'])
    cos, sin_signed = _rope_tables(dec_seq)
    sconv_rows_p = math.gcd(t_prompt, max(seq, 4096 // seq * seq))

    new_k, new_v = [], []
    big = ('w_mod', 'w_in', 'w_out', 'exp_w_gate', 'exp_w_up', 'exp_w_down')
    w_in, w_out = prm['w_in'].astype(BF16), prm['w_out'].astype(BF16)
    w_gate, w_up, w_down = prm['exp_w_gate'], prm['exp_w_up'], prm['exp_w_down']

    for l in range(depth):
        p = {name: w[l] for name, w in prm.items() if name not in big}
        mod = mod_all[l, :n_cond].reshape(n_cond, 1, 6 * d)

        h1 = _norm_mod_call(x, p['norm1_g'], mod, t_prompt, dec_seq)
        z = _proj_call(h1, w_in, l, "in_proj")

        attn_p, k_l, v_l = _attn_prompt_call(z, p['q_norm_g'], p['k_norm_g'], batch, seq)
        qr = _head_rope_call(z, 0, N_HEADS, cos, sin_signed, p['q_norm_g'], HEAD_DIM ** -0.5,
                             t_prompt, dec_batch, dec_seq)
        kr = _head_rope_call(z, COL_K, N_KV_HEADS, cos, sin_signed, p['k_norm_g'], 1.0,
                             t_prompt, dec_batch, dec_seq)
        attn_s = _attn_sample_call(z, qr, kr, cache_k[:, l].reshape(dec_batch, past, KV_DIM),
                                   cache_v[:, l].reshape(dec_batch, past, KV_DIM), t_prompt, dec_batch, dec_seq)
        new_k.append(k_l.reshape(batch, seq, N_KV_HEADS, HEAD_DIM))
        new_v.append(v_l.reshape(batch, seq, N_KV_HEADS, HEAD_DIM))

        hy_p = _hyena_group(z, 0, batch, seq, seq, p, sconv_rows_p)
        hy_s = _hyena_group(z, t_prompt, dec_batch, dec_seq, min(hy_block_sample, dec_seq), p, dec_seq)

        gm = _gmlp_call(z, p['gm_norm_g'], p['gm_norm_b'], p['gm_ws'], p['gm_bs'])

        mix = _mix_norm_call(attn_p, attn_s, hy_p, hy_s, gm, p['out_norm_g'], t_prompt)
        x = _proj_call(mix, w_out, l, "out_proj_residual", out_dtype=F32, residual=(x, mod, 2, t_prompt, dec_seq),
                       tn=512 if isinstance(x, tuple) else 1024)

        wr = jnp.concatenate([p['router_g_w'],
                              p['router_e_w'].transpose(1, 0, 2).reshape(d, N_EXPERTS)], axis=1)
        wr = jnp.pad(wr, ((0, 0), (0, LANES - wr.shape[1])))
        br = jnp.pad(jnp.concatenate([p['router_g_b'], p['router_e_b'].reshape(-1)]),
                     (0, LANES - N_GROUPS - N_EXPERTS)).reshape(1, LANES)
        h2, route = _router_call(x, p['norm2_g'], mod, wr, br, t_prompt, dec_seq)
        ys, dest = _moe(h2, route, w_gate, w_up, w_down, l)
        x = _combine_call(dest, x, route, ys, mod, final_norm_g, l == depth - 1, t_prompt, dec_seq)

    y_prompt = x[0].reshape(batch, seq, d)
    y_sample = x[1].reshape(dec_batch, dec_seq, d)
    return y_prompt, y_sample, jnp.stack(new_k, axis=1), jnp.stack(new_v, axis=1)


_PARAM_NAMES = ('norm1_g', 'norm2_g', 'w_mod', 'b_mod', 'w_in', 'q_norm_g', 'k_norm_g', 'hy_conv_w', 'hy_conv_b',
                'hf_w1', 'hf_b1', 'hf_w2', 'hf_b2', 'hf_w3', 'hf_b3', 'hf_freq', 'hy_bias', 'gm_norm_g',
                'gm_norm_b', 'gm_ws', 'gm_bs', 'out_norm_g', 'w_out', 'router_g_w', 'router_g_b', 'router_e_w',
                'router_e_b', 'exp_w_gate', 'exp_w_up', 'exp_w_down')


def kernel(x_prompt, x_sample, cache_k, cache_v, c, c_ctx, norm1_g, norm2_g, w_mod, b_mod, w_in, q_norm_g, k_norm_g, hy_conv_w, hy_conv_b, hf_w1, hf_b1, hf_w2, hf_b2, hf_w3, hf_b3, hf_freq, hy_bias, gm_norm_g, gm_norm_b, gm_ws, gm_bs, out_norm_g, w_out, router_g_w, router_g_b, router_e_w, router_e_b, exp_w_gate, exp_w_up, exp_w_down, final_norm_g):
    values = (norm1_g, norm2_g, w_mod, b_mod, w_in, q_norm_g, k_norm_g, hy_conv_w, hy_conv_b, hf_w1, hf_b1, hf_w2,
              hf_b2, hf_w3, hf_b3, hf_freq, hy_bias, gm_norm_g, gm_norm_b, gm_ws, gm_bs, out_norm_g, w_out,
              router_g_w, router_g_b, router_e_w, router_e_b, exp_w_gate, exp_w_up, exp_w_down)
    prm = dict(zip(_PARAM_NAMES, values))
    return _trunk(x_prompt, x_sample, cache_k, cache_v, c, c_ctx, prm, final_norm_g)
```

```python
import functools
import math

import numpy as np
import jax
import jax.numpy as jnp
from jax import lax
from jax.experimental import pallas as pl
from jax.experimental.pallas import tpu as pltpu

F32 = jnp.float32
BF16 = jnp.bfloat16

D_MODEL = 4096
GRID_W = 64
HEAD_DIM = 128
N_HEADS = 16
N_KV_HEADS = 4
Q_PER_KV = N_HEADS // N_KV_HEADS
ATTN_DIM = N_HEADS * HEAD_DIM
KV_DIM = N_KV_HEADS * HEAD_DIM
HY_DIM = 1024
GM_DIM = 1024
GM_HEADS = 8
CHUNK = 128
PROJ_DIM = ATTN_DIM + 2 * KV_DIM + 3 * HY_DIM + 2 * GM_DIM
ROPE_THETA = 10000.0
FILTER_EMB = 33
FILTER_HIDDEN = 64
DECAY_TARGET = 1e-2
FAST_DECAY_PCT = 0.3
SLOW_DECAY_PCT = 1.5
MOD_SHIFT = 0.05
N_GROUPS = 4
EXPERTS_PER_GROUP = 4
N_EXPERTS = 16
EXPERT_FF = 1024
EPS = 1e-6

COL_K = ATTN_DIM
COL_V = ATTN_DIM + KV_DIM
COL_HY = ATTN_DIM + 2 * KV_DIM
COL_GM = COL_HY + 3 * HY_DIM

LANES = 128
VMEM_LIMIT = 56 * 1024 * 1024
HI = lax.Precision.HIGHEST


def _cparams(*sem, **kw):
    return pltpu.CompilerParams(dimension_semantics=sem, vmem_limit_bytes=VMEM_LIMIT, **kw)


def _rms(x):
    return x * lax.rsqrt(jnp.mean(x * x, axis=-1, keepdims=True) + EPS)


def _sigmoid(x):
    return 1.0 / (1.0 + jnp.exp(-x))


def _dot_3pass(a, b):
    a_hi = a.astype(BF16)
    a_lo = (a - a_hi.astype(F32)).astype(BF16)
    b_hi = b.astype(BF16)
    b_lo = (b - b_hi.astype(F32)).astype(BF16)
    return (jnp.dot(a_hi, b_hi, preferred_element_type=F32) + jnp.dot(a_hi, b_lo, preferred_element_type=F32)
            + jnp.dot(a_lo, b_hi, preferred_element_type=F32))


def _pack_bf16_pairs(x):
    n = x.shape[1] // 2
    lo = lax.bitcast_convert_type(x[:, :n].astype(BF16).astype(F32), jnp.uint32)
    hi = lax.bitcast_convert_type(x[:, n:].astype(BF16).astype(F32), jnp.uint32)
    return hi | (lo >> 16)


def _unpack_bf16_pairs(u):
    lo = lax.bitcast_convert_type(u << 16, F32)
    hi = lax.bitcast_convert_type(u & jnp.uint32(0xFFFF0000), F32)
    return jnp.concatenate([lo, hi], axis=1)


def _seg_of_row(row, t_prompt, dec_seq):
    return jnp.where(row < t_prompt, 0, 1 + (row - t_prompt) // dec_seq)


def _mod_kernel(c_ref, w_ref, b_ref, o_ref):
    c = c_ref[...]
    s = (c * _sigmoid(c)).astype(BF16)
    o_ref[...] = jnp.dot(s, w_ref[...].astype(BF16), preferred_element_type=F32) + b_ref[...]


def _mod_call(cond, w_mod, b_mod):
    depth, d, n = w_mod.shape
    r = cond.shape[0]
    tn = 512
    return pl.pallas_call(
        _mod_kernel,
        grid=(depth, n // tn),
        in_specs=[pl.BlockSpec((r, d), lambda l, j: (0, 0)),
                  pl.BlockSpec((None, d, tn), lambda l, j: (l, 0, j)),
                  pl.BlockSpec((None, 1, tn), lambda l, j: (l, 0, j))],
        out_specs=pl.BlockSpec((None, r, tn), lambda l, j: (l, 0, j)),
        out_shape=jax.ShapeDtypeStruct((depth, r, n), F32),
        compiler_params=_cparams("parallel", "parallel"),
        name="adaln_mod",
    )(cond, w_mod, b_mod.reshape(depth, 1, n))


def _row_split_specs(x, tm, width, col):
    if not isinstance(x, tuple):
        return [x], [pl.BlockSpec((tm, width), lambda i, *r: (i, col(i, *r)))], None
    xp, xs = x
    ntp, nts = xp.shape[0] // tm, xs.shape[0] // tm
    return ([xp, xs],
            [pl.BlockSpec((tm, width), lambda i, *r: (jnp.minimum(i, ntp - 1), col(i, *r))),
             pl.BlockSpec((tm, width), lambda i, *r: (jnp.clip(i - ntp, 0, nts - 1), col(i, *r)))],
            ntp)


def _row_split_pick(refs, ntp):
    if len(refs) == 1:
        return refs[0][...]
    return jnp.where(pl.program_id(0) < ntp, refs[0][...], refs[1][...])


def _norm_mod_kernel(*refs, ntp):
    *x_refs, g_ref, sh_ref, sc_ref, o_ref = refs
    y = _rms(_row_split_pick(x_refs, ntp)) * g_ref[...]
    o_ref[...] = (y * (1.0 + sc_ref[...]) + sh_ref[...]).astype(o_ref.dtype)


def _norm_mod_call(x, g, mod, t_prompt, dec_seq, tm=512):
    d = g.shape[0]
    xs, x_specs, ntp = _row_split_specs(x, tm, d, lambda i: 0)
    t = sum(a.shape[0] for a in xs)
    seg = lambda i: _seg_of_row(i * tm, t_prompt, dec_seq)
    return pl.pallas_call(
        functools.partial(_norm_mod_kernel, ntp=ntp),
        grid=(t // tm,),
        in_specs=x_specs + [pl.BlockSpec((1, d), lambda i: (0, 0)),
                            pl.BlockSpec((None, 1, d), lambda i: (seg(i), 0, 0)),
                            pl.BlockSpec((None, 1, d), lambda i: (seg(i), 0, 1))],
        out_specs=pl.BlockSpec((tm, d), lambda i: (i, 0)),
        out_shape=jax.ShapeDtypeStruct((t, d), BF16),
        compiler_params=_cparams("parallel"),
        name="norm1_modulate",
    )(*xs, g.reshape(1, d), mod, mod)


def _proj_kernel(a_ref, w_ref, o_ref):
    o_ref[...] = jnp.dot(a_ref[...], w_ref[...], preferred_element_type=F32).astype(o_ref.dtype)


def _proj_residual_kernel(a_ref, w_ref, *refs, ntp):
    *x_refs, gate_ref, o_ref = refs
    y = jnp.dot(a_ref[...], w_ref[...], preferred_element_type=F32)
    o_ref[...] = _row_split_pick(x_refs, ntp) + gate_ref[...] * y


def _proj_call(a, w, layer, name, out_dtype=BF16, residual=None, tm=1024, tn=1024):
    t, k = a.shape
    n = w.shape[2]
    in_specs = [pl.BlockSpec((tm, k), lambda i, j: (i, 0)),
                pl.BlockSpec((None, k, tn), lambda i, j: (layer, 0, j))]
    args = [a, w]
    body = _proj_kernel
    if residual is not None:
        x, mod, gate_block, t_prompt, dec_seq = residual
        seg = lambda i: _seg_of_row(i * tm, t_prompt, dec_seq)
        xs, x_specs, ntp = _row_split_specs(x, tm, tn, lambda i, j: j)
        in_specs += x_specs + [pl.BlockSpec((None, 1, tn),
                                            lambda i, j: (seg(i), 0, gate_block * (n // tn) + j))]
        args += xs + [mod]
        body = functools.partial(_proj_residual_kernel, ntp=ntp)
    return pl.pallas_call(
        body,
        grid=(t // tm, n // tn),
        in_specs=in_specs,
        out_specs=pl.BlockSpec((tm, tn), lambda i, j: (i, j)),
        out_shape=jax.ShapeDtypeStruct((t, n), out_dtype),
        compiler_params=_cparams("parallel", "parallel"),
        name=name,
    )(*args)


def _attn_prompt_kernel(q_ref, k_ref, v_ref, qg_ref, kg_ref, o_ref, kc_ref, vc_ref):
    kn = _rms(k_ref[...].astype(F32)) * kg_ref[...]
    vb = v_ref[...]
    kc_ref[...] = kn
    vc_ref[...] = vb.astype(F32)
    kb = kn.astype(BF16)
    scale = HEAD_DIM ** -0.5
    for m in range(Q_PER_KV):
        sl = slice(m * HEAD_DIM, (m + 1) * HEAD_DIM)
        qn = (_rms(q_ref[:, sl].astype(F32)) * qg_ref[...] * scale).astype(BF16)
        s = lax.dot_general(qn, kb, (((1,), (1,)), ((), ())), preferred_element_type=F32)
        p = jnp.exp(s - jnp.max(s, axis=-1, keepdims=True))
        l = jnp.sum(p, axis=-1, keepdims=True)
        o = jnp.dot(p.astype(BF16), vb, preferred_element_type=F32)
        o_ref[:, sl] = (o / l).astype(o_ref.dtype)


def _attn_prompt_call(z, qg, kg, batch, seq):
    gw = Q_PER_KV * HEAD_DIM
    tp = batch * seq
    return pl.pallas_call(
        _attn_prompt_kernel,
        grid=(batch, N_KV_HEADS),
        in_specs=[pl.BlockSpec((seq, gw), lambda b, g: (b, g)),
                  pl.BlockSpec((seq, HEAD_DIM), lambda b, g: (b, COL_K // HEAD_DIM + g)),
                  pl.BlockSpec((seq, HEAD_DIM), lambda b, g: (b, COL_V // HEAD_DIM + g)),
                  pl.BlockSpec((1, HEAD_DIM), lambda b, g: (0, 0)),
                  pl.BlockSpec((1, HEAD_DIM), lambda b, g: (0, 0))],
        out_specs=[pl.BlockSpec((seq, gw), lambda b, g: (b, g)),
                   pl.BlockSpec((seq, HEAD_DIM), lambda b, g: (b, g)),
                   pl.BlockSpec((seq, HEAD_DIM), lambda b, g: (b, g))],
        out_shape=[jax.ShapeDtypeStruct((tp, ATTN_DIM), BF16),
                   jax.ShapeDtypeStruct((tp, KV_DIM), F32),
                   jax.ShapeDtypeStruct((tp, KV_DIM), F32)],
        compiler_params=_cparams("parallel", "parallel"),
        name="attn_context",
    )(z, z, z, qg.reshape(1, HEAD_DIM), kg.reshape(1, HEAD_DIM))


def _rope(x, cos, sin_signed):
    lane = lax.broadcasted_iota(jnp.int32, x.shape, 1)
    quarter = HEAD_DIM // 4
    fwd = pltpu.roll(x, HEAD_DIM - quarter, 1)
    bwd = pltpu.roll(x, quarter, 1)
    swapped = jnp.where((lane % (2 * quarter)) < quarter, fwd, bwd)
    return x * cos + swapped * sin_signed


def _head_rope_kernel(x_ref, c_ref, s_ref, g_ref, o_ref, *, scale):
    for h in range(x_ref.shape[1] // HEAD_DIM):
        sl = slice(h * HEAD_DIM, (h + 1) * HEAD_DIM)
        xn = _rms(x_ref[:, sl].astype(F32)) * g_ref[...]
        o_ref[:, sl] = (_rope(xn, c_ref[...], s_ref[...]) * scale).astype(o_ref.dtype)


def _head_rope_call(z, col0, n_heads, cos, sin_signed, g, scale, t_prompt, dec_batch, dec_seq, tr=512):
    nr = dec_seq // tr
    width = n_heads * HEAD_DIM
    return pl.pallas_call(
        functools.partial(_head_rope_kernel, scale=scale),
        grid=(dec_batch, nr),
        in_specs=[pl.BlockSpec((tr, width), lambda b, r: (t_prompt // tr + b * nr + r, col0 // width)),
                  pl.BlockSpec((tr, HEAD_DIM), lambda b, r: (r, 0)),
                  pl.BlockSpec((tr, HEAD_DIM), lambda b, r: (r, 0)),
                  pl.BlockSpec((1, HEAD_DIM), lambda b, r: (0, 0))],
        out_specs=pl.BlockSpec((tr, width), lambda b, r: (b * nr + r, 0)),
        out_shape=jax.ShapeDtypeStruct((dec_batch * dec_seq, width), BF16),
        compiler_params=_cparams("parallel", "parallel"),
        name="attn_head_rope",
    )(z, cos, sin_signed, g.reshape(1, HEAD_DIM))


def _attn_sample_kernel(q_ref, k_ref, v_ref, ck_ref, cv_ref, o_ref, m_ref, acc_ref, *, tk):
    m_ref[...] = jnp.full(m_ref.shape, -jnp.inf, F32)
    acc_ref[...] = jnp.zeros(acc_ref.shape, F32)

    def step(kb, vb):
        reps = kb.shape[0] // LANES
        v1 = jnp.concatenate([vb, jnp.ones(vb.shape, BF16)], axis=1)
        scores = [lax.dot_general(q_ref[:, m * HEAD_DIM:(m + 1) * HEAD_DIM], kb, (((1,), (1,)), ((), ())),
                                  preferred_element_type=F32) for m in range(Q_PER_KV)]
        for m in range(Q_PER_KV):
            s = scores[m]
            m_prev = m_ref[m]
            m_next = jnp.maximum(m_prev, jnp.max(s, axis=-1, keepdims=True))
            alpha = jnp.exp(m_prev - m_next)
            p = jnp.exp(s - jnp.concatenate([m_next] * reps, axis=1))
            acc_ref[m] = (jnp.concatenate([alpha, alpha], axis=1) * acc_ref[m]
                          + jnp.dot(p.astype(BF16), v1, preferred_element_type=F32))
            m_ref[m] = m_next

    step(ck_ref[...].astype(BF16), cv_ref[...].astype(BF16))

    def latent_tile(j, carry):
        rows = pl.ds(pl.multiple_of(j * tk, tk), tk)
        step(k_ref[rows, :], v_ref[rows, :])
        return carry

    lax.fori_loop(0, k_ref.shape[0] // tk, latent_tile, 0)

    for m in range(Q_PER_KV):
        o_ref[:, m * HEAD_DIM:(m + 1) * HEAD_DIM] = (
            acc_ref[m, :, :HEAD_DIM] / acc_ref[m, :, HEAD_DIM:]).astype(o_ref.dtype)


def _attn_sample_call(z, qr, kr, ctx_k, ctx_v, t_prompt, dec_batch, dec_seq, tq=512, tk=1024):
    gw = Q_PER_KV * HEAD_DIM
    past = ctx_k.shape[1]
    tk = min(tk, dec_seq)
    nq = dec_seq // tq
    return pl.pallas_call(
        functools.partial(_attn_sample_kernel, tk=tk),
        grid=(dec_batch, N_KV_HEADS, nq),
        in_specs=[pl.BlockSpec((tq, gw), lambda b, g, i: (b * nq + i, g)),
                  pl.BlockSpec((dec_seq, HEAD_DIM), lambda b, g, i: (b, g)),
                  pl.BlockSpec((dec_seq, HEAD_DIM),
                               lambda b, g, i: (t_prompt // dec_seq + b, COL_V // HEAD_DIM + g)),
                  pl.BlockSpec((None, past, HEAD_DIM), lambda b, g, i: (b, 0, g)),
                  pl.BlockSpec((None, past, HEAD_DIM), lambda b, g, i: (b, 0, g))],
        out_specs=pl.BlockSpec((tq, gw), lambda b, g, i: (b * nq + i, g)),
        out_shape=jax.ShapeDtypeStruct((dec_batch * dec_seq, ATTN_DIM), BF16),
        scratch_shapes=[pltpu.VMEM((Q_PER_KV, tq, LANES), F32),
                        pltpu.VMEM((Q_PER_KV, tq, 2 * HEAD_DIM), F32)],
        compiler_params=_cparams("parallel", "parallel", "parallel"),
        name="attn_latent",
    )(qr, kr, z, ctx_k, ctx_v)


def _rope_tables(n_tokens):
    quarter = HEAD_DIM // 4
    n_rows = n_tokens // GRID_W
    row = jnp.repeat(jnp.arange(n_rows, dtype=F32), GRID_W)
    col = jnp.tile(jnp.arange(GRID_W, dtype=F32), n_rows)
    freqs = ROPE_THETA ** (-jnp.arange(quarter, dtype=F32) / quarter)
    ar = row[:, None] * freqs[None, :]
    ac = col[:, None] * freqs[None, :]
    cos = jnp.concatenate([jnp.cos(ar), jnp.cos(ar), jnp.cos(ac), jnp.cos(ac)], axis=-1)
    sin = jnp.concatenate([-jnp.sin(ar), jnp.sin(ar), -jnp.sin(ac), jnp.sin(ac)], axis=-1)
    return cos, sin


def _gelu_tanh(x):
    return 0.5 * x * (1.0 + jnp.tanh(math.sqrt(2.0 / math.pi) * (x + 0.044715 * (x * x * x))))


def _gmlp_kernel(z_ref, g_ref, b_ref, ws_ref, bs_ref, o_ref):
    hw = GM_DIM // GM_HEADS
    for ch in range(z_ref.shape[0] // CHUNK):
        rows = slice(ch * CHUNK, (ch + 1) * CHUNK)
        u = _gelu_tanh(z_ref[rows, :GM_DIM].astype(F32))
        v = _gelu_tanh(z_ref[rows, GM_DIM:].astype(F32))
        mu = jnp.mean(v, axis=-1, keepdims=True)
        vc = v - mu
        var = jnp.mean(vc * vc, axis=-1, keepdims=True)
        vn = (vc * lax.rsqrt(var + EPS) * g_ref[...] + b_ref[...]).astype(BF16)
        for h in range(GM_HEADS):
            cols = slice(h * hw, (h + 1) * hw)
            s = jnp.dot(ws_ref[h], vn[:, cols], preferred_element_type=F32) + bs_ref[h]
            o_ref[rows, cols] = (u[:, cols] * s).astype(o_ref.dtype)


def _gmlp_call(z, g, b, ws, bs, tr=512):
    t = z.shape[0]
    return pl.pallas_call(
        _gmlp_kernel,
        grid=(t // tr,),
        in_specs=[pl.BlockSpec((tr, 2 * GM_DIM), lambda i: (i, COL_GM // (2 * GM_DIM))),
                  pl.BlockSpec((1, GM_DIM), lambda i: (0, 0)),
                  pl.BlockSpec((1, GM_DIM), lambda i: (0, 0)),
                  pl.BlockSpec((GM_HEADS, CHUNK, CHUNK), lambda i: (0, 0, 0)),
                  pl.BlockSpec((GM_HEADS, CHUNK, 1), lambda i: (0, 0, 0))],
        out_specs=pl.BlockSpec((tr, GM_DIM), lambda i: (i, 0)),
        out_shape=jax.ShapeDtypeStruct((t, GM_DIM), BF16),
        compiler_params=_cparams("parallel"),
        name="chunk_gmlp",
    )(z, g.reshape(1, GM_DIM), b.reshape(1, GM_DIM), ws.astype(BF16), bs.reshape(GM_HEADS, CHUNK, 1))


def _sconv_kernel(z_ref, w_ref, b_ref, o_ref, *, seg_len):
    x = z_ref[...].astype(F32)
    n = x.shape[0]
    pos = lax.broadcasted_iota(jnp.int32, x.shape, 0) % seg_len
    prev = jnp.where(pos == 0, 0.0, pltpu.roll(x, 1, 0))
    nxt = jnp.where(pos == seg_len - 1, 0.0, pltpu.roll(x, n - 1, 0))
    o_ref[...] = (b_ref[...] + prev * w_ref[0:1, :] + x * w_ref[1:2, :] + nxt * w_ref[2:3, :]).astype(o_ref.dtype)


def _sconv_call(z, w, b, row0, n_rows, seg_len, block_rows, cw=256):
    width = 3 * HY_DIM
    return pl.pallas_call(
        functools.partial(_sconv_kernel, seg_len=seg_len),
        grid=(n_rows // block_rows, width // cw),
        in_specs=[pl.BlockSpec((block_rows, cw), lambda i, c: (row0 // block_rows + i, COL_HY // cw + c)),
                  pl.BlockSpec((3, cw), lambda i, c: (0, c)),
                  pl.BlockSpec((1, cw), lambda i, c: (0, c))],
        out_specs=pl.BlockSpec((block_rows, cw), lambda i, c: (i, c)),
        out_shape=jax.ShapeDtypeStruct((n_rows, width), BF16),
        compiler_params=_cparams("parallel", "parallel"),
        name="hyena_short_conv",
    )(z, w, b.reshape(1, width))


def _filter_kernel(z_ref, t_ref, w1_ref, b1_ref, w2_ref, b2_ref, w3a_ref, b3a_ref, w3b_ref, b3b_ref,
                   fr_ref, dl_ref, o_ref, *, tr):
    fr = fr_ref[...]
    h = jnp.sin(fr * (jnp.dot(z_ref[...], w1_ref[...], precision=HI, preferred_element_type=F32) + b1_ref[...]))
    h = jnp.sin(fr * (jnp.dot(h, w2_ref[...], precision=HI, preferred_element_type=F32) + b2_ref[...]))
    decay = jnp.exp(-t_ref[...] * dl_ref[...]) + MOD_SHIFT
    row = lax.broadcasted_iota(jnp.int32, decay.shape, 0) + pl.program_id(0) * tr
    decay = jnp.where(row == 0, 0.0, decay)
    for o, (w3_ref, b3_ref) in enumerate(((w3a_ref, b3a_ref), (w3b_ref, b3b_ref))):
        taps = _dot_3pass(h, w3_ref[...]) + b3_ref[...]
        o_ref[:, o * HY_DIM:(o + 1) * HY_DIM] = taps * decay


def _filter_call(length, w1, b1, w2, b2, w3, b3, fr, tr=256):
    bands = (FILTER_EMB - 1) // 2
    t = np.linspace(0.0, 1.0, length)[:, None]
    wv = 2.0 * np.pi * np.arange(length)[:, None] / length
    f = np.linspace(1e-4, bands - 1, bands)[None, :]
    zf = np.concatenate([t, np.cos(f * wv), -np.sin(f * wv)], axis=-1)
    min_decay = math.log(DECAY_TARGET) / SLOW_DECAY_PCT
    max_decay = math.log(DECAY_TARGET) / FAST_DECAY_PCT
    deltas = np.abs(np.linspace(min_decay, max_decay, HY_DIM))[None, :]
    lag = np.minimum(np.abs(np.arange(2 * length) - length), length - 1)
    pad = LANES - FILTER_EMB
    hp = LANES - FILTER_HIDDEN
    z2 = jnp.asarray(np.pad(zf[lag], ((0, 0), (0, pad))).astype(np.float32))
    t2 = jnp.asarray(t[lag].astype(np.float32))
    dl = jnp.asarray(deltas.astype(np.float32))
    w1p = jnp.pad(w1, ((0, pad), (0, hp)))
    w2p = jnp.pad(w2, ((0, hp), (0, hp)))
    w3p = jnp.pad(w3, ((0, hp), (0, 0)))
    b1p = jnp.pad(b1, (0, hp)).reshape(1, LANES)
    b2p = jnp.pad(b2, (0, hp)).reshape(1, LANES)
    frp = jnp.pad(fr, (0, hp)).reshape(1, LANES)
    b3r = b3.reshape(1, -1)
    nblk = 2 * length // tr
    half = length // tr
    wcol = lambda r, o: o * 2 + jnp.where(r < half, 1, 0)
    small = pl.BlockSpec((1, LANES), lambda r: (0, 0))
    square = pl.BlockSpec((LANES, LANES), lambda r: (0, 0))
    return pl.pallas_call(
        functools.partial(_filter_kernel, tr=tr),
        grid=(nblk,),
        in_specs=[pl.BlockSpec((tr, LANES), lambda r: (r, 0)),
                  pl.BlockSpec((tr, 1), lambda r: (r, 0)),
                  square, small, square, small,
                  pl.BlockSpec((LANES, HY_DIM), lambda r: (0, wcol(r, 0))),
                  pl.BlockSpec((1, HY_DIM), lambda r: (0, wcol(r, 0))),
                  pl.BlockSpec((LANES, HY_DIM), lambda r: (0, wcol(r, 1))),
                  pl.BlockSpec((1, HY_DIM), lambda r: (0, wcol(r, 1))),
                  small,
                  pl.BlockSpec((1, HY_DIM), lambda r: (0, 0))],
        out_specs=pl.BlockSpec((tr, 2 * HY_DIM), lambda r: (r, 0)),
        out_shape=jax.ShapeDtypeStruct((2 * length, 2 * HY_DIM), F32),
        compiler_params=_cparams("parallel"),
        name="hyena_filter_mlp",
    )(z2, t2, w1p, b1p, w2p, b2p, w3p, b3r, w3p, b3r, frp, dl)


def _dft_constants(p):
    n = 2 * p
    idx = np.arange(p, dtype=np.float64)
    ang = 2.0 * np.pi * np.outer(idx, idx) / n
    re = np.cos(ang)
    im = -np.sin(ang)
    im[0, :] = np.cos(np.pi * idx)
    fwd = np.concatenate([re, im], axis=0)
    sign = np.where(np.arange(p) % 2 == 0, 1.0, -1.0)
    ar = (2.0 / n) * np.cos(ang)
    ar[:, 0] = 1.0 / n
    ai = -(2.0 / n) * np.sin(ang)
    ai[:, 0] = sign / n
    inv = np.concatenate([ar, ai], axis=1)
    as_bf16 = lambda a: jnp.asarray(a.astype(np.float32)).astype(BF16)
    return as_bf16(fwd), as_bf16(inv)


def _dft_kernel(f_ref, x_ref, o_ref):
    o_ref[...] = jnp.dot(f_ref[...], x_ref[...].astype(BF16), preferred_element_type=F32).astype(o_ref.dtype)


def _dft_call(x, fwd, p, col0, cw=256):
    nblk = x.shape[0] // p
    return pl.pallas_call(
        _dft_kernel,
        grid=(nblk, HY_DIM // cw),
        in_specs=[pl.BlockSpec((2 * p, p), lambda i, c: (0, 0)),
                  pl.BlockSpec((p, cw), lambda i, c: (i, col0 // cw + c))],
        out_specs=pl.BlockSpec((None, 2 * p, cw), lambda i, c: (i, 0, c)),
        out_shape=jax.ShapeDtypeStruct((nblk, 2 * p, HY_DIM), BF16),
        compiler_params=_cparams("parallel", "parallel"),
        name="hyena_block_dft",
    )(fwd, x)


def _filter_dft_kernel(f_ref, blk_ref, o_ref, tail_ref, *, p):
    b = pl.program_id(2)
    blk = blk_ref[...].astype(BF16)
    spec = jnp.dot(f_ref[...], blk, preferred_element_type=F32)

    @pl.when(b > 0)
    def _():
        o_ref[...] = (spec + tail_ref[...]).astype(o_ref.dtype)

    row = lax.broadcasted_iota(jnp.int32, spec.shape, 0)
    first = blk[0:1, :].astype(F32)
    real_row = jnp.logical_or(row < p, row == p)
    sign = (1 - 2 * (row % 2)).astype(F32)
    tail_ref[...] = sign * (spec - jnp.where(real_row, first, 0.0))


def _filter_dft_call(taps, fwd, p, cw=256):
    nb2 = taps.shape[0] // p
    nd = nb2 - 1
    ncb = HY_DIM // cw
    return pl.pallas_call(
        functools.partial(_filter_dft_kernel, p=p),
        grid=(2, ncb, nb2),
        in_specs=[pl.BlockSpec((2 * p, p), lambda o, c, b: (0, 0)),
                  pl.BlockSpec((p, cw), lambda o, c, b: (b, o * ncb + c))],
        out_specs=pl.BlockSpec((None, None, 2 * p, cw), lambda o, c, b: (o, jnp.maximum(b - 1, 0), 0, c)),
        out_shape=jax.ShapeDtypeStruct((2, nd, 2 * p, HY_DIM), BF16),
        scratch_shapes=[pltpu.VMEM((2 * p, cw), F32)],
        compiler_params=_cparams("parallel", "parallel", "arbitrary"),
        name="hyena_filter_dft",
    )(fwd, taps)


def _specconv_kernel(xs_ref, gs_ref, inv_ref, v_ref, gate_ref, bias_ref, o_ref, y_ref, *, nb, p, rc):
    i = pl.program_id(2)
    cw = o_ref.shape[1]

    for r in range(0, p, rc):
        def body(j, carry):
            yre, yim = carry
            d = i - j + nb - 1
            xre = xs_ref[j, r:r + rc, :]
            xim = xs_ref[j, p + r:p + r + rc, :]
            gre = gs_ref[d, r:r + rc, :]
            gim = gs_ref[d, p + r:p + r + rc, :]
            return (yre + (xre * gre - xim * gim).astype(F32), yim + (xre * gim + xim * gre).astype(F32))

        zero = jnp.zeros((rc, cw), F32)
        yre, yim = lax.fori_loop(0, nb, body, (zero, zero), unroll=True)
        y_ref[r:r + rc, :] = yre
        y_ref[p + r:p + r + rc, :] = yim

    def edge(j, carry):
        y0, yn = carry
        d = i - j + nb - 1
        x0 = xs_ref[j, 0:16, :].astype(F32)
        xn = xs_ref[j, p:p + 16, :].astype(F32)
        g0 = gs_ref[d, 0:16, :].astype(F32)
        gn = gs_ref[d, p:p + 16, :].astype(F32)
        return y0 + x0 * g0, yn + xn * gn

    zero16 = jnp.zeros((16, cw), F32)
    y0, yn = lax.fori_loop(0, nb, edge, (zero16, zero16))
    y_ref[0:1, :] = y0[0:1, :]
    y_ref[p:p + 1, :] = yn[0:1, :]

    y = jnp.dot(inv_ref[...], y_ref[...].astype(BF16), preferred_element_type=F32)
    v = v_ref[...].astype(F32)
    o_ref[...] = (gate_ref[...].astype(F32) * (y + bias_ref[...] * v)).astype(o_ref.dtype)


def _specconv_call(xs, gs, inv, zc, bias, nseq, nb, p, v_col0, gate_col0, v_src=None, cw=256, rc=32,
                   out_dtype=F32):
    n_rows = nseq * nb * p
    xs4 = xs.reshape(nseq, nb, 2 * p, HY_DIM)
    if v_src is None:
        v_arr, v_spec = zc, pl.BlockSpec((p, cw), lambda c, s, i: (s * nb + i, v_col0 // cw + c))
    else:
        v_arr, v_spec = v_src, pl.BlockSpec((p, cw), lambda c, s, i: (s * nb + i, c))
    return pl.pallas_call(
        functools.partial(_specconv_kernel, nb=nb, p=p, rc=rc),
        grid=(HY_DIM // cw, nseq, nb),
        in_specs=[pl.BlockSpec((None, nb, 2 * p, cw), lambda c, s, i: (s, 0, 0, c)),
                  pl.BlockSpec((2 * nb - 1, 2 * p, cw), lambda c, s, i: (0, 0, c)),
                  pl.BlockSpec((p, 2 * p), lambda c, s, i: (0, 0)),
                  v_spec,
                  pl.BlockSpec((p, cw), lambda c, s, i: (s * nb + i, gate_col0 // cw + c)),
                  pl.BlockSpec((1, cw), lambda c, s, i: (0, c))],
        out_specs=pl.BlockSpec((p, cw), lambda c, s, i: (s * nb + i, c)),
        out_shape=jax.ShapeDtypeStruct((n_rows, HY_DIM), out_dtype),
        scratch_shapes=[pltpu.VMEM((2 * p, cw), F32)],
        compiler_params=_cparams("parallel", "parallel", "parallel"),
        name="hyena_spectral_conv",
    )(xs4, gs, inv, v_arr, zc, bias.reshape(1, HY_DIM))


def _hyena_group(z, row0, nseq, length, p, prm, sconv_rows):
    nb = length // p
    zc = _sconv_call(z, prm['hy_conv_w'], prm['hy_conv_b'], row0, nseq * length, length, sconv_rows)
    taps = _filter_call(length, prm['hf_w1'], prm['hf_b1'], prm['hf_w2'], prm['hf_b2'],
                        prm['hf_w3'], prm['hf_b3'], prm['hf_freq'], tr=min(512, length))
    fwd, inv = _dft_constants(p)
    gs = _filter_dft_call(taps, fwd, p, cw=512)
    cw = 512 if nb == 1 else 256
    vs = _dft_call(zc, fwd, p, 2 * HY_DIM, cw=512)
    u = _specconv_call(vs, gs[0], inv, zc, prm['hy_bias'][0], nseq, nb, p, 2 * HY_DIM, 0, cw=cw)
    us = _dft_call(u, fwd, p, 0, cw=512)
    return _specconv_call(us, gs[1], inv, zc, prm['hy_bias'][1], nseq, nb, p, 0, HY_DIM, v_src=u, cw=cw,
                          out_dtype=BF16)


def _mix_norm_kernel(ap_ref, as_ref, hp_ref, hs_ref, m_ref, g_ref, o_ref, *, ntp):
    i = pl.program_id(0)

    def fill(a_ref, h_ref):
        o_ref[:, :ATTN_DIM] = (_rms(a_ref[...].astype(F32)) * g_ref[:, :ATTN_DIM]).astype(BF16)
        o_ref[:, ATTN_DIM:ATTN_DIM + HY_DIM] = (
            _rms(h_ref[...].astype(F32)) * g_ref[:, ATTN_DIM:ATTN_DIM + HY_DIM]).astype(BF16)
        o_ref[:, ATTN_DIM + HY_DIM:] = (
            _rms(m_ref[...].astype(F32)) * g_ref[:, ATTN_DIM + HY_DIM:]).astype(BF16)

    @pl.when(i < ntp)
    def _():
        fill(ap_ref, hp_ref)

    @pl.when(i >= ntp)
    def _():
        fill(as_ref, hs_ref)


def _mix_norm_call(attn_p, attn_s, hy_p, hy_s, gm, g, t_prompt, tm=512):
    t = gm.shape[0]
    mix_dim = ATTN_DIM + HY_DIM + GM_DIM
    ntp = t_prompt // tm
    nts = t // tm - ntp
    prow = lambda i: (jnp.minimum(i, ntp - 1), 0)
    srow = lambda i: (jnp.clip(i - ntp, 0, nts - 1), 0)
    return pl.pallas_call(
        functools.partial(_mix_norm_kernel, ntp=ntp),
        grid=(t // tm,),
        in_specs=[pl.BlockSpec((tm, ATTN_DIM), prow),
                  pl.BlockSpec((tm, ATTN_DIM), srow),
                  pl.BlockSpec((tm, HY_DIM), prow),
                  pl.BlockSpec((tm, HY_DIM), srow),
                  pl.BlockSpec((tm, GM_DIM), lambda i: (i, 0)),
                  pl.BlockSpec((1, mix_dim), lambda i: (0, 0))],
        out_specs=pl.BlockSpec((tm, mix_dim), lambda i: (i, 0)),
        out_shape=jax.ShapeDtypeStruct((t, mix_dim), BF16),
        compiler_params=_cparams("parallel"),
        name="mix_group_norm",
    )(attn_p, attn_s, hy_p, hy_s, gm, g.reshape(1, mix_dim))


def _router_kernel(x_ref, g_ref, sh_ref, sc_ref, wr_ref, br_ref, h_ref, r_ref):
    y = _rms(x_ref[...]) * g_ref[...]
    h = y * (1.0 + sc_ref[...]) + sh_ref[...]
    h_ref[...] = _pack_bf16_pairs(h)
    w = wr_ref[...]
    w_hi = w.astype(BF16)
    w_lo = (w - w_hi.astype(F32)).astype(BF16)
    h_hi = h.astype(BF16)
    h_lo = (h - h_hi.astype(F32)).astype(BF16)
    both = jnp.dot(h_hi, jnp.concatenate([w_hi, w_lo], axis=1), preferred_element_type=F32)
    logits = (both[:, :LANES] + both[:, LANES:]) + jnp.dot(h_lo, w_hi, preferred_element_type=F32) + br_ref[...]
    col = lambda k: logits[:, k:k + 1]

    lg = [col(k) for k in range(N_GROUPS)]
    g_max = functools.reduce(jnp.maximum, lg)
    g_den = sum(jnp.exp(v - g_max) for v in lg)
    g_top = 1.0 / g_den
    g_idx = jnp.full(g_max.shape, N_GROUPS - 1, jnp.int32)
    for k in range(N_GROUPS - 2, -1, -1):
        g_idx = jnp.where(lg[k] == g_max, k, g_idx)

    le = []
    for e in range(EXPERTS_PER_GROUP):
        v = col(N_GROUPS + (N_GROUPS - 1) * EXPERTS_PER_GROUP + e)
        for k in range(N_GROUPS - 2, -1, -1):
            v = jnp.where(g_idx == k, col(N_GROUPS + k * EXPERTS_PER_GROUP + e), v)
        le.append(v)

    e_max = functools.reduce(jnp.maximum, le)
    e1 = jnp.full(e_max.shape, EXPERTS_PER_GROUP - 1, jnp.int32)
    for e in range(EXPERTS_PER_GROUP - 2, -1, -1):
        e1 = jnp.where(le[e] == e_max, e, e1)
    neg = jnp.float32(-jnp.inf)
    rest = [jnp.where(e1 == e, neg, le[e]) for e in range(EXPERTS_PER_GROUP)]
    e2_max = functools.reduce(jnp.maximum, rest)
    e2 = jnp.full(e_max.shape, EXPERTS_PER_GROUP - 1, jnp.int32)
    for e in range(EXPERTS_PER_GROUP - 2, -1, -1):
        e2 = jnp.where(rest[e] == e2_max, e, e2)
    ratio = jnp.exp(e2_max - e_max)
    w1 = g_top / (1.0 + ratio)
    w2 = g_top * ratio / (1.0 + ratio)
    id1 = (g_idx * EXPERTS_PER_GROUP + e1).astype(F32)
    id2 = (g_idx * EXPERTS_PER_GROUP + e2).astype(F32)

    lane = lax.broadcasted_iota(jnp.int32, r_ref.shape, 1)
    r_ref[...] = jnp.where(lane == 0, id1, jnp.where(lane == 1, id2, jnp.where(lane == 2, w1, jnp.where(lane == 3, w2, 0.0))))


def _router_call(x, g, mod, wr, br, t_prompt, dec_seq, tm=512):
    t, d = x.shape
    seg = lambda i: _seg_of_row(i * tm, t_prompt, dec_seq)
    return pl.pallas_call(
        _router_kernel,
        grid=(t // tm,),
        in_specs=[pl.BlockSpec((tm, d), lambda i: (i, 0)),
                  pl.BlockSpec((1, d), lambda i: (0, 0)),
                  pl.BlockSpec((None, 1, d), lambda i: (seg(i), 0, 3)),
                  pl.BlockSpec((None, 1, d), lambda i: (seg(i), 0, 4)),
                  pl.BlockSpec((d, LANES), lambda i: (0, 0)),
                  pl.BlockSpec((1, LANES), lambda i: (0, 0))],
        out_specs=[pl.BlockSpec((tm, d // 2), lambda i: (i, 0)),
                   pl.BlockSpec((tm, LANES), lambda i: (i, 0))],
        out_shape=[jax.ShapeDtypeStruct((t, d // 2), jnp.uint32),
                   jax.ShapeDtypeStruct((t, LANES), F32)],
        compiler_params=_cparams("parallel"),
        name="norm2_router",
    )(x, g.reshape(1, d), mod, mod, wr, br)


def _dispatch_kernel(dest_ref, zt_ref, h_ref, xs_ref, zero_ref, sem, zsem, *, tm, t, expert_tile):
    i = pl.program_id(0)

    @pl.when(i == 0)
    def _():
        zero_ref[...] = jnp.zeros(zero_ref.shape, zero_ref.dtype)

        def clear(q):
            row = pl.multiple_of(jnp.maximum(zt_ref[q], 0), expert_tile)
            return pltpu.make_async_copy(zero_ref, xs_ref.at[pl.ds(row, expert_tile), :], zsem)

        for q in range(2 * N_EXPERTS):
            @pl.when(zt_ref[q] >= 0)
            def _():
                clear(q).start()

        for q in range(2 * N_EXPERTS):
            @pl.when(zt_ref[q] >= 0)
            def _():
                clear(q).wait()

    base = i * tm

    def body(r, c):
        for k in range(2):
            row = dest_ref[k * t + base + r]
            pltpu.make_async_copy(h_ref.at[pl.ds(r, 1), :], xs_ref.at[pl.ds(row, 1), :], sem).start()
        return c

    lax.fori_loop(0, tm, body, 0, unroll=8)
    for k in range(2):
        pltpu.make_async_copy(h_ref, xs_ref.at[pl.ds(0, tm), :], sem).wait()


def _dispatch_call(dest, zero_tiles, h, n_rows, expert_tile, tm=256):
    t, d = h.shape
    return pl.pallas_call(
        functools.partial(_dispatch_kernel, tm=tm, t=t, expert_tile=expert_tile),
        grid_spec=pltpu.PrefetchScalarGridSpec(
            num_scalar_prefetch=2,
            grid=(t // tm,),
            in_specs=[pl.BlockSpec((tm, d), lambda i, dest, zt: (i, 0))],
            out_specs=pl.BlockSpec(memory_space=pl.ANY),
            scratch_shapes=[pltpu.VMEM((expert_tile, d), h.dtype),
                            pltpu.SemaphoreType.DMA(()),
                            pltpu.SemaphoreType.DMA(())]),
        out_shape=jax.ShapeDtypeStruct((n_rows, d), h.dtype),
        compiler_params=_cparams("arbitrary", disable_bounds_checks=True),
        name="moe_dispatch_rows",
    )(dest, zero_tiles, h)


def _stream_expert_weights(plan, sources, wbuf, targets, sem):
    te_ref, first_ref, run_ref, nxt_ref = plan
    m = pl.program_id(1)

    def copies(e, slot):
        return [pltpu.make_async_copy(src, wbuf.at[slot, k], sem.at[slot]) for k, src in enumerate(sources(e))]

    @pl.when(m == 0)
    def _():
        for cp in copies(te_ref[0], 0):
            cp.start()

    @pl.when(first_ref[m] == 1)
    def _():
        slot = run_ref[m] % 2
        for cp in copies(te_ref[m], slot):
            cp.wait()

        @pl.when(nxt_ref[m] >= 0)
        def _():
            for cp in copies(nxt_ref[m], 1 - slot):
                cp.start()

        for k, tgt in enumerate(targets):
            tgt[...] = wbuf[slot, k].astype(BF16)


def _expert_up_kernel(te_ref, nv_ref, first_ref, run_ref, nxt_ref, x_ref, wg_ref, wu_ref, o_ref,
                      wbuf, wgb_ref, wub_ref, sem, *, tm, tf, layer):
    valid = pl.program_id(1) * tm < nv_ref[0]
    cols = pl.ds(pl.multiple_of(pl.program_id(0) * tf, tf), tf)
    _stream_expert_weights((te_ref, first_ref, run_ref, nxt_ref),
                           lambda e: (wg_ref.at[layer, e, :, cols], wu_ref.at[layer, e, :, cols]),
                           wbuf, (wgb_ref, wub_ref), sem)

    @pl.when(valid)
    def _():
        x = _unpack_bf16_pairs(x_ref[...]).astype(BF16)
        hg = jnp.dot(x, wgb_ref[...], preferred_element_type=F32)
        hu = jnp.dot(x, wub_ref[...], preferred_element_type=F32)
        o_ref[...] = (hg * _sigmoid(hg) * hu).astype(o_ref.dtype)

    @pl.when(jnp.logical_not(valid))
    def _():
        o_ref[...] = jnp.zeros(o_ref.shape, o_ref.dtype)


def _expert_up_call(plan, xs, wg, wu, layer, tm, nf=2):
    n, dh = xs.shape
    d, ff = wg.shape[2], wg.shape[3]
    tf = ff // nf
    return pl.pallas_call(
        functools.partial(_expert_up_kernel, tm=tm, tf=tf, layer=layer),
        grid_spec=pltpu.PrefetchScalarGridSpec(
            num_scalar_prefetch=5,
            grid=(nf, n // tm),
            in_specs=[pl.BlockSpec((tm, dh), lambda f, m, *_: (m, 0)),
                      pl.BlockSpec(memory_space=pl.ANY),
                      pl.BlockSpec(memory_space=pl.ANY)],
            out_specs=pl.BlockSpec((tm, tf), lambda f, m, *_: (m, f)),
            scratch_shapes=[pltpu.VMEM((2, 2, d, tf), F32),
                            pltpu.VMEM((d, tf), BF16), pltpu.VMEM((d, tf), BF16),
                            pltpu.SemaphoreType.DMA((2,))]),
        out_shape=jax.ShapeDtypeStruct((n, ff), BF16),
        compiler_params=_cparams("arbitrary", "arbitrary"),
        name="moe_expert_up",
    )(*plan, xs, wg, wu)


def _expert_down_kernel(te_ref, nv_ref, first_ref, run_ref, nxt_ref, a_ref, wd_ref, o_ref,
                        wbuf, wdb_ref, sem, *, tm, layer):
    valid = pl.program_id(1) * tm < nv_ref[0]
    _stream_expert_weights((te_ref, first_ref, run_ref, nxt_ref), lambda e: (wd_ref.at[layer, e],),
                           wbuf, (wdb_ref,), sem)

    @pl.when(valid)
    def _():
        o_ref[...] = _pack_bf16_pairs(jnp.dot(a_ref[...], wdb_ref[...], preferred_element_type=F32))

    @pl.when(jnp.logical_not(valid))
    def _():
        o_ref[...] = jnp.zeros(o_ref.shape, o_ref.dtype)


def _expert_down_call(plan, a, wd, layer, tm):
    n, ff = a.shape
    d = wd.shape[3]
    return pl.pallas_call(
        functools.partial(_expert_down_kernel, tm=tm, layer=layer),
        grid_spec=pltpu.PrefetchScalarGridSpec(
            num_scalar_prefetch=5,
            grid=(1, n // tm),
            in_specs=[pl.BlockSpec((tm, ff), lambda c, m, *_: (m, 0)),
                      pl.BlockSpec(memory_space=pl.ANY)],
            out_specs=pl.BlockSpec((tm, d // 2), lambda c, m, *_: (m, 0)),
            scratch_shapes=[pltpu.VMEM((2, 1, ff, d), F32),
                            pltpu.VMEM((ff, d), BF16),
                            pltpu.SemaphoreType.DMA((2,))]),
        out_shape=jax.ShapeDtypeStruct((n, d // 2), jnp.uint32),
        compiler_params=_cparams("arbitrary", "arbitrary"),
        name="moe_expert_down",
    )(*plan, a, wd)


def _combine_kernel(dest_ref, x_ref, r_ref, gate_ref, fg_ref, nsh_ref, nsc_ref, ys_ref, *rest,
                    tc, t, nt, ntp, final):
    out_refs, (ybuf, sem) = rest[:-2], rest[-2:]
    i = pl.program_id(0)

    def fetch(tile, slot):
        base = tile * tc

        def body(r, c):
            for k in range(2):
                row = dest_ref[k * t + base + r]
                pltpu.make_async_copy(ys_ref.at[pl.ds(row, 1), :], ybuf.at[slot, pl.ds(k * tc + r, 1), :],
                                      sem.at[slot]).start()
            return c

        lax.fori_loop(0, tc, body, 0, unroll=8)

    @pl.when(i == 0)
    def _():
        fetch(0, 0)

    @pl.when(i + 1 < nt)
    def _():
        fetch(i + 1, (i + 1) % 2)

    slot = i % 2
    pltpu.make_async_copy(ys_ref.at[pl.ds(0, 2 * tc), :], ybuf.at[slot], sem.at[slot]).wait()

    w1 = r_ref[:, 2:3]
    w2 = r_ref[:, 3:4]
    y = w1 * _unpack_bf16_pairs(ybuf[slot, 0:tc, :]) + w2 * _unpack_bf16_pairs(ybuf[slot, tc:2 * tc, :])
    x = x_ref[...] + gate_ref[...] * y
    if final:
        x = _rms(x) * fg_ref[...]
        op_ref, os_ref = out_refs

        @pl.when(i < ntp)
        def _():
            op_ref[...] = x

        @pl.when(i >= ntp)
        def _():
            os_ref[...] = x
    else:
        out_refs[0][...] = x
        out_refs[1][...] = (_rms(x) * fg_ref[...] * (1.0 + nsc_ref[...]) + nsh_ref[...]).astype(BF16)


def _combine_call(dest, x, route, ys, mod, norm_g, norm_mod, final, t_prompt, dec_seq, tc=128):
    t, d = x.shape
    seg = lambda i, *_: _seg_of_row(i * tc, t_prompt, dec_seq)
    nt = t // tc
    ntp = t_prompt // tc
    if final:
        out_specs = [pl.BlockSpec((tc, d), lambda i, dest: (jnp.minimum(i, ntp - 1), 0)),
                     pl.BlockSpec((tc, d), lambda i, dest: (jnp.maximum(i - ntp, 0), 0))]
        out_shape = [jax.ShapeDtypeStruct((t_prompt, d), F32), jax.ShapeDtypeStruct((t - t_prompt, d), F32)]
    else:
        out_specs = [pl.BlockSpec((tc, d), lambda i, dest: (i, 0)),
                     pl.BlockSpec((tc, d), lambda i, dest: (i, 0))]
        out_shape = [jax.ShapeDtypeStruct((t, d), F32), jax.ShapeDtypeStruct((t, d), BF16)]
    return pl.pallas_call(
        functools.partial(_combine_kernel, tc=tc, t=t, nt=nt, ntp=ntp, final=final),
        grid_spec=pltpu.PrefetchScalarGridSpec(
            num_scalar_prefetch=1,
            grid=(nt,),
            in_specs=[pl.BlockSpec((tc, d), lambda i, dest: (i, 0)),
                      pl.BlockSpec((tc, LANES), lambda i, dest: (i, 0)),
                      pl.BlockSpec((None, 1, d), lambda i, dest: (seg(i), 0, 5)),
                      pl.BlockSpec((1, d), lambda i, dest: (0, 0)),
                      pl.BlockSpec((None, 1, d), lambda i, dest: (seg(i), 0, 0)),
                      pl.BlockSpec((None, 1, d), lambda i, dest: (seg(i), 0, 1)),
                      pl.BlockSpec(memory_space=pl.ANY)],
            out_specs=out_specs,
            scratch_shapes=[pltpu.VMEM((2, 2 * tc, ys.shape[1]), ys.dtype),
                            pltpu.SemaphoreType.DMA((2,))]),
        out_shape=out_shape,
        compiler_params=_cparams("arbitrary", disable_bounds_checks=True),
        name="moe_combine_residual",
    )(dest, x, route, mod, norm_g.reshape(1, d), norm_mod, norm_mod, ys)


def _dispatch_plan(route, tm):
    t = route.shape[0]
    flat_e = route[:, 0:2].astype(jnp.int32).T.reshape(-1)
    n_pairs = 2 * t
    n_tiles = n_pairs // tm + N_EXPERTS
    onehot = (flat_e[:, None] == jnp.arange(N_EXPERTS, dtype=jnp.int32)[None, :]).astype(jnp.int32)
    running = jnp.cumsum(onehot, axis=0)
    counts = running[-1]
    rank = jnp.sum((running - 1) * onehot, axis=1)
    padded = ((counts + tm - 1) // tm) * tm
    pad_end = jnp.cumsum(padded)
    pad_start = pad_end - padded
    dest = (jnp.sum(onehot * pad_start[None, :], axis=1) + rank).astype(jnp.int32)
    tile_start = jnp.arange(n_tiles, dtype=jnp.int32) * tm
    tile_e = jnp.sum((pad_end[None, :] <= tile_start[:, None]).astype(jnp.int32), axis=1)
    tile_e = jnp.minimum(tile_e, N_EXPERTS - 1).astype(jnp.int32)
    n_valid = pad_end[-1:].astype(jnp.int32)
    last = jnp.where(counts > 0, pad_end - tm, -1)
    tail = pad_end[-1] + jnp.arange(N_EXPERTS, dtype=jnp.int32) * tm
    tail = jnp.where(tail < n_tiles * tm, tail, -1)
    zero_tiles = jnp.concatenate([last, tail]).astype(jnp.int32)
    used = tile_start < pad_end[-1]
    first = jnp.logical_and(used, jnp.concatenate([jnp.ones((1,), bool), tile_e[1:] != tile_e[:-1]]))
    run = jnp.maximum(jnp.cumsum(first.astype(jnp.int32)) - 1, 0)
    ids = jnp.arange(N_EXPERTS, dtype=jnp.int32)
    later = jnp.logical_and(ids[None, :] > ids[:, None], (counts > 0)[None, :])
    next_e = jnp.min(jnp.where(later, ids[None, :], N_EXPERTS), axis=1)
    next_e = jnp.where(next_e == N_EXPERTS, -1, next_e)
    tile_plan = (tile_e, n_valid, first.astype(jnp.int32), run.astype(jnp.int32), next_e[tile_e].astype(jnp.int32))
    return dest, tile_plan, zero_tiles, n_tiles * tm


def _moe(h, route, wg, wu, wd, layer, tm=256):
    dest, tile_plan, zero_tiles, n_rows = _dispatch_plan(route, tm)
    xs = _dispatch_call(dest, zero_tiles, h, n_rows, tm)
    a = _expert_up_call(tile_plan, xs, wg, wu, layer, tm)
    ys = _expert_down_call(tile_plan, a, wd, layer, tm)
    return ys, dest


def _trunk(x_prompt, x_sample, cache_k, cache_v, c, c_ctx, prm, final_norm_g, hy_block_sample=512):
    batch, seq, d = x_prompt.shape
    dec_batch, dec_seq, _ = x_sample.shape
    depth = prm['w_in'].shape[0]
    past = cache_k.shape[2]
    t_prompt = batch * seq
    t_sample = dec_batch * dec_seq

    x = (x_prompt.reshape(t_prompt, d), x_sample.reshape(t_sample, d))
    n_cond = 1 + dec_batch
    cond = jnp.concatenate([c_ctx[None, :], c, jnp.zeros((8 - n_cond % 8, d), F32)], axis=0)
    mod_all = _mod_call(cond, prm['w_mod'], prm['b_mod'])
    cos, sin_signed = _rope_tables(dec_seq)
    sconv_rows_p = math.gcd(t_prompt, max(seq, 4096 // seq * seq))

    new_k, new_v = [], []
    big = ('w_mod', 'w_in', 'w_out', 'exp_w_gate', 'exp_w_up', 'exp_w_down')
    w_in, w_out = prm['w_in'].astype(BF16), prm['w_out'].astype(BF16)
    w_gate, w_up, w_down = prm['exp_w_gate'], prm['exp_w_up'], prm['exp_w_down']

    for l in range(depth):
        p = {name: w[l] for name, w in prm.items() if name not in big}
        mod = mod_all[l, :n_cond].reshape(n_cond, 1, 6 * d)

        if l == 0:
            h1 = _norm_mod_call(x, p['norm1_g'], mod, t_prompt, dec_seq)
        z = _proj_call(h1, w_in, l, "in_proj")

        attn_p, k_l, v_l = _attn_prompt_call(z, p['q_norm_g'], p['k_norm_g'], batch, seq)
        qr = _head_rope_call(z, 0, N_HEADS, cos, sin_signed, p['q_norm_g'], HEAD_DIM ** -0.5,
                             t_prompt, dec_batch, dec_seq)
        kr = _head_rope_call(z, COL_K, N_KV_HEADS, cos, sin_signed, p['k_norm_g'], 1.0,
                             t_prompt, dec_batch, dec_seq)
        attn_s = _attn_sample_call(z, qr, kr, cache_k[:, l].reshape(dec_batch, past, KV_DIM),
                                   cache_v[:, l].reshape(dec_batch, past, KV_DIM), t_prompt, dec_batch, dec_seq)
        new_k.append(k_l.reshape(batch, seq, N_KV_HEADS, HEAD_DIM))
        new_v.append(v_l.reshape(batch, seq, N_KV_HEADS, HEAD_DIM))

        hy_p = _hyena_group(z, 0, batch, seq, seq, p, sconv_rows_p)
        hy_s = _hyena_group(z, t_prompt, dec_batch, dec_seq, min(hy_block_sample, dec_seq), p, dec_seq)

        gm = _gmlp_call(z, p['gm_norm_g'], p['gm_norm_b'], p['gm_ws'], p['gm_bs'])

        mix = _mix_norm_call(attn_p, attn_s, hy_p, hy_s, gm, p['out_norm_g'], t_prompt)
        x = _proj_call(mix, w_out, l, "out_proj_residual", out_dtype=F32, residual=(x, mod, 2, t_prompt, dec_seq),
                       tn=512 if isinstance(x, tuple) else 1024)

        wr = jnp.concatenate([p['router_g_w'],
                              p['router_e_w'].transpose(1, 0, 2).reshape(d, N_EXPERTS)], axis=1)
        wr = jnp.pad(wr, ((0, 0), (0, LANES - wr.shape[1])))
        br = jnp.pad(jnp.concatenate([p['router_g_b'], p['router_e_b'].reshape(-1)]),
                     (0, LANES - N_GROUPS - N_EXPERTS)).reshape(1, LANES)
        h2, route = _router_call(x, p['norm2_g'], mod, wr, br, t_prompt, dec_seq)
        ys, dest = _moe(h2, route, w_gate, w_up, w_down, l)
        if l == depth - 1:
            y_p, y_s = _combine_call(dest, x, route, ys, mod, final_norm_g, mod, True, t_prompt, dec_seq)
        else:
            next_mod = mod_all[l + 1, :n_cond].reshape(n_cond, 1, 6 * d)
            x, h1 = _combine_call(dest, x, route, ys, mod, prm['norm1_g'][l + 1], next_mod, False,
                                  t_prompt, dec_seq)

    y_prompt = y_p.reshape(batch, seq, d)
    y_sample = y_s.reshape(dec_batch, dec_seq, d)
    return y_prompt, y_sample, jnp.stack(new_k, axis=1), jnp.stack(new_v, axis=1)


_PARAM_NAMES = ('norm1_g', 'norm2_g', 'w_mod', 'b_mod', 'w_in', 'q_norm_g', 'k_norm_g', 'hy_conv_w', 'hy_conv_b',
                'hf_w1', 'hf_b1', 'hf_w2', 'hf_b2', 'hf_w3', 'hf_b3', 'hf_freq', 'hy_bias', 'gm_norm_g',
                'gm_norm_b', 'gm_ws', 'gm_bs', 'out_norm_g', 'w_out', 'router_g_w', 'router_g_b', 'router_e_w',
                'router_e_b', 'exp_w_gate', 'exp_w_up', 'exp_w_down')


def kernel(x_prompt, x_sample, cache_k, cache_v, c, c_ctx, norm1_g, norm2_g, w_mod, b_mod, w_in, q_norm_g, k_norm_g, hy_conv_w, hy_conv_b, hf_w1, hf_b1, hf_w2, hf_b2, hf_w3, hf_b3, hf_freq, hy_bias, gm_norm_g, gm_norm_b, gm_ws, gm_bs, out_norm_g, w_out, router_g_w, router_g_b, router_e_w, router_e_b, exp_w_gate, exp_w_up, exp_w_down, final_norm_g):
    values = (norm1_g, norm2_g, w_mod, b_mod, w_in, q_norm_g, k_norm_g, hy_conv_w, hy_conv_b, hf_w1, hf_b1, hf_w2,
              hf_b2, hf_w3, hf_b3, hf_freq, hy_bias, gm_norm_g, gm_norm_b, gm_ws, gm_bs, out_norm_g, w_out,
              router_g_w, router_g_b, router_e_w, router_e_b, exp_w_gate, exp_w_up, exp_w_down)
    prm = dict(zip(_PARAM_NAMES, values))
    return _trunk(x_prompt, x_sample, cache_k, cache_v, c, c_ctx, prm, final_norm_g)
```

```python
import functools
import math

import numpy as np
import jax
import jax.numpy as jnp
from jax import lax
from jax.experimental import pallas as pl
from jax.experimental.pallas import tpu as pltpu

F32 = jnp.float32
BF16 = jnp.bfloat16

D_MODEL = 4096
GRID_W = 64
HEAD_DIM = 128
N_HEADS = 16
N_KV_HEADS = 4
Q_PER_KV = N_HEADS // N_KV_HEADS
ATTN_DIM = N_HEADS * HEAD_DIM
KV_DIM = N_KV_HEADS * HEAD_DIM
HY_DIM = 1024
GM_DIM = 1024
GM_HEADS = 8
CHUNK = 128
PROJ_DIM = ATTN_DIM + 2 * KV_DIM + 3 * HY_DIM + 2 * GM_DIM
ROPE_THETA = 10000.0
FILTER_EMB = 33
FILTER_HIDDEN = 64
DECAY_TARGET = 1e-2
FAST_DECAY_PCT = 0.3
SLOW_DECAY_PCT = 1.5
MOD_SHIFT = 0.05
N_GROUPS = 4
EXPERTS_PER_GROUP = 4
N_EXPERTS = 16
EXPERT_FF = 1024
EPS = 1e-6

COL_K = ATTN_DIM
COL_V = ATTN_DIM + KV_DIM
COL_HY = ATTN_DIM + 2 * KV_DIM
COL_GM = COL_HY + 3 * HY_DIM

LANES = 128
VMEM_LIMIT = 56 * 1024 * 1024
HI = lax.Precision.HIGHEST


def _cparams(*sem, **kw):
    return pltpu.CompilerParams(dimension_semantics=sem, vmem_limit_bytes=VMEM_LIMIT, **kw)


def _rms(x):
    return x * lax.rsqrt(jnp.mean(x * x, axis=-1, keepdims=True) + EPS)


def _sigmoid(x):
    return 1.0 / (1.0 + jnp.exp(-x))


def _dot_3pass(a, b):
    a_hi = a.astype(BF16)
    a_lo = (a - a_hi.astype(F32)).astype(BF16)
    b_hi = b.astype(BF16)
    b_lo = (b - b_hi.astype(F32)).astype(BF16)
    return (jnp.dot(a_hi, b_hi, preferred_element_type=F32) + jnp.dot(a_hi, b_lo, preferred_element_type=F32)
            + jnp.dot(a_lo, b_hi, preferred_element_type=F32))


def _pack_bf16_pairs(x):
    n = x.shape[1] // 2
    lo = lax.bitcast_convert_type(x[:, :n].astype(BF16).astype(F32), jnp.uint32)
    hi = lax.bitcast_convert_type(x[:, n:].astype(BF16).astype(F32), jnp.uint32)
    return hi | (lo >> 16)


def _unpack_bf16_pairs(u):
    lo = lax.bitcast_convert_type(u << 16, F32)
    hi = lax.bitcast_convert_type(u & jnp.uint32(0xFFFF0000), F32)
    return jnp.concatenate([lo, hi], axis=1)


def _seg_of_row(row, t_prompt, dec_seq):
    return jnp.where(row < t_prompt, 0, 1 + (row - t_prompt) // dec_seq)


def _mod_kernel(c_ref, w_ref, b_ref, o_ref):
    c = c_ref[...]
    s = (c * _sigmoid(c)).astype(BF16)
    o_ref[...] = jnp.dot(s, w_ref[...].astype(BF16), preferred_element_type=F32) + b_ref[...]


def _mod_call(cond, w_mod, b_mod):
    depth, d, n = w_mod.shape
    r = cond.shape[0]
    tn = 512
    return pl.pallas_call(
        _mod_kernel,
        grid=(depth, n // tn),
        in_specs=[pl.BlockSpec((r, d), lambda l, j: (0, 0)),
                  pl.BlockSpec((None, d, tn), lambda l, j: (l, 0, j)),
                  pl.BlockSpec((None, 1, tn), lambda l, j: (l, 0, j))],
        out_specs=pl.BlockSpec((None, r, tn), lambda l, j: (l, 0, j)),
        out_shape=jax.ShapeDtypeStruct((depth, r, n), F32),
        compiler_params=_cparams("parallel", "parallel"),
        name="adaln_mod",
    )(cond, w_mod, b_mod.reshape(depth, 1, n))


def _row_split_specs(x, tm, width, col):
    if not isinstance(x, tuple):
        return [x], [pl.BlockSpec((tm, width), lambda i, *r: (i, col(i, *r)))], None
    xp, xs = x
    ntp, nts = xp.shape[0] // tm, xs.shape[0] // tm
    return ([xp, xs],
            [pl.BlockSpec((tm, width), lambda i, *r: (jnp.minimum(i, ntp - 1), col(i, *r))),
             pl.BlockSpec((tm, width), lambda i, *r: (jnp.clip(i - ntp, 0, nts - 1), col(i, *r)))],
            ntp)


def _row_split_pick(refs, ntp):
    if len(refs) == 1:
        return refs[0][...]
    return jnp.where(pl.program_id(0) < ntp, refs[0][...], refs[1][...])


def _norm_mod_kernel(*refs, ntp):
    *x_refs, g_ref, sh_ref, sc_ref, o_ref = refs
    y = _rms(_row_split_pick(x_refs, ntp)) * g_ref[...]
    o_ref[...] = (y * (1.0 + sc_ref[...]) + sh_ref[...]).astype(o_ref.dtype)


def _norm_mod_call(x, g, mod, t_prompt, dec_seq, tm=512):
    d = g.shape[0]
    xs, x_specs, ntp = _row_split_specs(x, tm, d, lambda i: 0)
    t = sum(a.shape[0] for a in xs)
    seg = lambda i: _seg_of_row(i * tm, t_prompt, dec_seq)
    return pl.pallas_call(
        functools.partial(_norm_mod_kernel, ntp=ntp),
        grid=(t // tm,),
        in_specs=x_specs + [pl.BlockSpec((1, d), lambda i: (0, 0)),
                            pl.BlockSpec((None, 1, d), lambda i: (seg(i), 0, 0)),
                            pl.BlockSpec((None, 1, d), lambda i: (seg(i), 0, 1))],
        out_specs=pl.BlockSpec((tm, d), lambda i: (i, 0)),
        out_shape=jax.ShapeDtypeStruct((t, d), BF16),
        compiler_params=_cparams("parallel"),
        name="norm1_modulate",
    )(*xs, g.reshape(1, d), mod, mod)


def _proj_kernel(a_ref, w_ref, o_ref):
    o_ref[...] = jnp.dot(a_ref[...], w_ref[...], preferred_element_type=F32).astype(o_ref.dtype)


def _proj_residual_kernel(a_ref, w_ref, *refs, ntp):
    *x_refs, gate_ref, o_ref = refs
    y = jnp.dot(a_ref[...], w_ref[...], preferred_element_type=F32)
    o_ref[...] = _row_split_pick(x_refs, ntp) + gate_ref[...] * y


def _proj_call(a, w, layer, name, out_dtype=BF16, residual=None, tm=1024, tn=1024):
    t, k = a.shape
    n = w.shape[2]
    in_specs = [pl.BlockSpec((tm, k), lambda i, j: (i, 0)),
                pl.BlockSpec((None, k, tn), lambda i, j: (layer, 0, j))]
    args = [a, w]
    body = _proj_kernel
    if residual is not None:
        x, mod, gate_block, t_prompt, dec_seq = residual
        seg = lambda i: _seg_of_row(i * tm, t_prompt, dec_seq)
        xs, x_specs, ntp = _row_split_specs(x, tm, tn, lambda i, j: j)
        in_specs += x_specs + [pl.BlockSpec((None, 1, tn),
                                            lambda i, j: (seg(i), 0, gate_block * (n // tn) + j))]
        args += xs + [mod]
        body = functools.partial(_proj_residual_kernel, ntp=ntp)
    return pl.pallas_call(
        body,
        grid=(t // tm, n // tn),
        in_specs=in_specs,
        out_specs=pl.BlockSpec((tm, tn), lambda i, j: (i, j)),
        out_shape=jax.ShapeDtypeStruct((t, n), out_dtype),
        compiler_params=_cparams("parallel", "parallel"),
        name=name,
    )(*args)


def _attn_prompt_kernel(q_ref, k_ref, v_ref, qg_ref, kg_ref, o_ref, kc_ref, vc_ref):
    kn = _rms(k_ref[...].astype(F32)) * kg_ref[...]
    vb = v_ref[...]
    kc_ref[...] = kn
    vc_ref[...] = vb.astype(F32)
    kb = kn.astype(BF16)
    scale = HEAD_DIM ** -0.5
    for m in range(Q_PER_KV):
        sl = slice(m * HEAD_DIM, (m + 1) * HEAD_DIM)
        qn = (_rms(q_ref[:, sl].astype(F32)) * qg_ref[...] * scale).astype(BF16)
        s = lax.dot_general(qn, kb, (((1,), (1,)), ((), ())), preferred_element_type=F32)
        p = jnp.exp(s - jnp.max(s, axis=-1, keepdims=True))
        l = jnp.sum(p, axis=-1, keepdims=True)
        o = jnp.dot(p.astype(BF16), vb, preferred_element_type=F32)
        o_ref[:, sl] = (o / l).astype(o_ref.dtype)


def _attn_prompt_call(z, qg, kg, batch, seq):
    gw = Q_PER_KV * HEAD_DIM
    tp = batch * seq
    return pl.pallas_call(
        _attn_prompt_kernel,
        grid=(batch, N_KV_HEADS),
        in_specs=[pl.BlockSpec((seq, gw), lambda b, g: (b, g)),
                  pl.BlockSpec((seq, HEAD_DIM), lambda b, g: (b, COL_K // HEAD_DIM + g)),
                  pl.BlockSpec((seq, HEAD_DIM), lambda b, g: (b, COL_V // HEAD_DIM + g)),
                  pl.BlockSpec((1, HEAD_DIM), lambda b, g: (0, 0)),
                  pl.BlockSpec((1, HEAD_DIM), lambda b, g: (0, 0))],
        out_specs=[pl.BlockSpec((seq, gw), lambda b, g: (b, g)),
                   pl.BlockSpec((seq, HEAD_DIM), lambda b, g: (b, g)),
                   pl.BlockSpec((seq, HEAD_DIM), lambda b, g: (b, g))],
        out_shape=[jax.ShapeDtypeStruct((tp, ATTN_DIM), BF16),
                   jax.ShapeDtypeStruct((tp, KV_DIM), F32),
                   jax.ShapeDtypeStruct((tp, KV_DIM), F32)],
        compiler_params=_cparams("parallel", "parallel"),
        name="attn_context",
    )(z, z, z, qg.reshape(1, HEAD_DIM), kg.reshape(1, HEAD_DIM))


def _rope(x, cos, sin_signed):
    lane = lax.broadcasted_iota(jnp.int32, x.shape, 1)
    quarter = HEAD_DIM // 4
    fwd = pltpu.roll(x, HEAD_DIM - quarter, 1)
    bwd = pltpu.roll(x, quarter, 1)
    swapped = jnp.where((lane % (2 * quarter)) < quarter, fwd, bwd)
    return x * cos + swapped * sin_signed


def _head_rope_kernel(x_ref, c_ref, s_ref, g_ref, o_ref, *, scale):
    for h in range(x_ref.shape[1] // HEAD_DIM):
        sl = slice(h * HEAD_DIM, (h + 1) * HEAD_DIM)
        xn = _rms(x_ref[:, sl].astype(F32)) * g_ref[...]
        o_ref[:, sl] = (_rope(xn, c_ref[...], s_ref[...]) * scale).astype(o_ref.dtype)


def _head_rope_call(z, col0, n_heads, cos, sin_signed, g, scale, t_prompt, dec_batch, dec_seq, tr=512):
    nr = dec_seq // tr
    width = n_heads * HEAD_DIM
    return pl.pallas_call(
        functools.partial(_head_rope_kernel, scale=scale),
        grid=(dec_batch, nr),
        in_specs=[pl.BlockSpec((tr, width), lambda b, r: (t_prompt // tr + b * nr + r, col0 // width)),
                  pl.BlockSpec((tr, HEAD_DIM), lambda b, r: (r, 0)),
                  pl.BlockSpec((tr, HEAD_DIM), lambda b, r: (r, 0)),
                  pl.BlockSpec((1, HEAD_DIM), lambda b, r: (0, 0))],
        out_specs=pl.BlockSpec((tr, width), lambda b, r: (b * nr + r, 0)),
        out_shape=jax.ShapeDtypeStruct((dec_batch * dec_seq, width), BF16),
        compiler_params=_cparams("parallel", "parallel"),
        name="attn_head_rope",
    )(z, cos, sin_signed, g.reshape(1, HEAD_DIM))


def _attn_sample_kernel(q_ref, k_ref, v_ref, ck_ref, cv_ref, o_ref, m_ref, acc_ref, *, tk):
    m_ref[...] = jnp.full(m_ref.shape, -jnp.inf, F32)
    acc_ref[...] = jnp.zeros(acc_ref.shape, F32)

    def step(kb, vb):
        reps = kb.shape[0] // LANES
        v1 = jnp.concatenate([vb, jnp.ones(vb.shape, BF16)], axis=1)
        scores = [lax.dot_general(q_ref[:, m * HEAD_DIM:(m + 1) * HEAD_DIM], kb, (((1,), (1,)), ((), ())),
                                  preferred_element_type=F32) for m in range(Q_PER_KV)]
        for m in range(Q_PER_KV):
            s = scores[m]
            m_prev = m_ref[m]
            m_next = jnp.maximum(m_prev, jnp.max(s, axis=-1, keepdims=True))
            alpha = jnp.exp(m_prev - m_next)
            p = jnp.exp(s - jnp.concatenate([m_next] * reps, axis=1))
            acc_ref[m] = (jnp.concatenate([alpha, alpha], axis=1) * acc_ref[m]
                          + jnp.dot(p.astype(BF16), v1, preferred_element_type=F32))
            m_ref[m] = m_next

    step(ck_ref[...].astype(BF16), cv_ref[...].astype(BF16))

    def latent_tile(j, carry):
        rows = pl.ds(pl.multiple_of(j * tk, tk), tk)
        step(k_ref[rows, :], v_ref[rows, :])
        return carry

    lax.fori_loop(0, k_ref.shape[0] // tk, latent_tile, 0)

    for m in range(Q_PER_KV):
        o_ref[:, m * HEAD_DIM:(m + 1) * HEAD_DIM] = (
            acc_ref[m, :, :HEAD_DIM] / acc_ref[m, :, HEAD_DIM:]).astype(o_ref.dtype)


def _attn_sample_call(z, qr, kr, ctx_k, ctx_v, t_prompt, dec_batch, dec_seq, tq=512, tk=1024):
    gw = Q_PER_KV * HEAD_DIM
    past = ctx_k.shape[1]
    tk = min(tk, dec_seq)
    nq = dec_seq // tq
    return pl.pallas_call(
        functools.partial(_attn_sample_kernel, tk=tk),
        grid=(dec_batch, N_KV_HEADS, nq),
        in_specs=[pl.BlockSpec((tq, gw), lambda b, g, i: (b * nq + i, g)),
                  pl.BlockSpec((dec_seq, HEAD_DIM), lambda b, g, i: (b, g)),
                  pl.BlockSpec((dec_seq, HEAD_DIM),
                               lambda b, g, i: (t_prompt // dec_seq + b, COL_V // HEAD_DIM + g)),
                  pl.BlockSpec((None, past, HEAD_DIM), lambda b, g, i: (b, 0, g)),
                  pl.BlockSpec((None, past, HEAD_DIM), lambda b, g, i: (b, 0, g))],
        out_specs=pl.BlockSpec((tq, gw), lambda b, g, i: (b * nq + i, g)),
        out_shape=jax.ShapeDtypeStruct((dec_batch * dec_seq, ATTN_DIM), BF16),
        scratch_shapes=[pltpu.VMEM((Q_PER_KV, tq, LANES), F32),
                        pltpu.VMEM((Q_PER_KV, tq, 2 * HEAD_DIM), F32)],
        compiler_params=_cparams("parallel", "parallel", "parallel"),
        name="attn_latent",
    )(qr, kr, z, ctx_k, ctx_v)


def _rope_tables(n_tokens):
    quarter = HEAD_DIM // 4
    n_rows = n_tokens // GRID_W
    row = jnp.repeat(jnp.arange(n_rows, dtype=F32), GRID_W)
    col = jnp.tile(jnp.arange(GRID_W, dtype=F32), n_rows)
    freqs = ROPE_THETA ** (-jnp.arange(quarter, dtype=F32) / quarter)
    ar = row[:, None] * freqs[None, :]
    ac = col[:, None] * freqs[None, :]
    cos = jnp.concatenate([jnp.cos(ar), jnp.cos(ar), jnp.cos(ac), jnp.cos(ac)], axis=-1)
    sin = jnp.concatenate([-jnp.sin(ar), jnp.sin(ar), -jnp.sin(ac), jnp.sin(ac)], axis=-1)
    return cos, sin


def _gelu_tanh(x):
    return 0.5 * x * (1.0 + jnp.tanh(math.sqrt(2.0 / math.pi) * (x + 0.044715 * (x * x * x))))


def _gmlp_kernel(z_ref, g_ref, b_ref, ws_ref, bs_ref, o_ref):
    hw = GM_DIM // GM_HEADS
    for ch in range(z_ref.shape[0] // CHUNK):
        rows = slice(ch * CHUNK, (ch + 1) * CHUNK)
        u = _gelu_tanh(z_ref[rows, :GM_DIM].astype(F32))
        v = _gelu_tanh(z_ref[rows, GM_DIM:].astype(F32))
        mu = jnp.mean(v, axis=-1, keepdims=True)
        vc = v - mu
        var = jnp.mean(vc * vc, axis=-1, keepdims=True)
        vn = (vc * lax.rsqrt(var + EPS) * g_ref[...] + b_ref[...]).astype(BF16)
        for h in range(GM_HEADS):
            cols = slice(h * hw, (h + 1) * hw)
            s = jnp.dot(ws_ref[h], vn[:, cols], preferred_element_type=F32) + bs_ref[h]
            o_ref[rows, cols] = (u[:, cols] * s).astype(o_ref.dtype)


def _gmlp_call(z, g, b, ws, bs, tr=512):
    t = z.shape[0]
    return pl.pallas_call(
        _gmlp_kernel,
        grid=(t // tr,),
        in_specs=[pl.BlockSpec((tr, 2 * GM_DIM), lambda i: (i, COL_GM // (2 * GM_DIM))),
                  pl.BlockSpec((1, GM_DIM), lambda i: (0, 0)),
                  pl.BlockSpec((1, GM_DIM), lambda i: (0, 0)),
                  pl.BlockSpec((GM_HEADS, CHUNK, CHUNK), lambda i: (0, 0, 0)),
                  pl.BlockSpec((GM_HEADS, CHUNK, 1), lambda i: (0, 0, 0))],
        out_specs=pl.BlockSpec((tr, GM_DIM), lambda i: (i, 0)),
        out_shape=jax.ShapeDtypeStruct((t, GM_DIM), BF16),
        compiler_params=_cparams("parallel"),
        name="chunk_gmlp",
    )(z, g.reshape(1, GM_DIM), b.reshape(1, GM_DIM), ws.astype(BF16), bs.reshape(GM_HEADS, CHUNK, 1))


def _sconv_kernel(z_ref, w_ref, b_ref, o_ref, *, seg_len):
    x = z_ref[...].astype(F32)
    n = x.shape[0]
    pos = lax.broadcasted_iota(jnp.int32, x.shape, 0) % seg_len
    prev = jnp.where(pos == 0, 0.0, pltpu.roll(x, 1, 0))
    nxt = jnp.where(pos == seg_len - 1, 0.0, pltpu.roll(x, n - 1, 0))
    o_ref[...] = (b_ref[...] + prev * w_ref[0:1, :] + x * w_ref[1:2, :] + nxt * w_ref[2:3, :]).astype(o_ref.dtype)


def _sconv_call(z, w, b, row0, n_rows, seg_len, block_rows, cw=256):
    width = 3 * HY_DIM
    return pl.pallas_call(
        functools.partial(_sconv_kernel, seg_len=seg_len),
        grid=(n_rows // block_rows, width // cw),
        in_specs=[pl.BlockSpec((block_rows, cw), lambda i, c: (row0 // block_rows + i, COL_HY // cw + c)),
                  pl.BlockSpec((3, cw), lambda i, c: (0, c)),
                  pl.BlockSpec((1, cw), lambda i, c: (0, c))],
        out_specs=pl.BlockSpec((block_rows, cw), lambda i, c: (i, c)),
        out_shape=jax.ShapeDtypeStruct((n_rows, width), BF16),
        compiler_params=_cparams("parallel", "parallel"),
        name="hyena_short_conv",
    )(z, w, b.reshape(1, width))


def _filter_kernel(z_ref, t_ref, w1_ref, b1_ref, w2_ref, b2_ref, w3a_ref, b3a_ref, w3b_ref, b3b_ref,
                   fr_ref, dl_ref, o_ref, *, tr):
    fr = fr_ref[...]
    h = jnp.sin(fr * (jnp.dot(z_ref[...], w1_ref[...], precision=HI, preferred_element_type=F32) + b1_ref[...]))
    h = jnp.sin(fr * (jnp.dot(h, w2_ref[...], precision=HI, preferred_element_type=F32) + b2_ref[...]))
    decay = jnp.exp(-t_ref[...] * dl_ref[...]) + MOD_SHIFT
    row = lax.broadcasted_iota(jnp.int32, decay.shape, 0) + pl.program_id(0) * tr
    decay = jnp.where(row == 0, 0.0, decay)
    for o, (w3_ref, b3_ref) in enumerate(((w3a_ref, b3a_ref), (w3b_ref, b3b_ref))):
        taps = _dot_3pass(h, w3_ref[...]) + b3_ref[...]
        o_ref[:, o * HY_DIM:(o + 1) * HY_DIM] = taps * decay


def _filter_call(length, w1, b1, w2, b2, w3, b3, fr, tr=256):
    bands = (FILTER_EMB - 1) // 2
    t = np.linspace(0.0, 1.0, length)[:, None]
    wv = 2.0 * np.pi * np.arange(length)[:, None] / length
    f = np.linspace(1e-4, bands - 1, bands)[None, :]
    zf = np.concatenate([t, np.cos(f * wv), -np.sin(f * wv)], axis=-1)
    min_decay = math.log(DECAY_TARGET) / SLOW_DECAY_PCT
    max_decay = math.log(DECAY_TARGET) / FAST_DECAY_PCT
    deltas = np.abs(np.linspace(min_decay, max_decay, HY_DIM))[None, :]
    lag = np.minimum(np.abs(np.arange(2 * length) - length), length - 1)
    pad = LANES - FILTER_EMB
    hp = LANES - FILTER_HIDDEN
    z2 = jnp.asarray(np.pad(zf[lag], ((0, 0), (0, pad))).astype(np.float32))
    t2 = jnp.asarray(t[lag].astype(np.float32))
    dl = jnp.asarray(deltas.astype(np.float32))
    w1p = jnp.pad(w1, ((0, pad), (0, hp)))
    w2p = jnp.pad(w2, ((0, hp), (0, hp)))
    w3p = jnp.pad(w3, ((0, hp), (0, 0)))
    b1p = jnp.pad(b1, (0, hp)).reshape(1, LANES)
    b2p = jnp.pad(b2, (0, hp)).reshape(1, LANES)
    frp = jnp.pad(fr, (0, hp)).reshape(1, LANES)
    b3r = b3.reshape(1, -1)
    nblk = 2 * length // tr
    half = length // tr
    wcol = lambda r, o: o * 2 + jnp.where(r < half, 1, 0)
    small = pl.BlockSpec((1, LANES), lambda r: (0, 0))
    square = pl.BlockSpec((LANES, LANES), lambda r: (0, 0))
    return pl.pallas_call(
        functools.partial(_filter_kernel, tr=tr),
        grid=(nblk,),
        in_specs=[pl.BlockSpec((tr, LANES), lambda r: (r, 0)),
                  pl.BlockSpec((tr, 1), lambda r: (r, 0)),
                  square, small, square, small,
                  pl.BlockSpec((LANES, HY_DIM), lambda r: (0, wcol(r, 0))),
                  pl.BlockSpec((1, HY_DIM), lambda r: (0, wcol(r, 0))),
                  pl.BlockSpec((LANES, HY_DIM), lambda r: (0, wcol(r, 1))),
                  pl.BlockSpec((1, HY_DIM), lambda r: (0, wcol(r, 1))),
                  small,
                  pl.BlockSpec((1, HY_DIM), lambda r: (0, 0))],
        out_specs=pl.BlockSpec((tr, 2 * HY_DIM), lambda r: (r, 0)),
        out_shape=jax.ShapeDtypeStruct((2 * length, 2 * HY_DIM), F32),
        compiler_params=_cparams("parallel"),
        name="hyena_filter_mlp",
    )(z2, t2, w1p, b1p, w2p, b2p, w3p, b3r, w3p, b3r, frp, dl)


def _dft_constants(p):
    n = 2 * p
    idx = np.arange(p, dtype=np.float64)
    ang = 2.0 * np.pi * np.outer(idx, idx) / n
    re = np.cos(ang)
    im = -np.sin(ang)
    im[0, :] = np.cos(np.pi * idx)
    fwd = np.concatenate([re, im], axis=0)
    sign = np.where(np.arange(p) % 2 == 0, 1.0, -1.0)
    ar = (2.0 / n) * np.cos(ang)
    ar[:, 0] = 1.0 / n
    ai = -(2.0 / n) * np.sin(ang)
    ai[:, 0] = sign / n
    inv = np.concatenate([ar, ai], axis=1)
    as_bf16 = lambda a: jnp.asarray(a.astype(np.float32)).astype(BF16)
    return as_bf16(fwd), as_bf16(inv)


def _dft_kernel(f_ref, x_ref, o_ref):
    o_ref[...] = jnp.dot(f_ref[...], x_ref[...].astype(BF16), preferred_element_type=F32).astype(o_ref.dtype)


def _dft_call(x, fwd, p, col0, cw=256):
    nblk = x.shape[0] // p
    return pl.pallas_call(
        _dft_kernel,
        grid=(nblk, HY_DIM // cw),
        in_specs=[pl.BlockSpec((2 * p, p), lambda i, c: (0, 0)),
                  pl.BlockSpec((p, cw), lambda i, c: (i, col0 // cw + c))],
        out_specs=pl.BlockSpec((None, 2 * p, cw), lambda i, c: (i, 0, c)),
        out_shape=jax.ShapeDtypeStruct((nblk, 2 * p, HY_DIM), BF16),
        compiler_params=_cparams("parallel", "parallel"),
        name="hyena_block_dft",
    )(fwd, x)


def _filter_dft_kernel(f_ref, blk_ref, o_ref, tail_ref, *, p):
    b = pl.program_id(2)
    blk = blk_ref[...].astype(BF16)
    spec = jnp.dot(f_ref[...], blk, preferred_element_type=F32)

    @pl.when(b > 0)
    def _():
        o_ref[...] = (spec + tail_ref[...]).astype(o_ref.dtype)

    row = lax.broadcasted_iota(jnp.int32, spec.shape, 0)
    first = blk[0:1, :].astype(F32)
    real_row = jnp.logical_or(row < p, row == p)
    sign = (1 - 2 * (row % 2)).astype(F32)
    tail_ref[...] = sign * (spec - jnp.where(real_row, first, 0.0))


def _filter_dft_call(taps, fwd, p, cw=256):
    nb2 = taps.shape[0] // p
    nd = nb2 - 1
    ncb = HY_DIM // cw
    return pl.pallas_call(
        functools.partial(_filter_dft_kernel, p=p),
        grid=(2, ncb, nb2),
        in_specs=[pl.BlockSpec((2 * p, p), lambda o, c, b: (0, 0)),
                  pl.BlockSpec((p, cw), lambda o, c, b: (b, o * ncb + c))],
        out_specs=pl.BlockSpec((None, None, 2 * p, cw), lambda o, c, b: (o, jnp.maximum(b - 1, 0), 0, c)),
        out_shape=jax.ShapeDtypeStruct((2, nd, 2 * p, HY_DIM), BF16),
        scratch_shapes=[pltpu.VMEM((2 * p, cw), F32)],
        compiler_params=_cparams("parallel", "parallel", "arbitrary"),
        name="hyena_filter_dft",
    )(fwd, taps)


def _specconv_kernel(xs_ref, gs_ref, inv_ref, v_ref, gate_ref, bias_ref, o_ref, y_ref, *, nb, p, rc):
    i = pl.program_id(2)
    cw = o_ref.shape[1]

    for r in range(0, p, rc):
        def body(j, carry):
            yre, yim = carry
            d = i - j + nb - 1
            xre = xs_ref[j, r:r + rc, :]
            xim = xs_ref[j, p + r:p + r + rc, :]
            gre = gs_ref[d, r:r + rc, :]
            gim = gs_ref[d, p + r:p + r + rc, :]
            return (yre + (xre * gre - xim * gim).astype(F32), yim + (xre * gim + xim * gre).astype(F32))

        zero = jnp.zeros((rc, cw), F32)
        yre, yim = lax.fori_loop(0, nb, body, (zero, zero), unroll=True)
        y_ref[r:r + rc, :] = yre
        y_ref[p + r:p + r + rc, :] = yim

    def edge(j, carry):
        y0, yn = carry
        d = i - j + nb - 1
        x0 = xs_ref[j, 0:16, :].astype(F32)
        xn = xs_ref[j, p:p + 16, :].astype(F32)
        g0 = gs_ref[d, 0:16, :].astype(F32)
        gn = gs_ref[d, p:p + 16, :].astype(F32)
        return y0 + x0 * g0, yn + xn * gn

    zero16 = jnp.zeros((16, cw), F32)
    y0, yn = lax.fori_loop(0, nb, edge, (zero16, zero16))
    y_ref[0:1, :] = y0[0:1, :]
    y_ref[p:p + 1, :] = yn[0:1, :]

    y = jnp.dot(inv_ref[...], y_ref[...].astype(BF16), preferred_element_type=F32)
    v = v_ref[...].astype(F32)
    o_ref[...] = (gate_ref[...].astype(F32) * (y + bias_ref[...] * v)).astype(o_ref.dtype)


def _specconv_call(xs, gs, inv, zc, bias, nseq, nb, p, v_col0, gate_col0, v_src=None, cw=256, rc=16,
                   out_dtype=F32):
    n_rows = nseq * nb * p
    xs4 = xs.reshape(nseq, nb, 2 * p, HY_DIM)
    if v_src is None:
        v_arr, v_spec = zc, pl.BlockSpec((p, cw), lambda c, s, i: (s * nb + i, v_col0 // cw + c))
    else:
        v_arr, v_spec = v_src, pl.BlockSpec((p, cw), lambda c, s, i: (s * nb + i, c))
    return pl.pallas_call(
        functools.partial(_specconv_kernel, nb=nb, p=p, rc=rc),
        grid=(HY_DIM // cw, nseq, nb),
        in_specs=[pl.BlockSpec((None, nb, 2 * p, cw), lambda c, s, i: (s, 0, 0, c)),
                  pl.BlockSpec((2 * nb - 1, 2 * p, cw), lambda c, s, i: (0, 0, c)),
                  pl.BlockSpec((p, 2 * p), lambda c, s, i: (0, 0)),
                  v_spec,
                  pl.BlockSpec((p, cw), lambda c, s, i: (s * nb + i, gate_col0 // cw + c)),
                  pl.BlockSpec((1, cw), lambda c, s, i: (0, c))],
        out_specs=pl.BlockSpec((p, cw), lambda c, s, i: (s * nb + i, c)),
        out_shape=jax.ShapeDtypeStruct((n_rows, HY_DIM), out_dtype),
        scratch_shapes=[pltpu.VMEM((2 * p, cw), F32)],
        compiler_params=_cparams("parallel", "parallel", "parallel"),
        name="hyena_spectral_conv",
    )(xs4, gs, inv, v_arr, zc, bias.reshape(1, HY_DIM))


def _hyena_group(z, row0, nseq, length, p, prm, sconv_rows):
    nb = length // p
    zc = _sconv_call(z, prm['hy_conv_w'], prm['hy_conv_b'], row0, nseq * length, length, sconv_rows)
    taps = _filter_call(length, prm['hf_w1'], prm['hf_b1'], prm['hf_w2'], prm['hf_b2'],
                        prm['hf_w3'], prm['hf_b3'], prm['hf_freq'], tr=min(512, length))
    fwd, inv = _dft_constants(p)
    gs = _filter_dft_call(taps, fwd, p, cw=512)
    cw = 512 if nb == 1 else 256
    vs = _dft_call(zc, fwd, p, 2 * HY_DIM, cw=512)
    u = _specconv_call(vs, gs[0], inv, zc, prm['hy_bias'][0], nseq, nb, p, 2 * HY_DIM, 0, cw=cw)
    us = _dft_call(u, fwd, p, 0, cw=512)
    return _specconv_call(us, gs[1], inv, zc, prm['hy_bias'][1], nseq, nb, p, 0, HY_DIM, v_src=u, cw=cw,
                          out_dtype=BF16)


def _mix_norm_kernel(ap_ref, as_ref, hp_ref, hs_ref, m_ref, g_ref, o_ref, *, ntp):
    i = pl.program_id(0)

    def fill(a_ref, h_ref):
        o_ref[:, :ATTN_DIM] = (_rms(a_ref[...].astype(F32)) * g_ref[:, :ATTN_DIM]).astype(BF16)
        o_ref[:, ATTN_DIM:ATTN_DIM + HY_DIM] = (
            _rms(h_ref[...].astype(F32)) * g_ref[:, ATTN_DIM:ATTN_DIM + HY_DIM]).astype(BF16)
        o_ref[:, ATTN_DIM + HY_DIM:] = (
            _rms(m_ref[...].astype(F32)) * g_ref[:, ATTN_DIM + HY_DIM:]).astype(BF16)

    @pl.when(i < ntp)
    def _():
        fill(ap_ref, hp_ref)

    @pl.when(i >= ntp)
    def _():
        fill(as_ref, hs_ref)


def _mix_norm_call(attn_p, attn_s, hy_p, hy_s, gm, g, t_prompt, tm=512):
    t = gm.shape[0]
    mix_dim = ATTN_DIM + HY_DIM + GM_DIM
    ntp = t_prompt // tm
    nts = t // tm - ntp
    prow = lambda i: (jnp.minimum(i, ntp - 1), 0)
    srow = lambda i: (jnp.clip(i - ntp, 0, nts - 1), 0)
    return pl.pallas_call(
        functools.partial(_mix_norm_kernel, ntp=ntp),
        grid=(t // tm,),
        in_specs=[pl.BlockSpec((tm, ATTN_DIM), prow),
                  pl.BlockSpec((tm, ATTN_DIM), srow),
                  pl.BlockSpec((tm, HY_DIM), prow),
                  pl.BlockSpec((tm, HY_DIM), srow),
                  pl.BlockSpec((tm, GM_DIM), lambda i: (i, 0)),
                  pl.BlockSpec((1, mix_dim), lambda i: (0, 0))],
        out_specs=pl.BlockSpec((tm, mix_dim), lambda i: (i, 0)),
        out_shape=jax.ShapeDtypeStruct((t, mix_dim), BF16),
        compiler_params=_cparams("parallel"),
        name="mix_group_norm",
    )(attn_p, attn_s, hy_p, hy_s, gm, g.reshape(1, mix_dim))


def _router_kernel(x_ref, g_ref, sh_ref, sc_ref, wr_ref, br_ref, h_ref, r_ref):
    y = _rms(x_ref[...]) * g_ref[...]
    h = y * (1.0 + sc_ref[...]) + sh_ref[...]
    h_ref[...] = _pack_bf16_pairs(h)
    w = wr_ref[...]
    w_hi = w.astype(BF16)
    w_lo = (w - w_hi.astype(F32)).astype(BF16)
    h_hi = h.astype(BF16)
    h_lo = (h - h_hi.astype(F32)).astype(BF16)
    both = jnp.dot(h_hi, jnp.concatenate([w_hi, w_lo], axis=1), preferred_element_type=F32)
    logits = (both[:, :LANES] + both[:, LANES:]) + jnp.dot(h_lo, w_hi, preferred_element_type=F32) + br_ref[...]
    col = lambda k: logits[:, k:k + 1]

    lg = [col(k) for k in range(N_GROUPS)]
    g_max = functools.reduce(jnp.maximum, lg)
    g_den = sum(jnp.exp(v - g_max) for v in lg)
    g_top = 1.0 / g_den
    g_idx = jnp.full(g_max.shape, N_GROUPS - 1, jnp.int32)
    for k in range(N_GROUPS - 2, -1, -1):
        g_idx = jnp.where(lg[k] == g_max, k, g_idx)

    le = []
    for e in range(EXPERTS_PER_GROUP):
        v = col(N_GROUPS + (N_GROUPS - 1) * EXPERTS_PER_GROUP + e)
        for k in range(N_GROUPS - 2, -1, -1):
            v = jnp.where(g_idx == k, col(N_GROUPS + k * EXPERTS_PER_GROUP + e), v)
        le.append(v)

    e_max = functools.reduce(jnp.maximum, le)
    e1 = jnp.full(e_max.shape, EXPERTS_PER_GROUP - 1, jnp.int32)
    for e in range(EXPERTS_PER_GROUP - 2, -1, -1):
        e1 = jnp.where(le[e] == e_max, e, e1)
    neg = jnp.float32(-jnp.inf)
    rest = [jnp.where(e1 == e, neg, le[e]) for e in range(EXPERTS_PER_GROUP)]
    e2_max = functools.reduce(jnp.maximum, rest)
    e2 = jnp.full(e_max.shape, EXPERTS_PER_GROUP - 1, jnp.int32)
    for e in range(EXPERTS_PER_GROUP - 2, -1, -1):
        e2 = jnp.where(rest[e] == e2_max, e, e2)
    ratio = jnp.exp(e2_max - e_max)
    w1 = g_top / (1.0 + ratio)
    w2 = g_top * ratio / (1.0 + ratio)
    id1 = (g_idx * EXPERTS_PER_GROUP + e1).astype(F32)
    id2 = (g_idx * EXPERTS_PER_GROUP + e2).astype(F32)

    lane = lax.broadcasted_iota(jnp.int32, r_ref.shape, 1)
    r_ref[...] = jnp.where(lane == 0, id1, jnp.where(lane == 1, id2, jnp.where(lane == 2, w1, jnp.where(lane == 3, w2, 0.0))))


def _router_call(x, g, mod, wr, br, t_prompt, dec_seq, tm=512):
    t, d = x.shape
    seg = lambda i: _seg_of_row(i * tm, t_prompt, dec_seq)
    return pl.pallas_call(
        _router_kernel,
        grid=(t // tm,),
        in_specs=[pl.BlockSpec((tm, d), lambda i: (i, 0)),
                  pl.BlockSpec((1, d), lambda i: (0, 0)),
                  pl.BlockSpec((None, 1, d), lambda i: (seg(i), 0, 3)),
                  pl.BlockSpec((None, 1, d), lambda i: (seg(i), 0, 4)),
                  pl.BlockSpec((d, LANES), lambda i: (0, 0)),
                  pl.BlockSpec((1, LANES), lambda i: (0, 0))],
        out_specs=[pl.BlockSpec((tm, d // 2), lambda i: (i, 0)),
                   pl.BlockSpec((tm, LANES), lambda i: (i, 0))],
        out_shape=[jax.ShapeDtypeStruct((t, d // 2), jnp.uint32),
                   jax.ShapeDtypeStruct((t, LANES), F32)],
        compiler_params=_cparams("parallel"),
        name="norm2_router",
    )(x, g.reshape(1, d), mod, mod, wr, br)


def _dispatch_kernel(dest_ref, zt_ref, h_ref, xs_ref, zero_ref, sem, zsem, *, tm, t, expert_tile):
    i = pl.program_id(0)

    @pl.when(i == 0)
    def _():
        zero_ref[...] = jnp.zeros(zero_ref.shape, zero_ref.dtype)

        def clear(q):
            row = pl.multiple_of(jnp.maximum(zt_ref[q], 0), expert_tile)
            return pltpu.make_async_copy(zero_ref, xs_ref.at[pl.ds(row, expert_tile), :], zsem)

        for q in range(2 * N_EXPERTS):
            @pl.when(zt_ref[q] >= 0)
            def _():
                clear(q).start()

        for q in range(2 * N_EXPERTS):
            @pl.when(zt_ref[q] >= 0)
            def _():
                clear(q).wait()

    base = i * tm

    def body(r, c):
        for k in range(2):
            row = dest_ref[k * t + base + r]
            pltpu.make_async_copy(h_ref.at[pl.ds(r, 1), :], xs_ref.at[pl.ds(row, 1), :], sem).start()
        return c

    lax.fori_loop(0, tm, body, 0, unroll=8)
    for k in range(2):
        pltpu.make_async_copy(h_ref, xs_ref.at[pl.ds(0, tm), :], sem).wait()


def _dispatch_call(dest, zero_tiles, h, n_rows, expert_tile, tm=256):
    t, d = h.shape
    return pl.pallas_call(
        functools.partial(_dispatch_kernel, tm=tm, t=t, expert_tile=expert_tile),
        grid_spec=pltpu.PrefetchScalarGridSpec(
            num_scalar_prefetch=2,
            grid=(t // tm,),
            in_specs=[pl.BlockSpec((tm, d), lambda i, dest, zt: (i, 0))],
            out_specs=pl.BlockSpec(memory_space=pl.ANY),
            scratch_shapes=[pltpu.VMEM((expert_tile, d), h.dtype),
                            pltpu.SemaphoreType.DMA(()),
                            pltpu.SemaphoreType.DMA(())]),
        out_shape=jax.ShapeDtypeStruct((n_rows, d), h.dtype),
        compiler_params=_cparams("arbitrary", disable_bounds_checks=True),
        name="moe_dispatch_rows",
    )(dest, zero_tiles, h)


def _stream_expert_weights(plan, sources, wbuf, targets, sem):
    te_ref, first_ref, run_ref, nxt_ref = plan
    m = pl.program_id(1)

    def copies(e, slot):
        return [pltpu.make_async_copy(src, wbuf.at[slot, k], sem.at[slot]) for k, src in enumerate(sources(e))]

    @pl.when(m == 0)
    def _():
        for cp in copies(te_ref[0], 0):
            cp.start()

    @pl.when(first_ref[m] == 1)
    def _():
        slot = run_ref[m] % 2
        for cp in copies(te_ref[m], slot):
            cp.wait()

        @pl.when(nxt_ref[m] >= 0)
        def _():
            for cp in copies(nxt_ref[m], 1 - slot):
                cp.start()

        for k, tgt in enumerate(targets):
            tgt[...] = wbuf[slot, k].astype(BF16)


def _expert_up_kernel(te_ref, nv_ref, first_ref, run_ref, nxt_ref, x_ref, wg_ref, wu_ref, o_ref,
                      wbuf, wgb_ref, wub_ref, sem, *, tm, tf, layer):
    valid = pl.program_id(1) * tm < nv_ref[0]
    cols = pl.ds(pl.multiple_of(pl.program_id(0) * tf, tf), tf)
    _stream_expert_weights((te_ref, first_ref, run_ref, nxt_ref),
                           lambda e: (wg_ref.at[layer, e, :, cols], wu_ref.at[layer, e, :, cols]),
                           wbuf, (wgb_ref, wub_ref), sem)

    @pl.when(valid)
    def _():
        x = _unpack_bf16_pairs(x_ref[...]).astype(BF16)
        hg = jnp.dot(x, wgb_ref[...], preferred_element_type=F32)
        hu = jnp.dot(x, wub_ref[...], preferred_element_type=F32)
        o_ref[...] = (hg * _sigmoid(hg) * hu).astype(o_ref.dtype)

    @pl.when(jnp.logical_not(valid))
    def _():
        o_ref[...] = jnp.zeros(o_ref.shape, o_ref.dtype)


def _expert_up_call(plan, xs, wg, wu, layer, tm, nf=2):
    n, dh = xs.shape
    d, ff = wg.shape[2], wg.shape[3]
    tf = ff // nf
    return pl.pallas_call(
        functools.partial(_expert_up_kernel, tm=tm, tf=tf, layer=layer),
        grid_spec=pltpu.PrefetchScalarGridSpec(
            num_scalar_prefetch=5,
            grid=(nf, n // tm),
            in_specs=[pl.BlockSpec((tm, dh), lambda f, m, *_: (m, 0)),
                      pl.BlockSpec(memory_space=pl.ANY),
                      pl.BlockSpec(memory_space=pl.ANY)],
            out_specs=pl.BlockSpec((tm, tf), lambda f, m, *_: (m, f)),
            scratch_shapes=[pltpu.VMEM((2, 2, d, tf), F32),
                            pltpu.VMEM((d, tf), BF16), pltpu.VMEM((d, tf), BF16),
                            pltpu.SemaphoreType.DMA((2,))]),
        out_shape=jax.ShapeDtypeStruct((n, ff), BF16),
        compiler_params=_cparams("arbitrary", "arbitrary"),
        name="moe_expert_up",
    )(*plan, xs, wg, wu)


def _expert_down_kernel(te_ref, nv_ref, first_ref, run_ref, nxt_ref, a_ref, wd_ref, o_ref,
                        wbuf, wdb_ref, sem, *, tm, layer):
    valid = pl.program_id(1) * tm < nv_ref[0]
    _stream_expert_weights((te_ref, first_ref, run_ref, nxt_ref), lambda e: (wd_ref.at[layer, e],),
                           wbuf, (wdb_ref,), sem)

    @pl.when(valid)
    def _():
        o_ref[...] = _pack_bf16_pairs(jnp.dot(a_ref[...], wdb_ref[...], preferred_element_type=F32))

    @pl.when(jnp.logical_not(valid))
    def _():
        o_ref[...] = jnp.zeros(o_ref.shape, o_ref.dtype)


def _expert_down_call(plan, a, wd, layer, tm):
    n, ff = a.shape
    d = wd.shape[3]
    return pl.pallas_call(
        functools.partial(_expert_down_kernel, tm=tm, layer=layer),
        grid_spec=pltpu.PrefetchScalarGridSpec(
            num_scalar_prefetch=5,
            grid=(1, n // tm),
            in_specs=[pl.BlockSpec((tm, ff), lambda c, m, *_: (m, 0)),
                      pl.BlockSpec(memory_space=pl.ANY)],
            out_specs=pl.BlockSpec((tm, d // 2), lambda c, m, *_: (m, 0)),
            scratch_shapes=[pltpu.VMEM((2, 1, ff, d), F32),
                            pltpu.VMEM((ff, d), BF16),
                            pltpu.SemaphoreType.DMA((2,))]),
        out_shape=jax.ShapeDtypeStruct((n, d // 2), jnp.uint32),
        compiler_params=_cparams("arbitrary", "arbitrary"),
        name="moe_expert_down",
    )(*plan, a, wd)


def _combine_kernel(dest_ref, x_ref, r_ref, gate_ref, fg_ref, nsh_ref, nsc_ref, ys_ref, *rest,
                    tc, t, nt, ntp, final):
    out_refs, (ybuf, sem) = rest[:-2], rest[-2:]
    i = pl.program_id(0)

    def fetch(tile, slot):
        base = tile * tc

        def body(r, c):
            for k in range(2):
                row = dest_ref[k * t + base + r]
                pltpu.make_async_copy(ys_ref.at[pl.ds(row, 1), :], ybuf.at[slot, pl.ds(k * tc + r, 1), :],
                                      sem.at[slot]).start()
            return c

        lax.fori_loop(0, tc, body, 0, unroll=8)

    @pl.when(i == 0)
    def _():
        fetch(0, 0)

    @pl.when(i + 1 < nt)
    def _():
        fetch(i + 1, (i + 1) % 2)

    slot = i % 2
    pltpu.make_async_copy(ys_ref.at[pl.ds(0, 2 * tc), :], ybuf.at[slot], sem.at[slot]).wait()

    w1 = r_ref[:, 2:3]
    w2 = r_ref[:, 3:4]
    y = w1 * _unpack_bf16_pairs(ybuf[slot, 0:tc, :]) + w2 * _unpack_bf16_pairs(ybuf[slot, tc:2 * tc, :])
    x = x_ref[...] + gate_ref[...] * y
    if final:
        x = _rms(x) * fg_ref[...]
        op_ref, os_ref = out_refs

        @pl.when(i < ntp)
        def _():
            op_ref[...] = x

        @pl.when(i >= ntp)
        def _():
            os_ref[...] = x
    else:
        out_refs[0][...] = x
        out_refs[1][...] = (_rms(x) * fg_ref[...] * (1.0 + nsc_ref[...]) + nsh_ref[...]).astype(BF16)


def _combine_call(dest, x, route, ys, mod, norm_g, norm_mod, final, t_prompt, dec_seq, tc=128):
    t, d = x.shape
    seg = lambda i, *_: _seg_of_row(i * tc, t_prompt, dec_seq)
    nt = t // tc
    ntp = t_prompt // tc
    if final:
        out_specs = [pl.BlockSpec((tc, d), lambda i, dest: (jnp.minimum(i, ntp - 1), 0)),
                     pl.BlockSpec((tc, d), lambda i, dest: (jnp.maximum(i - ntp, 0), 0))]
        out_shape = [jax.ShapeDtypeStruct((t_prompt, d), F32), jax.ShapeDtypeStruct((t - t_prompt, d), F32)]
    else:
        out_specs = [pl.BlockSpec((tc, d), lambda i, dest: (i, 0)),
                     pl.BlockSpec((tc, d), lambda i, dest: (i, 0))]
        out_shape = [jax.ShapeDtypeStruct((t, d), F32), jax.ShapeDtypeStruct((t, d), BF16)]
    return pl.pallas_call(
        functools.partial(_combine_kernel, tc=tc, t=t, nt=nt, ntp=ntp, final=final),
        grid_spec=pltpu.PrefetchScalarGridSpec(
            num_scalar_prefetch=1,
            grid=(nt,),
            in_specs=[pl.BlockSpec((tc, d), lambda i, dest: (i, 0)),
                      pl.BlockSpec((tc, LANES), lambda i, dest: (i, 0)),
                      pl.BlockSpec((None, 1, d), lambda i, dest: (seg(i), 0, 5)),
                      pl.BlockSpec((1, d), lambda i, dest: (0, 0)),
                      pl.BlockSpec((None, 1, d), lambda i, dest: (seg(i), 0, 0)),
                      pl.BlockSpec((None, 1, d), lambda i, dest: (seg(i), 0, 1)),
                      pl.BlockSpec(memory_space=pl.ANY)],
            out_specs=out_specs,
            scratch_shapes=[pltpu.VMEM((2, 2 * tc, ys.shape[1]), ys.dtype),
                            pltpu.SemaphoreType.DMA((2,))]),
        out_shape=out_shape,
        compiler_params=_cparams("arbitrary", disable_bounds_checks=True),
        name="moe_combine_residual",
    )(dest, x, route, mod, norm_g.reshape(1, d), norm_mod, norm_mod, ys)


def _dispatch_plan(route, tm):
    t = route.shape[0]
    flat_e = route[:, 0:2].astype(jnp.int32).T.reshape(-1)
    n_pairs = 2 * t
    n_tiles = n_pairs // tm + N_EXPERTS
    onehot = (flat_e[:, None] == jnp.arange(N_EXPERTS, dtype=jnp.int32)[None, :]).astype(jnp.int32)
    running = jnp.cumsum(onehot, axis=0)
    counts = running[-1]
    rank = jnp.sum((running - 1) * onehot, axis=1)
    padded = ((counts + tm - 1) // tm) * tm
    pad_end = jnp.cumsum(padded)
    pad_start = pad_end - padded
    dest = (jnp.sum(onehot * pad_start[None, :], axis=1) + rank).astype(jnp.int32)
    tile_start = jnp.arange(n_tiles, dtype=jnp.int32) * tm
    tile_e = jnp.sum((pad_end[None, :] <= tile_start[:, None]).astype(jnp.int32), axis=1)
    tile_e = jnp.minimum(tile_e, N_EXPERTS - 1).astype(jnp.int32)
    n_valid = pad_end[-1:].astype(jnp.int32)
    last = jnp.where(counts > 0, pad_end - tm, -1)
    tail = pad_end[-1] + jnp.arange(N_EXPERTS, dtype=jnp.int32) * tm
    tail = jnp.where(tail < n_tiles * tm, tail, -1)
    zero_tiles = jnp.concatenate([last, tail]).astype(jnp.int32)
    used = tile_start < pad_end[-1]
    first = jnp.logical_and(used, jnp.concatenate([jnp.ones((1,), bool), tile_e[1:] != tile_e[:-1]]))
    run = jnp.maximum(jnp.cumsum(first.astype(jnp.int32)) - 1, 0)
    ids = jnp.arange(N_EXPERTS, dtype=jnp.int32)
    later = jnp.logical_and(ids[None, :] > ids[:, None], (counts > 0)[None, :])
    next_e = jnp.min(jnp.where(later, ids[None, :], N_EXPERTS), axis=1)
    next_e = jnp.where(next_e == N_EXPERTS, -1, next_e)
    tile_plan = (tile_e, n_valid, first.astype(jnp.int32), run.astype(jnp.int32), next_e[tile_e].astype(jnp.int32))
    return dest, tile_plan, zero_tiles, n_tiles * tm


def _moe(h, route, wg, wu, wd, layer, tm=256):
    dest, tile_plan, zero_tiles, n_rows = _dispatch_plan(route, tm)
    xs = _dispatch_call(dest, zero_tiles, h, n_rows, tm)
    a = _expert_up_call(tile_plan, xs, wg, wu, layer, tm)
    ys = _expert_down_call(tile_plan, a, wd, layer, tm)
    return ys, dest


def _trunk(x_prompt, x_sample, cache_k, cache_v, c, c_ctx, prm, final_norm_g, hy_block_sample=512):
    batch, seq, d = x_prompt.shape
    dec_batch, dec_seq, _ = x_sample.shape
    depth = prm['w_in'].shape[0]
    past = cache_k.shape[2]
    t_prompt = batch * seq
    t_sample = dec_batch * dec_seq

    x = (x_prompt.reshape(t_prompt, d), x_sample.reshape(t_sample, d))
    n_cond = 1 + dec_batch
    cond = jnp.concatenate([c_ctx[None, :], c, jnp.zeros((8 - n_cond % 8, d), F32)], axis=0)
    mod_all = _mod_call(cond, prm['w_mod'], prm['b_mod'])
    cos, sin_signed = _rope_tables(dec_seq)
    sconv_rows_p = math.gcd(t_prompt, max(seq, 4096 // seq * seq))

    new_k, new_v = [], []
    big = ('w_mod', 'w_in', 'w_out', 'exp_w_gate', 'exp_w_up', 'exp_w_down')
    w_in, w_out = prm['w_in'].astype(BF16), prm['w_out'].astype(BF16)
    w_gate, w_up, w_down = prm['exp_w_gate'], prm['exp_w_up'], prm['exp_w_down']

    for l in range(depth):
        p = {name: w[l] for name, w in prm.items() if name not in big}
        mod = mod_all[l, :n_cond].reshape(n_cond, 1, 6 * d)

        if l == 0:
            h1 = _norm_mod_call(x, p['norm1_g'], mod, t_prompt, dec_seq)
        z = _proj_call(h1, w_in, l, "in_proj")

        attn_p, k_l, v_l = _attn_prompt_call(z, p['q_norm_g'], p['k_norm_g'], batch, seq)
        qr = _head_rope_call(z, 0, N_HEADS, cos, sin_signed, p['q_norm_g'], HEAD_DIM ** -0.5,
                             t_prompt, dec_batch, dec_seq)
        kr = _head_rope_call(z, COL_K, N_KV_HEADS, cos, sin_signed, p['k_norm_g'], 1.0,
                             t_prompt, dec_batch, dec_seq)
        attn_s = _attn_sample_call(z, qr, kr, cache_k[:, l].reshape(dec_batch, past, KV_DIM),
                                   cache_v[:, l].reshape(dec_batch, past, KV_DIM), t_prompt, dec_batch, dec_seq)
        new_k.append(k_l.reshape(batch, seq, N_KV_HEADS, HEAD_DIM))
        new_v.append(v_l.reshape(batch, seq, N_KV_HEADS, HEAD_DIM))

        hy_p = _hyena_group(z, 0, batch, seq, seq, p, sconv_rows_p)
        hy_s = _hyena_group(z, t_prompt, dec_batch, dec_seq, min(hy_block_sample, dec_seq), p, dec_seq)

        gm = _gmlp_call(z, p['gm_norm_g'], p['gm_norm_b'], p['gm_ws'], p['gm_bs'])

        mix = _mix_norm_call(attn_p, attn_s, hy_p, hy_s, gm, p['out_norm_g'], t_prompt)
        x = _proj_call(mix, w_out, l, "out_proj_residual", out_dtype=F32, residual=(x, mod, 2, t_prompt, dec_seq),
                       tn=512 if isinstance(x, tuple) else 1024)

        wr = jnp.concatenate([p['router_g_w'],
                              p['router_e_w'].transpose(1, 0, 2).reshape(d, N_EXPERTS)], axis=1)
        wr = jnp.pad(wr, ((0, 0), (0, LANES - wr.shape[1])))
        br = jnp.pad(jnp.concatenate([p['router_g_b'], p['router_e_b'].reshape(-1)]),
                     (0, LANES - N_GROUPS - N_EXPERTS)).reshape(1, LANES)
        h2, route = _router_call(x, p['norm2_g'], mod, wr, br, t_prompt, dec_seq)
        ys, dest = _moe(h2, route, w_gate, w_up, w_down, l)
        if l == depth - 1:
            y_p, y_s = _combine_call(dest, x, route, ys, mod, final_norm_g, mod, True, t_prompt, dec_seq)
        else:
            next_mod = mod_all[l + 1, :n_cond].reshape(n_cond, 1, 6 * d)
            x, h1 = _combine_call(dest, x, route, ys, mod, prm['norm1_g'][l + 1], next_mod, False,
                                  t_prompt, dec_seq)

    y_prompt = y_p.reshape(batch, seq, d)
    y_sample = y_s.reshape(dec_batch, dec_seq, d)
    return y_prompt, y_sample, jnp.stack(new_k, axis=1), jnp.stack(new_v, axis=1)


_PARAM_NAMES = ('norm1_g', 'norm2_g', 'w_mod', 'b_mod', 'w_in', 'q_norm_g', 'k_norm_g', 'hy_conv_w', 'hy_conv_b',
                'hf_w1', 'hf_b1', 'hf_w2', 'hf_b2', 'hf_w3', 'hf_b3', 'hf_freq', 'hy_bias', 'gm_norm_g',
                'gm_norm_b', 'gm_ws', 'gm_bs', 'out_norm_g', 'w_out', 'router_g_w', 'router_g_b', 'router_e_w',
                'router_e_b', 'exp_w_gate', 'exp_w_up', 'exp_w_down')


def kernel(x_prompt, x_sample, cache_k, cache_v, c, c_ctx, norm1_g, norm2_g, w_mod, b_mod, w_in, q_norm_g, k_norm_g, hy_conv_w, hy_conv_b, hf_w1, hf_b1, hf_w2, hf_b2, hf_w3, hf_b3, hf_freq, hy_bias, gm_norm_g, gm_norm_b, gm_ws, gm_bs, out_norm_g, w_out, router_g_w, router_g_b, router_e_w, router_e_b, exp_w_gate, exp_w_up, exp_w_down, final_norm_g):
    values = (norm1_g, norm2_g, w_mod, b_mod, w_in, q_norm_g, k_norm_g, hy_conv_w, hy_conv_b, hf_w1, hf_b1, hf_w2,
              hf_b2, hf_w3, hf_b3, hf_freq, hy_bias, gm_norm_g, gm_norm_b, gm_ws, gm_bs, out_norm_g, w_out,
              router_g_w, router_g_b, router_e_w, router_e_b, exp_w_gate, exp_w_up, exp_w_down)
    prm = dict(zip(_PARAM_NAMES, values))
    return _trunk(x_prompt, x_sample, cache_k, cache_v, c, c_ctx, prm, final_norm_g)
```
